```python
import math
import jax
import jax.numpy as jnp
from jax import lax
import numpy as np

D_MODEL = 2048
BATCH = 4
SEQ = 2048
DEPTH = 2

GRID_W = 64
CTX_LEN = 256
EPS = 1e-6
NEG_INF = -1e30
ROPE_BASE = 10000.0
Q_BLOCK = 128

DA_HEADS = 4
DA_QK_DIM = 64
DA_V_DIM = 2 * DA_QK_DIM
WB_HEADS = 8
WB_KV_HEADS = 2
WB_DIM = 64
WB_WINDOW = 128
WB_BLOCK = 128
NA_HEADS = 8
NA_DIM = 64
NA_ROWS = 8
NA_COLS = 16
S5_GROUPS = 32
S5_GROUP_CH = 16
S5_STATE = 64
S5_WIDTH = S5_GROUPS * S5_GROUP_CH
N_BRANCH = 4
BRANCH_WIDTH = 512
N_EXPERTS = 16
EXPERT_FF = 2048
EC_CAPACITY_FACTOR = 2

IN_WIDTHS = (
    2 * DA_HEADS * DA_QK_DIM, 2 * DA_HEADS * DA_QK_DIM, DA_HEADS * DA_V_DIM,
    WB_HEADS * WB_DIM, WB_KV_HEADS * WB_DIM, WB_KV_HEADS * WB_DIM,
    NA_HEADS * NA_DIM, NA_HEADS * NA_DIM, NA_HEADS * NA_DIM,
    S5_WIDTH,
)
N_MIX_IN = sum(IN_WIDTHS)
N_IN = N_MIX_IN + N_BRANCH * D_MODEL

kernel_name = 'hybrid_flow_block'


def rmsnorm(x, g):
    xf = x.astype(jnp.float32)
    y = xf * lax.rsqrt(jnp.mean(xf * xf, axis=-1, keepdims=True) + EPS)
    return (y * g.astype(jnp.float32)).astype(x.dtype)


def modulate(h, shift, scale):
    return h * (1.0 + scale) + shift


def split_cols(z, widths):
    return jnp.split(z, np.cumsum(widths)[:-1].tolist(), axis=-1)


def _rotate(x, pos):
    d = x.shape[-1]
    inv_freq = jnp.power(ROPE_BASE, -jnp.arange(0, d, 2, dtype=jnp.float32) / d)
    ang = pos.astype(jnp.float32)[:, None] * inv_freq[None, :]
    ang = ang.reshape((ang.shape[0],) + (1,) * (x.ndim - 3) + (d // 2,))
    cos, sin = jnp.cos(ang).astype(x.dtype), jnp.sin(ang).astype(x.dtype)
    x1, x2 = x[..., : d // 2], x[..., d // 2:]
    return jnp.concatenate([x1 * cos - x2 * sin, x2 * cos + x1 * sin], axis=-1)


def axial_rope(x, rows, cols):
    half = x.shape[-1] // 2
    return jnp.concatenate([_rotate(x[..., :half], rows), _rotate(x[..., half:], cols)], axis=-1)


def diff_attention(q_x, k_x, v_x, q_c, k_c, v_c, lam_vec, subln_g, layer, rows, cols, with_ctx):
    b, n, _ = q_x.shape
    heads = lambda t: t.reshape(t.shape[0], t.shape[1], DA_HEADS, 2, DA_QK_DIM)
    vheads = lambda t: t.reshape(t.shape[0], t.shape[1], DA_HEADS, DA_V_DIM)
    lam_init = 0.8 - 0.6 * math.exp(-0.3 * layer)
    lam = (jnp.exp(jnp.sum(lam_vec[0] * lam_vec[1])) - jnp.exp(jnp.sum(lam_vec[2] * lam_vec[3]))).astype(jnp.float32) + lam_init
    scale = DA_QK_DIM ** -0.5
    k_ctx, v_ctx = heads(k_c), vheads(v_c)
    k_all = jnp.concatenate([k_ctx, axial_rope(heads(k_x), rows, cols)], axis=1)
    v_all = jnp.concatenate([v_ctx, vheads(v_x)], axis=1)

    def attend(q, k, v):
        s = jnp.einsum('bqhmd,bkhmd->bhmqk', q, k).astype(jnp.float32) * scale
        p = jax.nn.softmax(s, axis=-1)
        w = p[:, :, 0] - lam * p[:, :, 1]
        return jnp.einsum('bhqk,bkhd->bqhd', w.astype(v.dtype), v)

    def post(o):
        return (rmsnorm(o, subln_g) * (1.0 - lam_init)).reshape(o.shape[0], o.shape[1], DA_HEADS * DA_V_DIM)

    nb = n // Q_BLOCK
    q_lat = axial_rope(heads(q_x), rows, cols).reshape(b, nb, Q_BLOCK, DA_HEADS, 2, DA_QK_DIM)
    o = lax.map(lambda qb: attend(qb, k_all, v_all), jnp.moveaxis(q_lat, 1, 0))
    y_x = post(jnp.moveaxis(o, 0, 1).reshape(b, n, DA_HEADS, DA_V_DIM))
    y_c = post(attend(heads(q_c), k_ctx, v_ctx)) if with_ctx else None
    return y_x, y_c


def window_gqa(q_x, k_x, v_x, q_c, k_c, v_c, sink, rows, cols, with_ctx):
    b, n, _ = q_x.shape
    g, r = WB_KV_HEADS, WB_HEADS // WB_KV_HEADS
    nb, blk, d = n // WB_BLOCK, WB_BLOCK, WB_DIM
    scale = d ** -0.5
    sink_gr = sink.astype(jnp.float32).reshape(g, r)
    q_lat = axial_rope(q_x.reshape(b, n, g, r, d), rows, cols).reshape(b, nb, blk, g, r, d)
    k_lat = axial_rope(k_x.reshape(b, n, g, d), rows, cols)
    v_lat = v_x.reshape(b, n, g, d)
    k_ctx = k_c.reshape(b, -1, g, d)
    v_ctx = v_c.reshape(b, -1, g, d)
    n_ctx = k_ctx.shape[1]

    def band(t):
        tp = jnp.pad(t.reshape(b, nb, blk, g, d), ((0, 0), (1, 1), (0, 0), (0, 0), (0, 0)))
        return jnp.concatenate([tp[:, :-2], tp[:, 1:-1], tp[:, 2:]], axis=2)

    k_band, v_band = band(k_lat), band(v_lat)
    blk_id = jnp.arange(nb)[:, None, None]
    q_pos = blk_id * blk + jnp.arange(blk)[None, :, None]
    k_pos = (blk_id - 1) * blk + jnp.arange(3 * blk)[None, None, :]
    valid = (jnp.abs(q_pos - k_pos) <= WB_WINDOW) & (k_pos >= 0) & (k_pos < n)
    s_band = jnp.einsum('bnqgrd,bnkgd->bngrqk', q_lat, k_band).astype(jnp.float32) * scale
    s_band = jnp.where(valid[None, :, None, None], s_band, NEG_INF)
    s_ctx = jnp.einsum('bnqgrd,bcgd->bngrqc', q_lat, k_ctx).astype(jnp.float32) * scale
    s_sink = jnp.broadcast_to(sink_gr[None, None, :, :, None, None], s_ctx.shape[:-1] + (1,))
    p = jax.nn.softmax(jnp.concatenate([s_band, s_ctx, s_sink], axis=-1), axis=-1)
    p_band = p[..., :3 * blk].astype(v_lat.dtype)
    p_ctx = p[..., 3 * blk:3 * blk + n_ctx].astype(v_lat.dtype)
    o = jnp.einsum('bngrqk,bnkgd->bnqgrd', p_band, v_band) + jnp.einsum('bngrqc,bcgd->bnqgrd', p_ctx, v_ctx)
    y_x = o.reshape(b, n, WB_HEADS * d)
    y_c = None
    if with_ctx:
        q_ctx = q_c.reshape(b, n_ctx, g, r, d)
        s = jnp.einsum('bqgrd,bcgd->bgrqc', q_ctx, k_ctx).astype(jnp.float32) * scale
        s_sink_c = jnp.broadcast_to(sink_gr[None, :, :, None, None], s.shape[:-1] + (1,))
        pc = jax.nn.softmax(jnp.concatenate([s, s_sink_c], axis=-1), axis=-1)[..., :n_ctx]
        y_c = jnp.einsum('bgrqc,bcgd->bqgrd', pc.astype(v_ctx.dtype), v_ctx).reshape(b, n_ctx, WB_HEADS * d)
    return y_x, y_c


def neighbourhood_attention(q_x, k_x, v_x, q_c, k_c, v_c, rpb, with_ctx):
    b, n, _ = q_x.shape
    n_rows = n // GRID_W
    kh = min(NA_ROWS, n_rows)
    h, d = NA_HEADS, NA_DIM
    scale = d ** -0.5
    grid = lambda t: t.reshape(b, n_rows, GRID_W, h, d)
    q_g, k_g, v_g = grid(q_x), grid(k_x), grid(v_x)
    k_ctx = k_c.reshape(b, -1, h, d)
    v_ctx = v_c.reshape(b, -1, h, d)
    n_ctx = k_ctx.shape[1]
    r = jnp.arange(n_rows)
    row_idx = jnp.clip(r - kh // 2, 0, n_rows - kh)[:, None] + jnp.arange(kh)[None, :]
    k_nb, v_nb = k_g[:, row_idx], v_g[:, row_idx]
    col = jnp.arange(GRID_W)
    col_start = jnp.clip(col - NA_COLS // 2, 0, GRID_W - NA_COLS)
    col_valid = (col[None, :] >= col_start[:, None]) & (col[None, :] < col_start[:, None] + NA_COLS)
    dr = row_idx - r[:, None] + (NA_ROWS - 1)
    dc = jnp.clip(col[None, :] - col[:, None], 1 - NA_COLS, NA_COLS - 1) + (NA_COLS - 1)
    bias = rpb.astype(jnp.float32)[:, dr[:, :, None, None], dc[None, None, :, :]]
    bias = jnp.transpose(bias, (1, 0, 3, 2, 4))
    s = jnp.einsum('brchd,brkwhd->brhckw', q_g, k_nb).astype(jnp.float32) * scale + bias[None]
    s = jnp.where(col_valid[:, None, :], s, NEG_INF).reshape(b, n_rows, h, GRID_W, kh * GRID_W)
    s_ctx = jnp.einsum('brchd,bshd->brhcs', q_g, k_ctx).astype(jnp.float32) * scale
    p = jax.nn.softmax(jnp.concatenate([s, s_ctx], axis=-1), axis=-1).astype(v_g.dtype)
    p_nb = p[..., :kh * GRID_W].reshape(b, n_rows, h, GRID_W, kh, GRID_W)
    o = jnp.einsum('brhckw,brkwhd->brchd', p_nb, v_nb) + jnp.einsum('brhcs,bshd->brchd', p[..., kh * GRID_W:], v_ctx)
    y_x = o.reshape(b, n, h * d)
    y_c = None
    if with_ctx:
        sc = jnp.einsum('bqhd,bkhd->bhqk', q_c.reshape(b, n_ctx, h, d), k_ctx).astype(jnp.float32) * scale
        pc = jax.nn.softmax(sc, axis=-1).astype(v_ctx.dtype)
        y_c = jnp.einsum('bhqk,bkhd->bqhd', pc, v_ctx).reshape(b, n_ctx, h * d)
    return y_x, y_c


def _diag_scan(a, bu, reverse):
    def combine(e1, e2):
        a1, b1 = e1
        a2, b2 = e2
        return a1 * a2, a2 * b1 + b2
    return lax.associative_scan(combine, (a, bu), reverse=reverse, axis=0)[1]


def s5_bidirectional(u_x, u_c, a_re, a_im, log_step, b_re, b_im, c_re, c_im, d_skip, w_glu, with_ctx):
    f32 = jnp.float32
    n_lat, n_ctx = u_x.shape[1], u_c.shape[1]

    def time_major(u):
        return jnp.moveaxis(u.astype(f32).reshape(u.shape[0], u.shape[1], S5_GROUPS, S5_GROUP_CH), 1, 0).astype(jnp.complex64)

    ux, uc = time_major(u_x), time_major(u_c)
    ys_x, ys_c = [], []
    for direction in range(2):
        reverse = direction == 1
        lam = lax.complex(a_re[direction].astype(f32), a_im[direction].astype(f32))
        lam_dt = lam * jnp.exp(log_step[direction].astype(f32))[:, None]
        a_bar = jnp.exp(lam_dt)
        b_bar = ((a_bar - 1.0) / lam)[:, :, None] * lax.complex(b_re[direction].astype(f32), b_im[direction].astype(f32))
        c_mat = lax.complex(c_re[direction].astype(f32), c_im[direction].astype(f32))
        h_c = _diag_scan(jnp.broadcast_to(a_bar, (n_ctx, 1) + a_bar.shape), jnp.einsum('tbgh,gph->tbgp', uc, b_bar), reverse)
        h0 = h_c[0] if reverse else h_c[-1]
        steps = jnp.arange(n_lat, 0, -1, dtype=f32) if reverse else jnp.arange(1, n_lat + 1, dtype=f32)
        carry = jnp.exp(lam_dt * steps[:, None, None])[:, None] * h0[None]
        h_x = _diag_scan(jnp.broadcast_to(a_bar, (n_lat, 1) + a_bar.shape), jnp.einsum('tbgh,gph->tbgp', ux, b_bar), reverse) + carry
        ys_x.append(jnp.einsum('tbgp,ghp->tbgh', h_x, c_mat).real)
        if with_ctx:
            ys_c.append(jnp.einsum('tbgp,ghp->tbgh', h_c, c_mat).real)

    def out(y, u):
        y = jnp.moveaxis(y, 0, 1).reshape(u.shape[0], u.shape[1], S5_WIDTH) + d_skip.astype(f32) * u.astype(f32)
        g = jax.nn.gelu(y)
        return (g * jax.nn.sigmoid(g @ w_glu.astype(f32))).astype(u.dtype)

    y_x = out(ys_x[0] + ys_x[1], u_x)
    y_c = out(ys_c[0] + ys_c[1], u_c) if with_ctx else None
    return y_x, y_c


def merge_branches(branches, gate_logits, w_branch, w_out):
    y = jnp.stack(branches, axis=2)
    proj = jnp.einsum('btnk,nkd->btnd', y, w_branch)
    gates = jax.nn.sigmoid(gate_logits.reshape(gate_logits.shape[0], gate_logits.shape[1], N_BRANCH, D_MODEL))
    return jnp.sum(gates * proj, axis=2) @ w_out


def expert_choice_moe(h, w_router, w1, w3, w2):
    b, t, _ = h.shape
    cap = EC_CAPACITY_FACTOR * t // N_EXPERTS
    aff = jax.nn.softmax(jnp.einsum('btd,de->bte', h, w_router).astype(jnp.float32), axis=-1)
    gate, idx = lax.top_k(jnp.swapaxes(aff, 1, 2), cap)
    bidx = jnp.arange(b)[:, None, None]
    xs = h[bidx, idx]
    a = jnp.einsum('becd,edf->becf', xs, w1)
    g = jnp.einsum('becd,edf->becf', xs, w3)
    out = jnp.einsum('becf,efd->becd', jax.nn.silu(a) * g, w2)
    out = out * gate[..., None].astype(out.dtype)
    return jnp.zeros_like(h).at[bidx, idx].add(out.astype(h.dtype))


def setup_inputs(seed: int = 0) -> dict:
    key = jax.random.key(seed)
    ks = jax.random.split(key, 30)
    f32 = jnp.float32

    def nrm(i, shape, scale):
        return jax.random.normal(ks[i], shape, f32) * scale

    s5_ap = (DEPTH, 2, S5_GROUPS, S5_STATE)
    return {
        'x': nrm(0, (BATCH, SEQ, D_MODEL), 1.0),
        'c': nrm(1, (BATCH, D_MODEL), 1.0),
        'ctx': nrm(2, (BATCH, CTX_LEN, D_MODEL), 1.0),
        'c_ctx': nrm(3, (D_MODEL,), 1.0),
        'ada_w': nrm(4, (DEPTH, D_MODEL, 6 * D_MODEL), 0.5 * D_MODEL ** -0.5),
        'ada_b': nrm(5, (DEPTH, 6 * D_MODEL), 0.02),
        'norm1_g': 1.0 + nrm(6, (DEPTH, D_MODEL), 0.02),
        'norm2_g': 1.0 + nrm(7, (DEPTH, D_MODEL), 0.02),
        'w_in': nrm(8, (DEPTH, D_MODEL, N_IN), D_MODEL ** -0.5),
        'da_lambda': nrm(9, (DEPTH, 4, DA_QK_DIM), 0.1),
        'da_subln_g': 1.0 + nrm(10, (DEPTH, DA_V_DIM), 0.02),
        'wb_sink': nrm(11, (DEPTH, WB_HEADS), 0.5),
        'na_rpb': nrm(12, (DEPTH, NA_HEADS, 2 * NA_ROWS - 1, 2 * NA_COLS - 1), 0.1),
        's5_a_re': -0.5 + nrm(13, s5_ap, 0.01),
        's5_a_im': jnp.pi * jnp.arange(S5_STATE, dtype=f32) + nrm(14, s5_ap, 0.01),
        's5_log_step': jax.random.uniform(ks[15], (DEPTH, 2, S5_GROUPS), f32, math.log(1e-3), math.log(1e-1)),
        's5_b_re': nrm(16, (DEPTH, 2, S5_GROUPS, S5_STATE, S5_GROUP_CH), (2 * S5_GROUP_CH) ** -0.5),
        's5_b_im': nrm(17, (DEPTH, 2, S5_GROUPS, S5_STATE, S5_GROUP_CH), (2 * S5_GROUP_CH) ** -0.5),
        's5_c_re': nrm(18, (DEPTH, 2, S5_GROUPS, S5_GROUP_CH, S5_STATE), S5_STATE ** -0.5),
        's5_c_im': nrm(19, (DEPTH, 2, S5_GROUPS, S5_GROUP_CH, S5_STATE), S5_STATE ** -0.5),
        's5_d': nrm(20, (DEPTH, S5_WIDTH), 1.0),
        's5_glu_w': nrm(21, (DEPTH, S5_WIDTH, S5_WIDTH), S5_WIDTH ** -0.5),
        'w_branch': nrm(22, (DEPTH, N_BRANCH, BRANCH_WIDTH, D_MODEL), BRANCH_WIDTH ** -0.5),
        'w_out': nrm(23, (DEPTH, D_MODEL, D_MODEL), D_MODEL ** -0.5),
        'w_router': nrm(24, (DEPTH, D_MODEL, N_EXPERTS), D_MODEL ** -0.5),
        'w_e1': nrm(25, (DEPTH, N_EXPERTS, D_MODEL, EXPERT_FF), D_MODEL ** -0.5),
        'w_e3': nrm(26, (DEPTH, N_EXPERTS, D_MODEL, EXPERT_FF), D_MODEL ** -0.5),
        'w_e2': nrm(27, (DEPTH, N_EXPERTS, EXPERT_FF, D_MODEL), EXPERT_FF ** -0.5),
        'final_g': 1.0 + nrm(28, (D_MODEL,), 0.02),
    }


def reference(x, c, ctx, c_ctx, ada_w, ada_b, norm1_g, norm2_g, w_in, da_lambda, da_subln_g, wb_sink, na_rpb,
              s5_a_re, s5_a_im, s5_log_step, s5_b_re, s5_b_im, s5_c_re, s5_c_im, s5_d, s5_glu_w,
              w_branch, w_out, w_router, w_e1, w_e3, w_e2, final_g):
    n_lat = x.shape[1]
    pos = jnp.arange(n_lat)
    rows, cols = pos // GRID_W, pos % GRID_W
    silu_c, silu_cc = jax.nn.silu(c), jax.nn.silu(c_ctx)
    for l in range(DEPTH):
        with_ctx = l < DEPTH - 1
        mod_x = jnp.split((silu_c @ ada_w[l] + ada_b[l])[:, None, :], 6, axis=-1)
        mod_c = jnp.split(silu_cc @ ada_w[l] + ada_b[l], 6, axis=-1)
        hx = modulate(rmsnorm(x, norm1_g[l]), mod_x[0], mod_x[1])
        hc = modulate(rmsnorm(ctx, norm1_g[l]), mod_c[0], mod_c[1])
        w_in_l = w_in[l]
        zx = hx @ w_in_l
        zc = hc @ (w_in_l if with_ctx else w_in_l[:, :N_MIX_IN])
        px = split_cols(zx[..., :N_MIX_IN], IN_WIDTHS)
        pc = split_cols(zc[..., :N_MIX_IN], IN_WIDTHS)
        ya = diff_attention(px[0], px[1], px[2], pc[0], pc[1], pc[2], da_lambda[l], da_subln_g[l], l, rows, cols, with_ctx)
        yb = window_gqa(px[3], px[4], px[5], pc[3], pc[4], pc[5], wb_sink[l], rows, cols, with_ctx)
        yc = neighbourhood_attention(px[6], px[7], px[8], pc[6], pc[7], pc[8], na_rpb[l], with_ctx)
        yd = s5_bidirectional(px[9], pc[9], s5_a_re[l], s5_a_im[l], s5_log_step[l], s5_b_re[l], s5_b_im[l],
                              s5_c_re[l], s5_c_im[l], s5_d[l], s5_glu_w[l], with_ctx)
        x = x + mod_x[2] * merge_branches([ya[0], yb[0], yc[0], yd[0]], zx[..., N_MIX_IN:], w_branch[l], w_out[l])
        if with_ctx:
            ctx = ctx + mod_c[2] * merge_branches([ya[1], yb[1], yc[1], yd[1]], zc[..., N_MIX_IN:], w_branch[l], w_out[l])
        hx = modulate(rmsnorm(x, norm2_g[l]), mod_x[3], mod_x[4])
        x = x + mod_x[5] * expert_choice_moe(hx, w_router[l], w_e1[l], w_e3[l], w_e2[l])
        if with_ctx:
            hc = modulate(rmsnorm(ctx, norm2_g[l]), mod_c[3], mod_c[4])
            ctx = ctx + mod_c[5] * expert_choice_moe(hc, w_router[l], w_e1[l], w_e3[l], w_e2[l])
    return rmsnorm(x, final_g)
```

```python
import functools
import math

import jax
import jax.numpy as jnp
from jax import lax
from jax.experimental import pallas as pl
from jax.experimental.pallas import tpu as pltpu

F32 = jnp.float32
BF16 = jnp.bfloat16
I32 = jnp.int32

GRID_W = 64
EPS = 1e-6
NEG_INF = -1e30
ROPE_BASE = 10000.0

DA_HEADS = 4
DA_QK_DIM = 64
DA_V_DIM = 2 * DA_QK_DIM
WB_HEADS = 8
WB_KV_HEADS = 2
WB_DIM = 64
WB_WINDOW = 128
WB_BLOCK = 128
NA_HEADS = 8
NA_DIM = 64
NA_ROWS = 8
NA_COLS = 16
S5_GROUPS = 32
S5_GROUP_CH = 16
S5_STATE = 64
S5_WIDTH = S5_GROUPS * S5_GROUP_CH
S5_CHUNK = 16
S5_ROWS = 8
N_BRANCH = 4
BRANCH_WIDTH = 512
N_EXPERTS = 16
EC_CAPACITY_FACTOR = 2

IN_WIDTHS = (
    2 * DA_HEADS * DA_QK_DIM, 2 * DA_HEADS * DA_QK_DIM, DA_HEADS * DA_V_DIM,
    WB_HEADS * WB_DIM, WB_KV_HEADS * WB_DIM, WB_KV_HEADS * WB_DIM,
    NA_HEADS * NA_DIM, NA_HEADS * NA_DIM, NA_HEADS * NA_DIM,
    S5_WIDTH,
)
N_MIX_IN = sum(IN_WIDTHS)
_OFFS = [0]
for _w in IN_WIDTHS:
    _OFFS.append(_OFFS[-1] + _w)
(OFF_DA_Q, OFF_DA_K, OFF_DA_V, OFF_WB_Q, OFF_WB_K, OFF_WB_V, OFF_NA_Q, OFF_NA_K, OFF_NA_V, OFF_S5, _) = _OFFS

LANE = 128
ADA_ROWS = 8
VMEM_LIMIT = 56 * 1024 * 1024

NN = (((1,), (0,)), ((), ()))
NT = (((1,), (1,)), ((), ()))


def _cp(*sem):
    return pltpu.CompilerParams(dimension_semantics=sem, vmem_limit_bytes=VMEM_LIMIT)


def _tile(n, pref, mult=LANE):
    if n <= pref:
        return n
    t = (pref // mult) * mult
    while t >= mult:
        if n % t == 0:
            return t
        t -= mult
    return n


def _split(a):
    hi = a.astype(BF16)
    lo = (a - hi.astype(F32)).astype(BF16)
    return hi, lo


def _dot(a, b, dims=NN):
    return lax.dot_general(a, b, dims, preferred_element_type=F32)


def _lane_group(lane, width):
    return lax.shift_right_logical(lane, int(math.log2(width)))


def _count(mask, axis):
    return jnp.sum(jnp.where(mask, 1.0, 0.0), axis=axis, keepdims=True)


def _dot3(a, b, dims=NN):
    ah, al = _split(a)
    bh, bl = _split(b)
    return _dot(ah, bh, dims) + (_dot(ah, bl, dims) + _dot(al, bh, dims))


def _ada_kernel(c_ref, w_ref, b_ref, o_ref):
    c = c_ref[...]
    s = c / (1.0 + jnp.exp(-c))
    o_ref[...] = _dot3(s, w_ref[...]) + b_ref[...]


def _ada_mod(cs, ada_w, ada_b):
    depth, d, n6 = ada_w.shape
    tn = _tile(n6, 512)
    return pl.pallas_call(
        _ada_kernel,
        grid=(depth, n6 // tn),
        in_specs=[
            pl.BlockSpec((ADA_ROWS, d), lambda l, j: (0, 0)),
            pl.BlockSpec((None, d, tn), lambda l, j: (l, 0, j)),
            pl.BlockSpec((None, 1, tn), lambda l, j: (l, 0, j)),
        ],
        out_specs=pl.BlockSpec((None, ADA_ROWS, tn), lambda l, j: (l, 0, j)),
        out_shape=jax.ShapeDtypeStruct((depth, ADA_ROWS, n6), F32),
        compiler_params=_cp("arbitrary", "arbitrary"),
        name="ada_mod",
    )(cs, ada_w, ada_b.reshape(depth, 1, n6))


def _norm_mod_rows(x, g, sh, sc):
    y = x * lax.rsqrt(jnp.mean(x * x, axis=-1, keepdims=True) + EPS) * g
    return y * (1.0 + sc) + sh


def _norm_mod_kernel(x_ref, g_ref, sh_ref, sc_ref, o_ref):
    o_ref[...] = _norm_mod_rows(x_ref[...], g_ref[...], sh_ref[...], sc_ref[...]).astype(o_ref.dtype)


def _norm_mod(x, g, sh, sc):
    b, t, d = x.shape
    tr = _tile(t, 256, 8)
    return pl.pallas_call(
        _norm_mod_kernel,
        grid=(b, t // tr),
        in_specs=[
            pl.BlockSpec((None, tr, d), lambda bi, i: (bi, i, 0)),
            pl.BlockSpec((1, d), lambda bi, i: (0, 0)),
            pl.BlockSpec((None, 1, d), lambda bi, i: (bi, 0, 0)),
            pl.BlockSpec((None, 1, d), lambda bi, i: (bi, 0, 0)),
        ],
        out_specs=pl.BlockSpec((None, tr, d), lambda bi, i: (bi, i, 0)),
        out_shape=jax.ShapeDtypeStruct((b, t, d), BF16),
        compiler_params=_cp("parallel", "parallel"),
        name="norm_mod",
    )(x, g.reshape(1, d), sh, sc)


def _norm_router_kernel(x_ref, g_ref, sh_ref, sc_ref, wrt_ref, h_ref, lg_ref):
    h = _norm_mod_rows(x_ref[...], g_ref[...], sh_ref[...], sc_ref[...])
    h_ref[...] = h.astype(h_ref.dtype)
    lg_ref[...] = _dot3(wrt_ref[...], h, NT)


def _norm_router(x, g, sh, sc, w_router_t):
    b, t, d = x.shape
    e = w_router_t.shape[0]
    tr = _tile(t, 256)
    nt = t // tr
    return pl.pallas_call(
        _norm_router_kernel,
        grid=(b, nt),
        in_specs=[
            pl.BlockSpec((None, tr, d), lambda bi, i: (bi, i, 0)),
            pl.BlockSpec((1, d), lambda bi, i: (0, 0)),
            pl.BlockSpec((None, 1, d), lambda bi, i: (bi, 0, 0)),
            pl.BlockSpec((None, 1, d), lambda bi, i: (bi, 0, 0)),
            pl.BlockSpec((e, d), lambda bi, i: (0, 0)),
        ],
        out_specs=[
            pl.BlockSpec((None, tr, d), lambda bi, i: (bi, i, 0)),
            pl.BlockSpec((e, tr), lambda bi, i: (0, bi * nt + i)),
        ],
        out_shape=[jax.ShapeDtypeStruct((b, t, d), BF16), jax.ShapeDtypeStruct((e, b * t), F32)],
        compiler_params=_cp("parallel", "parallel"),
        name="norm_router",
    )(x, g.reshape(1, d), sh, sc, w_router_t)


def _final_norm_kernel(x_ref, g_ref, o_ref):
    x = x_ref[...]
    o_ref[...] = x * lax.rsqrt(jnp.mean(x * x, axis=-1, keepdims=True) + EPS) * g_ref[...]


def _final_norm(x, g):
    b, t, d = x.shape
    tr = _tile(t, 256, 8)
    return pl.pallas_call(
        _final_norm_kernel,
        grid=(b, t // tr),
        in_specs=[pl.BlockSpec((None, tr, d), lambda bi, i: (bi, i, 0)), pl.BlockSpec((1, d), lambda bi, i: (0, 0))],
        out_specs=pl.BlockSpec((None, tr, d), lambda bi, i: (bi, i, 0)),
        out_shape=jax.ShapeDtypeStruct((b, t, d), F32),
        compiler_params=_cp("parallel", "parallel"),
        name="final_norm",
    )(x, g.reshape(1, d))


def _mm_kernel(a_ref, w_ref, o_ref, wb_ref):
    @pl.when(pl.program_id(1) == 0)
    def _():
        wb_ref[...] = w_ref[...].astype(BF16)

    o_ref[...] = _dot(a_ref[...], wb_ref[...]).astype(o_ref.dtype)


def _mm(a, w, layer, col0, ncols, out_dtype=F32):
    m, k = a.shape
    tn = 256
    assert ncols % tn == 0 and col0 % tn == 0
    tm = _tile(m, 1024, 8)
    j0 = col0 // tn
    return pl.pallas_call(
        _mm_kernel,
        grid=(ncols // tn, m // tm),
        in_specs=[
            pl.BlockSpec((tm, k), lambda j, i: (i, 0)),
            pl.BlockSpec((None, k, tn), lambda j, i: (layer, 0, j0 + j)),
        ],
        out_specs=pl.BlockSpec((tm, tn), lambda j, i: (i, j)),
        out_shape=jax.ShapeDtypeStruct((m, ncols), out_dtype),
        scratch_shapes=[pltpu.VMEM((k, tn), BF16)],
        compiler_params=_cp("arbitrary", "arbitrary"),
        name="mm_in",
    )(a, w)


def _mm_res_kernel(a_ref, w_ref, x_ref, gate_ref, o_ref, wb_ref):
    @pl.when(pl.program_id(1) == 0)
    def _():
        wb_ref[...] = w_ref[...].astype(BF16)

    o_ref[...] = x_ref[...] + gate_ref[...] * _dot(a_ref[...], wb_ref[...])


def _mm_res(a, w, layer, x, gate):
    b, t, n = x.shape
    m, k = a.shape
    tn = _tile(n, 512)
    tm = _tile(t, 1024, 8)
    per_b = t // tm
    out = pl.pallas_call(
        _mm_res_kernel,
        grid=(n // tn, m // tm),
        in_specs=[
            pl.BlockSpec((tm, k), lambda j, i: (i, 0)),
            pl.BlockSpec((None, k, tn), lambda j, i: (layer, 0, j)),
            pl.BlockSpec((tm, tn), lambda j, i: (i, j)),
            pl.BlockSpec((None, 1, tn), lambda j, i: (i // per_b, 0, j)),
        ],
        out_specs=pl.BlockSpec((tm, tn), lambda j, i: (i, j)),
        out_shape=jax.ShapeDtypeStruct((m, n), F32),
        scratch_shapes=[pltpu.VMEM((k, tn), BF16)],
        compiler_params=_cp("arbitrary", "arbitrary"),
        name="mm_out_res",
    )(a, w, x.reshape(m, n), gate)
    return out.reshape(b, t, n)


def _rope_kernel(z_ref, c_ref, sa_ref, sb_ref, o_ref):
    x = z_ref[...]
    o_ref[...] = x * c_ref[...] + pltpu.roll(x, LANE - 16, 1) * sa_ref[...] + pltpu.roll(x, 16, 1) * sb_ref[...]


def _rope_tables(n):
    pos = jnp.arange(n)
    rows, cols = (pos // GRID_W).astype(F32), (pos % GRID_W).astype(F32)
    half = DA_QK_DIM // 2
    inv_freq = jnp.power(ROPE_BASE, -jnp.arange(0, half, 2, dtype=F32) / half)
    ang_r, ang_c = rows[:, None] * inv_freq[None, :], cols[:, None] * inv_freq[None, :]
    cos64 = jnp.concatenate([jnp.cos(ang_r)] * 2 + [jnp.cos(ang_c)] * 2, axis=-1)
    sin64 = jnp.concatenate([jnp.sin(ang_r)] * 2 + [jnp.sin(ang_c)] * 2, axis=-1)
    cos, sin = jnp.tile(cos64, (1, 2)), jnp.tile(sin64, (1, 2))
    first = (jnp.arange(LANE) % 32) < 16
    return cos, jnp.where(first, -sin, 0.0), jnp.where(first, 0.0, sin)


ROPE_COLS = 2 * IN_WIDTHS[0] + IN_WIDTHS[3] + IN_WIDTHS[4]
QK_DA_Q, QK_DA_K, QK_WB_Q, QK_WB_K = 0, IN_WIDTHS[0], 2 * IN_WIDTHS[0], 2 * IN_WIDTHS[0] + IN_WIDTHS[3]


def _rope(zx, tables):
    b, n, _ = zx.shape
    n_da = 2 * IN_WIDTHS[0] // LANE
    skip = (OFF_WB_Q - 2 * IN_WIDTHS[0]) // LANE
    cos, sa, sb = tables
    tab = pl.BlockSpec((n, LANE), lambda bi, j: (0, 0))
    return pl.pallas_call(
        _rope_kernel,
        grid=(b, ROPE_COLS // LANE),
        in_specs=[pl.BlockSpec((None, n, LANE), lambda bi, j: (bi, 0, jnp.where(j < n_da, j, j + skip))), tab, tab, tab],
        out_specs=pl.BlockSpec((None, n, LANE), lambda bi, j: (bi, 0, j)),
        out_shape=jax.ShapeDtypeStruct((b, n, ROPE_COLS), F32),
        compiler_params=_cp("parallel", "parallel"),
        name="rope",
    )(zx, cos, sa, sb)


def _diff_attn_kernel(*refs, has_x, post_scale):
    if has_x:
        lam_ref, g_ref, q_ref, kc_ref, vc_ref, kx_ref, vx_ref, o_ref = refs
    else:
        lam_ref, g_ref, q_ref, kc_ref, vc_ref, o_ref = refs
    q = q_ref[...] * (DA_QK_DIM ** -0.5)
    lane = lax.broadcasted_iota(I32, q.shape, 1)
    kc, vc = kc_ref[...].astype(BF16), vc_ref[...].astype(BF16)
    if has_x:
        kx, vx = kx_ref[...].astype(BF16), vx_ref[...].astype(BF16)

    def attend(qm):
        sc = _dot(qm, kc, NT)
        m = jnp.max(sc, axis=-1, keepdims=True)
        if has_x:
            sx = _dot(qm, kx, NT)
            m = jnp.maximum(m, jnp.max(sx, axis=-1, keepdims=True))
        pc = jnp.exp(sc - m)
        l = jnp.sum(pc, axis=-1, keepdims=True)
        o = _dot(pc.astype(BF16), vc)
        if has_x:
            px = jnp.exp(sx - m)
            l = l + jnp.sum(px, axis=-1, keepdims=True)
            o = o + _dot(px.astype(BF16), vx)
        return o / l

    o = attend(jnp.where(lane < DA_QK_DIM, q, 0.0).astype(BF16)) \
        - lam_ref[...] * attend(jnp.where(lane >= DA_QK_DIM, q, 0.0).astype(BF16))
    y = o * lax.rsqrt(jnp.mean(o * o, axis=-1, keepdims=True) + EPS) * g_ref[...] * post_scale
    o_ref[...] = y.astype(o_ref.dtype)


def _diff_attn(q_arr, q_blk0, zc, kx_arr, kx_blk0, vx_arr, lam, subln_g, lam_init, has_x):
    b, tq_all, _ = q_arr.shape
    c = zc.shape[1]
    tq = _tile(tq_all, 256, 8)
    kc_blk0, vc_blk0 = OFF_DA_K // LANE, OFF_DA_V // LANE
    in_specs = [
        pl.BlockSpec((1, 1), lambda bi, h, i: (0, 0)),
        pl.BlockSpec((1, DA_V_DIM), lambda bi, h, i: (0, 0)),
        pl.BlockSpec((None, tq, LANE), lambda bi, h, i: (bi, i, q_blk0 + h)),
        pl.BlockSpec((None, c, LANE), lambda bi, h, i: (bi, 0, kc_blk0 + h)),
        pl.BlockSpec((None, c, LANE), lambda bi, h, i: (bi, 0, vc_blk0 + h)),
    ]
    args = [lam.reshape(1, 1), subln_g.reshape(1, DA_V_DIM), q_arr, zc, zc]
    if has_x:
        n = kx_arr.shape[1]
        in_specs += [
            pl.BlockSpec((None, n, LANE), lambda bi, h, i: (bi, 0, kx_blk0 + h)),
            pl.BlockSpec((None, n, LANE), lambda bi, h, i: (bi, 0, vc_blk0 + h)),
        ]
        args += [kx_arr, vx_arr]
    return pl.pallas_call(
        functools.partial(_diff_attn_kernel, has_x=has_x, post_scale=1.0 - lam_init),
        grid=(b, DA_HEADS, tq_all // tq),
        in_specs=in_specs,
        out_specs=pl.BlockSpec((None, tq, LANE), lambda bi, h, i: (bi, i, h)),
        out_shape=jax.ShapeDtypeStruct((b, tq_all, DA_HEADS * DA_V_DIM), BF16),
        compiler_params=_cp("parallel", "parallel", "arbitrary"),
        name="diff_attn_x" if has_x else "diff_attn_c",
    )(*args)


def _group_cols(x, g):
    lane = lax.broadcasted_iota(I32, x.shape, 1)
    own = jnp.where(_lane_group(lane, WB_DIM) == g, x, pltpu.roll(x, WB_DIM, 1))
    return jnp.concatenate([own, own], axis=1).astype(BF16)


def _win_attn_kernel(*refs, has_x, n_tok):
    if has_x:
        sink_ref, q_ref, kc_ref, vc_ref, kp_ref, k0_ref, kn_ref, vp_ref, v0_ref, vn_ref, o_ref = refs
    else:
        sink_ref, q_ref, kc_ref, vc_ref, o_ref = refs
    g, i = pl.program_id(1), pl.program_id(2)
    rep = WB_HEADS // WB_KV_HEADS
    q = q_ref[...] * (WB_DIM ** -0.5)
    blk = q.shape[0]
    lane = lax.broadcasted_iota(I32, q.shape, 1)
    kc, vc = _group_cols(kc_ref[...], g), _group_cols(vc_ref[...], g)
    if has_x:
        kb = _group_cols(jnp.concatenate([kp_ref[...], k0_ref[...], kn_ref[...]], axis=0), g)
        vb = _group_cols(jnp.concatenate([vp_ref[...], v0_ref[...], vn_ref[...]], axis=0), g)
        q_pos = i * blk + lax.broadcasted_iota(I32, (blk, 3 * blk), 0)
        k_pos = (i - 1) * blk + lax.broadcasted_iota(I32, (blk, 3 * blk), 1)
        valid = (jnp.abs(q_pos - k_pos) <= WB_WINDOW) & (k_pos >= 0) & (k_pos < n_tok)
    acc = jnp.zeros(q.shape, F32)
    for r in range(rep):
        mine = _lane_group(lane, WB_DIM) == r
        qr = jnp.where(mine, q, 0.0).astype(BF16)
        sink = sink_ref[:, r * WB_DIM:r * WB_DIM + 1]
        sc = _dot(qr, kc, NT)
        m = jnp.maximum(jnp.max(sc, axis=-1, keepdims=True), sink)
        if has_x:
            sb = jnp.where(valid, _dot(qr, kb, NT), NEG_INF)
            m = jnp.maximum(m, jnp.max(sb, axis=-1, keepdims=True))
        pc = jnp.exp(sc - m)
        l = jnp.sum(pc, axis=-1, keepdims=True) + jnp.exp(sink - m)
        o = _dot(pc.astype(BF16), vc)
        if has_x:
            pb = jnp.exp(sb - m)
            l = l + jnp.sum(pb, axis=-1, keepdims=True)
            o = o + _dot(pb.astype(BF16), vb)
        acc = acc + jnp.where(mine, o / l, 0.0)
    o_ref[...] = acc.astype(o_ref.dtype)


def _win_attn(q_arr, q_blk0, zc, qk, zx, sink, has_x):
    b, tq_all, _ = q_arr.shape
    c = zc.shape[1]
    blk = WB_BLOCK
    nb = tq_all // blk
    rep = WB_HEADS // WB_KV_HEADS
    gw = rep * WB_DIM
    sink_arr = jnp.repeat(sink.astype(F32).reshape(WB_KV_HEADS, 1, rep), WB_DIM, axis=-1)
    kc_blk, vc_blk = OFF_WB_K // LANE, OFF_WB_V // LANE
    in_specs = [
        pl.BlockSpec((None, 1, gw), lambda bi, g, i: (g, 0, 0)),
        pl.BlockSpec((None, blk, gw), lambda bi, g, i: (bi, i, q_blk0 + g)),
        pl.BlockSpec((None, c, LANE), lambda bi, g, i: (bi, 0, kc_blk)),
        pl.BlockSpec((None, c, LANE), lambda bi, g, i: (bi, 0, vc_blk)),
    ]
    args = [sink_arr, q_arr, zc, zc]
    n_tok = 0
    if has_x:
        n_tok = qk.shape[1]
        kx_blk = QK_WB_K // LANE
        prev = lambda i: jnp.maximum(i - 1, 0)
        nxt = lambda i: jnp.minimum(i + 1, nb - 1)
        for arr, cb in ((qk, kx_blk), (zx, vc_blk)):
            in_specs += [
                pl.BlockSpec((None, blk, LANE), lambda bi, g, i, cb=cb: (bi, prev(i), cb)),
                pl.BlockSpec((None, blk, LANE), lambda bi, g, i, cb=cb: (bi, i, cb)),
                pl.BlockSpec((None, blk, LANE), lambda bi, g, i, cb=cb: (bi, nxt(i), cb)),
            ]
            args += [arr, arr, arr]
    return pl.pallas_call(
        functools.partial(_win_attn_kernel, has_x=has_x, n_tok=n_tok),
        grid=(b, WB_KV_HEADS, nb),
        in_specs=in_specs,
        out_specs=pl.BlockSpec((None, blk, gw), lambda bi, g, i: (bi, i, g)),
        out_shape=jax.ShapeDtypeStruct((b, tq_all, WB_HEADS * WB_DIM), BF16),
        compiler_params=_cp("parallel", "parallel", "arbitrary"),
        name="win_attn_x" if has_x else "win_attn_c",
    )(*args)


def _na_kernel(*refs, has_x, n_rows):
    if has_x:
        q_ref, kc_ref, vc_ref, k_ref, v_ref, bias_ref, o_ref = refs
    else:
        q_ref, kc_ref, vc_ref, o_ref = refs
    q = q_ref[...] * (NA_DIM ** -0.5)
    lane = lax.broadcasted_iota(I32, q.shape, 1)
    kc, vc = kc_ref[...].astype(BF16), vc_ref[...].astype(BF16)
    if has_x:
        r = pl.program_id(2)
        start = jnp.clip(r - NA_ROWS // 2, 0, n_rows - NA_ROWS)
        off = pl.multiple_of(start * GRID_W, GRID_W)
        nk = NA_ROWS * GRID_W
        k, v = k_ref[pl.ds(off, nk), :].astype(BF16), v_ref[pl.ds(off, nk), :].astype(BF16)
        col = lax.broadcasted_iota(I32, (GRID_W, nk), 0)
        w = lax.broadcasted_iota(I32, (GRID_W, nk), 1) & (GRID_W - 1)
        col_start = jnp.clip(col - NA_COLS // 2, 0, GRID_W - NA_COLS)
        valid = (w >= col_start) & (w < col_start + NA_COLS)
    acc = jnp.zeros(q.shape, F32)
    for hh in range(LANE // NA_DIM):
        mine = _lane_group(lane, NA_DIM) == hh
        qh = jnp.where(mine, q, 0.0).astype(BF16)
        sc = _dot(qh, kc, NT)
        m = jnp.max(sc, axis=-1, keepdims=True)
        if has_x:
            s = jnp.where(valid, _dot(qh, k, NT) + bias_ref[hh], NEG_INF)
            m = jnp.maximum(m, jnp.max(s, axis=-1, keepdims=True))
        pc = jnp.exp(sc - m)
        l = jnp.sum(pc, axis=-1, keepdims=True)
        o = _dot(pc.astype(BF16), vc)
        if has_x:
            p = jnp.exp(s - m)
            l = l + jnp.sum(p, axis=-1, keepdims=True)
            o = o + _dot(p.astype(BF16), v)
        acc = acc + jnp.where(mine, o / l, 0.0)
    o_ref[...] = acc.astype(o_ref.dtype)


def _na_bias_table(rpb):
    oi = jnp.arange(NA_ROWS)
    k = jnp.arange(NA_ROWS)
    dr = k[None, :] - oi[:, None] + (NA_ROWS - 1)
    col = jnp.arange(GRID_W)
    dc = jnp.clip(col[None, :] - col[:, None], 1 - NA_COLS, NA_COLS - 1) + (NA_COLS - 1)
    t = rpb.astype(F32)[:, dr[:, :, None, None], dc[None, None, :, :]]
    t = jnp.transpose(t, (0, 1, 3, 2, 4))
    return t.reshape(NA_HEADS, NA_ROWS, GRID_W, NA_ROWS * GRID_W)


def _na_attn(q_arr, zc, zx, bias_tab, has_x):
    b, tq_all, _ = q_arr.shape
    c = zc.shape[1]
    per = LANE // NA_DIM
    nj = NA_HEADS // per
    q_blk0, k_blk0, v_blk0 = OFF_NA_Q // LANE, OFF_NA_K // LANE, OFF_NA_V // LANE
    tq = GRID_W if has_x else _tile(tq_all, 256, 8)
    nq = tq_all // tq
    in_specs = [
        pl.BlockSpec((None, tq, LANE), lambda bi, j, r: (bi, r, q_blk0 + j)),
        pl.BlockSpec((None, c, LANE), lambda bi, j, r: (bi, 0, k_blk0 + j)),
        pl.BlockSpec((None, c, LANE), lambda bi, j, r: (bi, 0, v_blk0 + j)),
    ]
    args = [q_arr, zc, zc]
    if has_x:
        n = zx.shape[1]
        assert nq >= NA_ROWS
        oi = lambda r: r - jnp.clip(r - NA_ROWS // 2, 0, nq - NA_ROWS)
        in_specs += [
            pl.BlockSpec((None, n, LANE), lambda bi, j, r: (bi, 0, k_blk0 + j)),
            pl.BlockSpec((None, n, LANE), lambda bi, j, r: (bi, 0, v_blk0 + j)),
            pl.BlockSpec((per, None, GRID_W, NA_ROWS * GRID_W), lambda bi, j, r: (j, oi(r), 0, 0)),
        ]
        args += [zx, zx, bias_tab]
    return pl.pallas_call(
        functools.partial(_na_kernel, has_x=has_x, n_rows=nq),
        grid=(b, nj, nq),
        in_specs=in_specs,
        out_specs=pl.BlockSpec((None, tq, LANE), lambda bi, j, r: (bi, r, j)),
        out_shape=jax.ShapeDtypeStruct((b, tq_all, NA_HEADS * NA_DIM), BF16),
        compiler_params=_cp("parallel", "parallel", "arbitrary"),
        name="na_attn_x" if has_x else "na_attn_c",
    )(*args)


def _s5_kernel(u_ref, km_ref, b1_ref, b2_ref, a1_ref, a2_ref, a3_ref, cp_ref, y_ref, s1_ref, s2_ref, h1_ref, *, n_chunks):
    u = u_ref[...]
    uh, ul = _split(u)

    def udot(w_ref):
        wh, wl = _split(w_ref[...])
        return _dot(uh, wh) + (_dot(uh, wl) + _dot(ul, wh))

    s1_ref[...] = udot(b1_ref)
    s2_ref[...] = udot(b2_ref)
    a1, a2, a3 = a1_ref[...], a2_ref[...], a3_ref[...]

    def step(c, carry):
        v1, v2 = carry
        r0 = pl.multiple_of(c * S5_ROWS, S5_ROWS)
        h1_ref[pl.ds(r0, S5_ROWS), :] = v1
        n1 = a1 * v1 + a2 * v2 + s1_ref[pl.ds(r0, S5_ROWS), :]
        n2 = a1 * v2 + a3 * v1 + s2_ref[pl.ds(r0, S5_ROWS), :]
        return n1, n2

    zero = jnp.zeros((S5_ROWS, 2 * S5_STATE), F32)
    lax.fori_loop(0, n_chunks, step, (zero, zero))
    y_ref[...] = udot(km_ref) + _dot3(h1_ref[...], cp_ref[...])


def _s5_tables(a_re, a_im, log_step, b_re, b_im, c_re, c_im):
    ell, hch, p = S5_CHUNK, S5_GROUP_CH, S5_STATE
    lam = lax.complex(a_re.astype(F32), a_im.astype(F32))
    lam_dt = lam * jnp.exp(log_step.astype(F32))[:, :, None]
    a_bar = jnp.exp(lam_dt)
    b_bar = ((a_bar - 1.0) / lam)[:, :, :, None] * lax.complex(b_re.astype(F32), b_im.astype(F32))
    c_mat = lax.complex(c_re.astype(F32), c_im.astype(F32))
    steps = jnp.arange(ell + 1, dtype=F32)
    apow = jnp.exp(lam_dt[:, :, None, :] * steps[None, None, :, None])
    cb = c_mat[:, :, None, :, None, :] * jnp.swapaxes(b_bar, 2, 3)[:, :, None, None, :, :]
    kk = jnp.sum(cb * apow[:, :, :ell, None, None, :], axis=-1).real
    i_idx = jnp.arange(ell)
    lag = i_idx[None, :] - i_idx[:, None]
    kt = kk[:, :, jnp.clip(lag, 0, ell - 1)]
    kt = jnp.where((lag >= 0)[None, None, :, :, None, None], kt, 0.0)
    kmat = jnp.transpose(kt, (0, 1, 2, 5, 3, 4)).reshape(2, S5_GROUPS, ell * hch, ell * hch)
    bp = apow[:, :, ell - 1 - i_idx, None, :] * jnp.swapaxes(b_bar, 2, 3)[:, :, None, :, :]
    bp = bp.reshape(2, S5_GROUPS, ell * hch, p)
    b1 = jnp.concatenate([bp.real, bp.imag], axis=-1)
    b2 = jnp.concatenate([bp.imag, bp.real], axis=-1)
    al = apow[:, :, ell, :]
    rows = lambda v: jnp.broadcast_to(v[:, :, None, :], (2, S5_GROUPS, S5_ROWS, 2 * p))
    a1 = rows(jnp.concatenate([al.real, al.real], axis=-1))
    a2 = rows(jnp.concatenate([-al.imag, al.imag], axis=-1))
    a3 = rows(jnp.concatenate([al.imag, -al.imag], axis=-1))
    gm = c_mat[:, :, None, :, :] * apow[:, :, 1:, None, :]
    gm = jnp.transpose(gm, (0, 1, 4, 2, 3)).reshape(2, S5_GROUPS, p, ell * hch)
    cp = jnp.concatenate([gm.real, -gm.imag], axis=2)
    return kmat, b1, b2, a1, a2, a3, cp


def _s5_scan(u_x, u_c, tables):
    b, n, _ = u_x.shape
    c = u_c.shape[1]
    tt = c + n
    ell, hch = S5_CHUNK, S5_GROUP_CH
    nc = tt // ell
    assert tt % ell == 0 and b <= S5_ROWS
    seq_f = jnp.concatenate([u_c, u_x], axis=1)
    seq_r = jnp.concatenate([jnp.flip(u_c, 1), jnp.flip(u_x, 1)], axis=1)
    u = jnp.stack([seq_f, seq_r]).reshape(2, b, nc, ell, S5_GROUPS, hch)
    u = jnp.transpose(u, (0, 4, 2, 1, 3, 5))
    u = jnp.pad(u, ((0, 0), (0, 0), (0, 0), (0, S5_ROWS - b), (0, 0), (0, 0)))
    mr = nc * S5_ROWS
    w = ell * hch
    u = u.reshape(2, S5_GROUPS, mr, w)
    kmat, b1, b2, a1, a2, a3, cp = tables
    p2 = 2 * S5_STATE
    spec = lambda r, cdim: pl.BlockSpec((None, None, r, cdim), lambda d, g: (d, g, 0, 0))
    y = pl.pallas_call(
        functools.partial(_s5_kernel, n_chunks=nc),
        grid=(2, S5_GROUPS),
        in_specs=[spec(mr, w), spec(w, w), spec(w, p2), spec(w, p2), spec(S5_ROWS, p2), spec(S5_ROWS, p2),
                  spec(S5_ROWS, p2), spec(p2, w)],
        out_specs=spec(mr, w),
        out_shape=jax.ShapeDtypeStruct((2, S5_GROUPS, mr, w), F32),
        scratch_shapes=[pltpu.VMEM((mr, p2), F32), pltpu.VMEM((mr, p2), F32), pltpu.VMEM((mr, p2), F32)],
        compiler_params=_cp("parallel", "parallel"),
        name="s5_scan",
    )(u, kmat, b1, b2, a1, a2, a3, cp)
    y = y.reshape(2, S5_GROUPS, nc, S5_ROWS, ell, hch)[:, :, :, :b]
    y = jnp.transpose(y, (0, 3, 2, 4, 1, 5)).reshape(2, b, tt, S5_WIDTH)
    yf_c, yf_x = y[0, :, :c], y[0, :, c:]
    yr_c, yr_x = jnp.flip(y[1, :, :c], 1), jnp.flip(y[1, :, c:], 1)
    return (yf_x, yr_x), (yf_c, yr_c)


def _s5_out_kernel(yf_ref, yr_ref, u_ref, d_ref, w_ref, o_ref):
    y = yf_ref[...] + yr_ref[...] + d_ref[...] * u_ref[...]
    g = jax.nn.gelu(y)
    o_ref[...] = (g * jax.nn.sigmoid(_dot(g.astype(BF16), w_ref[...].astype(BF16)))).astype(o_ref.dtype)


def _s5_out(yf, yr, u, d_skip, w_glu):
    b, t, w = u.shape
    tr = _tile(t, 512, 8)
    row = pl.BlockSpec((None, tr, w), lambda bi, i: (bi, i, 0))
    return pl.pallas_call(
        _s5_out_kernel,
        grid=(b, t // tr),
        in_specs=[row, row, row, pl.BlockSpec((1, w), lambda bi, i: (0, 0)), pl.BlockSpec((w, w), lambda bi, i: (0, 0))],
        out_specs=row,
        out_shape=jax.ShapeDtypeStruct((b, t, w), BF16),
        compiler_params=_cp("parallel", "parallel"),
        name="s5_out",
    )(yf, yr, u, d_skip.reshape(1, w).astype(F32), w_glu)


def _merge_kernel(h_ref, ya_ref, yb_ref, yc_ref, yd_ref, wg0_ref, wg1_ref, wg2_ref, wg3_ref, wb_ref, o_ref, wgs_ref, wbs_ref):
    @pl.when(pl.program_id(1) == 0)
    def _():
        for n, wg_ref in enumerate((wg0_ref, wg1_ref, wg2_ref, wg3_ref)):
            wgs_ref[n] = wg_ref[...].astype(BF16)
        wbs_ref[...] = wb_ref[...].astype(BF16)

    h = h_ref[...]
    acc = None
    for n, y_ref in enumerate((ya_ref, yb_ref, yc_ref, yd_ref)):
        term = jax.nn.sigmoid(_dot(h, wgs_ref[n])) * _dot(y_ref[...], wbs_ref[n])
        acc = term if acc is None else acc + term
    o_ref[...] = acc.astype(o_ref.dtype)


def _merge(h, branches, w_in, w_branch, layer):
    m, d = h.shape
    bw = branches[0].shape[1]
    tn = 256
    tm = _tile(m, 512, 8)
    g0 = N_MIX_IN // tn
    per = d // tn
    assert N_MIX_IN % tn == 0 and d % tn == 0
    wg_specs = [pl.BlockSpec((None, d, tn), lambda j, i, n=n: (layer, 0, g0 + n * per + j)) for n in range(N_BRANCH)]
    y_spec = pl.BlockSpec((tm, bw), lambda j, i: (i, 0))
    return pl.pallas_call(
        _merge_kernel,
        grid=(d // tn, m // tm),
        in_specs=[pl.BlockSpec((tm, d), lambda j, i: (i, 0)), y_spec, y_spec, y_spec, y_spec] + wg_specs
                 + [pl.BlockSpec((None, N_BRANCH, bw, tn), lambda j, i: (layer, 0, 0, j))],
        out_specs=pl.BlockSpec((tm, tn), lambda j, i: (i, j)),
        out_shape=jax.ShapeDtypeStruct((m, d), BF16),
        scratch_shapes=[pltpu.VMEM((N_BRANCH, d, tn), BF16), pltpu.VMEM((N_BRANCH, bw, tn), BF16)],
        compiler_params=_cp("arbitrary", "arbitrary"),
        name="merge",
    )(h, *branches, w_in, w_in, w_in, w_in, w_branch)


def _topk_kernel(lg_ref, slot_ref, aff_ref, tri_ref, *, cap):
    e, t = lg_ref.shape

    @pl.when(pl.program_id(0) == 0)
    def _():
        rows = 256 if t % 256 == 0 else t
        for r0 in range(0, t, rows):
            ri = lax.broadcasted_iota(I32, (rows, t), 0) + r0
            ci = lax.broadcasted_iota(I32, (rows, t), 1)
            tri_ref[r0:r0 + rows, :] = jnp.where(ri < ci, 1.0, 0.0).astype(BF16)

    lg = lg_ref[...]
    ex = jnp.exp(lg - jnp.max(lg, axis=0, keepdims=True))
    aff = ex / jnp.sum(ex, axis=0, keepdims=True)
    aff_ref[...] = aff
    bits = pltpu.bitcast(aff, I32)

    def search(_, carry):
        lo, hi = carry
        mid = lo + lax.shift_right_logical(hi - lo + 1, 1)
        ok = _count(bits >= mid, 1) >= cap
        return jnp.where(ok, mid, lo), jnp.where(ok, hi, mid - 1)

    lo0 = jnp.zeros((e, 1), I32)
    hi0 = jnp.full((e, 1), 0x7F800000, I32)
    thr, _ = lax.fori_loop(0, 32, search, (lo0, hi0))
    gt = bits > thr
    eq = bits == thr
    need = cap - _count(gt, 1)
    tri = tri_ref[...]
    eq_before = _dot(jnp.where(eq, 1.0, 0.0).astype(BF16), tri)
    sel = gt | (eq & (eq_before < need))
    sel_before = _dot(jnp.where(sel, 1.0, 0.0).astype(BF16), tri)
    slot_ref[...] = jnp.where(sel, sel_before.astype(I32), -1)


def _topk(logits_t, b, cap):
    e, bt = logits_t.shape
    t = bt // b
    return pl.pallas_call(
        functools.partial(_topk_kernel, cap=cap),
        grid=(b,),
        in_specs=[pl.BlockSpec((e, t), lambda bi: (0, bi))],
        out_specs=[pl.BlockSpec((None, e, t), lambda bi: (bi, 0, 0)), pl.BlockSpec((None, e, t), lambda bi: (bi, 0, 0))],
        out_shape=[jax.ShapeDtypeStruct((b, e, t), I32), jax.ShapeDtypeStruct((b, e, t), F32)],
        scratch_shapes=[pltpu.VMEM((t, t), BF16)],
        compiler_params=_cp("arbitrary"),
        name="route_topk",
    )(logits_t)


def _gather_kernel(slot_ref, aff_ref, h_ref, xs_ref, gate_ref, *, cap):
    t = h_ref.shape[0]
    pick = lax.broadcasted_iota(I32, (cap, t), 0) == slot_ref[...]
    xs_ref[...] = _dot(jnp.where(pick, 1.0, 0.0).astype(BF16), h_ref[...]).astype(xs_ref.dtype)
    gate_ref[...] = jnp.sum(jnp.where(pick, aff_ref[...], 0.0), axis=1, keepdims=True)


def _gather(slot, aff, h, cap):
    b, e, t = slot.shape
    d = h.shape[2]
    row = pl.BlockSpec((None, None, 1, t), lambda bi, ei: (bi, ei, 0, 0))
    return pl.pallas_call(
        functools.partial(_gather_kernel, cap=cap),
        grid=(b, e),
        in_specs=[row, row, pl.BlockSpec((None, t, d), lambda bi, ei: (bi, 0, 0))],
        out_specs=[pl.BlockSpec((None, cap, d), lambda bi, ei: (ei, bi, 0)),
                   pl.BlockSpec((None, cap, 1), lambda bi, ei: (ei, bi, 0))],
        out_shape=[jax.ShapeDtypeStruct((e, b * cap, d), BF16), jax.ShapeDtypeStruct((e, b * cap, 1), F32)],
        compiler_params=_cp("parallel", "arbitrary"),
        name="moe_gather",
    )(slot.reshape(b, e, 1, t), aff.reshape(b, e, 1, t), h)


def _ffn_kernel(*refs, n_streams):
    xs = refs[:n_streams]
    gates = refs[n_streams:2 * n_streams]
    w1_ref, w3_ref, w2_ref = refs[2 * n_streams:2 * n_streams + 3]
    outs = refs[2 * n_streams + 3:3 * n_streams + 3]
    accs = refs[3 * n_streams + 3:]
    f = pl.program_id(1)
    w1, w3, w2 = w1_ref[...].astype(BF16), w3_ref[...].astype(BF16), w2_ref[...].astype(BF16)
    for x_ref, gate_ref, o_ref, acc_ref in zip(xs, gates, outs, accs):
        rows = x_ref.shape[0]
        rc = _tile(rows, 256, 8)
        for r0 in range(0, rows, rc):
            x = x_ref[r0:r0 + rc, :]
            a = _dot(x, w1)
            g = _dot(x, w3)
            part = _dot(((a / (1.0 + jnp.exp(-a))) * g).astype(BF16), w2)

            @pl.when(f == 0)
            def _():
                acc_ref[r0:r0 + rc, :] = part

            @pl.when(f > 0)
            def _():
                acc_ref[r0:r0 + rc, :] += part

        @pl.when(f == pl.num_programs(1) - 1)
        def _():
            o_ref[...] = (acc_ref[...] * gate_ref[...]).astype(o_ref.dtype)


def _ffn(xs_list, gate_list, w1, w3, w2, layer):
    ns = len(xs_list)
    e, _, d = xs_list[0].shape
    ff = w1.shape[-1]
    tf = _tile(ff, 256)
    x_specs = [pl.BlockSpec((None, x.shape[1], d), lambda ei, f: (ei, 0, 0)) for x in xs_list]
    g_specs = [pl.BlockSpec((None, x.shape[1], 1), lambda ei, f: (ei, 0, 0)) for x in xs_list]
    return pl.pallas_call(
        functools.partial(_ffn_kernel, n_streams=ns),
        grid=(e, ff // tf),
        in_specs=x_specs + g_specs + [
            pl.BlockSpec((None, None, d, tf), lambda ei, f: (layer, ei, 0, f)),
            pl.BlockSpec((None, None, d, tf), lambda ei, f: (layer, ei, 0, f)),
            pl.BlockSpec((None, None, tf, d), lambda ei, f: (layer, ei, f, 0)),
        ],
        out_specs=x_specs,
        out_shape=[jax.ShapeDtypeStruct(x.shape, BF16) for x in xs_list],
        scratch_shapes=[pltpu.VMEM((x.shape[1], d), F32) for x in xs_list],
        compiler_params=_cp("parallel", "arbitrary"),
        name="moe_ffn",
    )(*xs_list, *gate_list, w1, w3, w2)


def _combine_kernel(slot_ref, y_ref, x_ref, gate_ref, o_ref, pt_ref, *, cap):
    e = slot_ref.shape[1]
    tm = slot_ref.shape[0]

    @pl.when(pl.program_id(2) == 0)
    def _():
        slot = slot_ref[...]
        lane = lax.broadcasted_iota(I32, (tm, cap), 1)
        for ei in range(e):
            pt_ref[ei] = jnp.where(lane == slot[:, ei:ei + 1], 1.0, 0.0).astype(BF16)

    acc = jnp.zeros(o_ref.shape, F32)
    for ei in range(e):
        acc = acc + _dot(pt_ref[ei], y_ref[ei])
    o_ref[...] = x_ref[...] + gate_ref[...] * acc


def _combine(slot_te, ys, x, gate, cap):
    b, t, d = x.shape
    e = slot_te.shape[2]
    tm = _tile(t, 512, 8)
    tn = _tile(d, 512)
    return pl.pallas_call(
        functools.partial(_combine_kernel, cap=cap),
        grid=(b, t // tm, d // tn),
        in_specs=[
            pl.BlockSpec((None, tm, e), lambda bi, i, j: (bi, i, 0)),
            pl.BlockSpec((e, cap, tn), lambda bi, i, j: (0, bi, j)),
            pl.BlockSpec((None, tm, tn), lambda bi, i, j: (bi, i, j)),
            pl.BlockSpec((None, 1, tn), lambda bi, i, j: (bi, 0, j)),
        ],
        out_specs=pl.BlockSpec((None, tm, tn), lambda bi, i, j: (bi, i, j)),
        out_shape=jax.ShapeDtypeStruct((b, t, d), F32),
        scratch_shapes=[pltpu.VMEM((e, tm, cap), BF16)],
        compiler_params=_cp("parallel", "arbitrary", "arbitrary"),
        name="moe_combine",
    )(slot_te, ys, x, gate)


def kernel(x, c, ctx, c_ctx, ada_w, ada_b, norm1_g, norm2_g, w_in, da_lambda, da_subln_g, wb_sink, na_rpb, s5_a_re, s5_a_im, s5_log_step, s5_b_re, s5_b_im, s5_c_re, s5_c_im, s5_d, s5_glu_w, w_branch, w_out, w_router, w_e1, w_e3, w_e2, final_g):
    b, n, d = x.shape
    n_ctx = ctx.shape[1]
    depth = ada_w.shape[0]
    assert b + 1 <= ADA_ROWS and n % GRID_W == 0

    cs = jnp.zeros((ADA_ROWS, d), F32).at[:b].set(c).at[b].set(c_ctx)
    mods = _ada_mod(cs, ada_w, ada_b)
    rope_tab = _rope_tables(n)

    for l in range(depth):
        with_ctx = l < depth - 1
        mod_x = [mods[l, :b, k * d:(k + 1) * d].reshape(b, 1, d) for k in range(6)]
        mod_c = [jnp.broadcast_to(mods[l, b, k * d:(k + 1) * d], (b, 1, d)) for k in range(6)]

        hx = _norm_mod(x, norm1_g[l], mod_x[0], mod_x[1])
        hc = _norm_mod(ctx, norm1_g[l], mod_c[0], mod_c[1])
        zx = _mm(hx.reshape(b * n, d), w_in, l, 0, N_MIX_IN).reshape(b, n, N_MIX_IN)
        zc = _mm(hc.reshape(b * n_ctx, d), w_in, l, 0, N_MIX_IN).reshape(b, n_ctx, N_MIX_IN)
        qk = _rope(zx, rope_tab)

        lam_init = 0.8 - 0.6 * math.exp(-0.3 * l)
        lv = da_lambda[l].astype(F32)
        lam = jnp.exp(jnp.sum(lv[0] * lv[1])) - jnp.exp(jnp.sum(lv[2] * lv[3])) + lam_init
        bias_tab = _na_bias_table(na_rpb[l])
        s5_tab = _s5_tables(s5_a_re[l], s5_a_im[l], s5_log_step[l], s5_b_re[l], s5_b_im[l], s5_c_re[l], s5_c_im[l])

        ya_x = _diff_attn(qk, QK_DA_Q // LANE, zc, qk, QK_DA_K // LANE, zx, lam, da_subln_g[l], lam_init, True)
        yb_x = _win_attn(qk, QK_WB_Q // (WB_DIM * WB_HEADS // WB_KV_HEADS), zc, qk, zx, wb_sink[l], True)
        yc_x = _na_attn(zx, zc, zx, bias_tab, True)
        u_x, u_c = zx[:, :, OFF_S5:], zc[:, :, OFF_S5:]
        (yf_x, yr_x), (yf_c, yr_c) = _s5_scan(u_x, u_c, s5_tab)
        yd_x = _s5_out(yf_x, yr_x, u_x, s5_d[l], s5_glu_w[l])
        gx = _merge(hx.reshape(b * n, d), [t.reshape(b * n, -1) for t in (ya_x, yb_x, yc_x, yd_x)], w_in, w_branch, l)
        x = _mm_res(gx, w_out, l, x, mod_x[2])

        if with_ctx:
            ya_c = _diff_attn(zc, OFF_DA_Q // LANE, zc, None, 0, None, lam, da_subln_g[l], lam_init, False)
            yb_c = _win_attn(zc, OFF_WB_Q // (WB_DIM * WB_HEADS // WB_KV_HEADS), zc, None, None, wb_sink[l], False)
            yc_c = _na_attn(zc, zc, None, None, False)
            yd_c = _s5_out(yf_c, yr_c, u_c, s5_d[l], s5_glu_w[l])
            gc = _merge(hc.reshape(b * n_ctx, d), [t.reshape(b * n_ctx, -1) for t in (ya_c, yb_c, yc_c, yd_c)],
                        w_in, w_branch, l)
            ctx = _mm_res(gc, w_out, l, ctx, mod_c[2])

        w_router_t = jnp.transpose(w_router[l]).astype(F32)
        streams = [(x, mod_x)] + ([(ctx, mod_c)] if with_ctx else [])
        routed = []
        for s, mod in streams:
            t = s.shape[1]
            cap = EC_CAPACITY_FACTOR * t // N_EXPERTS
            h2, logits_t = _norm_router(s, norm2_g[l], mod[3], mod[4], w_router_t)
            slot, aff = _topk(logits_t, b, cap)
            xs, gate = _gather(slot, aff, h2, cap)
            routed.append((jnp.transpose(slot, (0, 2, 1)), xs, gate, cap))
        ys = _ffn([r[1] for r in routed], [r[2] for r in routed], w_e1, w_e3, w_e2, l)
        x = _combine(routed[0][0], ys[0], x, mod_x[5], routed[0][3])
        if with_ctx:
            ctx = _combine(routed[1][0], ys[1], ctx, mod_c[5], routed[1][3])

    return _final_norm(x, final_g)
```

```python
import functools
import math

import jax
import jax.numpy as jnp
from jax import lax
from jax.experimental import pallas as pl
from jax.experimental.pallas import tpu as pltpu

F32 = jnp.float32
BF16 = jnp.bfloat16
I32 = jnp.int32

GRID_W = 64
EPS = 1e-6
NEG_INF = -1e30
ROPE_BASE = 10000.0

DA_HEADS = 4
DA_QK_DIM = 64
DA_V_DIM = 2 * DA_QK_DIM
WB_HEADS = 8
WB_KV_HEADS = 2
WB_DIM = 64
WB_WINDOW = 128
WB_BLOCK = 128
NA_HEADS = 8
NA_DIM = 64
NA_ROWS = 8
NA_COLS = 16
S5_GROUPS = 32
S5_GROUP_CH = 16
S5_STATE = 64
S5_WIDTH = S5_GROUPS * S5_GROUP_CH
S5_CHUNK = 8
N_BRANCH = 4
BRANCH_WIDTH = 512
N_EXPERTS = 16
EC_CAPACITY_FACTOR = 2

IN_WIDTHS = (
    2 * DA_HEADS * DA_QK_DIM, 2 * DA_HEADS * DA_QK_DIM, DA_HEADS * DA_V_DIM,
    WB_HEADS * WB_DIM, WB_KV_HEADS * WB_DIM, WB_KV_HEADS * WB_DIM,
    NA_HEADS * NA_DIM, NA_HEADS * NA_DIM, NA_HEADS * NA_DIM,
    S5_WIDTH,
)
N_MIX_IN = sum(IN_WIDTHS)
_OFFS = [0]
for _w in IN_WIDTHS:
    _OFFS.append(_OFFS[-1] + _w)
(OFF_DA_Q, OFF_DA_K, OFF_DA_V, OFF_WB_Q, OFF_WB_K, OFF_WB_V, OFF_NA_Q, OFF_NA_K, OFF_NA_V, OFF_S5, _) = _OFFS

LANE = 128
ADA_ROWS = 8
VMEM_LIMIT = 56 * 1024 * 1024

NN = (((1,), (0,)), ((), ()))
NT = (((1,), (1,)), ((), ()))


def _cp(*sem):
    return pltpu.CompilerParams(dimension_semantics=sem, vmem_limit_bytes=VMEM_LIMIT)


def _tile(n, pref, mult=LANE):
    if n <= pref:
        return n
    t = (pref // mult) * mult
    while t >= mult:
        if n % t == 0:
            return t
        t -= mult
    return n


def _split(a):
    hi = a.astype(BF16)
    lo = (a - hi.astype(F32)).astype(BF16)
    return hi, lo


def _dot(a, b, dims=NN):
    return lax.dot_general(a, b, dims, preferred_element_type=F32)


def _lane_group(lane, width):
    return lax.shift_right_logical(lane, int(math.log2(width)))


def _count(mask, axis):
    return jnp.sum(jnp.where(mask, 1.0, 0.0), axis=axis, keepdims=True)


def _dot3(a, b, dims=NN):
    ah, al = _split(a)
    bh, bl = _split(b)
    return _dot(ah, bh, dims) + (_dot(ah, bl, dims) + _dot(al, bh, dims))


def _ada_kernel(c_ref, w_ref, b_ref, o_ref):
    c = c_ref[...]
    s = c / (1.0 + jnp.exp(-c))
    o_ref[...] = _dot3(s, w_ref[...]) + b_ref[...]


def _ada_mod(cs, ada_w, ada_b):
    depth, d, n6 = ada_w.shape
    tn = _tile(n6, 512)
    return pl.pallas_call(
        _ada_kernel,
        grid=(depth, n6 // tn),
        in_specs=[
            pl.BlockSpec((ADA_ROWS, d), lambda l, j: (0, 0)),
            pl.BlockSpec((None, d, tn), lambda l, j: (l, 0, j)),
            pl.BlockSpec((None, 1, tn), lambda l, j: (l, 0, j)),
        ],
        out_specs=pl.BlockSpec((None, ADA_ROWS, tn), lambda l, j: (l, 0, j)),
        out_shape=jax.ShapeDtypeStruct((depth, ADA_ROWS, n6), F32),
        compiler_params=_cp("arbitrary", "arbitrary"),
        name="ada_mod",
    )(cs, ada_w, ada_b.reshape(depth, 1, n6))


def _norm_mod_rows(x, g, sh, sc):
    y = x * lax.rsqrt(jnp.mean(x * x, axis=-1, keepdims=True) + EPS) * g
    return y * (1.0 + sc) + sh


def _norm_mod_kernel(x_ref, g_ref, sh_ref, sc_ref, o_ref):
    o_ref[...] = _norm_mod_rows(x_ref[...], g_ref[...], sh_ref[...], sc_ref[...]).astype(o_ref.dtype)


def _norm_mod(x, g, sh, sc):
    b, t, d = x.shape
    tr = _tile(t, 256, 8)
    return pl.pallas_call(
        _norm_mod_kernel,
        grid=(b, t // tr),
        in_specs=[
            pl.BlockSpec((None, tr, d), lambda bi, i: (bi, i, 0)),
            pl.BlockSpec((1, d), lambda bi, i: (0, 0)),
            pl.BlockSpec((None, 1, d), lambda bi, i: (bi, 0, 0)),
            pl.BlockSpec((None, 1, d), lambda bi, i: (bi, 0, 0)),
        ],
        out_specs=pl.BlockSpec((None, tr, d), lambda bi, i: (bi, i, 0)),
        out_shape=jax.ShapeDtypeStruct((b, t, d), BF16),
        compiler_params=_cp("parallel", "parallel"),
        name="norm_mod",
    )(x, g.reshape(1, d), sh, sc)


def _norm_router_kernel(x_ref, g_ref, sh_ref, sc_ref, wrt_ref, h_ref, lg_ref):
    h = _norm_mod_rows(x_ref[...], g_ref[...], sh_ref[...], sc_ref[...])
    h_ref[...] = h.astype(h_ref.dtype)
    lg_ref[...] = _dot3(wrt_ref[...], h, NT)


def _norm_router(x, g, sh, sc, w_router_t):
    b, t, d = x.shape
    e = w_router_t.shape[0]
    tr = _tile(t, 256)
    nt = t // tr
    return pl.pallas_call(
        _norm_router_kernel,
        grid=(b, nt),
        in_specs=[
            pl.BlockSpec((None, tr, d), lambda bi, i: (bi, i, 0)),
            pl.BlockSpec((1, d), lambda bi, i: (0, 0)),
            pl.BlockSpec((None, 1, d), lambda bi, i: (bi, 0, 0)),
            pl.BlockSpec((None, 1, d), lambda bi, i: (bi, 0, 0)),
            pl.BlockSpec((e, d), lambda bi, i: (0, 0)),
        ],
        out_specs=[
            pl.BlockSpec((None, tr, d), lambda bi, i: (bi, i, 0)),
            pl.BlockSpec((e, tr), lambda bi, i: (0, bi * nt + i)),
        ],
        out_shape=[jax.ShapeDtypeStruct((b, t, d), BF16), jax.ShapeDtypeStruct((e, b * t), F32)],
        compiler_params=_cp("parallel", "parallel"),
        name="norm_router",
    )(x, g.reshape(1, d), sh, sc, w_router_t)


def _final_norm_kernel(x_ref, g_ref, o_ref):
    x = x_ref[...]
    o_ref[...] = x * lax.rsqrt(jnp.mean(x * x, axis=-1, keepdims=True) + EPS) * g_ref[...]


def _final_norm(x, g):
    b, t, d = x.shape
    tr = _tile(t, 256, 8)
    return pl.pallas_call(
        _final_norm_kernel,
        grid=(b, t // tr),
        in_specs=[pl.BlockSpec((None, tr, d), lambda bi, i: (bi, i, 0)), pl.BlockSpec((1, d), lambda bi, i: (0, 0))],
        out_specs=pl.BlockSpec((None, tr, d), lambda bi, i: (bi, i, 0)),
        out_shape=jax.ShapeDtypeStruct((b, t, d), F32),
        compiler_params=_cp("parallel", "parallel"),
        name="final_norm",
    )(x, g.reshape(1, d))


def _mm_kernel(a_ref, w_ref, o_ref, wb_ref):
    @pl.when(pl.program_id(1) == 0)
    def _():
        wb_ref[...] = w_ref[...].astype(BF16)

    o_ref[...] = _dot(a_ref[...], wb_ref[...]).astype(o_ref.dtype)


def _mm(a, w, layer, col0, ncols, out_dtype=F32):
    m, k = a.shape
    tn = 256
    assert ncols % tn == 0 and col0 % tn == 0
    tm = _tile(m, 1024, 8)
    j0 = col0 // tn
    return pl.pallas_call(
        _mm_kernel,
        grid=(ncols // tn, m // tm),
        in_specs=[
            pl.BlockSpec((tm, k), lambda j, i: (i, 0)),
            pl.BlockSpec((None, k, tn), lambda j, i: (layer, 0, j0 + j)),
        ],
        out_specs=pl.BlockSpec((tm, tn), lambda j, i: (i, j)),
        out_shape=jax.ShapeDtypeStruct((m, ncols), out_dtype),
        scratch_shapes=[pltpu.VMEM((k, tn), BF16)],
        compiler_params=_cp("arbitrary", "arbitrary"),
        name="mm_in",
    )(a, w)


def _mm_res_kernel(a_ref, w_ref, x_ref, gate_ref, o_ref, wb_ref):
    @pl.when(pl.program_id(1) == 0)
    def _():
        wb_ref[...] = w_ref[...].astype(BF16)

    o_ref[...] = x_ref[...] + gate_ref[...] * _dot(a_ref[...], wb_ref[...])


def _mm_res(a, w, layer, x, gate):
    b, t, n = x.shape
    m, k = a.shape
    tn = _tile(n, 512)
    tm = _tile(t, 1024, 8)
    per_b = t // tm
    out = pl.pallas_call(
        _mm_res_kernel,
        grid=(n // tn, m // tm),
        in_specs=[
            pl.BlockSpec((tm, k), lambda j, i: (i, 0)),
            pl.BlockSpec((None, k, tn), lambda j, i: (layer, 0, j)),
            pl.BlockSpec((tm, tn), lambda j, i: (i, j)),
            pl.BlockSpec((None, 1, tn), lambda j, i: (i // per_b, 0, j)),
        ],
        out_specs=pl.BlockSpec((tm, tn), lambda j, i: (i, j)),
        out_shape=jax.ShapeDtypeStruct((m, n), F32),
        scratch_shapes=[pltpu.VMEM((k, tn), BF16)],
        compiler_params=_cp("arbitrary", "arbitrary"),
        name="mm_out_res",
    )(a, w, x.reshape(m, n), gate)
    return out.reshape(b, t, n)


def _rope_kernel(z_ref, c_ref, sa_ref, sb_ref, o_ref):
    x = z_ref[...]
    o_ref[...] = x * c_ref[...] + pltpu.roll(x, LANE - 16, 1) * sa_ref[...] + pltpu.roll(x, 16, 1) * sb_ref[...]


def _rope_tables(n):
    pos = jnp.arange(n)
    rows, cols = (pos // GRID_W).astype(F32), (pos % GRID_W).astype(F32)
    half = DA_QK_DIM // 2
    inv_freq = jnp.power(ROPE_BASE, -jnp.arange(0, half, 2, dtype=F32) / half)
    ang_r, ang_c = rows[:, None] * inv_freq[None, :], cols[:, None] * inv_freq[None, :]
    cos64 = jnp.concatenate([jnp.cos(ang_r)] * 2 + [jnp.cos(ang_c)] * 2, axis=-1)
    sin64 = jnp.concatenate([jnp.sin(ang_r)] * 2 + [jnp.sin(ang_c)] * 2, axis=-1)
    cos, sin = jnp.tile(cos64, (1, 2)), jnp.tile(sin64, (1, 2))
    first = (jnp.arange(LANE) % 32) < 16
    return cos, jnp.where(first, -sin, 0.0), jnp.where(first, 0.0, sin)


ROPE_COLS = 2 * IN_WIDTHS[0] + IN_WIDTHS[3] + IN_WIDTHS[4]
QK_DA_Q, QK_DA_K, QK_WB_Q, QK_WB_K = 0, IN_WIDTHS[0], 2 * IN_WIDTHS[0], 2 * IN_WIDTHS[0] + IN_WIDTHS[3]


def _rope(zx, tables):
    b, n, _ = zx.shape
    n_da = 2 * IN_WIDTHS[0] // LANE
    skip = (OFF_WB_Q - 2 * IN_WIDTHS[0]) // LANE
    cos, sa, sb = tables
    tab = pl.BlockSpec((n, LANE), lambda bi, j: (0, 0))
    return pl.pallas_call(
        _rope_kernel,
        grid=(b, ROPE_COLS // LANE),
        in_specs=[pl.BlockSpec((None, n, LANE), lambda bi, j: (bi, 0, jnp.where(j < n_da, j, j + skip))), tab, tab, tab],
        out_specs=pl.BlockSpec((None, n, LANE), lambda bi, j: (bi, 0, j)),
        out_shape=jax.ShapeDtypeStruct((b, n, ROPE_COLS), F32),
        compiler_params=_cp("parallel", "parallel"),
        name="rope",
    )(zx, cos, sa, sb)


def _diff_attn_kernel(*refs, has_x, post_scale):
    if has_x:
        lam_ref, g_ref, q_ref, kc_ref, vc_ref, kx_ref, vx_ref, o_ref = refs
    else:
        lam_ref, g_ref, q_ref, kc_ref, vc_ref, o_ref = refs
    q = q_ref[...] * (DA_QK_DIM ** -0.5)
    lane = lax.broadcasted_iota(I32, q.shape, 1)
    kc, vc = kc_ref[...].astype(BF16), vc_ref[...].astype(BF16)
    if has_x:
        kx, vx = kx_ref[...].astype(BF16), vx_ref[...].astype(BF16)

    def attend(qm):
        sc = _dot(qm, kc, NT)
        m = jnp.max(sc, axis=-1, keepdims=True)
        if has_x:
            sx = _dot(qm, kx, NT)
            m = jnp.maximum(m, jnp.max(sx, axis=-1, keepdims=True))
        pc = jnp.exp(sc - m)
        l = jnp.sum(pc, axis=-1, keepdims=True)
        o = _dot(pc.astype(BF16), vc)
        if has_x:
            px = jnp.exp(sx - m)
            l = l + jnp.sum(px, axis=-1, keepdims=True)
            o = o + _dot(px.astype(BF16), vx)
        return o / l

    o = attend(jnp.where(lane < DA_QK_DIM, q, 0.0).astype(BF16)) \
        - lam_ref[...] * attend(jnp.where(lane >= DA_QK_DIM, q, 0.0).astype(BF16))
    y = o * lax.rsqrt(jnp.mean(o * o, axis=-1, keepdims=True) + EPS) * g_ref[...] * post_scale
    o_ref[...] = y.astype(o_ref.dtype)


def _diff_attn(q_arr, q_blk0, zc, kx_arr, kx_blk0, vx_arr, lam, subln_g, lam_init, has_x):
    b, tq_all, _ = q_arr.shape
    c = zc.shape[1]
    tq = _tile(tq_all, 256, 8)
    kc_blk0, vc_blk0 = OFF_DA_K // LANE, OFF_DA_V // LANE
    in_specs = [
        pl.BlockSpec((1, 1), lambda bi, h, i: (0, 0)),
        pl.BlockSpec((1, DA_V_DIM), lambda bi, h, i: (0, 0)),
        pl.BlockSpec((None, tq, LANE), lambda bi, h, i: (bi, i, q_blk0 + h)),
        pl.BlockSpec((None, c, LANE), lambda bi, h, i: (bi, 0, kc_blk0 + h)),
        pl.BlockSpec((None, c, LANE), lambda bi, h, i: (bi, 0, vc_blk0 + h)),
    ]
    args = [lam.reshape(1, 1), subln_g.reshape(1, DA_V_DIM), q_arr, zc, zc]
    if has_x:
        n = kx_arr.shape[1]
        in_specs += [
            pl.BlockSpec((None, n, LANE), lambda bi, h, i: (bi, 0, kx_blk0 + h)),
            pl.BlockSpec((None, n, LANE), lambda bi, h, i: (bi, 0, vc_blk0 + h)),
        ]
        args += [kx_arr, vx_arr]
    return pl.pallas_call(
        functools.partial(_diff_attn_kernel, has_x=has_x, post_scale=1.0 - lam_init),
        grid=(b, DA_HEADS, tq_all // tq),
        in_specs=in_specs,
        out_specs=pl.BlockSpec((None, tq, LANE), lambda bi, h, i: (bi, i, h)),
        out_shape=jax.ShapeDtypeStruct((b, tq_all, DA_HEADS * DA_V_DIM), BF16),
        compiler_params=_cp("parallel", "parallel", "arbitrary"),
        name="diff_attn_x" if has_x else "diff_attn_c",
    )(*args)


def _group_cols(x, g):
    lane = lax.broadcasted_iota(I32, x.shape, 1)
    own = jnp.where(_lane_group(lane, WB_DIM) == g, x, pltpu.roll(x, WB_DIM, 1))
    return jnp.concatenate([own, own], axis=1).astype(BF16)


def _win_attn_kernel(*refs, has_x, n_tok):
    if has_x:
        sink_ref, q_ref, kc_ref, vc_ref, kp_ref, k0_ref, kn_ref, vp_ref, v0_ref, vn_ref, o_ref = refs
    else:
        sink_ref, q_ref, kc_ref, vc_ref, o_ref = refs
    g, i = pl.program_id(1), pl.program_id(2)
    rep = WB_HEADS // WB_KV_HEADS
    q = q_ref[...] * (WB_DIM ** -0.5)
    blk = q.shape[0]
    lane = lax.broadcasted_iota(I32, q.shape, 1)
    kc, vc = _group_cols(kc_ref[...], g), _group_cols(vc_ref[...], g)
    if has_x:
        kb = _group_cols(jnp.concatenate([kp_ref[...], k0_ref[...], kn_ref[...]], axis=0), g)
        vb = _group_cols(jnp.concatenate([vp_ref[...], v0_ref[...], vn_ref[...]], axis=0), g)
        q_pos = i * blk + lax.broadcasted_iota(I32, (blk, 3 * blk), 0)
        k_pos = (i - 1) * blk + lax.broadcasted_iota(I32, (blk, 3 * blk), 1)
        valid = (jnp.abs(q_pos - k_pos) <= WB_WINDOW) & (k_pos >= 0) & (k_pos < n_tok)
    acc = jnp.zeros(q.shape, F32)
    for r in range(rep):
        mine = _lane_group(lane, WB_DIM) == r
        qr = jnp.where(mine, q, 0.0).astype(BF16)
        sink = sink_ref[:, r * WB_DIM:r * WB_DIM + 1]
        sc = _dot(qr, kc, NT)
        m = jnp.maximum(jnp.max(sc, axis=-1, keepdims=True), sink)
        if has_x:
            sb = jnp.where(valid, _dot(qr, kb, NT), NEG_INF)
            m = jnp.maximum(m, jnp.max(sb, axis=-1, keepdims=True))
        pc = jnp.exp(sc - m)
        l = jnp.sum(pc, axis=-1, keepdims=True) + jnp.exp(sink - m)
        o = _dot(pc.astype(BF16), vc)
        if has_x:
            pb = jnp.exp(sb - m)
            l = l + jnp.sum(pb, axis=-1, keepdims=True)
            o = o + _dot(pb.astype(BF16), vb)
        acc = acc + jnp.where(mine, o / l, 0.0)
    o_ref[...] = acc.astype(o_ref.dtype)


def _win_attn(q_arr, q_blk0, zc, qk, zx, sink, has_x):
    b, tq_all, _ = q_arr.shape
    c = zc.shape[1]
    blk = WB_BLOCK
    nb = tq_all // blk
    rep = WB_HEADS // WB_KV_HEADS
    gw = rep * WB_DIM
    sink_arr = jnp.repeat(sink.astype(F32).reshape(WB_KV_HEADS, 1, rep), WB_DIM, axis=-1)
    kc_blk, vc_blk = OFF_WB_K // LANE, OFF_WB_V // LANE
    in_specs = [
        pl.BlockSpec((None, 1, gw), lambda bi, g, i: (g, 0, 0)),
        pl.BlockSpec((None, blk, gw), lambda bi, g, i: (bi, i, q_blk0 + g)),
        pl.BlockSpec((None, c, LANE), lambda bi, g, i: (bi, 0, kc_blk)),
        pl.BlockSpec((None, c, LANE), lambda bi, g, i: (bi, 0, vc_blk)),
    ]
    args = [sink_arr, q_arr, zc, zc]
    n_tok = 0
    if has_x:
        n_tok = qk.shape[1]
        kx_blk = QK_WB_K // LANE
        prev = lambda i: jnp.maximum(i - 1, 0)
        nxt = lambda i: jnp.minimum(i + 1, nb - 1)
        for arr, cb in ((qk, kx_blk), (zx, vc_blk)):
            in_specs += [
                pl.BlockSpec((None, blk, LANE), lambda bi, g, i, cb=cb: (bi, prev(i), cb)),
                pl.BlockSpec((None, blk, LANE), lambda bi, g, i, cb=cb: (bi, i, cb)),
                pl.BlockSpec((None, blk, LANE), lambda bi, g, i, cb=cb: (bi, nxt(i), cb)),
            ]
            args += [arr, arr, arr]
    return pl.pallas_call(
        functools.partial(_win_attn_kernel, has_x=has_x, n_tok=n_tok),
        grid=(b, WB_KV_HEADS, nb),
        in_specs=in_specs,
        out_specs=pl.BlockSpec((None, blk, gw), lambda bi, g, i: (bi, i, g)),
        out_shape=jax.ShapeDtypeStruct((b, tq_all, WB_HEADS * WB_DIM), BF16),
        compiler_params=_cp("parallel", "parallel", "arbitrary"),
        name="win_attn_x" if has_x else "win_attn_c",
    )(*args)


def _na_kernel(*refs, has_x, n_rows):
    if has_x:
        q_ref, kc_ref, vc_ref, k_ref, v_ref, bias_ref, o_ref = refs
    else:
        q_ref, kc_ref, vc_ref, o_ref = refs
    q = q_ref[...] * (NA_DIM ** -0.5)
    lane = lax.broadcasted_iota(I32, q.shape, 1)
    kc, vc = kc_ref[...].astype(BF16), vc_ref[...].astype(BF16)
    if has_x:
        r = pl.program_id(2)
        start = jnp.clip(r - NA_ROWS // 2, 0, n_rows - NA_ROWS)
        off = pl.multiple_of(start * GRID_W, GRID_W)
        nk = NA_ROWS * GRID_W
        k, v = k_ref[pl.ds(off, nk), :].astype(BF16), v_ref[pl.ds(off, nk), :].astype(BF16)
        col = lax.broadcasted_iota(I32, (GRID_W, nk), 0)
        w = lax.broadcasted_iota(I32, (GRID_W, nk), 1) & (GRID_W - 1)
        col_start = jnp.clip(col - NA_COLS // 2, 0, GRID_W - NA_COLS)
        valid = (w >= col_start) & (w < col_start + NA_COLS)
    acc = jnp.zeros(q.shape, F32)
    for hh in range(LANE // NA_DIM):
        mine = _lane_group(lane, NA_DIM) == hh
        qh = jnp.where(mine, q, 0.0).astype(BF16)
        sc = _dot(qh, kc, NT)
        m = jnp.max(sc, axis=-1, keepdims=True)
        if has_x:
            s = jnp.where(valid, _dot(qh, k, NT) + bias_ref[hh], NEG_INF)
            m = jnp.maximum(m, jnp.max(s, axis=-1, keepdims=True))
        pc = jnp.exp(sc - m)
        l = jnp.sum(pc, axis=-1, keepdims=True)
        o = _dot(pc.astype(BF16), vc)
        if has_x:
            p = jnp.exp(s - m)
            l = l + jnp.sum(p, axis=-1, keepdims=True)
            o = o + _dot(p.astype(BF16), v)
        acc = acc + jnp.where(mine, o / l, 0.0)
    o_ref[...] = acc.astype(o_ref.dtype)


def _na_bias_table(rpb):
    r = rpb.astype(F32)
    edge = GRID_W - NA_COLS
    ext = jnp.concatenate([jnp.repeat(r[..., :1], edge, -1), r, jnp.repeat(r[..., -1:], edge, -1)], axis=-1)
    by_col = jnp.stack([ext[..., GRID_W - 1 - c:2 * GRID_W - 1 - c] for c in range(GRID_W)], axis=-2)
    t = jnp.stack([by_col[:, NA_ROWS - 1 - oi:2 * NA_ROWS - 1 - oi] for oi in range(NA_ROWS)], axis=1)
    t = jnp.transpose(t, (0, 1, 3, 2, 4))
    return t.reshape(NA_HEADS, NA_ROWS, GRID_W, NA_ROWS * GRID_W)


def _na_attn(q_arr, zc, zx, bias_tab, has_x):
    b, tq_all, _ = q_arr.shape
    c = zc.shape[1]
    per = LANE // NA_DIM
    nj = NA_HEADS // per
    q_blk0, k_blk0, v_blk0 = OFF_NA_Q // LANE, OFF_NA_K // LANE, OFF_NA_V // LANE
    tq = GRID_W if has_x else _tile(tq_all, 256, 8)
    nq = tq_all // tq
    in_specs = [
        pl.BlockSpec((None, tq, LANE), lambda bi, j, r: (bi, r, q_blk0 + j)),
        pl.BlockSpec((None, c, LANE), lambda bi, j, r: (bi, 0, k_blk0 + j)),
        pl.BlockSpec((None, c, LANE), lambda bi, j, r: (bi, 0, v_blk0 + j)),
    ]
    args = [q_arr, zc, zc]
    if has_x:
        n = zx.shape[1]
        assert nq >= NA_ROWS
        oi = lambda r: r - jnp.clip(r - NA_ROWS // 2, 0, nq - NA_ROWS)
        in_specs += [
            pl.BlockSpec((None, n, LANE), lambda bi, j, r: (bi, 0, k_blk0 + j)),
            pl.BlockSpec((None, n, LANE), lambda bi, j, r: (bi, 0, v_blk0 + j)),
            pl.BlockSpec((per, None, GRID_W, NA_ROWS * GRID_W), lambda bi, j, r: (j, oi(r), 0, 0)),
        ]
        args += [zx, zx, bias_tab]
    return pl.pallas_call(
        functools.partial(_na_kernel, has_x=has_x, n_rows=nq),
        grid=(b, nj, nq),
        in_specs=in_specs,
        out_specs=pl.BlockSpec((None, tq, LANE), lambda bi, j, r: (bi, r, j)),
        out_shape=jax.ShapeDtypeStruct((b, tq_all, NA_HEADS * NA_DIM), BF16),
        compiler_params=_cp("parallel", "parallel", "arbitrary"),
        name="na_attn_x" if has_x else "na_attn_c",
    )(*args)


def _s5_kernel(*refs, ctx_out):
    if ctx_out:
        (uc_ref, ux_ref, d_ref, b1_ref, b2_ref, a1_ref, a2_ref, a3_ref, cb_ref, yc_ref, yx_ref,
         kbig, bb1, bb2, xc, xx, s1c, s2c, s1x, s2x, hc, hx) = refs
    else:
        (uc_ref, ux_ref, d_ref, b1_ref, b2_ref, a1_ref, a2_ref, a3_ref, cb_ref, yx_ref,
         kbig, bb1, bb2, xc, xx, s1c, s2c, s1x, s2x, hc, hx) = refs
        yc_ref = None
    dirn = pl.program_id(1)
    ell, hch = S5_CHUNK, S5_GROUP_CH
    gq = LANE // hch
    nb = ux_ref.shape[0]
    streams = ((uc_ref, xc, s1c, s2c, hc, yc_ref, uc_ref.shape[1] // ell),
               (ux_ref, xx, s1x, s2x, hx, yx_ref, ux_ref.shape[1] // ell))

    for i in range(ell):
        for j in range(ell):
            kbig[i * LANE:(i + 1) * LANE, j * LANE:(j + 1) * LANE] = d_ref[j - i + ell - 1]
    bb1[...] = jnp.zeros(bb1.shape, BF16)
    bb2[...] = jnp.zeros(bb2.shape, BF16)
    for i in range(ell):
        for gl in range(gq):
            r0 = i * LANE + gl * hch
            bb1[r0:r0 + hch, gl * LANE:(gl + 1) * LANE] = b1_ref[gl, i * hch:(i + 1) * hch, :].astype(BF16)
            bb2[r0:r0 + hch, gl * LANE:(gl + 1) * LANE] = b2_ref[gl, i * hch:(i + 1) * hch, :].astype(BF16)

    @pl.when(dirn == 0)
    def _():
        for u_ref, x_s, _, _, _, _, nc in streams:
            for b in range(nb):
                for i in range(ell):
                    x_s[b * nc:(b + 1) * nc, i * LANE:(i + 1) * LANE] = u_ref[b, pl.ds(i, nc, stride=ell), :].astype(BF16)

    for _, x_s, s1, s2, _, _, _ in streams:
        loc1 = _dot(x_s[...], bb1[...])
        loc2 = _dot(x_s[...], bb2[...])
        for k in range(gq):
            s1[k] = loc1[:, k * LANE:(k + 1) * LANE]
            s2[k] = loc2[:, k * LANE:(k + 1) * LANE]

    def lane_blocks(a_ref):
        return [jnp.broadcast_to(a_ref[:, k * LANE:(k + 1) * LANE], (nb, LANE)) for k in range(gq)]

    a1, a2, a3 = lane_blocks(a1_ref), lane_blocks(a2_ref), lane_blocks(a3_ref)
    zero = tuple(jnp.zeros((nb, LANE), F32) for _ in range(gq))
    carry = (zero, zero)
    for _, _, s1, s2, h, _, nc in streams:
        def step(t, vs, s1=s1, s2=s2, h=h, nc=nc):
            v1, v2 = vs
            c = jnp.where(dirn == 0, t, nc - 1 - t)
            rows = pl.ds(c, nb, stride=nc)
            n1, n2 = [], []
            for k in range(gq):
                h[k, rows, :] = v1[k]
                n1.append(a1[k] * v1[k] + a2[k] * v2[k] + s1[k, rows, :])
                n2.append(a1[k] * v2[k] + a3[k] * v1[k] + s2[k, rows, :])
            return tuple(n1), tuple(n2)

        carry = lax.fori_loop(0, nc, step, carry)

    for _, x_s, _, _, h, y_ref, nc in streams:
        if y_ref is None:
            continue
        h_all = jnp.concatenate([h[k] for k in range(gq)], axis=1).astype(BF16)
        y = _dot(x_s[...], kbig[...]) + _dot(h_all, cb_ref[...])

        @pl.when(dirn == 0)
        def _():
            for b in range(nb):
                for j in range(ell):
                    y_ref[b, pl.ds(j, nc, stride=ell), :] = y[b * nc:(b + 1) * nc, j * LANE:(j + 1) * LANE]

        @pl.when(dirn == 1)
        def _():
            for b in range(nb):
                for j in range(ell):
                    y_ref[b, pl.ds(j, nc, stride=ell), :] += y[b * nc:(b + 1) * nc, j * LANE:(j + 1) * LANE]


def _s5_tables(a_re, a_im, log_step, b_re, b_im, c_re, c_im):
    ell, hch, p = S5_CHUNK, S5_GROUP_CH, S5_STATE
    lam = lax.complex(a_re.astype(F32), a_im.astype(F32))
    lam_dt = lam * jnp.exp(log_step.astype(F32))[:, :, None]
    a_bar = jnp.exp(lam_dt)
    b_bar = ((a_bar - 1.0) / lam)[:, :, :, None] * lax.complex(b_re.astype(F32), b_im.astype(F32))
    c_mat = lax.complex(c_re.astype(F32), c_im.astype(F32))
    gq = LANE // hch
    nq = S5_GROUPS // gq
    hp = lax.Precision.HIGHEST
    steps = jnp.arange(ell + 1, dtype=F32)
    apow = jnp.exp(lam_dt[:, :, None, :] * steps[None, None, :, None])
    apow_re, apow_im = apow.real, apow.imag
    bt_re, bt_im = jnp.swapaxes(b_bar.real, 2, 3), jnp.swapaxes(b_bar.imag, 2, 3)
    c_re_, c_im_ = c_mat.real, c_mat.imag
    eye = jnp.eye(gq, dtype=F32)

    def cmul(ar, ai, br, bi):
        return ar * br - ai * bi, ar * bi + ai * br

    m_re, m_im = cmul(apow_re[:, :, :ell, None, :], apow_im[:, :, :ell, None, :], bt_re[:, :, None], bt_im[:, :, None])
    kk = (jnp.einsum('dgkip,dgop->dgkio', m_re, c_re_, precision=hp)
          - jnp.einsum('dgkip,dgop->dgkio', m_im, c_im_, precision=hp))
    zeros = jnp.zeros_like(kk[0, :, 1:])
    signed = jnp.stack([jnp.concatenate([zeros, kk[0]], axis=1),
                        jnp.concatenate([jnp.flip(kk[1], axis=1), zeros], axis=1)])
    signed = signed.reshape(2, nq, gq, 2 * ell - 1, hch, hch)
    dblk = jnp.einsum('dqgkio,gh->dqkgiho', signed, eye).reshape(2, nq, 2 * ell - 1, LANE, LANE).astype(BF16)
    i_idx = jnp.arange(ell)
    e_re = jnp.stack([apow_re[0][:, ell - 1 - i_idx], apow_re[1][:, i_idx]])
    e_im = jnp.stack([apow_im[0][:, ell - 1 - i_idx], apow_im[1][:, i_idx]])
    bp_re, bp_im = cmul(e_re[:, :, :, None, :], e_im[:, :, :, None, :], bt_re[:, :, None], bt_im[:, :, None])
    bp_re, bp_im = bp_re.reshape(2, S5_GROUPS, ell * hch, p), bp_im.reshape(2, S5_GROUPS, ell * hch, p)
    b1 = jnp.concatenate([bp_re, bp_im], axis=-1)
    b2 = jnp.concatenate([bp_im, bp_re], axis=-1)
    al_re, al_im = apow_re[:, :, ell, :], apow_im[:, :, ell, :]
    lanes = lambda u, v: jnp.concatenate([u, v], axis=-1).reshape(2, nq, 1, gq * 2 * p)
    a1, a2, a3 = lanes(al_re, al_re), lanes(-al_im, al_im), lanes(al_im, -al_im)
    f_re = jnp.stack([apow_re[0][:, 1 + i_idx], apow_re[1][:, ell - i_idx]])
    f_im = jnp.stack([apow_im[0][:, 1 + i_idx], apow_im[1][:, ell - i_idx]])
    g_re, g_im = cmul(c_re_[:, :, None], c_im_[:, :, None], f_re[:, :, :, None, :], f_im[:, :, :, None, :])
    cp = jnp.concatenate([jnp.transpose(g_re, (0, 1, 4, 2, 3)), -jnp.transpose(g_im, (0, 1, 4, 2, 3))], axis=2)
    cp = cp.reshape(2, nq, gq, 2 * p, ell, hch)
    cbig = jnp.einsum('dqgsjo,gh->dqgsjho', cp, eye).reshape(2, nq, gq * 2 * p, ell * LANE).astype(BF16)
    return dblk, b1, b2, a1, a2, a3, cbig


def _s5_scan(zx, zc, tables, ctx_out):
    b, n, _ = zx.shape
    c = zc.shape[1]
    ell, hch = S5_CHUNK, S5_GROUP_CH
    gq = LANE // hch
    nq = S5_GROUPS // gq
    p2 = 2 * S5_STATE
    ncx, ncc = n // ell, c // ell
    assert n % ell == 0 and c % ell == 0 and ncc % 16 == 0
    dblk, b1, b2, a1, a2, a3, cbig = tables
    u0 = OFF_S5 // LANE
    sw = gq * p2
    xw = ell * LANE
    a_spec = pl.BlockSpec((None, None, 1, sw), lambda q, d: (d, q, 0, 0))
    b_spec = pl.BlockSpec((None, gq, ell * hch, p2), lambda q, d: (d, q, 0, 0))
    y_specs = [pl.BlockSpec((b, n, LANE), lambda q, d: (0, 0, q))]
    y_shapes = [jax.ShapeDtypeStruct((b, n, S5_WIDTH), F32)]
    if ctx_out:
        y_specs = [pl.BlockSpec((b, c, LANE), lambda q, d: (0, 0, q))] + y_specs
        y_shapes = [jax.ShapeDtypeStruct((b, c, S5_WIDTH), F32)] + y_shapes
    out = pl.pallas_call(
        functools.partial(_s5_kernel, ctx_out=ctx_out),
        grid=(nq, 2),
        in_specs=[
            pl.BlockSpec((b, c, LANE), lambda q, d: (0, 0, u0 + q)),
            pl.BlockSpec((b, n, LANE), lambda q, d: (0, 0, u0 + q)),
            pl.BlockSpec((None, None, 2 * ell - 1, LANE, LANE), lambda q, d: (d, q, 0, 0, 0)),
            b_spec, b_spec, a_spec, a_spec, a_spec,
            pl.BlockSpec((None, None, sw, xw), lambda q, d: (d, q, 0, 0)),
        ],
        out_specs=y_specs,
        out_shape=y_shapes,
        scratch_shapes=[
            pltpu.VMEM((xw, xw), BF16), pltpu.VMEM((xw, sw), BF16), pltpu.VMEM((xw, sw), BF16),
            pltpu.VMEM((b * ncc, xw), BF16), pltpu.VMEM((b * ncx, xw), BF16),
            pltpu.VMEM((gq, b * ncc, p2), F32), pltpu.VMEM((gq, b * ncc, p2), F32),
            pltpu.VMEM((gq, b * ncx, p2), F32), pltpu.VMEM((gq, b * ncx, p2), F32),
            pltpu.VMEM((gq, b * ncc, p2), F32), pltpu.VMEM((gq, b * ncx, p2), F32),
        ],
        compiler_params=_cp("parallel", "arbitrary"),
        name="s5_scan",
    )(zc, zx, dblk, b1, b2, a1, a2, a3, cbig)
    return (out[0], out[1]) if ctx_out else (None, out[0])


def _s5_out_kernel(y_ref, u0_ref, u1_ref, u2_ref, u3_ref, d_ref, w_ref, o_ref):
    u = jnp.concatenate([u0_ref[...], u1_ref[...], u2_ref[...], u3_ref[...]], axis=1)
    g = jax.nn.gelu(y_ref[...] + d_ref[...] * u)
    o_ref[...] = (g * jax.nn.sigmoid(_dot(g.astype(BF16), w_ref[...].astype(BF16)))).astype(o_ref.dtype)


def _s5_out(y, z, d_skip, w_glu):
    b, t, w = y.shape
    tr = _tile(t, 512, 8)
    u0 = OFF_S5 // LANE
    row = pl.BlockSpec((None, tr, w), lambda bi, i: (bi, i, 0))
    u_specs = [pl.BlockSpec((None, tr, LANE), lambda bi, i, k=k: (bi, i, u0 + k)) for k in range(w // LANE)]
    return pl.pallas_call(
        _s5_out_kernel,
        grid=(b, t // tr),
        in_specs=[row] + u_specs + [pl.BlockSpec((1, w), lambda bi, i: (0, 0)), pl.BlockSpec((w, w), lambda bi, i: (0, 0))],
        out_specs=row,
        out_shape=jax.ShapeDtypeStruct((b, t, w), BF16),
        compiler_params=_cp("parallel", "parallel"),
        name="s5_out",
    )(y, z, z, z, z, d_skip.reshape(1, w).astype(F32), w_glu)


def _merge_kernel(h_ref, ya_ref, yb_ref, yc_ref, yd_ref, wg0_ref, wg1_ref, wg2_ref, wg3_ref, wb_ref, o_ref, wgs_ref, wbs_ref):
    @pl.when(pl.program_id(1) == 0)
    def _():
        for n, wg_ref in enumerate((wg0_ref, wg1_ref, wg2_ref, wg3_ref)):
            wgs_ref[n] = wg_ref[...].astype(BF16)
        wbs_ref[...] = wb_ref[...].astype(BF16)

    h = h_ref[...]
    acc = None
    for n, y_ref in enumerate((ya_ref, yb_ref, yc_ref, yd_ref)):
        term = jax.nn.sigmoid(_dot(h, wgs_ref[n])) * _dot(y_ref[...], wbs_ref[n])
        acc = term if acc is None else acc + term
    o_ref[...] = acc.astype(o_ref.dtype)


def _merge(h, branches, w_in, w_branch, layer):
    m, d = h.shape
    bw = branches[0].shape[1]
    tn = 256
    tm = _tile(m, 512, 8)
    g0 = N_MIX_IN // tn
    per = d // tn
    assert N_MIX_IN % tn == 0 and d % tn == 0
    wg_specs = [pl.BlockSpec((None, d, tn), lambda j, i, n=n: (layer, 0, g0 + n * per + j)) for n in range(N_BRANCH)]
    y_spec = pl.BlockSpec((tm, bw), lambda j, i: (i, 0))
    return pl.pallas_call(
        _merge_kernel,
        grid=(d // tn, m // tm),
        in_specs=[pl.BlockSpec((tm, d), lambda j, i: (i, 0)), y_spec, y_spec, y_spec, y_spec] + wg_specs
                 + [pl.BlockSpec((None, N_BRANCH, bw, tn), lambda j, i: (layer, 0, 0, j))],
        out_specs=pl.BlockSpec((tm, tn), lambda j, i: (i, j)),
        out_shape=jax.ShapeDtypeStruct((m, d), BF16),
        scratch_shapes=[pltpu.VMEM((N_BRANCH, d, tn), BF16), pltpu.VMEM((N_BRANCH, bw, tn), BF16)],
        compiler_params=_cp("arbitrary", "arbitrary"),
        name="merge",
    )(h, *branches, w_in, w_in, w_in, w_in, w_branch)


def _topk_kernel(lg_ref, slot_ref, aff_ref, tri_ref, *, cap):
    e, t = lg_ref.shape

    @pl.when(pl.program_id(0) == 0)
    def _():
        rows = 256 if t % 256 == 0 else t
        for r0 in range(0, t, rows):
            ri = lax.broadcasted_iota(I32, (rows, t), 0) + r0
            ci = lax.broadcasted_iota(I32, (rows, t), 1)
            tri_ref[r0:r0 + rows, :] = jnp.where(ri < ci, 1.0, 0.0).astype(BF16)

    lg = lg_ref[...]
    ex = jnp.exp(lg - jnp.max(lg, axis=0, keepdims=True))
    aff = ex / jnp.sum(ex, axis=0, keepdims=True)
    aff_ref[...] = aff
    bits = pltpu.bitcast(aff, I32)

    def search(_, carry):
        lo, hi = carry
        mid = lo + lax.shift_right_logical(hi - lo + 1, 1)
        ok = _count(bits >= mid, 1) >= cap
        return jnp.where(ok, mid, lo), jnp.where(ok, hi, mid - 1)

    lo0 = jnp.zeros((e, 1), I32)
    hi0 = jnp.full((e, 1), 0x7F800000, I32)
    thr, _ = lax.fori_loop(0, 32, search, (lo0, hi0))
    gt = bits > thr
    eq = bits == thr
    need = cap - _count(gt, 1)
    tri = tri_ref[...]
    eq_before = _dot(jnp.where(eq, 1.0, 0.0).astype(BF16), tri)
    sel = gt | (eq & (eq_before < need))
    sel_before = _dot(jnp.where(sel, 1.0, 0.0).astype(BF16), tri)
    slot_ref[...] = jnp.where(sel, sel_before.astype(I32), -1)


def _topk(logits_t, b, cap):
    e, bt = logits_t.shape
    t = bt // b
    return pl.pallas_call(
        functools.partial(_topk_kernel, cap=cap),
        grid=(b,),
        in_specs=[pl.BlockSpec((e, t), lambda bi: (0, bi))],
        out_specs=[pl.BlockSpec((None, e, t), lambda bi: (bi, 0, 0)), pl.BlockSpec((None, e, t), lambda bi: (bi, 0, 0))],
        out_shape=[jax.ShapeDtypeStruct((b, e, t), I32), jax.ShapeDtypeStruct((b, e, t), F32)],
        scratch_shapes=[pltpu.VMEM((t, t), BF16)],
        compiler_params=_cp("arbitrary"),
        name="route_topk",
    )(logits_t)


def _gather_kernel(slot_ref, aff_ref, h_ref, xs_ref, gate_ref, *, cap):
    t = h_ref.shape[0]
    pick = lax.broadcasted_iota(I32, (cap, t), 0) == slot_ref[...]
    xs_ref[...] = _dot(jnp.where(pick, 1.0, 0.0).astype(BF16), h_ref[...]).astype(xs_ref.dtype)
    gate_ref[...] = jnp.sum(jnp.where(pick, aff_ref[...], 0.0), axis=1, keepdims=True)


def _gather(slot, aff, h, cap):
    b, e, t = slot.shape
    d = h.shape[2]
    row = pl.BlockSpec((None, None, 1, t), lambda bi, ei: (bi, ei, 0, 0))
    return pl.pallas_call(
        functools.partial(_gather_kernel, cap=cap),
        grid=(b, e),
        in_specs=[row, row, pl.BlockSpec((None, t, d), lambda bi, ei: (bi, 0, 0))],
        out_specs=[pl.BlockSpec((None, cap, d), lambda bi, ei: (ei, bi, 0)),
                   pl.BlockSpec((None, cap, 1), lambda bi, ei: (ei, bi, 0))],
        out_shape=[jax.ShapeDtypeStruct((e, b * cap, d), BF16), jax.ShapeDtypeStruct((e, b * cap, 1), F32)],
        compiler_params=_cp("parallel", "arbitrary"),
        name="moe_gather",
    )(slot.reshape(b, e, 1, t), aff.reshape(b, e, 1, t), h)


def _ffn_kernel(*refs, n_streams):
    xs = refs[:n_streams]
    gates = refs[n_streams:2 * n_streams]
    w1_ref, w3_ref, w2_ref = refs[2 * n_streams:2 * n_streams + 3]
    outs = refs[2 * n_streams + 3:3 * n_streams + 3]
    accs = refs[3 * n_streams + 3:]
    f = pl.program_id(1)
    w1, w3, w2 = w1_ref[...].astype(BF16), w3_ref[...].astype(BF16), w2_ref[...].astype(BF16)
    for x_ref, gate_ref, o_ref, acc_ref in zip(xs, gates, outs, accs):
        x = x_ref[...]
        a = _dot(x, w1)
        g = _dot(x, w3)
        mid = ((a / (1.0 + jnp.exp(-a))) * g).astype(BF16)
        d = w2.shape[1]
        nct = _tile(d, 512)
        for n0 in range(0, d, nct):
            part = _dot(mid, w2[:, n0:n0 + nct])

            @pl.when(f == 0)
            def _():
                acc_ref[:, n0:n0 + nct] = part

            @pl.when(f > 0)
            def _():
                acc_ref[:, n0:n0 + nct] += part

        @pl.when(f == pl.num_programs(1) - 1)
        def _():
            o_ref[...] = (acc_ref[...] * gate_ref[...]).astype(o_ref.dtype)


def _ffn(xs_list, gate_list, w1, w3, w2, layer):
    ns = len(xs_list)
    e, _, d = xs_list[0].shape
    ff = w1.shape[-1]
    tf = _tile(ff, 256)
    x_specs = [pl.BlockSpec((None, x.shape[1], d), lambda ei, f: (ei, 0, 0)) for x in xs_list]
    g_specs = [pl.BlockSpec((None, x.shape[1], 1), lambda ei, f: (ei, 0, 0)) for x in xs_list]
    return pl.pallas_call(
        functools.partial(_ffn_kernel, n_streams=ns),
        grid=(e, ff // tf),
        in_specs=x_specs + g_specs + [
            pl.BlockSpec((None, None, d, tf), lambda ei, f: (layer, ei, 0, f)),
            pl.BlockSpec((None, None, d, tf), lambda ei, f: (layer, ei, 0, f)),
            pl.BlockSpec((None, None, tf, d), lambda ei, f: (layer, ei, f, 0)),
        ],
        out_specs=x_specs,
        out_shape=[jax.ShapeDtypeStruct(x.shape, BF16) for x in xs_list],
        scratch_shapes=[pltpu.VMEM((x.shape[1], d), F32) for x in xs_list],
        compiler_params=_cp("parallel", "arbitrary"),
        name="moe_ffn",
    )(*xs_list, *gate_list, w1, w3, w2)


def _combine_kernel(slot_ref, y_ref, x_ref, gate_ref, o_ref, pt_ref, *, cap):
    e = slot_ref.shape[1]
    tm = slot_ref.shape[0]

    @pl.when(pl.program_id(2) == 0)
    def _():
        slot = slot_ref[...]
        lane = lax.broadcasted_iota(I32, (tm, cap), 1)
        for ei in range(e):
            pt_ref[ei] = jnp.where(lane == slot[:, ei:ei + 1], 1.0, 0.0).astype(BF16)

    acc = jnp.zeros(o_ref.shape, F32)
    for ei in range(e):
        acc = acc + _dot(pt_ref[ei], y_ref[ei])
    o_ref[...] = x_ref[...] + gate_ref[...] * acc


def _combine(slot_te, ys, x, gate, cap):
    b, t, d = x.shape
    e = slot_te.shape[2]
    tm = _tile(t, 512, 8)
    tn = _tile(d, 512)
    return pl.pallas_call(
        functools.partial(_combine_kernel, cap=cap),
        grid=(b, t // tm, d // tn),
        in_specs=[
            pl.BlockSpec((None, tm, e), lambda bi, i, j: (bi, i, 0)),
            pl.BlockSpec((e, cap, tn), lambda bi, i, j: (0, bi, j)),
            pl.BlockSpec((None, tm, tn), lambda bi, i, j: (bi, i, j)),
            pl.BlockSpec((None, 1, tn), lambda bi, i, j: (bi, 0, j)),
        ],
        out_specs=pl.BlockSpec((None, tm, tn), lambda bi, i, j: (bi, i, j)),
        out_shape=jax.ShapeDtypeStruct((b, t, d), F32),
        scratch_shapes=[pltpu.VMEM((e, tm, cap), BF16)],
        compiler_params=_cp("parallel", "arbitrary", "arbitrary"),
        name="moe_combine",
    )(slot_te, ys, x, gate)


def kernel(x, c, ctx, c_ctx, ada_w, ada_b, norm1_g, norm2_g, w_in, da_lambda, da_subln_g, wb_sink, na_rpb, s5_a_re, s5_a_im, s5_log_step, s5_b_re, s5_b_im, s5_c_re, s5_c_im, s5_d, s5_glu_w, w_branch, w_out, w_router, w_e1, w_e3, w_e2, final_g):
    b, n, d = x.shape
    n_ctx = ctx.shape[1]
    depth = ada_w.shape[0]
    assert b + 1 <= ADA_ROWS and n % GRID_W == 0

    cs = jnp.zeros((ADA_ROWS, d), F32).at[:b].set(c).at[b].set(c_ctx)
    mods = _ada_mod(cs, ada_w, ada_b)
    rope_tab = _rope_tables(n)

    for l in range(depth):
        with_ctx = l < depth - 1
        mod_x = [mods[l, :b, k * d:(k + 1) * d].reshape(b, 1, d) for k in range(6)]
        mod_c = [jnp.broadcast_to(mods[l, b, k * d:(k + 1) * d], (b, 1, d)) for k in range(6)]

        hx = _norm_mod(x, norm1_g[l], mod_x[0], mod_x[1])
        hc = _norm_mod(ctx, norm1_g[l], mod_c[0], mod_c[1])
        zx = _mm(hx.reshape(b * n, d), w_in, l, 0, N_MIX_IN).reshape(b, n, N_MIX_IN)
        zc = _mm(hc.reshape(b * n_ctx, d), w_in, l, 0, N_MIX_IN).reshape(b, n_ctx, N_MIX_IN)
        qk = _rope(zx, rope_tab)

        lam_init = 0.8 - 0.6 * math.exp(-0.3 * l)
        lv = da_lambda[l].astype(F32)
        lam = jnp.exp(jnp.sum(lv[0] * lv[1])) - jnp.exp(jnp.sum(lv[2] * lv[3])) + lam_init
        bias_tab = _na_bias_table(na_rpb[l])
        s5_tab = _s5_tables(s5_a_re[l], s5_a_im[l], s5_log_step[l], s5_b_re[l], s5_b_im[l], s5_c_re[l], s5_c_im[l])

        ya_x = _diff_attn(qk, QK_DA_Q // LANE, zc, qk, QK_DA_K // LANE, zx, lam, da_subln_g[l], lam_init, True)
        yb_x = _win_attn(qk, QK_WB_Q // (WB_DIM * WB_HEADS // WB_KV_HEADS), zc, qk, zx, wb_sink[l], True)
        yc_x = _na_attn(zx, zc, zx, bias_tab, True)
        ys_c, ys_x = _s5_scan(zx, zc, s5_tab, with_ctx)
        yd_x = _s5_out(ys_x, zx, s5_d[l], s5_glu_w[l])
        gx = _merge(hx.reshape(b * n, d), [t.reshape(b * n, -1) for t in (ya_x, yb_x, yc_x, yd_x)], w_in, w_branch, l)
        x = _mm_res(gx, w_out, l, x, mod_x[2])

        if with_ctx:
            ya_c = _diff_attn(zc, OFF_DA_Q // LANE, zc, None, 0, None, lam, da_subln_g[l], lam_init, False)
            yb_c = _win_attn(zc, OFF_WB_Q // (WB_DIM * WB_HEADS // WB_KV_HEADS), zc, None, None, wb_sink[l], False)
            yc_c = _na_attn(zc, zc, None, None, False)
            yd_c = _s5_out(ys_c, zc, s5_d[l], s5_glu_w[l])
            gc = _merge(hc.reshape(b * n_ctx, d), [t.reshape(b * n_ctx, -1) for t in (ya_c, yb_c, yc_c, yd_c)],
                        w_in, w_branch, l)
            ctx = _mm_res(gc, w_out, l, ctx, mod_c[2])

        w_router_t = jnp.transpose(w_router[l]).astype(F32)
        streams = [(x, mod_x)] + ([(ctx, mod_c)] if with_ctx else [])
        routed = []
        for s, mod in streams:
            t = s.shape[1]
            cap = EC_CAPACITY_FACTOR * t // N_EXPERTS
            h2, logits_t = _norm_router(s, norm2_g[l], mod[3], mod[4], w_router_t)
            slot, aff = _topk(logits_t, b, cap)
            xs, gate = _gather(slot, aff, h2, cap)
            routed.append((jnp.transpose(slot, (0, 2, 1)), xs, gate, cap))
        ys = _ffn([r[1] for r in routed], [r[2] for r in routed], w_e1, w_e3, w_e2, l)
        x = _combine(routed[0][0], ys[0], x, mod_x[5], routed[0][3])
        if with_ctx:
            ctx = _combine(routed[1][0], ys[1], ctx, mod_c[5], routed[1][3])

    return _final_norm(x, final_g)
```

```python
import functools
import math

import jax
import jax.numpy as jnp
from jax import lax
from jax.experimental import pallas as pl
from jax.experimental.pallas import tpu as pltpu

F32 = jnp.float32
BF16 = jnp.bfloat16
I32 = jnp.int32

GRID_W = 64
EPS = 1e-6
NEG_INF = -1e30
ROPE_BASE = 10000.0

DA_HEADS = 4
DA_QK_DIM = 64
DA_V_DIM = 2 * DA_QK_DIM
WB_HEADS = 8
WB_KV_HEADS = 2
WB_DIM = 64
WB_WINDOW = 128
WB_BLOCK = 128
NA_HEADS = 8
NA_DIM = 64
NA_ROWS = 8
NA_COLS = 16
S5_GROUPS = 32
S5_GROUP_CH = 16
S5_STATE = 64
S5_WIDTH = S5_GROUPS * S5_GROUP_CH
S5_CHUNK = 8
N_BRANCH = 4
BRANCH_WIDTH = 512
N_EXPERTS = 16
EC_CAPACITY_FACTOR = 2

IN_WIDTHS = (
    2 * DA_HEADS * DA_QK_DIM, 2 * DA_HEADS * DA_QK_DIM, DA_HEADS * DA_V_DIM,
    WB_HEADS * WB_DIM, WB_KV_HEADS * WB_DIM, WB_KV_HEADS * WB_DIM,
    NA_HEADS * NA_DIM, NA_HEADS * NA_DIM, NA_HEADS * NA_DIM,
    S5_WIDTH,
)
N_MIX_IN = sum(IN_WIDTHS)
_OFFS = [0]
for _w in IN_WIDTHS:
    _OFFS.append(_OFFS[-1] + _w)
(OFF_DA_Q, OFF_DA_K, OFF_DA_V, OFF_WB_Q, OFF_WB_K, OFF_WB_V, OFF_NA_Q, OFF_NA_K, OFF_NA_V, OFF_S5, _) = _OFFS

MIX_PROJ = (
    (OFF_DA_Q, OFF_WB_Q - OFF_DA_Q, 768, BF16),
    (OFF_WB_Q, OFF_NA_Q - OFF_WB_Q, 768, BF16),
    (OFF_NA_Q, OFF_S5 - OFF_NA_Q, 768, BF16),
    (OFF_S5, S5_WIDTH, 256, F32),
)

LANE = 128
ADA_ROWS = 8
VMEM_LIMIT = 56 * 1024 * 1024

NN = (((1,), (0,)), ((), ()))
NT = (((1,), (1,)), ((), ()))


def _cp(*sem):
    return pltpu.CompilerParams(dimension_semantics=sem, vmem_limit_bytes=VMEM_LIMIT)


def _tile(n, pref, mult=LANE):
    if n <= pref:
        return n
    t = (pref // mult) * mult
    while t >= mult:
        if n % t == 0:
            return t
        t -= mult
    return n


def _split(a):
    hi = a.astype(BF16)
    lo = (a - hi.astype(F32)).astype(BF16)
    return hi, lo


def _dot(a, b, dims=NN):
    return lax.dot_general(a, b, dims, preferred_element_type=F32)


def _lane_group(lane, width):
    return lax.shift_right_logical(lane, int(math.log2(width)))


def _count(mask, axis):
    return jnp.sum(jnp.where(mask, 1.0, 0.0), axis=axis, keepdims=True)


def _dot3(a, b, dims=NN):
    ah, al = _split(a)
    bh, bl = _split(b)
    return _dot(ah, bh, dims) + (_dot(ah, bl, dims) + _dot(al, bh, dims))


def _ada_kernel(c_ref, w_ref, b_ref, o_ref):
    c = c_ref[...]
    s = c / (1.0 + jnp.exp(-c))
    o_ref[...] = _dot3(s, w_ref[...]) + b_ref[...]


def _ada_mod(cs, ada_w, ada_b):
    depth, d, n6 = ada_w.shape
    tn = _tile(n6, 512)
    return pl.pallas_call(
        _ada_kernel,
        grid=(depth, n6 // tn),
        in_specs=[
            pl.BlockSpec((ADA_ROWS, d), lambda l, j: (0, 0)),
            pl.BlockSpec((None, d, tn), lambda l, j: (l, 0, j)),
            pl.BlockSpec((None, 1, tn), lambda l, j: (l, 0, j)),
        ],
        out_specs=pl.BlockSpec((None, ADA_ROWS, tn), lambda l, j: (l, 0, j)),
        out_shape=jax.ShapeDtypeStruct((depth, ADA_ROWS, n6), F32),
        compiler_params=_cp("arbitrary", "arbitrary"),
        name="ada_mod",
    )(cs, ada_w, ada_b.reshape(depth, 1, n6))


def _norm_mod_rows(x, g, sh, sc):
    y = x * lax.rsqrt(jnp.mean(x * x, axis=-1, keepdims=True) + EPS) * g
    return y * (1.0 + sc) + sh


def _norm_mod_kernel(x_ref, g_ref, sh_ref, sc_ref, o_ref):
    o_ref[...] = _norm_mod_rows(x_ref[...], g_ref[...], sh_ref[...], sc_ref[...]).astype(o_ref.dtype)


def _norm_mod(x, g, sh, sc):
    b, t, d = x.shape
    tr = _tile(t, 256, 8)
    return pl.pallas_call(
        _norm_mod_kernel,
        grid=(b, t // tr),
        in_specs=[
            pl.BlockSpec((None, tr, d), lambda bi, i: (bi, i, 0)),
            pl.BlockSpec((1, d), lambda bi, i: (0, 0)),
            pl.BlockSpec((None, 1, d), lambda bi, i: (bi, 0, 0)),
            pl.BlockSpec((None, 1, d), lambda bi, i: (bi, 0, 0)),
        ],
        out_specs=pl.BlockSpec((None, tr, d), lambda bi, i: (bi, i, 0)),
        out_shape=jax.ShapeDtypeStruct((b, t, d), BF16),
        compiler_params=_cp("parallel", "parallel"),
        name="norm_mod",
    )(x, g.reshape(1, d), sh, sc)


def _norm_router_kernel(x_ref, g_ref, sh_ref, sc_ref, wrt_ref, h_ref, lg_ref):
    h = _norm_mod_rows(x_ref[...], g_ref[...], sh_ref[...], sc_ref[...])
    h_ref[...] = h.astype(h_ref.dtype)
    lg_ref[...] = _dot3(wrt_ref[...], h, NT)


def _norm_router(x, g, sh, sc, w_router_t):
    b, t, d = x.shape
    e = w_router_t.shape[0]
    tr = _tile(t, 256)
    nt = t // tr
    return pl.pallas_call(
        _norm_router_kernel,
        grid=(b, nt),
        in_specs=[
            pl.BlockSpec((None, tr, d), lambda bi, i: (bi, i, 0)),
            pl.BlockSpec((1, d), lambda bi, i: (0, 0)),
            pl.BlockSpec((None, 1, d), lambda bi, i: (bi, 0, 0)),
            pl.BlockSpec((None, 1, d), lambda bi, i: (bi, 0, 0)),
            pl.BlockSpec((e, d), lambda bi, i: (0, 0)),
        ],
        out_specs=[
            pl.BlockSpec((None, tr, d), lambda bi, i: (bi, i, 0)),
            pl.BlockSpec((e, tr), lambda bi, i: (0, bi * nt + i)),
        ],
        out_shape=[jax.ShapeDtypeStruct((b, t, d), BF16), jax.ShapeDtypeStruct((e, b * t), F32)],
        compiler_params=_cp("parallel", "parallel"),
        name="norm_router",
    )(x, g.reshape(1, d), sh, sc, w_router_t)


def _final_norm_kernel(x_ref, g_ref, o_ref):
    x = x_ref[...]
    o_ref[...] = x * lax.rsqrt(jnp.mean(x * x, axis=-1, keepdims=True) + EPS) * g_ref[...]


def _final_norm(x, g):
    b, t, d = x.shape
    tr = _tile(t, 256, 8)
    return pl.pallas_call(
        _final_norm_kernel,
        grid=(b, t // tr),
        in_specs=[pl.BlockSpec((None, tr, d), lambda bi, i: (bi, i, 0)), pl.BlockSpec((1, d), lambda bi, i: (0, 0))],
        out_specs=pl.BlockSpec((None, tr, d), lambda bi, i: (bi, i, 0)),
        out_shape=jax.ShapeDtypeStruct((b, t, d), F32),
        compiler_params=_cp("parallel", "parallel"),
        name="final_norm",
    )(x, g.reshape(1, d))


def _mm_kernel(a_ref, w_ref, o_ref, wb_ref):
    @pl.when(pl.program_id(1) == 0)
    def _():
        wb_ref[...] = w_ref[...].astype(BF16)

    o_ref[...] = _dot(a_ref[...], wb_ref[...]).astype(o_ref.dtype)


def _mm(a, w, layer, col0, ncols, tn, out_dtype):
    m, k = a.shape
    assert ncols % tn == 0 and col0 % tn == 0
    tm = _tile(m, 1024, 8)
    j0 = col0 // tn
    return pl.pallas_call(
        _mm_kernel,
        grid=(ncols // tn, m // tm),
        in_specs=[
            pl.BlockSpec((tm, k), lambda j, i: (i, 0)),
            pl.BlockSpec((None, k, tn), lambda j, i: (layer, 0, j0 + j)),
        ],
        out_specs=pl.BlockSpec((tm, tn), lambda j, i: (i, j)),
        out_shape=jax.ShapeDtypeStruct((m, ncols), out_dtype),
        scratch_shapes=[pltpu.VMEM((k, tn), BF16)],
        compiler_params=_cp("arbitrary", "arbitrary"),
        name="mm_in",
    )(a, w)


def _mm_res_kernel(a_ref, w_ref, x_ref, gate_ref, o_ref, wb_ref):
    @pl.when(pl.program_id(1) == 0)
    def _():
        wb_ref[...] = w_ref[...].astype(BF16)

    o_ref[...] = x_ref[...] + gate_ref[...] * _dot(a_ref[...], wb_ref[...])


def _mm_res(a, w, layer, x, gate):
    b, t, n = x.shape
    m, k = a.shape
    tn = _tile(n, 512)
    tm = _tile(t, 1024, 8)
    per_b = t // tm
    out = pl.pallas_call(
        _mm_res_kernel,
        grid=(n // tn, m // tm),
        in_specs=[
            pl.BlockSpec((tm, k), lambda j, i: (i, 0)),
            pl.BlockSpec((None, k, tn), lambda j, i: (layer, 0, j)),
            pl.BlockSpec((tm, tn), lambda j, i: (i, j)),
            pl.BlockSpec((None, 1, tn), lambda j, i: (i // per_b, 0, j)),
        ],
        out_specs=pl.BlockSpec((tm, tn), lambda j, i: (i, j)),
        out_shape=jax.ShapeDtypeStruct((m, n), F32),
        scratch_shapes=[pltpu.VMEM((k, tn), BF16)],
        compiler_params=_cp("arbitrary", "arbitrary"),
        name="mm_out_res",
    )(a, w, x.reshape(m, n), gate)
    return out.reshape(b, t, n)


def _rope_kernel(z_ref, c_ref, sa_ref, sb_ref, o_ref):
    cos, sa, sb = c_ref[...], sa_ref[...], sb_ref[...]
    for j in range(o_ref.shape[1] // LANE):
        x = z_ref[:, j * LANE:(j + 1) * LANE].astype(F32)
        y = x * cos + pltpu.roll(x, LANE - 16, 1) * sa + pltpu.roll(x, 16, 1) * sb
        o_ref[:, j * LANE:(j + 1) * LANE] = y.astype(o_ref.dtype)


def _rope_tables(n):
    pos = jnp.arange(n)
    rows, cols = (pos // GRID_W).astype(F32), (pos % GRID_W).astype(F32)
    half = DA_QK_DIM // 2
    inv_freq = jnp.power(ROPE_BASE, -jnp.arange(0, half, 2, dtype=F32) / half)
    ang_r, ang_c = rows[:, None] * inv_freq[None, :], cols[:, None] * inv_freq[None, :]
    cos64 = jnp.concatenate([jnp.cos(ang_r)] * 2 + [jnp.cos(ang_c)] * 2, axis=-1)
    sin64 = jnp.concatenate([jnp.sin(ang_r)] * 2 + [jnp.sin(ang_c)] * 2, axis=-1)
    cos, sin = jnp.tile(cos64, (1, 2)), jnp.tile(sin64, (1, 2))
    first = (jnp.arange(LANE) % 32) < 16
    return cos, jnp.where(first, -sin, 0.0), jnp.where(first, 0.0, sin)


def _rope(z, ncols, tables):
    b, n, _ = z.shape
    cos, sa, sb = tables
    tr = _tile(n, 512, 8)
    tab = pl.BlockSpec((tr, LANE), lambda bi, i: (i, 0))
    return pl.pallas_call(
        _rope_kernel,
        grid=(b, n // tr),
        in_specs=[pl.BlockSpec((None, tr, ncols), lambda bi, i: (bi, i, 0)), tab, tab, tab],
        out_specs=pl.BlockSpec((None, tr, ncols), lambda bi, i: (bi, i, 0)),
        out_shape=jax.ShapeDtypeStruct((b, n, ncols), BF16),
        compiler_params=_cp("parallel", "parallel"),
        name="rope",
    )(z, cos, sa, sb)


def _diff_attn_kernel(*refs, has_x, post_scale):
    if has_x:
        lam_ref, g_ref, q_ref, kc_ref, vc_ref, kx_ref, vx_ref, o_ref = refs
    else:
        lam_ref, g_ref, q_ref, kc_ref, vc_ref, o_ref = refs
    q = q_ref[...].astype(F32) * (DA_QK_DIM ** -0.5)
    lane = lax.broadcasted_iota(I32, q.shape, 1)
    kc, vc = kc_ref[...], vc_ref[...]
    if has_x:
        kx, vx = kx_ref[...], vx_ref[...]

    def attend(qm):
        sc = _dot(qm, kc, NT)
        m = jnp.max(sc, axis=-1, keepdims=True)
        if has_x:
            sx = _dot(qm, kx, NT)
            m = jnp.maximum(m, jnp.max(sx, axis=-1, keepdims=True))
        pc = jnp.exp(sc - m)
        l = jnp.sum(pc, axis=-1, keepdims=True)
        o = _dot(pc.astype(BF16), vc)
        if has_x:
            px = jnp.exp(sx - m)
            l = l + jnp.sum(px, axis=-1, keepdims=True)
            o = o + _dot(px.astype(BF16), vx)
        return o / l

    o = attend(jnp.where(lane < DA_QK_DIM, q, 0.0).astype(BF16)) \
        - lam_ref[...] * attend(jnp.where(lane >= DA_QK_DIM, q, 0.0).astype(BF16))
    y = o * lax.rsqrt(jnp.mean(o * o, axis=-1, keepdims=True) + EPS) * g_ref[...] * post_scale
    o_ref[...] = y.astype(o_ref.dtype)


def _diff_attn(q_arr, zc, kx_arr, vx_arr, lam, subln_g, lam_init, has_x):
    b, tq_all, _ = q_arr.shape
    c = zc.shape[1]
    tq = _tile(tq_all, 512, 8)
    q_blk0, kc_blk0, vc_blk0 = 0, IN_WIDTHS[0] // LANE, 2 * IN_WIDTHS[0] // LANE
    kx_blk0 = kc_blk0
    in_specs = [
        pl.BlockSpec((1, 1), lambda bi, h, i: (0, 0)),
        pl.BlockSpec((1, DA_V_DIM), lambda bi, h, i: (0, 0)),
        pl.BlockSpec((None, tq, LANE), lambda bi, h, i: (bi, i, q_blk0 + h)),
        pl.BlockSpec((None, c, LANE), lambda bi, h, i: (bi, 0, kc_blk0 + h)),
        pl.BlockSpec((None, c, LANE), lambda bi, h, i: (bi, 0, vc_blk0 + h)),
    ]
    args = [lam.reshape(1, 1), subln_g.reshape(1, DA_V_DIM), q_arr, zc, zc]
    if has_x:
        n = kx_arr.shape[1]
        in_specs += [
            pl.BlockSpec((None, n, LANE), lambda bi, h, i: (bi, 0, kx_blk0 + h)),
            pl.BlockSpec((None, n, LANE), lambda bi, h, i: (bi, 0, vc_blk0 + h)),
        ]
        args += [kx_arr, vx_arr]
    return pl.pallas_call(
        functools.partial(_diff_attn_kernel, has_x=has_x, post_scale=1.0 - lam_init),
        grid=(b, DA_HEADS, tq_all // tq),
        in_specs=in_specs,
        out_specs=pl.BlockSpec((None, tq, LANE), lambda bi, h, i: (bi, i, h)),
        out_shape=jax.ShapeDtypeStruct((b, tq_all, DA_HEADS * DA_V_DIM), BF16),
        compiler_params=_cp("parallel", "parallel", "arbitrary"),
        name="diff_attn_x" if has_x else "diff_attn_c",
    )(*args)


def _group_cols(x, g):
    x = x.astype(F32)
    lane = lax.broadcasted_iota(I32, x.shape, 1)
    own = jnp.where(_lane_group(lane, WB_DIM) == g, x, pltpu.roll(x, WB_DIM, 1))
    return jnp.concatenate([own, own], axis=1).astype(BF16)


def _win_attn_kernel(*refs, has_x, n_tok):
    if has_x:
        sink_ref, q_ref, kc_ref, vc_ref, kp_ref, k0_ref, kn_ref, vp_ref, v0_ref, vn_ref, o_ref = refs
    else:
        sink_ref, q_ref, kc_ref, vc_ref, o_ref = refs
    i = pl.program_id(1)
    rep = WB_HEADS // WB_KV_HEADS
    gw = rep * WB_DIM
    q_all = q_ref[...].astype(F32) * (WB_DIM ** -0.5)
    blk = q_all.shape[0]
    lane = lax.broadcasted_iota(I32, (blk, gw), 1)
    kc_in, vc_in = kc_ref[...], vc_ref[...]
    if has_x:
        kb_in = jnp.concatenate([kp_ref[...], k0_ref[...], kn_ref[...]], axis=0)
        vb_in = jnp.concatenate([vp_ref[...], v0_ref[...], vn_ref[...]], axis=0)
        q_pos = i * blk + lax.broadcasted_iota(I32, (blk, 3 * blk), 0)
        k_pos = (i - 1) * blk + lax.broadcasted_iota(I32, (blk, 3 * blk), 1)
        valid = (jnp.abs(q_pos - k_pos) <= WB_WINDOW) & (k_pos >= 0) & (k_pos < n_tok)
    outs = []
    for g in range(WB_KV_HEADS):
        q = q_all[:, g * gw:(g + 1) * gw]
        kc, vc = _group_cols(kc_in, g), _group_cols(vc_in, g)
        if has_x:
            kb, vb = _group_cols(kb_in, g), _group_cols(vb_in, g)
        acc = jnp.zeros(q.shape, F32)
        for r in range(rep):
            mine = _lane_group(lane, WB_DIM) == r
            qr = jnp.where(mine, q, 0.0).astype(BF16)
            h0 = (g * rep + r) * WB_DIM
            sink = sink_ref[:, h0:h0 + 1]
            sc = _dot(qr, kc, NT)
            m = jnp.maximum(jnp.max(sc, axis=-1, keepdims=True), sink)
            if has_x:
                sb = jnp.where(valid, _dot(qr, kb, NT), NEG_INF)
                m = jnp.maximum(m, jnp.max(sb, axis=-1, keepdims=True))
            pc = jnp.exp(sc - m)
            l = jnp.sum(pc, axis=-1, keepdims=True) + jnp.exp(sink - m)
            o = _dot(pc.astype(BF16), vc)
            if has_x:
                pb = jnp.exp(sb - m)
                l = l + jnp.sum(pb, axis=-1, keepdims=True)
                o = o + _dot(pb.astype(BF16), vb)
            acc = acc + jnp.where(mine, o / l, 0.0)
        outs.append(acc)
    o_ref[...] = jnp.concatenate(outs, axis=1).astype(o_ref.dtype)


def _win_attn(q_arr, zc, kx_arr, vx_arr, sink, has_x):
    b, tq_all, _ = q_arr.shape
    c = zc.shape[1]
    blk = WB_BLOCK
    nb = tq_all // blk
    qw = WB_HEADS * WB_DIM
    sink_arr = jnp.repeat(sink.astype(F32), WB_DIM).reshape(1, qw)
    k_blk, v_blk = qw // LANE, qw // LANE + 1
    in_specs = [
        pl.BlockSpec((1, qw), lambda bi, i: (0, 0)),
        pl.BlockSpec((None, blk, qw), lambda bi, i: (bi, i, 0)),
        pl.BlockSpec((None, c, LANE), lambda bi, i: (bi, 0, k_blk)),
        pl.BlockSpec((None, c, LANE), lambda bi, i: (bi, 0, v_blk)),
    ]
    args = [sink_arr, q_arr, zc, zc]
    n_tok = 0
    if has_x:
        n_tok = kx_arr.shape[1]
        prev = lambda i: jnp.maximum(i - 1, 0)
        nxt = lambda i: jnp.minimum(i + 1, nb - 1)
        for arr, cb in ((kx_arr, k_blk), (vx_arr, v_blk)):
            in_specs += [
                pl.BlockSpec((None, blk, LANE), lambda bi, i, cb=cb: (bi, prev(i), cb)),
                pl.BlockSpec((None, blk, LANE), lambda bi, i, cb=cb: (bi, i, cb)),
                pl.BlockSpec((None, blk, LANE), lambda bi, i, cb=cb: (bi, nxt(i), cb)),
            ]
            args += [arr, arr, arr]
    return pl.pallas_call(
        functools.partial(_win_attn_kernel, has_x=has_x, n_tok=n_tok),
        grid=(b, nb),
        in_specs=in_specs,
        out_specs=pl.BlockSpec((None, blk, qw), lambda bi, i: (bi, i, 0)),
        out_shape=jax.ShapeDtypeStruct((b, tq_all, qw), BF16),
        compiler_params=_cp("parallel", "arbitrary"),
        name="win_attn_x" if has_x else "win_attn_c",
    )(*args)


def _na_kernel(*refs, has_x, n_rows):
    if has_x:
        q_ref, kc_ref, vc_ref, k_ref, v_ref, bias_ref, o_ref = refs
    else:
        q_ref, kc_ref, vc_ref, o_ref = refs
    tq = q_ref.shape[0]
    lane = lax.broadcasted_iota(I32, (tq, LANE), 1)
    per = LANE // NA_DIM
    if has_x:
        r = pl.program_id(1)
        start = jnp.clip(r - NA_ROWS // 2, 0, n_rows - NA_ROWS)
        off = pl.multiple_of(start * GRID_W, GRID_W)
        nk = NA_ROWS * GRID_W
        col = lax.broadcasted_iota(I32, (GRID_W, nk), 0)
        w = lax.broadcasted_iota(I32, (GRID_W, nk), 1) & (GRID_W - 1)
        col_start = jnp.clip(col - NA_COLS // 2, 0, GRID_W - NA_COLS)
        valid = (w >= col_start) & (w < col_start + NA_COLS)
    for j in range(NA_HEADS // per):
        cols = slice(j * LANE, (j + 1) * LANE)
        q = q_ref[:, cols].astype(F32) * (NA_DIM ** -0.5)
        kc, vc = kc_ref[:, cols], vc_ref[:, cols]
        if has_x:
            k, v = k_ref[pl.ds(off, nk), cols], v_ref[pl.ds(off, nk), cols]
        acc = jnp.zeros(q.shape, F32)
        for hh in range(per):
            mine = _lane_group(lane, NA_DIM) == hh
            qh = jnp.where(mine, q, 0.0).astype(BF16)
            sc = _dot(qh, kc, NT)
            m = jnp.max(sc, axis=-1, keepdims=True)
            if has_x:
                s = jnp.where(valid, _dot(qh, k, NT) + bias_ref[j * per + hh], NEG_INF)
                m = jnp.maximum(m, jnp.max(s, axis=-1, keepdims=True))
            pc = jnp.exp(sc - m)
            l = jnp.sum(pc, axis=-1, keepdims=True)
            o = _dot(pc.astype(BF16), vc)
            if has_x:
                p = jnp.exp(s - m)
                l = l + jnp.sum(p, axis=-1, keepdims=True)
                o = o + _dot(p.astype(BF16), v)
            acc = acc + jnp.where(mine, o / l, 0.0)
        o_ref[:, cols] = acc.astype(o_ref.dtype)


def _na_bias_table(rpb):
    r = rpb.astype(F32)
    edge = GRID_W - NA_COLS
    ext = jnp.concatenate([jnp.repeat(r[..., :1], edge, -1), r, jnp.repeat(r[..., -1:], edge, -1)], axis=-1)
    by_col = jnp.stack([ext[..., GRID_W - 1 - c:2 * GRID_W - 1 - c] for c in range(GRID_W)], axis=-2)
    t = jnp.stack([by_col[:, NA_ROWS - 1 - oi:2 * NA_ROWS - 1 - oi] for oi in range(NA_ROWS)], axis=1)
    t = jnp.transpose(t, (0, 1, 3, 2, 4))
    return t.reshape(NA_HEADS, NA_ROWS, GRID_W, NA_ROWS * GRID_W)


def _na_attn(q_arr, zc, zx, bias_tab, has_x):
    b, tq_all, _ = q_arr.shape
    c = zc.shape[1]
    hw = NA_HEADS * NA_DIM
    tq = GRID_W if has_x else _tile(tq_all, 256, 8)
    nq = tq_all // tq
    in_specs = [
        pl.BlockSpec((None, tq, hw), lambda bi, r: (bi, r, 0)),
        pl.BlockSpec((None, c, hw), lambda bi, r: (bi, 0, 1)),
        pl.BlockSpec((None, c, hw), lambda bi, r: (bi, 0, 2)),
    ]
    args = [q_arr, zc, zc]
    if has_x:
        n = zx.shape[1]
        assert nq >= NA_ROWS
        oi = lambda r: r - jnp.clip(r - NA_ROWS // 2, 0, nq - NA_ROWS)
        in_specs += [
            pl.BlockSpec((None, n, hw), lambda bi, r: (bi, 0, 1)),
            pl.BlockSpec((None, n, hw), lambda bi, r: (bi, 0, 2)),
            pl.BlockSpec((NA_HEADS, None, GRID_W, NA_ROWS * GRID_W), lambda bi, r: (0, oi(r), 0, 0)),
        ]
        args += [zx, zx, bias_tab]
    return pl.pallas_call(
        functools.partial(_na_kernel, has_x=has_x, n_rows=nq),
        grid=(b, nq),
        in_specs=in_specs,
        out_specs=pl.BlockSpec((None, tq, hw), lambda bi, r: (bi, r, 0)),
        out_shape=jax.ShapeDtypeStruct((b, tq_all, hw), BF16),
        compiler_params=_cp("parallel", "arbitrary"),
        name="na_attn_x" if has_x else "na_attn_c",
    )(*args)


def _s5_kernel(*refs, ctx_out):
    if ctx_out:
        (uc_ref, ux_ref, d_ref, b1_ref, b2_ref, a1_ref, a2_ref, a3_ref, cb_ref, yc_ref, yx_ref,
         kbig, bb1, bb2, xc, xx, s1c, s2c, s1x, s2x, hc, hx) = refs
    else:
        (uc_ref, ux_ref, d_ref, b1_ref, b2_ref, a1_ref, a2_ref, a3_ref, cb_ref, yx_ref,
         kbig, bb1, bb2, xc, xx, s1c, s2c, s1x, s2x, hc, hx) = refs
        yc_ref = None
    dirn = pl.program_id(1)
    ell, hch = S5_CHUNK, S5_GROUP_CH
    gq = LANE // hch
    nb = ux_ref.shape[0]
    streams = ((uc_ref, xc, s1c, s2c, hc, yc_ref, uc_ref.shape[1] // ell),
               (ux_ref, xx, s1x, s2x, hx, yx_ref, ux_ref.shape[1] // ell))

    for i in range(ell):
        for j in range(ell):
            kbig[i * LANE:(i + 1) * LANE, j * LANE:(j + 1) * LANE] = d_ref[j - i + ell - 1]
    bb1[...] = jnp.zeros(bb1.shape, BF16)
    bb2[...] = jnp.zeros(bb2.shape, BF16)
    for i in range(ell):
        for gl in range(gq):
            r0 = i * LANE + gl * hch
            bb1[r0:r0 + hch, gl * LANE:(gl + 1) * LANE] = b1_ref[gl, i * hch:(i + 1) * hch, :].astype(BF16)
            bb2[r0:r0 + hch, gl * LANE:(gl + 1) * LANE] = b2_ref[gl, i * hch:(i + 1) * hch, :].astype(BF16)

    @pl.when(dirn == 0)
    def _():
        for u_ref, x_s, _, _, _, _, nc in streams:
            for b in range(nb):
                for i in range(ell):
                    x_s[b * nc:(b + 1) * nc, i * LANE:(i + 1) * LANE] = u_ref[b, pl.ds(i, nc, stride=ell), :].astype(BF16)

    for _, x_s, s1, s2, _, _, _ in streams:
        loc1 = _dot(x_s[...], bb1[...])
        loc2 = _dot(x_s[...], bb2[...])
        for k in range(gq):
            s1[k] = loc1[:, k * LANE:(k + 1) * LANE]
            s2[k] = loc2[:, k * LANE:(k + 1) * LANE]

    def lane_blocks(a_ref):
        return [jnp.broadcast_to(a_ref[:, k * LANE:(k + 1) * LANE], (nb, LANE)) for k in range(gq)]

    a1, a2, a3 = lane_blocks(a1_ref), lane_blocks(a2_ref), lane_blocks(a3_ref)
    zero = tuple(jnp.zeros((nb, LANE), F32) for _ in range(gq))
    carry = (zero, zero)
    for _, _, s1, s2, h, _, nc in streams:
        def step(t, vs, s1=s1, s2=s2, h=h, nc=nc):
            v1, v2 = vs
            c = jnp.where(dirn == 0, t, nc - 1 - t)
            rows = pl.ds(c, nb, stride=nc)
            n1, n2 = [], []
            for k in range(gq):
                h[k, rows, :] = v1[k]
                n1.append(a1[k] * v1[k] + a2[k] * v2[k] + s1[k, rows, :])
                n2.append(a1[k] * v2[k] + a3[k] * v1[k] + s2[k, rows, :])
            return tuple(n1), tuple(n2)

        carry = lax.fori_loop(0, nc, step, carry)

    for _, x_s, _, _, h, y_ref, nc in streams:
        if y_ref is None:
            continue
        h_all = jnp.concatenate([h[k] for k in range(gq)], axis=1).astype(BF16)
        y = _dot(x_s[...], kbig[...]) + _dot(h_all, cb_ref[...])

        @pl.when(dirn == 0)
        def _():
            for b in range(nb):
                for j in range(ell):
                    y_ref[b, pl.ds(j, nc, stride=ell), :] = y[b * nc:(b + 1) * nc, j * LANE:(j + 1) * LANE]

        @pl.when(dirn == 1)
        def _():
            for b in range(nb):
                for j in range(ell):
                    y_ref[b, pl.ds(j, nc, stride=ell), :] += y[b * nc:(b + 1) * nc, j * LANE:(j + 1) * LANE]


def _s5_tables(a_re, a_im, log_step, b_re, b_im, c_re, c_im):
    ell, hch, p = S5_CHUNK, S5_GROUP_CH, S5_STATE
    lam = lax.complex(a_re.astype(F32), a_im.astype(F32))
    lam_dt = lam * jnp.exp(log_step.astype(F32))[:, :, None]
    a_bar = jnp.exp(lam_dt)
    b_bar = ((a_bar - 1.0) / lam)[:, :, :, None] * lax.complex(b_re.astype(F32), b_im.astype(F32))
    c_mat = lax.complex(c_re.astype(F32), c_im.astype(F32))
    gq = LANE // hch
    nq = S5_GROUPS // gq
    hp = lax.Precision.HIGHEST
    steps = jnp.arange(ell + 1, dtype=F32)
    apow = jnp.exp(lam_dt[:, :, None, :] * steps[None, None, :, None])
    apow_re, apow_im = apow.real, apow.imag
    bt_re, bt_im = jnp.swapaxes(b_bar.real, 2, 3), jnp.swapaxes(b_bar.imag, 2, 3)
    c_re_, c_im_ = c_mat.real, c_mat.imag
    eye = jnp.eye(gq, dtype=F32)

    def cmul(ar, ai, br, bi):
        return ar * br - ai * bi, ar * bi + ai * br

    m_re, m_im = cmul(apow_re[:, :, :ell, None, :], apow_im[:, :, :ell, None, :], bt_re[:, :, None], bt_im[:, :, None])
    kk = (jnp.einsum('dgkip,dgop->dgkio', m_re, c_re_, precision=hp)
          - jnp.einsum('dgkip,dgop->dgkio', m_im, c_im_, precision=hp))
    zeros = jnp.zeros_like(kk[0, :, 1:])
    signed = jnp.stack([jnp.concatenate([zeros, kk[0]], axis=1),
                        jnp.concatenate([jnp.flip(kk[1], axis=1), zeros], axis=1)])
    signed = signed.reshape(2, nq, gq, 2 * ell - 1, hch, hch)
    dblk = jnp.einsum('dqgkio,gh->dqkgiho', signed, eye).reshape(2, nq, 2 * ell - 1, LANE, LANE).astype(BF16)
    i_idx = jnp.arange(ell)
    e_re = jnp.stack([apow_re[0][:, ell - 1 - i_idx], apow_re[1][:, i_idx]])
    e_im = jnp.stack([apow_im[0][:, ell - 1 - i_idx], apow_im[1][:, i_idx]])
    bp_re, bp_im = cmul(e_re[:, :, :, None, :], e_im[:, :, :, None, :], bt_re[:, :, None], bt_im[:, :, None])
    bp_re, bp_im = bp_re.reshape(2, S5_GROUPS, ell * hch, p), bp_im.reshape(2, S5_GROUPS, ell * hch, p)
    b1 = jnp.concatenate([bp_re, bp_im], axis=-1)
    b2 = jnp.concatenate([bp_im, bp_re], axis=-1)
    al_re, al_im = apow_re[:, :, ell, :], apow_im[:, :, ell, :]
    lanes = lambda u, v: jnp.concatenate([u, v], axis=-1).reshape(2, nq, 1, gq * 2 * p)
    a1, a2, a3 = lanes(al_re, al_re), lanes(-al_im, al_im), lanes(al_im, -al_im)
    f_re = jnp.stack([apow_re[0][:, 1 + i_idx], apow_re[1][:, ell - i_idx]])
    f_im = jnp.stack([apow_im[0][:, 1 + i_idx], apow_im[1][:, ell - i_idx]])
    g_re, g_im = cmul(c_re_[:, :, None], c_im_[:, :, None], f_re[:, :, :, None, :], f_im[:, :, :, None, :])
    cp = jnp.concatenate([jnp.transpose(g_re, (0, 1, 4, 2, 3)), -jnp.transpose(g_im, (0, 1, 4, 2, 3))], axis=2)
    cp = cp.reshape(2, nq, gq, 2 * p, ell, hch)
    cbig = jnp.einsum('dqgsjo,gh->dqgsjho', cp, eye).reshape(2, nq, gq * 2 * p, ell * LANE).astype(BF16)
    return dblk, b1, b2, a1, a2, a3, cbig


def _s5_scan(zx, zc, tables, ctx_out):
    b, n, _ = zx.shape
    c = zc.shape[1]
    ell, hch = S5_CHUNK, S5_GROUP_CH
    gq = LANE // hch
    nq = S5_GROUPS // gq
    p2 = 2 * S5_STATE
    ncx, ncc = n // ell, c // ell
    assert n % ell == 0 and c % ell == 0 and ncc % 16 == 0
    dblk, b1, b2, a1, a2, a3, cbig = tables
    u0 = 0
    sw = gq * p2
    xw = ell * LANE
    a_spec = pl.BlockSpec((None, None, 1, sw), lambda q, d: (d, q, 0, 0))
    b_spec = pl.BlockSpec((None, gq, ell * hch, p2), lambda q, d: (d, q, 0, 0))
    y_specs = [pl.BlockSpec((b, n, LANE), lambda q, d: (0, 0, q))]
    y_shapes = [jax.ShapeDtypeStruct((b, n, S5_WIDTH), F32)]
    if ctx_out:
        y_specs = [pl.BlockSpec((b, c, LANE), lambda q, d: (0, 0, q))] + y_specs
        y_shapes = [jax.ShapeDtypeStruct((b, c, S5_WIDTH), F32)] + y_shapes
    out = pl.pallas_call(
        functools.partial(_s5_kernel, ctx_out=ctx_out),
        grid=(nq, 2),
        in_specs=[
            pl.BlockSpec((b, c, LANE), lambda q, d: (0, 0, u0 + q)),
            pl.BlockSpec((b, n, LANE), lambda q, d: (0, 0, u0 + q)),
            pl.BlockSpec((None, None, 2 * ell - 1, LANE, LANE), lambda q, d: (d, q, 0, 0, 0)),
            b_spec, b_spec, a_spec, a_spec, a_spec,
            pl.BlockSpec((None, None, sw, xw), lambda q, d: (d, q, 0, 0)),
        ],
        out_specs=y_specs,
        out_shape=y_shapes,
        scratch_shapes=[
            pltpu.VMEM((xw, xw), BF16), pltpu.VMEM((xw, sw), BF16), pltpu.VMEM((xw, sw), BF16),
            pltpu.VMEM((b * ncc, xw), BF16), pltpu.VMEM((b * ncx, xw), BF16),
            pltpu.VMEM((gq, b * ncc, p2), F32), pltpu.VMEM((gq, b * ncc, p2), F32),
            pltpu.VMEM((gq, b * ncx, p2), F32), pltpu.VMEM((gq, b * ncx, p2), F32),
            pltpu.VMEM((gq, b * ncc, p2), F32), pltpu.VMEM((gq, b * ncx, p2), F32),
        ],
        compiler_params=_cp("parallel", "arbitrary"),
        name="s5_scan",
    )(zc, zx, dblk, b1, b2, a1, a2, a3, cbig)
    return (out[0], out[1]) if ctx_out else (None, out[0])


def _s5_out_kernel(y_ref, u_ref, d_ref, w_ref, o_ref):
    g = jax.nn.gelu(y_ref[...] + d_ref[...] * u_ref[...])
    o_ref[...] = (g * jax.nn.sigmoid(_dot(g.astype(BF16), w_ref[...].astype(BF16)))).astype(o_ref.dtype)


def _s5_out(y, u, d_skip, w_glu):
    b, t, w = y.shape
    tr = _tile(t, 512, 8)
    row = pl.BlockSpec((None, tr, w), lambda bi, i: (bi, i, 0))
    return pl.pallas_call(
        _s5_out_kernel,
        grid=(b, t // tr),
        in_specs=[row, row, pl.BlockSpec((1, w), lambda bi, i: (0, 0)), pl.BlockSpec((w, w), lambda bi, i: (0, 0))],
        out_specs=row,
        out_shape=jax.ShapeDtypeStruct((b, t, w), BF16),
        compiler_params=_cp("parallel", "parallel"),
        name="s5_out",
    )(y, u, d_skip.reshape(1, w).astype(F32), w_glu)


def _merge_kernel(h_ref, ya_ref, yb_ref, yc_ref, yd_ref, wg0_ref, wg1_ref, wg2_ref, wg3_ref, wb_ref, o_ref, wgs_ref, wbs_ref):
    @pl.when(pl.program_id(1) == 0)
    def _():
        for n, wg_ref in enumerate((wg0_ref, wg1_ref, wg2_ref, wg3_ref)):
            wgs_ref[n] = wg_ref[...].astype(BF16)
        wbs_ref[...] = wb_ref[...].astype(BF16)

    h = h_ref[...]
    acc = None
    for n, y_ref in enumerate((ya_ref, yb_ref, yc_ref, yd_ref)):
        term = jax.nn.sigmoid(_dot(h, wgs_ref[n])) * _dot(y_ref[...], wbs_ref[n])
        acc = term if acc is None else acc + term
    o_ref[...] = acc.astype(o_ref.dtype)


def _merge(h, branches, w_in, w_branch, layer):
    m, d = h.shape
    bw = branches[0].shape[1]
    tn = 256
    tm = _tile(m, 1024, 8)
    g0 = N_MIX_IN // tn
    per = d // tn
    assert N_MIX_IN % tn == 0 and d % tn == 0
    wg_specs = [pl.BlockSpec((None, d, tn), lambda j, i, n=n: (layer, 0, g0 + n * per + j)) for n in range(N_BRANCH)]
    y_spec = pl.BlockSpec((tm, bw), lambda j, i: (i, 0))
    return pl.pallas_call(
        _merge_kernel,
        grid=(d // tn, m // tm),
        in_specs=[pl.BlockSpec((tm, d), lambda j, i: (i, 0)), y_spec, y_spec, y_spec, y_spec] + wg_specs
                 + [pl.BlockSpec((None, N_BRANCH, bw, tn), lambda j, i: (layer, 0, 0, j))],
        out_specs=pl.BlockSpec((tm, tn), lambda j, i: (i, j)),
        out_shape=jax.ShapeDtypeStruct((m, d), BF16),
        scratch_shapes=[pltpu.VMEM((N_BRANCH, d, tn), BF16), pltpu.VMEM((N_BRANCH, bw, tn), BF16)],
        compiler_params=_cp("arbitrary", "arbitrary"),
        name="merge",
    )(h, *branches, w_in, w_in, w_in, w_in, w_branch)


def _topk_kernel(lg_ref, slot_ref, aff_ref, tri_ref, *, cap):
    e, t = lg_ref.shape

    @pl.when(pl.program_id(0) == 0)
    def _():
        rows = 256 if t % 256 == 0 else t
        for r0 in range(0, t, rows):
            ri = lax.broadcasted_iota(I32, (rows, t), 0) + r0
            ci = lax.broadcasted_iota(I32, (rows, t), 1)
            tri_ref[r0:r0 + rows, :] = jnp.where(ri < ci, 1.0, 0.0).astype(BF16)

    lg = lg_ref[...]
    ex = jnp.exp(lg - jnp.max(lg, axis=0, keepdims=True))
    aff = ex / jnp.sum(ex, axis=0, keepdims=True)
    aff_ref[...] = aff
    bits = pltpu.bitcast(aff, I32)

    def search(_, carry):
        lo, hi = carry
        mid = lo + lax.shift_right_logical(hi - lo + 1, 1)
        ok = _count(bits >= mid, 1) >= cap
        return jnp.where(ok, mid, lo), jnp.where(ok, hi, mid - 1)

    lo0 = jnp.zeros((e, 1), I32)
    hi0 = jnp.full((e, 1), 0x7F800000, I32)
    thr, _ = lax.fori_loop(0, 32, search, (lo0, hi0))
    gt = bits > thr
    eq = bits == thr
    need = cap - _count(gt, 1)
    tri = tri_ref[...]
    eq_before = _dot(jnp.where(eq, 1.0, 0.0).astype(BF16), tri)
    sel = gt | (eq & (eq_before < need))
    sel_before = _dot(jnp.where(sel, 1.0, 0.0).astype(BF16), tri)
    slot_ref[...] = jnp.where(sel, sel_before.astype(I32), -1)


def _topk(logits_t, b, cap):
    e, bt = logits_t.shape
    t = bt // b
    return pl.pallas_call(
        functools.partial(_topk_kernel, cap=cap),
        grid=(b,),
        in_specs=[pl.BlockSpec((e, t), lambda bi: (0, bi))],
        out_specs=[pl.BlockSpec((None, e, t), lambda bi: (bi, 0, 0)), pl.BlockSpec((None, e, t), lambda bi: (bi, 0, 0))],
        out_shape=[jax.ShapeDtypeStruct((b, e, t), I32), jax.ShapeDtypeStruct((b, e, t), F32)],
        scratch_shapes=[pltpu.VMEM((t, t), BF16)],
        compiler_params=_cp("arbitrary"),
        name="route_topk",
    )(logits_t)


def _gather_kernel(slot_ref, aff_ref, h_ref, xs_ref, gate_ref, *, cap):
    t = h_ref.shape[0]
    pick = lax.broadcasted_iota(I32, (cap, t), 0) == slot_ref[...]
    xs_ref[...] = _dot(jnp.where(pick, 1.0, 0.0).astype(BF16), h_ref[...]).astype(xs_ref.dtype)
    gate_ref[...] = jnp.sum(jnp.where(pick, aff_ref[...], 0.0), axis=1, keepdims=True)


def _gather(slot, aff, h, cap):
    b, e, t = slot.shape
    d = h.shape[2]
    row = pl.BlockSpec((None, None, 1, t), lambda bi, ei: (bi, ei, 0, 0))
    return pl.pallas_call(
        functools.partial(_gather_kernel, cap=cap),
        grid=(b, e),
        in_specs=[row, row, pl.BlockSpec((None, t, d), lambda bi, ei: (bi, 0, 0))],
        out_specs=[pl.BlockSpec((None, cap, d), lambda bi, ei: (ei, bi, 0)),
                   pl.BlockSpec((None, cap, 1), lambda bi, ei: (ei, bi, 0))],
        out_shape=[jax.ShapeDtypeStruct((e, b * cap, d), BF16), jax.ShapeDtypeStruct((e, b * cap, 1), F32)],
        compiler_params=_cp("parallel", "arbitrary"),
        name="moe_gather",
    )(slot.reshape(b, e, 1, t), aff.reshape(b, e, 1, t), h)


def _ffn_up_kernel(*refs, n_streams):
    xs = refs[:n_streams]
    w1_ref, w3_ref = refs[n_streams:n_streams + 2]
    outs = refs[n_streams + 2:]
    w1, w3 = w1_ref[...].astype(BF16), w3_ref[...].astype(BF16)
    for x_ref, o_ref in zip(xs, outs):
        x = x_ref[...]
        a = _dot(x, w1)
        g = _dot(x, w3)
        o_ref[...] = ((a / (1.0 + jnp.exp(-a))) * g).astype(o_ref.dtype)


def _ffn_down_kernel(*refs, n_streams):
    mids = refs[:n_streams]
    gates = refs[n_streams:2 * n_streams]
    w2_ref = refs[2 * n_streams]
    outs = refs[2 * n_streams + 1:]
    w2 = w2_ref[...].astype(BF16)
    for m_ref, gate_ref, o_ref in zip(mids, gates, outs):
        o_ref[...] = (_dot(m_ref[...], w2) * gate_ref[...]).astype(o_ref.dtype)


def _ffn(xs_list, gate_list, w1, w3, w2, layer):
    ns = len(xs_list)
    e, _, d = xs_list[0].shape
    ff = w1.shape[-1]
    tf = _tile(ff, 512)
    tn = _tile(d, 512)
    rows = [x.shape[1] for x in xs_list]
    mids = pl.pallas_call(
        functools.partial(_ffn_up_kernel, n_streams=ns),
        grid=(e, ff // tf),
        in_specs=[pl.BlockSpec((None, r, d), lambda ei, f: (ei, 0, 0)) for r in rows] + [
            pl.BlockSpec((None, None, d, tf), lambda ei, f: (layer, ei, 0, f)),
            pl.BlockSpec((None, None, d, tf), lambda ei, f: (layer, ei, 0, f)),
        ],
        out_specs=[pl.BlockSpec((None, r, tf), lambda ei, f: (ei, 0, f)) for r in rows],
        out_shape=[jax.ShapeDtypeStruct((e, r, ff), BF16) for r in rows],
        compiler_params=_cp("parallel", "arbitrary"),
        name="moe_ffn_up",
    )(*xs_list, w1, w3)
    return pl.pallas_call(
        functools.partial(_ffn_down_kernel, n_streams=ns),
        grid=(e, d // tn),
        in_specs=[pl.BlockSpec((None, r, ff), lambda ei, j: (ei, 0, 0)) for r in rows]
                 + [pl.BlockSpec((None, r, 1), lambda ei, j: (ei, 0, 0)) for r in rows]
                 + [pl.BlockSpec((None, None, ff, tn), lambda ei, j: (layer, ei, 0, j))],
        out_specs=[pl.BlockSpec((None, r, tn), lambda ei, j: (ei, 0, j)) for r in rows],
        out_shape=[jax.ShapeDtypeStruct((e, r, d), BF16) for r in rows],
        compiler_params=_cp("parallel", "arbitrary"),
        name="moe_ffn_down",
    )(*mids, *gate_list, w2)


def _combine_kernel(slot_ref, y_ref, x_ref, gate_ref, o_ref, pt_ref, *, cap):
    e = slot_ref.shape[1]
    tm = slot_ref.shape[0]

    @pl.when(pl.program_id(2) == 0)
    def _():
        slot = slot_ref[...]
        lane = lax.broadcasted_iota(I32, (tm, cap), 1)
        for ei in range(e):
            pt_ref[ei] = jnp.where(lane == slot[:, ei:ei + 1], 1.0, 0.0).astype(BF16)

    acc = jnp.zeros(o_ref.shape, F32)
    for ei in range(e):
        acc = acc + _dot(pt_ref[ei], y_ref[ei])
    o_ref[...] = x_ref[...] + gate_ref[...] * acc


def _combine(slot_te, ys, x, gate, cap):
    b, t, d = x.shape
    e = slot_te.shape[2]
    tm = _tile(t, 512, 8)
    tn = _tile(d, 512)
    return pl.pallas_call(
        functools.partial(_combine_kernel, cap=cap),
        grid=(b, t // tm, d // tn),
        in_specs=[
            pl.BlockSpec((None, tm, e), lambda bi, i, j: (bi, i, 0)),
            pl.BlockSpec((e, cap, tn), lambda bi, i, j: (0, bi, j)),
            pl.BlockSpec((None, tm, tn), lambda bi, i, j: (bi, i, j)),
            pl.BlockSpec((None, 1, tn), lambda bi, i, j: (bi, 0, j)),
        ],
        out_specs=pl.BlockSpec((None, tm, tn), lambda bi, i, j: (bi, i, j)),
        out_shape=jax.ShapeDtypeStruct((b, t, d), F32),
        scratch_shapes=[pltpu.VMEM((e, tm, cap), BF16)],
        compiler_params=_cp("parallel", "arbitrary", "arbitrary"),
        name="moe_combine",
    )(slot_te, ys, x, gate)


def kernel(x, c, ctx, c_ctx, ada_w, ada_b, norm1_g, norm2_g, w_in, da_lambda, da_subln_g, wb_sink, na_rpb, s5_a_re, s5_a_im, s5_log_step, s5_b_re, s5_b_im, s5_c_re, s5_c_im, s5_d, s5_glu_w, w_branch, w_out, w_router, w_e1, w_e3, w_e2, final_g):
    b, n, d = x.shape
    n_ctx = ctx.shape[1]
    depth = ada_w.shape[0]
    assert b + 1 <= ADA_ROWS and n % GRID_W == 0

    cs = jnp.zeros((ADA_ROWS, d), F32).at[:b].set(c).at[b].set(c_ctx)
    mods = _ada_mod(cs, ada_w, ada_b)
    rope_tab = _rope_tables(n)

    for l in range(depth):
        with_ctx = l < depth - 1
        mod_x = [mods[l, :b, k * d:(k + 1) * d].reshape(b, 1, d) for k in range(6)]
        mod_c = [jnp.broadcast_to(mods[l, b, k * d:(k + 1) * d], (b, 1, d)) for k in range(6)]

        hx = _norm_mod(x, norm1_g[l], mod_x[0], mod_x[1])
        hc = _norm_mod(ctx, norm1_g[l], mod_c[0], mod_c[1])
        proj = []
        for h, t in ((hx, n), (hc, n_ctx)):
            h2d = h.reshape(b * t, d)
            proj.append([_mm(h2d, w_in, l, col0, ncols, tn, dt).reshape(b, t, ncols) for col0, ncols, tn, dt in MIX_PROJ])
        (za_x, zb_x, zn_x, zs_x), (za_c, zb_c, zn_c, zs_c) = proj
        qa_x = _rope(za_x, IN_WIDTHS[0] + IN_WIDTHS[1], rope_tab)
        qb_x = _rope(zb_x, IN_WIDTHS[3] + IN_WIDTHS[4], rope_tab)

        lam_init = 0.8 - 0.6 * math.exp(-0.3 * l)
        lv = da_lambda[l].astype(F32)
        lam = jnp.exp(jnp.sum(lv[0] * lv[1])) - jnp.exp(jnp.sum(lv[2] * lv[3])) + lam_init
        bias_tab = _na_bias_table(na_rpb[l])
        s5_tab = _s5_tables(s5_a_re[l], s5_a_im[l], s5_log_step[l], s5_b_re[l], s5_b_im[l], s5_c_re[l], s5_c_im[l])

        ya_x = _diff_attn(qa_x, za_c, qa_x, za_x, lam, da_subln_g[l], lam_init, True)
        yb_x = _win_attn(qb_x, zb_c, qb_x, zb_x, wb_sink[l], True)
        yc_x = _na_attn(zn_x, zn_c, zn_x, bias_tab, True)
        ys_c, ys_x = _s5_scan(zs_x, zs_c, s5_tab, with_ctx)
        yd_x = _s5_out(ys_x, zs_x, s5_d[l], s5_glu_w[l])
        gx = _merge(hx.reshape(b * n, d), [t.reshape(b * n, -1) for t in (ya_x, yb_x, yc_x, yd_x)], w_in, w_branch, l)
        x = _mm_res(gx, w_out, l, x, mod_x[2])

        if with_ctx:
            ya_c = _diff_attn(za_c, za_c, None, None, lam, da_subln_g[l], lam_init, False)
            yb_c = _win_attn(zb_c, zb_c, None, None, wb_sink[l], False)
            yc_c = _na_attn(zn_c, zn_c, None, None, False)
            yd_c = _s5_out(ys_c, zs_c, s5_d[l], s5_glu_w[l])
            gc = _merge(hc.reshape(b * n_ctx, d), [t.reshape(b * n_ctx, -1) for t in (ya_c, yb_c, yc_c, yd_c)],
                        w_in, w_branch, l)
            ctx = _mm_res(gc, w_out, l, ctx, mod_c[2])

        w_router_t = jnp.transpose(w_router[l]).astype(F32)
        streams = [(x, mod_x)] + ([(ctx, mod_c)] if with_ctx else [])
        routed = []
        for s, mod in streams:
            t = s.shape[1]
            cap = EC_CAPACITY_FACTOR * t // N_EXPERTS
            h2, logits_t = _norm_router(s, norm2_g[l], mod[3], mod[4], w_router_t)
            slot, aff = _topk(logits_t, b, cap)
            xs, gate = _gather(slot, aff, h2, cap)
            routed.append((jnp.transpose(slot, (0, 2, 1)), xs, gate, cap))
        ys = _ffn([r[1] for r in routed], [r[2] for r in routed], w_e1, w_e3, w_e2, l)
        x = _combine(routed[0][0], ys[0], x, mod_x[5], routed[0][3])
        if with_ctx:
            ctx = _combine(routed[1][0], ys[1], ctx, mod_c[5], routed[1][3])

    return _final_norm(x, final_g)
```

```python
import functools
import math

import jax
import jax.numpy as jnp
from jax import lax
from jax.experimental import pallas as pl
from jax.experimental.pallas import tpu as pltpu

F32 = jnp.float32
BF16 = jnp.bfloat16
I32 = jnp.int32

GRID_W = 64
EPS = 1e-6
NEG_INF = -1e30
ROPE_BASE = 10000.0

DA_HEADS = 4
DA_QK_DIM = 64
DA_V_DIM = 2 * DA_QK_DIM
WB_HEADS = 8
WB_KV_HEADS = 2
WB_DIM = 64
WB_WINDOW = 128
WB_BLOCK = 128
NA_HEADS = 8
NA_DIM = 64
NA_ROWS = 8
NA_COLS = 16
S5_GROUPS = 32
S5_GROUP_CH = 16
S5_STATE = 64
S5_WIDTH = S5_GROUPS * S5_GROUP_CH
S5_CHUNK = 8
N_BRANCH = 4
BRANCH_WIDTH = 512
N_EXPERTS = 16
EC_CAPACITY_FACTOR = 2

IN_WIDTHS = (
    2 * DA_HEADS * DA_QK_DIM, 2 * DA_HEADS * DA_QK_DIM, DA_HEADS * DA_V_DIM,
    WB_HEADS * WB_DIM, WB_KV_HEADS * WB_DIM, WB_KV_HEADS * WB_DIM,
    NA_HEADS * NA_DIM, NA_HEADS * NA_DIM, NA_HEADS * NA_DIM,
    S5_WIDTH,
)
N_MIX_IN = sum(IN_WIDTHS)
_OFFS = [0]
for _w in IN_WIDTHS:
    _OFFS.append(_OFFS[-1] + _w)
(OFF_DA_Q, OFF_DA_K, OFF_DA_V, OFF_WB_Q, OFF_WB_K, OFF_WB_V, OFF_NA_Q, OFF_NA_K, OFF_NA_V, OFF_S5, _) = _OFFS

MIX_PROJ = (
    (OFF_DA_Q, OFF_WB_Q - OFF_DA_Q, 768, BF16),
    (OFF_WB_Q, OFF_NA_Q - OFF_WB_Q, 768, BF16),
    (OFF_NA_Q, OFF_S5 - OFF_NA_Q, 768, BF16),
    (OFF_S5, S5_WIDTH, 256, F32),
)

LANE = 128
ADA_ROWS = 8
VMEM_LIMIT = 56 * 1024 * 1024

NN = (((1,), (0,)), ((), ()))
NT = (((1,), (1,)), ((), ()))


def _cp(*sem):
    return pltpu.CompilerParams(dimension_semantics=sem, vmem_limit_bytes=VMEM_LIMIT)


def _tile(n, pref, mult=LANE):
    if n <= pref:
        return n
    t = (pref // mult) * mult
    while t >= mult:
        if n % t == 0:
            return t
        t -= mult
    return n


def _split(a):
    hi = a.astype(BF16)
    lo = (a - hi.astype(F32)).astype(BF16)
    return hi, lo


def _dot(a, b, dims=NN):
    return lax.dot_general(a, b, dims, preferred_element_type=F32)


def _lane_group(lane, width):
    return lax.shift_right_logical(lane, int(math.log2(width)))


def _count(mask, axis):
    return jnp.sum(jnp.where(mask, 1.0, 0.0), axis=axis, keepdims=True)


def _dot3(a, b, dims=NN):
    ah, al = _split(a)
    bh, bl = _split(b)
    return _dot(ah, bh, dims) + (_dot(ah, bl, dims) + _dot(al, bh, dims))


def _ada_kernel(c_ref, w_ref, b_ref, o_ref):
    c = c_ref[...]
    s = c / (1.0 + jnp.exp(-c))
    o_ref[...] = _dot3(s, w_ref[...]) + b_ref[...]


def _ada_mod(cs, ada_w, ada_b):
    depth, d, n6 = ada_w.shape
    tn = _tile(n6, 512)
    return pl.pallas_call(
        _ada_kernel,
        grid=(depth, n6 // tn),
        in_specs=[
            pl.BlockSpec((ADA_ROWS, d), lambda l, j: (0, 0)),
            pl.BlockSpec((None, d, tn), lambda l, j: (l, 0, j)),
            pl.BlockSpec((None, 1, tn), lambda l, j: (l, 0, j)),
        ],
        out_specs=pl.BlockSpec((None, ADA_ROWS, tn), lambda l, j: (l, 0, j)),
        out_shape=jax.ShapeDtypeStruct((depth, ADA_ROWS, n6), F32),
        compiler_params=_cp("arbitrary", "arbitrary"),
        name="ada_mod",
    )(cs, ada_w, ada_b.reshape(depth, 1, n6))


def _norm_mod_rows(x, g, sh, sc):
    y = x * lax.rsqrt(jnp.mean(x * x, axis=-1, keepdims=True) + EPS) * g
    return y * (1.0 + sc) + sh


def _norm_mod_kernel(x_ref, g_ref, sh_ref, sc_ref, o_ref):
    o_ref[...] = _norm_mod_rows(x_ref[...], g_ref[...], sh_ref[...], sc_ref[...]).astype(o_ref.dtype)


def _norm_mod(x, g, sh, sc):
    b, t, d = x.shape
    tr = _tile(t, 256, 8)
    return pl.pallas_call(
        _norm_mod_kernel,
        grid=(b, t // tr),
        in_specs=[
            pl.BlockSpec((None, tr, d), lambda bi, i: (bi, i, 0)),
            pl.BlockSpec((1, d), lambda bi, i: (0, 0)),
            pl.BlockSpec((None, 1, d), lambda bi, i: (bi, 0, 0)),
            pl.BlockSpec((None, 1, d), lambda bi, i: (bi, 0, 0)),
        ],
        out_specs=pl.BlockSpec((None, tr, d), lambda bi, i: (bi, i, 0)),
        out_shape=jax.ShapeDtypeStruct((b, t, d), BF16),
        compiler_params=_cp("parallel", "parallel"),
        name="norm_mod",
    )(x, g.reshape(1, d), sh, sc)


def _norm_router_kernel(x_ref, g_ref, sh_ref, sc_ref, wrt_ref, h_ref, lg_ref):
    h = _norm_mod_rows(x_ref[...], g_ref[...], sh_ref[...], sc_ref[...])
    h_ref[...] = h.astype(h_ref.dtype)
    lg_ref[...] = _dot3(wrt_ref[...], h, NT)


def _norm_router(x, g, sh, sc, w_router_t):
    b, t, d = x.shape
    e = w_router_t.shape[0]
    tr = _tile(t, 256)
    nt = t // tr
    return pl.pallas_call(
        _norm_router_kernel,
        grid=(b, nt),
        in_specs=[
            pl.BlockSpec((None, tr, d), lambda bi, i: (bi, i, 0)),
            pl.BlockSpec((1, d), lambda bi, i: (0, 0)),
            pl.BlockSpec((None, 1, d), lambda bi, i: (bi, 0, 0)),
            pl.BlockSpec((None, 1, d), lambda bi, i: (bi, 0, 0)),
            pl.BlockSpec((e, d), lambda bi, i: (0, 0)),
        ],
        out_specs=[
            pl.BlockSpec((None, tr, d), lambda bi, i: (bi, i, 0)),
            pl.BlockSpec((e, tr), lambda bi, i: (0, bi * nt + i)),
        ],
        out_shape=[jax.ShapeDtypeStruct((b, t, d), BF16), jax.ShapeDtypeStruct((e, b * t), F32)],
        compiler_params=_cp("parallel", "parallel"),
        name="norm_router",
    )(x, g.reshape(1, d), sh, sc, w_router_t)


def _final_norm_kernel(x_ref, g_ref, o_ref):
    x = x_ref[...]
    o_ref[...] = x * lax.rsqrt(jnp.mean(x * x, axis=-1, keepdims=True) + EPS) * g_ref[...]


def _final_norm(x, g):
    b, t, d = x.shape
    tr = _tile(t, 256, 8)
    return pl.pallas_call(
        _final_norm_kernel,
        grid=(b, t // tr),
        in_specs=[pl.BlockSpec((None, tr, d), lambda bi, i: (bi, i, 0)), pl.BlockSpec((1, d), lambda bi, i: (0, 0))],
        out_specs=pl.BlockSpec((None, tr, d), lambda bi, i: (bi, i, 0)),
        out_shape=jax.ShapeDtypeStruct((b, t, d), F32),
        compiler_params=_cp("parallel", "parallel"),
        name="final_norm",
    )(x, g.reshape(1, d))


def _mm_kernel(a_ref, w_ref, o_ref, wb_ref):
    @pl.when(pl.program_id(1) == 0)
    def _():
        wb_ref[...] = w_ref[...].astype(BF16)

    o_ref[...] = _dot(a_ref[...], wb_ref[...]).astype(o_ref.dtype)


def _mm(a, w, layer, col0, ncols, tn, out_dtype):
    m, k = a.shape
    assert ncols % tn == 0 and col0 % tn == 0
    tm = _tile(m, 1024, 8)
    j0 = col0 // tn
    return pl.pallas_call(
        _mm_kernel,
        grid=(ncols // tn, m // tm),
        in_specs=[
            pl.BlockSpec((tm, k), lambda j, i: (i, 0)),
            pl.BlockSpec((None, k, tn), lambda j, i: (layer, 0, j0 + j)),
        ],
        out_specs=pl.BlockSpec((tm, tn), lambda j, i: (i, j)),
        out_shape=jax.ShapeDtypeStruct((m, ncols), out_dtype),
        scratch_shapes=[pltpu.VMEM((k, tn), BF16)],
        compiler_params=_cp("arbitrary", "arbitrary"),
        name="mm_in",
    )(a, w)


def _mm_res_kernel(a_ref, w_ref, x_ref, gate_ref, o_ref, wb_ref):
    @pl.when(pl.program_id(1) == 0)
    def _():
        wb_ref[...] = w_ref[...].astype(BF16)

    o_ref[...] = x_ref[...] + gate_ref[...] * _dot(a_ref[...], wb_ref[...])


def _mm_res(a, w, layer, x, gate):
    b, t, n = x.shape
    m, k = a.shape
    tn = _tile(n, 1024)
    tm = _tile(t, 1024, 8)
    per_b = t // tm
    out = pl.pallas_call(
        _mm_res_kernel,
        grid=(n // tn, m // tm),
        in_specs=[
            pl.BlockSpec((tm, k), lambda j, i: (i, 0)),
            pl.BlockSpec((None, k, tn), lambda j, i: (layer, 0, j)),
            pl.BlockSpec((tm, tn), lambda j, i: (i, j)),
            pl.BlockSpec((None, 1, tn), lambda j, i: (i // per_b, 0, j)),
        ],
        out_specs=pl.BlockSpec((tm, tn), lambda j, i: (i, j)),
        out_shape=jax.ShapeDtypeStruct((m, n), F32),
        scratch_shapes=[pltpu.VMEM((k, tn), BF16)],
        compiler_params=_cp("arbitrary", "arbitrary"),
        name="mm_out_res",
    )(a, w, x.reshape(m, n), gate)
    return out.reshape(b, t, n)


def _rope_kernel(z_ref, c_ref, sa_ref, sb_ref, o_ref):
    cos, sa, sb = c_ref[...], sa_ref[...], sb_ref[...]
    for j in range(o_ref.shape[1] // LANE):
        x = z_ref[:, j * LANE:(j + 1) * LANE].astype(F32)
        y = x * cos + pltpu.roll(x, LANE - 16, 1) * sa + pltpu.roll(x, 16, 1) * sb
        o_ref[:, j * LANE:(j + 1) * LANE] = y.astype(o_ref.dtype)


def _rope_tables(n):
    pos = jnp.arange(n)
    rows, cols = (pos // GRID_W).astype(F32), (pos % GRID_W).astype(F32)
    half = DA_QK_DIM // 2
    inv_freq = jnp.power(ROPE_BASE, -jnp.arange(0, half, 2, dtype=F32) / half)
    ang_r, ang_c = rows[:, None] * inv_freq[None, :], cols[:, None] * inv_freq[None, :]
    cos64 = jnp.concatenate([jnp.cos(ang_r)] * 2 + [jnp.cos(ang_c)] * 2, axis=-1)
    sin64 = jnp.concatenate([jnp.sin(ang_r)] * 2 + [jnp.sin(ang_c)] * 2, axis=-1)
    cos, sin = jnp.tile(cos64, (1, 2)), jnp.tile(sin64, (1, 2))
    first = (jnp.arange(LANE) % 32) < 16
    return cos, jnp.where(first, -sin, 0.0), jnp.where(first, 0.0, sin)


def _rope(z, ncols, tables):
    b, n, _ = z.shape
    cos, sa, sb = tables
    tr = _tile(n, 512, 8)
    tab = pl.BlockSpec((tr, LANE), lambda bi, i: (i, 0))
    return pl.pallas_call(
        _rope_kernel,
        grid=(b, n // tr),
        in_specs=[pl.BlockSpec((None, tr, ncols), lambda bi, i: (bi, i, 0)), tab, tab, tab],
        out_specs=pl.BlockSpec((None, tr, ncols), lambda bi, i: (bi, i, 0)),
        out_shape=jax.ShapeDtypeStruct((b, n, ncols), BF16),
        compiler_params=_cp("parallel", "parallel"),
        name="rope",
    )(z, cos, sa, sb)


def _diff_attn_kernel(*refs, has_x, post_scale):
    if has_x:
        lam_ref, g_ref, q_ref, kc_ref, vc_ref, kx_ref, vx_ref, o_ref = refs
    else:
        lam_ref, g_ref, q_ref, kc_ref, vc_ref, o_ref = refs
    q = q_ref[...].astype(F32) * (DA_QK_DIM ** -0.5)
    lane = lax.broadcasted_iota(I32, q.shape, 1)
    if has_x:
        k = jnp.concatenate([kc_ref[...], kx_ref[...]], axis=0)
        v = jnp.concatenate([vc_ref[...], vx_ref[...]], axis=0)
    else:
        k, v = kc_ref[...], vc_ref[...]

    def attend(qm):
        s = _dot(qm, k, NT)
        p = jnp.exp(s - jnp.max(s, axis=-1, keepdims=True))
        return _dot(p.astype(BF16), v) / jnp.sum(p, axis=-1, keepdims=True)

    o = attend(jnp.where(lane < DA_QK_DIM, q, 0.0).astype(BF16)) \
        - lam_ref[...] * attend(jnp.where(lane >= DA_QK_DIM, q, 0.0).astype(BF16))
    y = o * lax.rsqrt(jnp.mean(o * o, axis=-1, keepdims=True) + EPS) * g_ref[...] * post_scale
    o_ref[...] = y.astype(o_ref.dtype)


def _diff_attn(q_arr, zc, kx_arr, vx_arr, lam, subln_g, lam_init, has_x):
    b, tq_all, _ = q_arr.shape
    c = zc.shape[1]
    tq = _tile(tq_all, 512, 8)
    q_blk0, kc_blk0, vc_blk0 = 0, IN_WIDTHS[0] // LANE, 2 * IN_WIDTHS[0] // LANE
    kx_blk0 = kc_blk0
    in_specs = [
        pl.BlockSpec((1, 1), lambda bi, h, i: (0, 0)),
        pl.BlockSpec((1, DA_V_DIM), lambda bi, h, i: (0, 0)),
        pl.BlockSpec((None, tq, LANE), lambda bi, h, i: (bi, i, q_blk0 + h)),
        pl.BlockSpec((None, c, LANE), lambda bi, h, i: (bi, 0, kc_blk0 + h)),
        pl.BlockSpec((None, c, LANE), lambda bi, h, i: (bi, 0, vc_blk0 + h)),
    ]
    args = [lam.reshape(1, 1), subln_g.reshape(1, DA_V_DIM), q_arr, zc, zc]
    if has_x:
        n = kx_arr.shape[1]
        in_specs += [
            pl.BlockSpec((None, n, LANE), lambda bi, h, i: (bi, 0, kx_blk0 + h)),
            pl.BlockSpec((None, n, LANE), lambda bi, h, i: (bi, 0, vc_blk0 + h)),
        ]
        args += [kx_arr, vx_arr]
    return pl.pallas_call(
        functools.partial(_diff_attn_kernel, has_x=has_x, post_scale=1.0 - lam_init),
        grid=(b, DA_HEADS, tq_all // tq),
        in_specs=in_specs,
        out_specs=pl.BlockSpec((None, tq, LANE), lambda bi, h, i: (bi, i, h)),
        out_shape=jax.ShapeDtypeStruct((b, tq_all, DA_HEADS * DA_V_DIM), BF16),
        compiler_params=_cp("parallel", "parallel", "arbitrary"),
        name="diff_attn_x" if has_x else "diff_attn_c",
    )(*args)


def _group_cols(x, g):
    x = x.astype(F32)
    lane = lax.broadcasted_iota(I32, x.shape, 1)
    own = jnp.where(_lane_group(lane, WB_DIM) == g, x, pltpu.roll(x, WB_DIM, 1))
    return jnp.concatenate([own, own], axis=1).astype(BF16)


def _win_attn_kernel(*refs, has_x, n_tok):
    if has_x:
        sink_ref, q_ref, kc_ref, vc_ref, kp_ref, k0_ref, kn_ref, vp_ref, v0_ref, vn_ref, o_ref = refs
    else:
        sink_ref, q_ref, kc_ref, vc_ref, o_ref = refs
    i = pl.program_id(1)
    rep = WB_HEADS // WB_KV_HEADS
    gw = rep * WB_DIM
    q_all = q_ref[...].astype(F32) * (WB_DIM ** -0.5)
    blk = q_all.shape[0]
    lane = lax.broadcasted_iota(I32, (blk, gw), 1)
    if has_x:
        k_in = jnp.concatenate([kp_ref[...], k0_ref[...], kn_ref[...], kc_ref[...]], axis=0)
        v_in = jnp.concatenate([vp_ref[...], v0_ref[...], vn_ref[...], vc_ref[...]], axis=0)
        q_pos = i * blk + (lax.broadcasted_iota(I32, (rep * blk, 3 * blk), 0) & (blk - 1))
        k_pos = (i - 1) * blk + lax.broadcasted_iota(I32, (rep * blk, 3 * blk), 1)
        valid = (jnp.abs(q_pos - k_pos) <= WB_WINDOW) & (k_pos >= 0) & (k_pos < n_tok)
    else:
        k_in, v_in = kc_ref[...], vc_ref[...]
    outs = []
    for g in range(WB_KV_HEADS):
        q = q_all[:, g * gw:(g + 1) * gw]
        k, v = _group_cols(k_in, g), _group_cols(v_in, g)
        mine = [_lane_group(lane, WB_DIM) == r for r in range(rep)]
        qs = jnp.concatenate([jnp.where(mine[r], q, 0.0) for r in range(rep)], axis=0).astype(BF16)
        sink = jnp.concatenate([jnp.broadcast_to(sink_ref[:, (g * rep + r) * WB_DIM:(g * rep + r) * WB_DIM + 1], (blk, 1))
                                for r in range(rep)], axis=0)
        s = _dot(qs, k, NT)
        if has_x:
            s = jnp.concatenate([jnp.where(valid, s[:, :3 * blk], NEG_INF), s[:, 3 * blk:]], axis=1)
        m = jnp.maximum(jnp.max(s, axis=-1, keepdims=True), sink)
        pr = jnp.exp(s - m)
        l = jnp.sum(pr, axis=-1, keepdims=True) + jnp.exp(sink - m)
        o = _dot(pr.astype(BF16), v) / l
        acc = jnp.zeros(q.shape, F32)
        for r in range(rep):
            acc = acc + jnp.where(mine[r], o[r * blk:(r + 1) * blk], 0.0)
        outs.append(acc)
    o_ref[...] = jnp.concatenate(outs, axis=1).astype(o_ref.dtype)


def _win_attn(q_arr, zc, kx_arr, vx_arr, sink, has_x):
    b, tq_all, _ = q_arr.shape
    c = zc.shape[1]
    blk = WB_BLOCK
    nb = tq_all // blk
    qw = WB_HEADS * WB_DIM
    sink_arr = jnp.repeat(sink.astype(F32), WB_DIM).reshape(1, qw)
    k_blk, v_blk = qw // LANE, qw // LANE + 1
    in_specs = [
        pl.BlockSpec((1, qw), lambda bi, i: (0, 0)),
        pl.BlockSpec((None, blk, qw), lambda bi, i: (bi, i, 0)),
        pl.BlockSpec((None, c, LANE), lambda bi, i: (bi, 0, k_blk)),
        pl.BlockSpec((None, c, LANE), lambda bi, i: (bi, 0, v_blk)),
    ]
    args = [sink_arr, q_arr, zc, zc]
    n_tok = 0
    if has_x:
        n_tok = kx_arr.shape[1]
        prev = lambda i: jnp.maximum(i - 1, 0)
        nxt = lambda i: jnp.minimum(i + 1, nb - 1)
        for arr, cb in ((kx_arr, k_blk), (vx_arr, v_blk)):
            in_specs += [
                pl.BlockSpec((None, blk, LANE), lambda bi, i, cb=cb: (bi, prev(i), cb)),
                pl.BlockSpec((None, blk, LANE), lambda bi, i, cb=cb: (bi, i, cb)),
                pl.BlockSpec((None, blk, LANE), lambda bi, i, cb=cb: (bi, nxt(i), cb)),
            ]
            args += [arr, arr, arr]
    return pl.pallas_call(
        functools.partial(_win_attn_kernel, has_x=has_x, n_tok=n_tok),
        grid=(b, nb),
        in_specs=in_specs,
        out_specs=pl.BlockSpec((None, blk, qw), lambda bi, i: (bi, i, 0)),
        out_shape=jax.ShapeDtypeStruct((b, tq_all, qw), BF16),
        compiler_params=_cp("parallel", "arbitrary"),
        name="win_attn_x" if has_x else "win_attn_c",
    )(*args)


NA_QROWS = 4
NA_KROWS = 12


def _na_kernel(*refs, has_x, n_blocks):
    if has_x:
        q_ref, kc_ref, vc_ref, k_ref, v_ref, t2_ref, o_ref = refs
    else:
        q_ref, kc_ref, vc_ref, o_ref = refs
    tq = q_ref.shape[0]
    lane = lax.broadcasted_iota(I32, (tq, LANE), 1)
    per = LANE // NA_DIM
    if has_x:
        p = pl.program_id(1)
        is_first, is_last = p == 0, p == n_blocks - 1
        n_rows = n_blocks * NA_QROWS
        start = jnp.where(is_first, 0, jnp.where(is_last, n_rows - NA_KROWS, p * NA_QROWS - NA_ROWS // 2))
        off = pl.multiple_of(start * GRID_W, GRID_W)
        nk = NA_KROWS * GRID_W
        row_i = lax.broadcasted_iota(I32, (tq, nk), 0)
        key_i = lax.broadcasted_iota(I32, (tq, nk), 1)
        qi, col = _lane_group(row_i, GRID_W), row_i & (GRID_W - 1)
        kp, w = _lane_group(key_i, GRID_W), key_i & (GRID_W - 1)
        lo = jnp.where(is_first, 0, jnp.where(is_last, NA_KROWS - NA_ROWS, qi))
        col_start = jnp.clip(col - NA_COLS // 2, 0, GRID_W - NA_COLS)
        valid = (w >= col_start) & (w < col_start + NA_COLS) & (kp >= lo) & (kp < lo + NA_ROWS)
        n_pair = t2_ref.shape[1]
        first_pair = [jnp.where(is_first, NA_ROWS - 1 - i, jnp.where(is_last, -1 - i, NA_ROWS // 2 - 1 - i))
                      for i in range(NA_QROWS)]
    for j in range(NA_HEADS // per):
        cols = slice(j * LANE, (j + 1) * LANE)
        q = q_ref[:, cols].astype(F32) * (NA_DIM ** -0.5)
        if has_x:
            k = jnp.concatenate([k_ref[pl.ds(off, nk), cols], kc_ref[:, cols]], axis=0)
            v = jnp.concatenate([v_ref[pl.ds(off, nk), cols], vc_ref[:, cols]], axis=0)
        else:
            k, v = kc_ref[:, cols], vc_ref[:, cols]
        acc = jnp.zeros(q.shape, F32)
        for hh in range(per):
            mine = _lane_group(lane, NA_DIM) == hh
            qh = jnp.where(mine, q, 0.0).astype(BF16)
            s = _dot(qh, k, NT)
            if has_x:
                head = j * per + hh
                bias = jnp.concatenate([
                    jnp.concatenate([t2_ref[head, jnp.clip(first_pair[i] + 2 * m, 0, n_pair - 1)]
                                     for m in range(NA_KROWS // 2)], axis=1)
                    for i in range(NA_QROWS)], axis=0)
                s = jnp.concatenate([jnp.where(valid, s[:, :nk] + bias, NEG_INF), s[:, nk:]], axis=1)
            m = jnp.max(s, axis=-1, keepdims=True)
            pr = jnp.exp(s - m)
            l = jnp.sum(pr, axis=-1, keepdims=True)
            acc = acc + jnp.where(mine, _dot(pr.astype(BF16), v) / l, 0.0)
        o_ref[:, cols] = acc.astype(o_ref.dtype)


def _na_bias_table(rpb):
    r = rpb.astype(F32)
    edge = GRID_W - NA_COLS
    ext = jnp.concatenate([jnp.repeat(r[..., :1], edge, -1), r, jnp.repeat(r[..., -1:], edge, -1)], axis=-1)
    by_col = jnp.stack([ext[..., GRID_W - 1 - c:2 * GRID_W - 1 - c] for c in range(GRID_W)], axis=-2)
    return jnp.concatenate([by_col[:, :-1], by_col[:, 1:]], axis=-1)


def _na_attn(q_arr, zc, zx, bias_tab, has_x):
    b, tq_all, _ = q_arr.shape
    c = zc.shape[1]
    hw = NA_HEADS * NA_DIM
    tq = NA_QROWS * GRID_W if has_x else _tile(tq_all, 256, 8)
    nq = tq_all // tq
    in_specs = [
        pl.BlockSpec((None, tq, hw), lambda bi, r: (bi, r, 0)),
        pl.BlockSpec((None, c, hw), lambda bi, r: (bi, 0, 1)),
        pl.BlockSpec((None, c, hw), lambda bi, r: (bi, 0, 2)),
    ]
    args = [q_arr, zc, zc]
    if has_x:
        n = zx.shape[1]
        assert tq_all % tq == 0 and nq * NA_QROWS >= NA_KROWS
        in_specs += [
            pl.BlockSpec((None, n, hw), lambda bi, r: (bi, 0, 1)),
            pl.BlockSpec((None, n, hw), lambda bi, r: (bi, 0, 2)),
            pl.BlockSpec(bias_tab.shape, lambda bi, r: (0, 0, 0, 0)),
        ]
        args += [zx, zx, bias_tab]
    return pl.pallas_call(
        functools.partial(_na_kernel, has_x=has_x, n_blocks=nq),
        grid=(b, nq),
        in_specs=in_specs,
        out_specs=pl.BlockSpec((None, tq, hw), lambda bi, r: (bi, r, 0)),
        out_shape=jax.ShapeDtypeStruct((b, tq_all, hw), BF16),
        compiler_params=_cp("parallel", "arbitrary"),
        name="na_attn_x" if has_x else "na_attn_c",
    )(*args)


def _s5_kernel(*refs, ctx_out):
    if ctx_out:
        (uc_ref, ux_ref, d_ref, b1_ref, b2_ref, a1_ref, a2_ref, a3_ref, cb_ref, yc_ref, yx_ref,
         kbig, bb1, bb2, xc, xx, s1c, s2c, s1x, s2x, hc, hx) = refs
    else:
        (uc_ref, ux_ref, d_ref, b1_ref, b2_ref, a1_ref, a2_ref, a3_ref, cb_ref, yx_ref,
         kbig, bb1, bb2, xc, xx, s1c, s2c, s1x, s2x, hc, hx) = refs
        yc_ref = None
    dirn = pl.program_id(1)
    ell, hch = S5_CHUNK, S5_GROUP_CH
    gq = LANE // hch
    nb = ux_ref.shape[0]
    streams = ((uc_ref, xc, s1c, s2c, hc, yc_ref, uc_ref.shape[1] // ell),
               (ux_ref, xx, s1x, s2x, hx, yx_ref, ux_ref.shape[1] // ell))

    for i in range(ell):
        for j in range(ell):
            kbig[i * LANE:(i + 1) * LANE, j * LANE:(j + 1) * LANE] = d_ref[j - i + ell - 1]
    bb1[...] = jnp.zeros(bb1.shape, BF16)
    bb2[...] = jnp.zeros(bb2.shape, BF16)
    for i in range(ell):
        for gl in range(gq):
            r0 = i * LANE + gl * hch
            bb1[r0:r0 + hch, gl * LANE:(gl + 1) * LANE] = b1_ref[gl, i * hch:(i + 1) * hch, :].astype(BF16)
            bb2[r0:r0 + hch, gl * LANE:(gl + 1) * LANE] = b2_ref[gl, i * hch:(i + 1) * hch, :].astype(BF16)

    @pl.when(dirn == 0)
    def _():
        for u_ref, x_s, _, _, _, _, nc in streams:
            for b in range(nb):
                for i in range(ell):
                    x_s[b * nc:(b + 1) * nc, i * LANE:(i + 1) * LANE] = u_ref[b, pl.ds(i, nc, stride=ell), :].astype(BF16)

    for _, x_s, s1, s2, _, _, _ in streams:
        loc1 = _dot(x_s[...], bb1[...])
        loc2 = _dot(x_s[...], bb2[...])
        for k in range(gq):
            s1[k] = loc1[:, k * LANE:(k + 1) * LANE]
            s2[k] = loc2[:, k * LANE:(k + 1) * LANE]

    def lane_blocks(a_ref):
        return [jnp.broadcast_to(a_ref[:, k * LANE:(k + 1) * LANE], (nb, LANE)) for k in range(gq)]

    a1, a2, a3 = lane_blocks(a1_ref), lane_blocks(a2_ref), lane_blocks(a3_ref)
    zero = tuple(jnp.zeros((nb, LANE), F32) for _ in range(gq))
    carry = (zero, zero)
    for _, _, s1, s2, h, _, nc in streams:
        def step(t, vs, s1=s1, s2=s2, h=h, nc=nc):
            v1, v2 = vs
            c = jnp.where(dirn == 0, t, nc - 1 - t)
            rows = pl.ds(c, nb, stride=nc)
            n1, n2 = [], []
            for k in range(gq):
                h[k, rows, :] = v1[k]
                n1.append(a1[k] * v1[k] + a2[k] * v2[k] + s1[k, rows, :])
                n2.append(a1[k] * v2[k] + a3[k] * v1[k] + s2[k, rows, :])
            return tuple(n1), tuple(n2)

        carry = lax.fori_loop(0, nc, step, carry, unroll=4)

    for _, x_s, _, _, h, y_ref, nc in streams:
        if y_ref is None:
            continue
        h_all = jnp.concatenate([h[k] for k in range(gq)], axis=1).astype(BF16)
        y = _dot(x_s[...], kbig[...]) + _dot(h_all, cb_ref[...])

        @pl.when(dirn == 0)
        def _():
            for b in range(nb):
                for j in range(ell):
                    y_ref[b, pl.ds(j, nc, stride=ell), :] = y[b * nc:(b + 1) * nc, j * LANE:(j + 1) * LANE]

        @pl.when(dirn == 1)
        def _():
            for b in range(nb):
                for j in range(ell):
                    y_ref[b, pl.ds(j, nc, stride=ell), :] += y[b * nc:(b + 1) * nc, j * LANE:(j + 1) * LANE]


def _s5_tables(a_re, a_im, log_step, b_re, b_im, c_re, c_im):
    ell, hch, p = S5_CHUNK, S5_GROUP_CH, S5_STATE
    lam = lax.complex(a_re.astype(F32), a_im.astype(F32))
    lam_dt = lam * jnp.exp(log_step.astype(F32))[:, :, None]
    a_bar = jnp.exp(lam_dt)
    b_bar = ((a_bar - 1.0) / lam)[:, :, :, None] * lax.complex(b_re.astype(F32), b_im.astype(F32))
    c_mat = lax.complex(c_re.astype(F32), c_im.astype(F32))
    gq = LANE // hch
    nq = S5_GROUPS // gq
    hp = lax.Precision.HIGHEST
    steps = jnp.arange(ell + 1, dtype=F32)
    apow = jnp.exp(lam_dt[:, :, None, :] * steps[None, None, :, None])
    apow_re, apow_im = apow.real, apow.imag
    bt_re, bt_im = jnp.swapaxes(b_bar.real, 2, 3), jnp.swapaxes(b_bar.imag, 2, 3)
    c_re_, c_im_ = c_mat.real, c_mat.imag
    eye = jnp.eye(gq, dtype=F32)

    def cmul(ar, ai, br, bi):
        return ar * br - ai * bi, ar * bi + ai * br

    m_re, m_im = cmul(apow_re[:, :, :ell, None, :], apow_im[:, :, :ell, None, :], bt_re[:, :, None], bt_im[:, :, None])
    kk = (jnp.einsum('dgkip,dgop->dgkio', m_re, c_re_, precision=hp)
          - jnp.einsum('dgkip,dgop->dgkio', m_im, c_im_, precision=hp))
    zeros = jnp.zeros_like(kk[0, :, 1:])
    signed = jnp.stack([jnp.concatenate([zeros, kk[0]], axis=1),
                        jnp.concatenate([jnp.flip(kk[1], axis=1), zeros], axis=1)])
    signed = signed.reshape(2, nq, gq, 2 * ell - 1, hch, hch)
    dblk = jnp.einsum('dqgkio,gh->dqkgiho', signed, eye).reshape(2, nq, 2 * ell - 1, LANE, LANE).astype(BF16)
    i_idx = jnp.arange(ell)
    e_re = jnp.stack([apow_re[0][:, ell - 1 - i_idx], apow_re[1][:, i_idx]])
    e_im = jnp.stack([apow_im[0][:, ell - 1 - i_idx], apow_im[1][:, i_idx]])
    bp_re, bp_im = cmul(e_re[:, :, :, None, :], e_im[:, :, :, None, :], bt_re[:, :, None], bt_im[:, :, None])
    bp_re, bp_im = bp_re.reshape(2, S5_GROUPS, ell * hch, p), bp_im.reshape(2, S5_GROUPS, ell * hch, p)
    b1 = jnp.concatenate([bp_re, bp_im], axis=-1)
    b2 = jnp.concatenate([bp_im, bp_re], axis=-1)
    al_re, al_im = apow_re[:, :, ell, :], apow_im[:, :, ell, :]
    lanes = lambda u, v: jnp.concatenate([u, v], axis=-1).reshape(2, nq, 1, gq * 2 * p)
    a1, a2, a3 = lanes(al_re, al_re), lanes(-al_im, al_im), lanes(al_im, -al_im)
    f_re = jnp.stack([apow_re[0][:, 1 + i_idx], apow_re[1][:, ell - i_idx]])
    f_im = jnp.stack([apow_im[0][:, 1 + i_idx], apow_im[1][:, ell - i_idx]])
    g_re, g_im = cmul(c_re_[:, :, None], c_im_[:, :, None], f_re[:, :, :, None, :], f_im[:, :, :, None, :])
    cp = jnp.concatenate([jnp.transpose(g_re, (0, 1, 4, 2, 3)), -jnp.transpose(g_im, (0, 1, 4, 2, 3))], axis=2)
    cp = cp.reshape(2, nq, gq, 2 * p, ell, hch)
    cbig = jnp.einsum('dqgsjo,gh->dqgsjho', cp, eye).reshape(2, nq, gq * 2 * p, ell * LANE).astype(BF16)
    return dblk, b1, b2, a1, a2, a3, cbig


def _s5_scan(zx, zc, tables, ctx_out):
    b, n, _ = zx.shape
    c = zc.shape[1]
    ell, hch = S5_CHUNK, S5_GROUP_CH
    gq = LANE // hch
    nq = S5_GROUPS // gq
    p2 = 2 * S5_STATE
    ncx, ncc = n // ell, c // ell
    assert n % ell == 0 and c % ell == 0 and ncc % 16 == 0
    dblk, b1, b2, a1, a2, a3, cbig = tables
    u0 = 0
    sw = gq * p2
    xw = ell * LANE
    a_spec = pl.BlockSpec((None, None, 1, sw), lambda q, d: (d, q, 0, 0))
    b_spec = pl.BlockSpec((None, gq, ell * hch, p2), lambda q, d: (d, q, 0, 0))
    y_specs = [pl.BlockSpec((b, n, LANE), lambda q, d: (0, 0, q))]
    y_shapes = [jax.ShapeDtypeStruct((b, n, S5_WIDTH), F32)]
    if ctx_out:
        y_specs = [pl.BlockSpec((b, c, LANE), lambda q, d: (0, 0, q))] + y_specs
        y_shapes = [jax.ShapeDtypeStruct((b, c, S5_WIDTH), F32)] + y_shapes
    out = pl.pallas_call(
        functools.partial(_s5_kernel, ctx_out=ctx_out),
        grid=(nq, 2),
        in_specs=[
            pl.BlockSpec((b, c, LANE), lambda q, d: (0, 0, u0 + q)),
            pl.BlockSpec((b, n, LANE), lambda q, d: (0, 0, u0 + q)),
            pl.BlockSpec((None, None, 2 * ell - 1, LANE, LANE), lambda q, d: (d, q, 0, 0, 0)),
            b_spec, b_spec, a_spec, a_spec, a_spec,
            pl.BlockSpec((None, None, sw, xw), lambda q, d: (d, q, 0, 0)),
        ],
        out_specs=y_specs,
        out_shape=y_shapes,
        scratch_shapes=[
            pltpu.VMEM((xw, xw), BF16), pltpu.VMEM((xw, sw), BF16), pltpu.VMEM((xw, sw), BF16),
            pltpu.VMEM((b * ncc, xw), BF16), pltpu.VMEM((b * ncx, xw), BF16),
            pltpu.VMEM((gq, b * ncc, p2), F32), pltpu.VMEM((gq, b * ncc, p2), F32),
            pltpu.VMEM((gq, b * ncx, p2), F32), pltpu.VMEM((gq, b * ncx, p2), F32),
            pltpu.VMEM((gq, b * ncc, p2), F32), pltpu.VMEM((gq, b * ncx, p2), F32),
        ],
        compiler_params=_cp("parallel", "arbitrary"),
        name="s5_scan",
    )(zc, zx, dblk, b1, b2, a1, a2, a3, cbig)
    return (out[0], out[1]) if ctx_out else (None, out[0])


def _s5_out_kernel(y_ref, u_ref, d_ref, w_ref, o_ref):
    g = jax.nn.gelu(y_ref[...] + d_ref[...] * u_ref[...])
    o_ref[...] = (g * jax.nn.sigmoid(_dot(g.astype(BF16), w_ref[...].astype(BF16)))).astype(o_ref.dtype)


def _s5_out(y, u, d_skip, w_glu):
    b, t, w = y.shape
    tr = _tile(t, 512, 8)
    row = pl.BlockSpec((None, tr, w), lambda bi, i: (bi, i, 0))
    return pl.pallas_call(
        _s5_out_kernel,
        grid=(b, t // tr),
        in_specs=[row, row, pl.BlockSpec((1, w), lambda bi, i: (0, 0)), pl.BlockSpec((w, w), lambda bi, i: (0, 0))],
        out_specs=row,
        out_shape=jax.ShapeDtypeStruct((b, t, w), BF16),
        compiler_params=_cp("parallel", "parallel"),
        name="s5_out",
    )(y, u, d_skip.reshape(1, w).astype(F32), w_glu)


def _merge_kernel(h_ref, ya_ref, yb_ref, yc_ref, yd_ref, wg0_ref, wg1_ref, wg2_ref, wg3_ref, wb_ref, o_ref, wgs_ref, wbs_ref):
    @pl.when(pl.program_id(1) == 0)
    def _():
        for n, wg_ref in enumerate((wg0_ref, wg1_ref, wg2_ref, wg3_ref)):
            wgs_ref[n] = wg_ref[...].astype(BF16)
        wbs_ref[...] = wb_ref[...].astype(BF16)

    h = h_ref[...]
    acc = None
    for n, y_ref in enumerate((ya_ref, yb_ref, yc_ref, yd_ref)):
        term = jax.nn.sigmoid(_dot(h, wgs_ref[n])) * _dot(y_ref[...], wbs_ref[n])
        acc = term if acc is None else acc + term
    o_ref[...] = acc.astype(o_ref.dtype)


def _merge(h, branches, w_in, w_branch, layer):
    m, d = h.shape
    bw = branches[0].shape[1]
    tn = 256
    tm = _tile(m, 1024, 8)
    g0 = N_MIX_IN // tn
    per = d // tn
    assert N_MIX_IN % tn == 0 and d % tn == 0
    wg_specs = [pl.BlockSpec((None, d, tn), lambda j, i, n=n: (layer, 0, g0 + n * per + j)) for n in range(N_BRANCH)]
    y_spec = pl.BlockSpec((tm, bw), lambda j, i: (i, 0))
    return pl.pallas_call(
        _merge_kernel,
        grid=(d // tn, m // tm),
        in_specs=[pl.BlockSpec((tm, d), lambda j, i: (i, 0)), y_spec, y_spec, y_spec, y_spec] + wg_specs
                 + [pl.BlockSpec((None, N_BRANCH, bw, tn), lambda j, i: (layer, 0, 0, j))],
        out_specs=pl.BlockSpec((tm, tn), lambda j, i: (i, j)),
        out_shape=jax.ShapeDtypeStruct((m, d), BF16),
        scratch_shapes=[pltpu.VMEM((N_BRANCH, d, tn), BF16), pltpu.VMEM((N_BRANCH, bw, tn), BF16)],
        compiler_params=_cp("arbitrary", "arbitrary"),
        name="merge",
    )(h, *branches, w_in, w_in, w_in, w_in, w_branch)


def _topk_kernel(lg_ref, slot_ref, aff_ref, tri_ref, *, cap):
    e, t = lg_ref.shape

    @pl.when(pl.program_id(0) == 0)
    def _():
        rows = 256 if t % 256 == 0 else t
        for r0 in range(0, t, rows):
            ri = lax.broadcasted_iota(I32, (rows, t), 0) + r0
            ci = lax.broadcasted_iota(I32, (rows, t), 1)
            tri_ref[r0:r0 + rows, :] = jnp.where(ri < ci, 1.0, 0.0).astype(BF16)

    lg = lg_ref[...]
    ex = jnp.exp(lg - jnp.max(lg, axis=0, keepdims=True))
    aff = ex / jnp.sum(ex, axis=0, keepdims=True)
    aff_ref[...] = aff
    bits = pltpu.bitcast(aff, I32)

    def search(_, carry):
        lo, hi = carry
        mid = lo + lax.shift_right_logical(hi - lo + 1, 1)
        ok = _count(bits >= mid, 1) >= cap
        return jnp.where(ok, mid, lo), jnp.where(ok, hi, mid - 1)

    lo0 = jnp.zeros((e, 1), I32)
    hi0 = jnp.full((e, 1), 0x7F800000, I32)
    thr, _ = lax.fori_loop(0, 32, search, (lo0, hi0))
    gt = bits > thr
    eq = bits == thr
    need = cap - _count(gt, 1)
    tri = tri_ref[...]
    eq_before = _dot(jnp.where(eq, 1.0, 0.0).astype(BF16), tri)
    sel = gt | (eq & (eq_before < need))
    sel_before = _dot(jnp.where(sel, 1.0, 0.0).astype(BF16), tri)
    slot_ref[...] = jnp.where(sel, sel_before.astype(I32), -1)


def _topk(logits_t, b, cap):
    e, bt = logits_t.shape
    t = bt // b
    return pl.pallas_call(
        functools.partial(_topk_kernel, cap=cap),
        grid=(b,),
        in_specs=[pl.BlockSpec((e, t), lambda bi: (0, bi))],
        out_specs=[pl.BlockSpec((None, e, t), lambda bi: (bi, 0, 0)), pl.BlockSpec((None, e, t), lambda bi: (bi, 0, 0))],
        out_shape=[jax.ShapeDtypeStruct((b, e, t), I32), jax.ShapeDtypeStruct((b, e, t), F32)],
        scratch_shapes=[pltpu.VMEM((t, t), BF16)],
        compiler_params=_cp("arbitrary"),
        name="route_topk",
    )(logits_t)


def _gather_kernel(slot_ref, aff_ref, h_ref, xs_ref, gate_ref, *, cap):
    t = h_ref.shape[0]
    pick = lax.broadcasted_iota(I32, (cap, t), 0) == slot_ref[...]
    xs_ref[...] = _dot(jnp.where(pick, 1.0, 0.0).astype(BF16), h_ref[...]).astype(xs_ref.dtype)
    gate_ref[...] = jnp.sum(jnp.where(pick, aff_ref[...], 0.0), axis=1, keepdims=True)


def _gather(slot, aff, h, cap):
    b, e, t = slot.shape
    d = h.shape[2]
    row = pl.BlockSpec((None, None, 1, t), lambda bi, ei: (bi, ei, 0, 0))
    return pl.pallas_call(
        functools.partial(_gather_kernel, cap=cap),
        grid=(b, e),
        in_specs=[row, row, pl.BlockSpec((None, t, d), lambda bi, ei: (bi, 0, 0))],
        out_specs=[pl.BlockSpec((None, cap, d), lambda bi, ei: (ei, bi, 0)),
                   pl.BlockSpec((None, cap, 1), lambda bi, ei: (ei, bi, 0))],
        out_shape=[jax.ShapeDtypeStruct((e, b * cap, d), BF16), jax.ShapeDtypeStruct((e, b * cap, 1), F32)],
        compiler_params=_cp("parallel", "arbitrary"),
        name="moe_gather",
    )(slot.reshape(b, e, 1, t), aff.reshape(b, e, 1, t), h)


def _ffn_up_kernel(*refs, n_streams):
    xs = refs[:n_streams]
    w1_ref, w3_ref = refs[n_streams:n_streams + 2]
    outs = refs[n_streams + 2:]
    w1, w3 = w1_ref[...].astype(BF16), w3_ref[...].astype(BF16)
    for x_ref, o_ref in zip(xs, outs):
        x = x_ref[...]
        a = _dot(x, w1)
        g = _dot(x, w3)
        o_ref[...] = ((a / (1.0 + jnp.exp(-a))) * g).astype(o_ref.dtype)


def _ffn_down_kernel(*refs, n_streams):
    mids = refs[:n_streams]
    gates = refs[n_streams:2 * n_streams]
    w2_ref = refs[2 * n_streams]
    outs = refs[2 * n_streams + 1:]
    w2 = w2_ref[...].astype(BF16)
    for m_ref, gate_ref, o_ref in zip(mids, gates, outs):
        o_ref[...] = (_dot(m_ref[...], w2) * gate_ref[...]).astype(o_ref.dtype)


def _ffn(xs_list, gate_list, w1, w3, w2, layer):
    ns = len(xs_list)
    e, _, d = xs_list[0].shape
    ff = w1.shape[-1]
    tf = _tile(ff, 512)
    tn = _tile(d, 512)
    rows = [x.shape[1] for x in xs_list]
    mids = pl.pallas_call(
        functools.partial(_ffn_up_kernel, n_streams=ns),
        grid=(e, ff // tf),
        in_specs=[pl.BlockSpec((None, r, d), lambda ei, f: (ei, 0, 0)) for r in rows] + [
            pl.BlockSpec((None, None, d, tf), lambda ei, f: (layer, ei, 0, f)),
            pl.BlockSpec((None, None, d, tf), lambda ei, f: (layer, ei, 0, f)),
        ],
        out_specs=[pl.BlockSpec((None, r, tf), lambda ei, f: (ei, 0, f)) for r in rows],
        out_shape=[jax.ShapeDtypeStruct((e, r, ff), BF16) for r in rows],
        compiler_params=_cp("parallel", "arbitrary"),
        name="moe_ffn_up",
    )(*xs_list, w1, w3)
    return pl.pallas_call(
        functools.partial(_ffn_down_kernel, n_streams=ns),
        grid=(e, d // tn),
        in_specs=[pl.BlockSpec((None, r, ff), lambda ei, j: (ei, 0, 0)) for r in rows]
                 + [pl.BlockSpec((None, r, 1), lambda ei, j: (ei, 0, 0)) for r in rows]
                 + [pl.BlockSpec((None, None, ff, tn), lambda ei, j: (layer, ei, 0, j))],
        out_specs=[pl.BlockSpec((None, r, tn), lambda ei, j: (ei, 0, j)) for r in rows],
        out_shape=[jax.ShapeDtypeStruct((e, r, d), BF16) for r in rows],
        compiler_params=_cp("parallel", "arbitrary"),
        name="moe_ffn_down",
    )(*mids, *gate_list, w2)


def _combine_kernel(slot_ref, y_ref, x_ref, gate_ref, o_ref, pt_ref, *, cap):
    e = slot_ref.shape[1]
    tm = slot_ref.shape[0]

    @pl.when(pl.program_id(2) == 0)
    def _():
        slot = slot_ref[...]
        lane = lax.broadcasted_iota(I32, (tm, cap), 1)
        for ei in range(e):
            pt_ref[:, ei * cap:(ei + 1) * cap] = jnp.where(lane == slot[:, ei:ei + 1], 1.0, 0.0).astype(BF16)

    y = y_ref[...]
    acc = _dot(pt_ref[...], y.reshape(e * cap, y.shape[2]))
    o_ref[...] = x_ref[...] + gate_ref[...] * acc


def _combine(slot_te, ys, x, gate, cap):
    b, t, d = x.shape
    e = slot_te.shape[2]
    tm = _tile(t, 512, 8)
    tn = _tile(d, 512)
    return pl.pallas_call(
        functools.partial(_combine_kernel, cap=cap),
        grid=(b, t // tm, d // tn),
        in_specs=[
            pl.BlockSpec((None, tm, e), lambda bi, i, j: (bi, i, 0)),
            pl.BlockSpec((e, cap, tn), lambda bi, i, j: (0, bi, j)),
            pl.BlockSpec((None, tm, tn), lambda bi, i, j: (bi, i, j)),
            pl.BlockSpec((None, 1, tn), lambda bi, i, j: (bi, 0, j)),
        ],
        out_specs=pl.BlockSpec((None, tm, tn), lambda bi, i, j: (bi, i, j)),
        out_shape=jax.ShapeDtypeStruct((b, t, d), F32),
        scratch_shapes=[pltpu.VMEM((tm, e * cap), BF16)],
        compiler_params=_cp("parallel", "arbitrary", "arbitrary"),
        name="moe_combine",
    )(slot_te, ys, x, gate)


def kernel(x, c, ctx, c_ctx, ada_w, ada_b, norm1_g, norm2_g, w_in, da_lambda, da_subln_g, wb_sink, na_rpb, s5_a_re, s5_a_im, s5_log_step, s5_b_re, s5_b_im, s5_c_re, s5_c_im, s5_d, s5_glu_w, w_branch, w_out, w_router, w_e1, w_e3, w_e2, final_g):
    b, n, d = x.shape
    n_ctx = ctx.shape[1]
    depth = ada_w.shape[0]
    assert b + 1 <= ADA_ROWS and n % GRID_W == 0

    cs = jnp.zeros((ADA_ROWS, d), F32).at[:b].set(c).at[b].set(c_ctx)
    mods = _ada_mod(cs, ada_w, ada_b)
    rope_tab = _rope_tables(n)

    for l in range(depth):
        with_ctx = l < depth - 1
        mod_x = [mods[l, :b, k * d:(k + 1) * d].reshape(b, 1, d) for k in range(6)]
        mod_c = [jnp.broadcast_to(mods[l, b, k * d:(k + 1) * d], (b, 1, d)) for k in range(6)]

        hx = _norm_mod(x, norm1_g[l], mod_x[0], mod_x[1])
        hc = _norm_mod(ctx, norm1_g[l], mod_c[0], mod_c[1])
        proj = []
        for h, t in ((hx, n), (hc, n_ctx)):
            h2d = h.reshape(b * t, d)
            proj.append([_mm(h2d, w_in, l, col0, ncols, tn, dt).reshape(b, t, ncols) for col0, ncols, tn, dt in MIX_PROJ])
        (za_x, zb_x, zn_x, zs_x), (za_c, zb_c, zn_c, zs_c) = proj
        qa_x = _rope(za_x, IN_WIDTHS[0] + IN_WIDTHS[1], rope_tab)
        qb_x = _rope(zb_x, IN_WIDTHS[3] + IN_WIDTHS[4], rope_tab)

        lam_init = 0.8 - 0.6 * math.exp(-0.3 * l)
        lv = da_lambda[l].astype(F32)
        lam = jnp.exp(jnp.sum(lv[0] * lv[1])) - jnp.exp(jnp.sum(lv[2] * lv[3])) + lam_init
        bias_tab = _na_bias_table(na_rpb[l])
        s5_tab = _s5_tables(s5_a_re[l], s5_a_im[l], s5_log_step[l], s5_b_re[l], s5_b_im[l], s5_c_re[l], s5_c_im[l])

        ya_x = _diff_attn(qa_x, za_c, qa_x, za_x, lam, da_subln_g[l], lam_init, True)
        yb_x = _win_attn(qb_x, zb_c, qb_x, zb_x, wb_sink[l], True)
        yc_x = _na_attn(zn_x, zn_c, zn_x, bias_tab, True)
        ys_c, ys_x = _s5_scan(zs_x, zs_c, s5_tab, with_ctx)
        yd_x = _s5_out(ys_x, zs_x, s5_d[l], s5_glu_w[l])
        gx = _merge(hx.reshape(b * n, d), [t.reshape(b * n, -1) for t in (ya_x, yb_x, yc_x, yd_x)], w_in, w_branch, l)
        x = _mm_res(gx, w_out, l, x, mod_x[2])

        if with_ctx:
            ya_c = _diff_attn(za_c, za_c, None, None, lam, da_subln_g[l], lam_init, False)
            yb_c = _win_attn(zb_c, zb_c, None, None, wb_sink[l], False)
            yc_c = _na_attn(zn_c, zn_c, None, None, False)
            yd_c = _s5_out(ys_c, zs_c, s5_d[l], s5_glu_w[l])
            gc = _merge(hc.reshape(b * n_ctx, d), [t.reshape(b * n_ctx, -1) for t in (ya_c, yb_c, yc_c, yd_c)],
                        w_in, w_branch, l)
            ctx = _mm_res(gc, w_out, l, ctx, mod_c[2])

        w_router_t = jnp.transpose(w_router[l]).astype(F32)
        streams = [(x, mod_x)] + ([(ctx, mod_c)] if with_ctx else [])
        routed = []
        for s, mod in streams:
            t = s.shape[1]
            cap = EC_CAPACITY_FACTOR * t // N_EXPERTS
            h2, logits_t = _norm_router(s, norm2_g[l], mod[3], mod[4], w_router_t)
            slot, aff = _topk(logits_t, b, cap)
            xs, gate = _gather(slot, aff, h2, cap)
            routed.append((jnp.transpose(slot, (0, 2, 1)), xs, gate, cap))
        ys = _ffn([r[1] for r in routed], [r[2] for r in routed], w_e1, w_e3, w_e2, l)
        x = _combine(routed[0][0], ys[0], x, mod_x[5], routed[0][3])
        if with_ctx:
            ctx = _combine(routed[1][0], ys[1], ctx, mod_c[5], routed[1][3])

    return _final_norm(x, final_g)
```

```python
import functools
import math

import jax
import jax.numpy as jnp
from jax import lax
from jax.experimental import pallas as pl
from jax.experimental.pallas import tpu as pltpu

F32 = jnp.float32
BF16 = jnp.bfloat16
I32 = jnp.int32

GRID_W = 64
EPS = 1e-6
NEG_INF = -1e30
LOG2E = math.log2(math.e)
ROPE_BASE = 10000.0

DA_HEADS = 4
DA_QK_DIM = 64
DA_V_DIM = 2 * DA_QK_DIM
WB_HEADS = 8
WB_KV_HEADS = 2
WB_DIM = 64
WB_WINDOW = 128
WB_BLOCK = 128
NA_HEADS = 8
NA_DIM = 64
NA_ROWS = 8
NA_COLS = 16
S5_GROUPS = 32
S5_GROUP_CH = 16
S5_STATE = 64
S5_WIDTH = S5_GROUPS * S5_GROUP_CH
S5_CHUNK = 8
N_BRANCH = 4
BRANCH_WIDTH = 512
N_EXPERTS = 16
EC_CAPACITY_FACTOR = 2

IN_WIDTHS = (
    2 * DA_HEADS * DA_QK_DIM, 2 * DA_HEADS * DA_QK_DIM, DA_HEADS * DA_V_DIM,
    WB_HEADS * WB_DIM, WB_KV_HEADS * WB_DIM, WB_KV_HEADS * WB_DIM,
    NA_HEADS * NA_DIM, NA_HEADS * NA_DIM, NA_HEADS * NA_DIM,
    S5_WIDTH,
)
N_MIX_IN = sum(IN_WIDTHS)
_OFFS = [0]
for _w in IN_WIDTHS:
    _OFFS.append(_OFFS[-1] + _w)
(OFF_DA_Q, OFF_DA_K, OFF_DA_V, OFF_WB_Q, OFF_WB_K, OFF_WB_V, OFF_NA_Q, OFF_NA_K, OFF_NA_V, OFF_S5, _) = _OFFS

MIX_PROJ = (
    (OFF_DA_Q, OFF_WB_Q - OFF_DA_Q, 768, BF16),
    (OFF_WB_Q, OFF_NA_Q - OFF_WB_Q, 768, BF16),
    (OFF_NA_Q, OFF_S5 - OFF_NA_Q, 768, BF16),
    (OFF_S5, S5_WIDTH, 256, F32),
)

LANE = 128
SUBLANE = 8
ADA_ROWS = 8
VMEM_LIMIT = 56 * 1024 * 1024

NN = (((1,), (0,)), ((), ()))
NT = (((1,), (1,)), ((), ()))


def _cp(*sem):
    return pltpu.CompilerParams(dimension_semantics=sem, vmem_limit_bytes=VMEM_LIMIT)


def _tile(n, pref, mult=LANE):
    if n <= pref:
        return n
    t = (pref // mult) * mult
    while t >= mult:
        if n % t == 0:
            return t
        t -= mult
    return n


def _split(a):
    hi = a.astype(BF16)
    lo = (a - hi.astype(F32)).astype(BF16)
    return hi, lo


def _dot(a, b, dims=NN):
    return lax.dot_general(a, b, dims, preferred_element_type=F32)


def _lane_group(lane, width):
    return lax.shift_right_logical(lane, int(math.log2(width)))


def _count(mask, axis):
    return jnp.sum(jnp.where(mask, 1.0, 0.0), axis=axis, keepdims=True)


def _dot3(a, b, dims=NN):
    ah, al = _split(a)
    bh, bl = _split(b)
    return _dot(ah, bh, dims) + (_dot(ah, bl, dims) + _dot(al, bh, dims))


def _ada_kernel(c_ref, w_ref, b_ref, o_ref):
    c = c_ref[...]
    s = c / (1.0 + jnp.exp(-c))
    o_ref[...] = _dot3(s, w_ref[...]) + b_ref[...]


def _ada_mod(cs, ada_w, ada_b):
    depth, d, n6 = ada_w.shape
    tn = _tile(n6, 512)
    return pl.pallas_call(
        _ada_kernel,
        grid=(depth, n6 // tn),
        in_specs=[
            pl.BlockSpec((ADA_ROWS, d), lambda l, j: (0, 0)),
            pl.BlockSpec((None, d, tn), lambda l, j: (l, 0, j)),
            pl.BlockSpec((None, 1, tn), lambda l, j: (l, 0, j)),
        ],
        out_specs=pl.BlockSpec((None, ADA_ROWS, tn), lambda l, j: (l, 0, j)),
        out_shape=jax.ShapeDtypeStruct((depth, ADA_ROWS, n6), F32),
        compiler_params=_cp("arbitrary", "arbitrary"),
        name="ada_mod",
    )(cs, ada_w, ada_b.reshape(depth, 1, n6))


def _norm_mod_rows(x, g, sh, sc):
    y = x * lax.rsqrt(jnp.mean(x * x, axis=-1, keepdims=True) + EPS) * g
    return y * (1.0 + sc) + sh


def _norm_mod_kernel(x_ref, g_ref, sh_ref, sc_ref, o_ref):
    o_ref[...] = _norm_mod_rows(x_ref[...], g_ref[...], sh_ref[...], sc_ref[...]).astype(o_ref.dtype)


def _norm_mod(x, g, sh, sc):
    b, t, d = x.shape
    tr = _tile(t, 256, 8)
    return pl.pallas_call(
        _norm_mod_kernel,
        grid=(b, t // tr),
        in_specs=[
            pl.BlockSpec((None, tr, d), lambda bi, i: (bi, i, 0)),
            pl.BlockSpec((1, d), lambda bi, i: (0, 0)),
            pl.BlockSpec((None, 1, d), lambda bi, i: (bi, 0, 0)),
            pl.BlockSpec((None, 1, d), lambda bi, i: (bi, 0, 0)),
        ],
        out_specs=pl.BlockSpec((None, tr, d), lambda bi, i: (bi, i, 0)),
        out_shape=jax.ShapeDtypeStruct((b, t, d), BF16),
        compiler_params=_cp("parallel", "parallel"),
        name="norm_mod",
    )(x, g.reshape(1, d), sh, sc)


def _norm_router_kernel(x_ref, g_ref, sh_ref, sc_ref, wrt_ref, h_ref, lg_ref):
    h = _norm_mod_rows(x_ref[...], g_ref[...], sh_ref[...], sc_ref[...])
    h_ref[...] = h.astype(h_ref.dtype)
    lg_ref[...] = _dot3(wrt_ref[...], h, NT)


def _norm_router(x, g, sh, sc, w_router_t):
    b, t, d = x.shape
    e = w_router_t.shape[0]
    tr = _tile(t, 256)
    nt = t // tr
    return pl.pallas_call(
        _norm_router_kernel,
        grid=(b, nt),
        in_specs=[
            pl.BlockSpec((None, tr, d), lambda bi, i: (bi, i, 0)),
            pl.BlockSpec((1, d), lambda bi, i: (0, 0)),
            pl.BlockSpec((None, 1, d), lambda bi, i: (bi, 0, 0)),
            pl.BlockSpec((None, 1, d), lambda bi, i: (bi, 0, 0)),
            pl.BlockSpec((e, d), lambda bi, i: (0, 0)),
        ],
        out_specs=[
            pl.BlockSpec((None, tr, d), lambda bi, i: (bi, i, 0)),
            pl.BlockSpec((e, tr), lambda bi, i: (0, bi * nt + i)),
        ],
        out_shape=[jax.ShapeDtypeStruct((b, t, d), BF16), jax.ShapeDtypeStruct((e, b * t), F32)],
        compiler_params=_cp("parallel", "parallel"),
        name="norm_router",
    )(x, g.reshape(1, d), sh, sc, w_router_t)


def _final_norm_kernel(x_ref, g_ref, o_ref):
    x = x_ref[...]
    o_ref[...] = x * lax.rsqrt(jnp.mean(x * x, axis=-1, keepdims=True) + EPS) * g_ref[...]


def _final_norm(x, g):
    b, t, d = x.shape
    tr = _tile(t, 256, 8)
    return pl.pallas_call(
        _final_norm_kernel,
        grid=(b, t // tr),
        in_specs=[pl.BlockSpec((None, tr, d), lambda bi, i: (bi, i, 0)), pl.BlockSpec((1, d), lambda bi, i: (0, 0))],
        out_specs=pl.BlockSpec((None, tr, d), lambda bi, i: (bi, i, 0)),
        out_shape=jax.ShapeDtypeStruct((b, t, d), F32),
        compiler_params=_cp("parallel", "parallel"),
        name="final_norm",
    )(x, g.reshape(1, d))


def _mm_kernel(a_ref, w_ref, o_ref, wb_ref):
    @pl.when(pl.program_id(1) == 0)
    def _():
        wb_ref[...] = w_ref[...].astype(BF16)

    o_ref[...] = _dot(a_ref[...], wb_ref[...]).astype(o_ref.dtype)


def _mm(a, w, layer, col0, ncols, tn, out_dtype):
    m, k = a.shape
    assert ncols % tn == 0 and col0 % tn == 0
    tm = _tile(m, 1024, 8)
    j0 = col0 // tn
    return pl.pallas_call(
        _mm_kernel,
        grid=(ncols // tn, m // tm),
        in_specs=[
            pl.BlockSpec((tm, k), lambda j, i: (i, 0)),
            pl.BlockSpec((None, k, tn), lambda j, i: (layer, 0, j0 + j)),
        ],
        out_specs=pl.BlockSpec((tm, tn), lambda j, i: (i, j)),
        out_shape=jax.ShapeDtypeStruct((m, ncols), out_dtype),
        scratch_shapes=[pltpu.VMEM((k, tn), BF16)],
        compiler_params=_cp("arbitrary", "arbitrary"),
        name="mm_in",
    )(a, w)


def _mm_res_kernel(a_ref, w_ref, x_ref, gate_ref, o_ref, wb_ref):
    @pl.when(pl.program_id(1) == 0)
    def _():
        wb_ref[...] = w_ref[...].astype(BF16)

    o_ref[...] = x_ref[...] + gate_ref[...] * _dot(a_ref[...], wb_ref[...])


def _mm_res(a, w, layer, x, gate):
    b, t, n = x.shape
    m, k = a.shape
    tn = _tile(n, 1024)
    tm = _tile(t, 1024, 8)
    per_b = t // tm
    out = pl.pallas_call(
        _mm_res_kernel,
        grid=(n // tn, m // tm),
        in_specs=[
            pl.BlockSpec((tm, k), lambda j, i: (i, 0)),
            pl.BlockSpec((None, k, tn), lambda j, i: (layer, 0, j)),
            pl.BlockSpec((tm, tn), lambda j, i: (i, j)),
            pl.BlockSpec((None, 1, tn), lambda j, i: (i // per_b, 0, j)),
        ],
        out_specs=pl.BlockSpec((tm, tn), lambda j, i: (i, j)),
        out_shape=jax.ShapeDtypeStruct((m, n), F32),
        scratch_shapes=[pltpu.VMEM((k, tn), BF16)],
        compiler_params=_cp("arbitrary", "arbitrary"),
        name="mm_out_res",
    )(a, w, x.reshape(m, n), gate)
    return out.reshape(b, t, n)


def _rope_kernel(z_ref, c_ref, sa_ref, sb_ref, o_ref):
    cos, sa, sb = c_ref[...], sa_ref[...], sb_ref[...]
    for j in range(o_ref.shape[1] // LANE):
        x = z_ref[:, j * LANE:(j + 1) * LANE].astype(F32)
        y = x * cos + pltpu.roll(x, LANE - 16, 1) * sa + pltpu.roll(x, 16, 1) * sb
        o_ref[:, j * LANE:(j + 1) * LANE] = y.astype(o_ref.dtype)


def _rope_tables(n):
    pos = jnp.arange(n)
    rows, cols = (pos // GRID_W).astype(F32), (pos % GRID_W).astype(F32)
    half = DA_QK_DIM // 2
    inv_freq = jnp.power(ROPE_BASE, -jnp.arange(0, half, 2, dtype=F32) / half)
    ang_r, ang_c = rows[:, None] * inv_freq[None, :], cols[:, None] * inv_freq[None, :]
    cos64 = jnp.concatenate([jnp.cos(ang_r)] * 2 + [jnp.cos(ang_c)] * 2, axis=-1)
    sin64 = jnp.concatenate([jnp.sin(ang_r)] * 2 + [jnp.sin(ang_c)] * 2, axis=-1)
    cos, sin = jnp.tile(cos64, (1, 2)), jnp.tile(sin64, (1, 2))
    first = (jnp.arange(LANE) % 32) < 16
    return cos, jnp.where(first, -sin, 0.0), jnp.where(first, 0.0, sin)


def _rope(z, ncols, tables):
    b, n, _ = z.shape
    cos, sa, sb = tables
    tr = _tile(n, 512, 8)
    tab = pl.BlockSpec((tr, LANE), lambda bi, i: (i, 0))
    return pl.pallas_call(
        _rope_kernel,
        grid=(b, n // tr),
        in_specs=[pl.BlockSpec((None, tr, ncols), lambda bi, i: (bi, i, 0)), tab, tab, tab],
        out_specs=pl.BlockSpec((None, tr, ncols), lambda bi, i: (bi, i, 0)),
        out_shape=jax.ShapeDtypeStruct((b, n, ncols), BF16),
        compiler_params=_cp("parallel", "parallel"),
        name="rope",
    )(z, cos, sa, sb)


def _diff_attn_kernel(*refs, has_x, post_scale):
    if has_x:
        lam_ref, g_ref, q_ref, kc_ref, vc_ref, kx_ref, vx_ref, o_ref = refs
    else:
        lam_ref, g_ref, q_ref, kc_ref, vc_ref, o_ref = refs
    q = q_ref[...].astype(F32) * (DA_QK_DIM ** -0.5 * LOG2E)
    lane = lax.broadcasted_iota(I32, q.shape, 1)
    if has_x:
        k = jnp.concatenate([kc_ref[...], kx_ref[...]], axis=0)
        v = jnp.concatenate([vc_ref[...], vx_ref[...]], axis=0)
    else:
        k, v = kc_ref[...], vc_ref[...]

    def attend(qm):
        s = _dot(qm, k, NT)
        p = jnp.exp2(s - jnp.max(s, axis=-1, keepdims=True))
        return _dot(p.astype(BF16), v) / jnp.sum(p, axis=-1, keepdims=True)

    o = attend(jnp.where(lane < DA_QK_DIM, q, 0.0).astype(BF16)) \
        - lam_ref[...] * attend(jnp.where(lane >= DA_QK_DIM, q, 0.0).astype(BF16))
    y = o * lax.rsqrt(jnp.mean(o * o, axis=-1, keepdims=True) + EPS) * g_ref[...] * post_scale
    o_ref[...] = y.astype(o_ref.dtype)


def _diff_attn(q_arr, zc, kx_arr, vx_arr, lam, subln_g, lam_init, has_x):
    b, tq_all, _ = q_arr.shape
    c = zc.shape[1]
    tq = _tile(tq_all, 512, 8)
    q_blk0, kc_blk0, vc_blk0 = 0, IN_WIDTHS[0] // LANE, 2 * IN_WIDTHS[0] // LANE
    kx_blk0 = kc_blk0
    in_specs = [
        pl.BlockSpec((1, 1), lambda bi, h, i: (0, 0)),
        pl.BlockSpec((1, DA_V_DIM), lambda bi, h, i: (0, 0)),
        pl.BlockSpec((None, tq, LANE), lambda bi, h, i: (bi, i, q_blk0 + h)),
        pl.BlockSpec((None, c, LANE), lambda bi, h, i: (bi, 0, kc_blk0 + h)),
        pl.BlockSpec((None, c, LANE), lambda bi, h, i: (bi, 0, vc_blk0 + h)),
    ]
    args = [lam.reshape(1, 1), subln_g.reshape(1, DA_V_DIM), q_arr, zc, zc]
    if has_x:
        n = kx_arr.shape[1]
        in_specs += [
            pl.BlockSpec((None, n, LANE), lambda bi, h, i: (bi, 0, kx_blk0 + h)),
            pl.BlockSpec((None, n, LANE), lambda bi, h, i: (bi, 0, vc_blk0 + h)),
        ]
        args += [kx_arr, vx_arr]
    return pl.pallas_call(
        functools.partial(_diff_attn_kernel, has_x=has_x, post_scale=1.0 - lam_init),
        grid=(b, DA_HEADS, tq_all // tq),
        in_specs=in_specs,
        out_specs=pl.BlockSpec((None, tq, LANE), lambda bi, h, i: (bi, i, h)),
        out_shape=jax.ShapeDtypeStruct((b, tq_all, DA_HEADS * DA_V_DIM), BF16),
        compiler_params=_cp("parallel", "parallel", "arbitrary"),
        name="diff_attn_x" if has_x else "diff_attn_c",
    )(*args)


def _group_cols(x, g):
    x = x.astype(F32)
    lane = lax.broadcasted_iota(I32, x.shape, 1)
    own = jnp.where(_lane_group(lane, WB_DIM) == g, x, pltpu.roll(x, WB_DIM, 1))
    return jnp.concatenate([own, own], axis=1).astype(BF16)


def _win_attn_kernel(*refs, has_x, n_tok):
    if has_x:
        sink_ref, q_ref, kc_ref, vc_ref, kp_ref, k0_ref, kn_ref, vp_ref, v0_ref, vn_ref, o_ref = refs
    else:
        sink_ref, q_ref, kc_ref, vc_ref, o_ref = refs
    i = pl.program_id(1)
    rep = WB_HEADS // WB_KV_HEADS
    gw = rep * WB_DIM
    q_all = q_ref[...].astype(F32) * (WB_DIM ** -0.5 * LOG2E)
    blk = q_all.shape[0]
    lane = lax.broadcasted_iota(I32, (blk, gw), 1)
    if has_x:
        k_in = jnp.concatenate([kp_ref[...], k0_ref[...], kn_ref[...], kc_ref[...]], axis=0)
        v_in = jnp.concatenate([vp_ref[...], v0_ref[...], vn_ref[...], vc_ref[...]], axis=0)
        q_pos = i * blk + (lax.broadcasted_iota(I32, (rep * blk, 3 * blk), 0) & (blk - 1))
        k_pos = (i - 1) * blk + lax.broadcasted_iota(I32, (rep * blk, 3 * blk), 1)
        valid = (jnp.abs(q_pos - k_pos) <= WB_WINDOW) & (k_pos >= 0) & (k_pos < n_tok)
    else:
        k_in, v_in = kc_ref[...], vc_ref[...]
    outs = []
    for g in range(WB_KV_HEADS):
        q = q_all[:, g * gw:(g + 1) * gw]
        k, v = _group_cols(k_in, g), _group_cols(v_in, g)
        mine = [_lane_group(lane, WB_DIM) == r for r in range(rep)]
        qs = jnp.concatenate([jnp.where(mine[r], q, 0.0) for r in range(rep)], axis=0).astype(BF16)
        sink = jnp.concatenate([jnp.broadcast_to(sink_ref[:, (g * rep + r) * WB_DIM:(g * rep + r) * WB_DIM + 1], (blk, 1))
                                for r in range(rep)], axis=0)
        s = _dot(qs, k, NT)
        if has_x:
            s = jnp.concatenate([jnp.where(valid, s[:, :3 * blk], NEG_INF), s[:, 3 * blk:]], axis=1)
        m = jnp.maximum(jnp.max(s, axis=-1, keepdims=True), sink)
        pr = jnp.exp2(s - m)
        l = jnp.sum(pr, axis=-1, keepdims=True) + jnp.exp2(sink - m)
        o = _dot(pr.astype(BF16), v) / l
        acc = jnp.zeros(q.shape, F32)
        for r in range(rep):
            acc = acc + jnp.where(mine[r], o[r * blk:(r + 1) * blk], 0.0)
        outs.append(acc)
    o_ref[...] = jnp.concatenate(outs, axis=1).astype(o_ref.dtype)


def _win_attn(q_arr, zc, kx_arr, vx_arr, sink, has_x):
    b, tq_all, _ = q_arr.shape
    c = zc.shape[1]
    blk = WB_BLOCK
    nb = tq_all // blk
    qw = WB_HEADS * WB_DIM
    sink_arr = jnp.repeat(sink.astype(F32) * LOG2E, WB_DIM).reshape(1, qw)
    k_blk, v_blk = qw // LANE, qw // LANE + 1
    in_specs = [
        pl.BlockSpec((1, qw), lambda bi, i: (0, 0)),
        pl.BlockSpec((None, blk, qw), lambda bi, i: (bi, i, 0)),
        pl.BlockSpec((None, c, LANE), lambda bi, i: (bi, 0, k_blk)),
        pl.BlockSpec((None, c, LANE), lambda bi, i: (bi, 0, v_blk)),
    ]
    args = [sink_arr, q_arr, zc, zc]
    n_tok = 0
    if has_x:
        n_tok = kx_arr.shape[1]
        prev = lambda i: jnp.maximum(i - 1, 0)
        nxt = lambda i: jnp.minimum(i + 1, nb - 1)
        for arr, cb in ((kx_arr, k_blk), (vx_arr, v_blk)):
            in_specs += [
                pl.BlockSpec((None, blk, LANE), lambda bi, i, cb=cb: (bi, prev(i), cb)),
                pl.BlockSpec((None, blk, LANE), lambda bi, i, cb=cb: (bi, i, cb)),
                pl.BlockSpec((None, blk, LANE), lambda bi, i, cb=cb: (bi, nxt(i), cb)),
            ]
            args += [arr, arr, arr]
    return pl.pallas_call(
        functools.partial(_win_attn_kernel, has_x=has_x, n_tok=n_tok),
        grid=(b, nb),
        in_specs=in_specs,
        out_specs=pl.BlockSpec((None, blk, qw), lambda bi, i: (bi, i, 0)),
        out_shape=jax.ShapeDtypeStruct((b, tq_all, qw), BF16),
        compiler_params=_cp("parallel", "arbitrary"),
        name="win_attn_x" if has_x else "win_attn_c",
    )(*args)


NA_QROWS = 4
NA_KROWS = 12


def _na_kernel(*refs, has_x, n_blocks):
    if has_x:
        q_ref, kc_ref, vc_ref, k_ref, v_ref, t2_ref, o_ref = refs
    else:
        q_ref, kc_ref, vc_ref, o_ref = refs
    tq = q_ref.shape[0]
    lane = lax.broadcasted_iota(I32, (tq, LANE), 1)
    per = LANE // NA_DIM
    if has_x:
        p = pl.program_id(1)
        is_first, is_last = p == 0, p == n_blocks - 1
        n_rows = n_blocks * NA_QROWS
        start = jnp.where(is_first, 0, jnp.where(is_last, n_rows - NA_KROWS, p * NA_QROWS - NA_ROWS // 2))
        off = pl.multiple_of(start * GRID_W, GRID_W)
        nk = NA_KROWS * GRID_W
        row_i = lax.broadcasted_iota(I32, (tq, nk), 0)
        key_i = lax.broadcasted_iota(I32, (tq, nk), 1)
        qi, col = _lane_group(row_i, GRID_W), row_i & (GRID_W - 1)
        kp, w = _lane_group(key_i, GRID_W), key_i & (GRID_W - 1)
        lo = jnp.where(is_first, 0, jnp.where(is_last, NA_KROWS - NA_ROWS, qi))
        col_start = jnp.clip(col - NA_COLS // 2, 0, GRID_W - NA_COLS)
        valid = (w >= col_start) & (w < col_start + NA_COLS) & (kp >= lo) & (kp < lo + NA_ROWS)
        n_pair = t2_ref.shape[1]
        first_pair = [jnp.where(is_first, NA_ROWS - 1 - i, jnp.where(is_last, -1 - i, NA_ROWS // 2 - 1 - i))
                      for i in range(NA_QROWS)]
    for j in range(NA_HEADS // per):
        cols = slice(j * LANE, (j + 1) * LANE)
        q = q_ref[:, cols].astype(F32) * (NA_DIM ** -0.5 * LOG2E)
        if has_x:
            k = jnp.concatenate([k_ref[pl.ds(off, nk), cols], kc_ref[:, cols]], axis=0)
            v = jnp.concatenate([v_ref[pl.ds(off, nk), cols], vc_ref[:, cols]], axis=0)
        else:
            k, v = kc_ref[:, cols], vc_ref[:, cols]
        acc = jnp.zeros(q.shape, F32)
        for hh in range(per):
            mine = _lane_group(lane, NA_DIM) == hh
            qh = jnp.where(mine, q, 0.0).astype(BF16)
            s = _dot(qh, k, NT)
            if has_x:
                head = j * per + hh
                bias = jnp.concatenate([
                    jnp.concatenate([t2_ref[head, jnp.clip(first_pair[i] + 2 * m, 0, n_pair - 1)]
                                     for m in range(NA_KROWS // 2)], axis=1)
                    for i in range(NA_QROWS)], axis=0)
                s = jnp.concatenate([jnp.where(valid, s[:, :nk] + bias, NEG_INF), s[:, nk:]], axis=1)
            m = jnp.max(s, axis=-1, keepdims=True)
            pr = jnp.exp2(s - m)
            l = jnp.sum(pr, axis=-1, keepdims=True)
            acc = acc + jnp.where(mine, _dot(pr.astype(BF16), v) / l, 0.0)
        o_ref[:, cols] = acc.astype(o_ref.dtype)


def _na_bias_table(rpb):
    r = rpb.astype(F32) * LOG2E
    edge = GRID_W - NA_COLS
    ext = jnp.concatenate([jnp.repeat(r[..., :1], edge, -1), r, jnp.repeat(r[..., -1:], edge, -1)], axis=-1)
    by_col = jnp.stack([ext[..., GRID_W - 1 - c:2 * GRID_W - 1 - c] for c in range(GRID_W)], axis=-2)
    return jnp.concatenate([by_col[:, :-1], by_col[:, 1:]], axis=-1)


def _na_attn(q_arr, zc, zx, bias_tab, layer, has_x):
    b, tq_all, _ = q_arr.shape
    c = zc.shape[1]
    hw = NA_HEADS * NA_DIM
    tq = NA_QROWS * GRID_W if has_x else _tile(tq_all, 256, 8)
    nq = tq_all // tq
    in_specs = [
        pl.BlockSpec((None, tq, hw), lambda bi, r: (bi, r, 0)),
        pl.BlockSpec((None, c, hw), lambda bi, r: (bi, 0, 1)),
        pl.BlockSpec((None, c, hw), lambda bi, r: (bi, 0, 2)),
    ]
    args = [q_arr, zc, zc]
    if has_x:
        n = zx.shape[1]
        assert tq_all % tq == 0 and nq * NA_QROWS >= NA_KROWS
        in_specs += [
            pl.BlockSpec((None, n, hw), lambda bi, r: (bi, 0, 1)),
            pl.BlockSpec((None, n, hw), lambda bi, r: (bi, 0, 2)),
            pl.BlockSpec((None,) + bias_tab.shape[1:], lambda bi, r: (layer, 0, 0, 0, 0)),
        ]
        args += [zx, zx, bias_tab]
    return pl.pallas_call(
        functools.partial(_na_kernel, has_x=has_x, n_blocks=nq),
        grid=(b, nq),
        in_specs=in_specs,
        out_specs=pl.BlockSpec((None, tq, hw), lambda bi, r: (bi, r, 0)),
        out_shape=jax.ShapeDtypeStruct((b, tq_all, hw), BF16),
        compiler_params=_cp("parallel", "arbitrary"),
        name="na_attn_x" if has_x else "na_attn_c",
    )(*args)


def _s5_kernel(*refs, ctx_out):
    if ctx_out:
        (uc_ref, ux_ref, d_ref, b1_ref, b2_ref, a1_ref, a2_ref, a3_ref, cb_ref, yc_ref, yx_ref,
         kbig, bb1, bb2, xc, xx, s1c, s2c, s1x, s2x, hc, hx) = refs
    else:
        (uc_ref, ux_ref, d_ref, b1_ref, b2_ref, a1_ref, a2_ref, a3_ref, cb_ref, yx_ref,
         kbig, bb1, bb2, xc, xx, s1c, s2c, s1x, s2x, hc, hx) = refs
        yc_ref = None
    dirn = pl.program_id(1)
    ell, hch = S5_CHUNK, S5_GROUP_CH
    gq = LANE // hch
    nb = ux_ref.shape[0]
    streams = ((uc_ref, xc, s1c, s2c, hc, yc_ref, uc_ref.shape[1] // ell),
               (ux_ref, xx, s1x, s2x, hx, yx_ref, ux_ref.shape[1] // ell))

    for i in range(ell):
        for j in range(ell):
            kbig[i * LANE:(i + 1) * LANE, j * LANE:(j + 1) * LANE] = d_ref[j - i + ell - 1]
    bb1[...] = jnp.zeros(bb1.shape, BF16)
    bb2[...] = jnp.zeros(bb2.shape, BF16)
    for i in range(ell):
        for gl in range(gq):
            r0 = i * LANE + gl * hch
            bb1[r0:r0 + hch, gl * LANE:(gl + 1) * LANE] = b1_ref[gl, i * hch:(i + 1) * hch, :].astype(BF16)
            bb2[r0:r0 + hch, gl * LANE:(gl + 1) * LANE] = b2_ref[gl, i * hch:(i + 1) * hch, :].astype(BF16)

    @pl.when(dirn == 0)
    def _():
        for u_ref, x_s, _, _, _, _, nc in streams:
            for b in range(nb):
                for i in range(ell):
                    x_s[b * nc:(b + 1) * nc, i * LANE:(i + 1) * LANE] = u_ref[b, pl.ds(i, nc, stride=ell), :].astype(BF16)

    cpt = SUBLANE // nb
    for _, x_s, s1, s2, _, _, nc in streams:
        loc1 = _dot(x_s[...], bb1[...])
        loc2 = _dot(x_s[...], bb2[...])
        for k in range(gq):
            for b in range(nb):
                s1[k, pl.ds(b, nc, stride=nb), :] = loc1[b * nc:(b + 1) * nc, k * LANE:(k + 1) * LANE]
                s2[k, pl.ds(b, nc, stride=nb), :] = loc2[b * nc:(b + 1) * nc, k * LANE:(k + 1) * LANE]

    def lane_blocks(a_ref):
        return [jnp.broadcast_to(a_ref[:, k * LANE:(k + 1) * LANE], (SUBLANE, LANE)) for k in range(gq)]

    a1, a2, a3 = lane_blocks(a1_ref), lane_blocks(a2_ref), lane_blocks(a3_ref)
    fwd = dirn == 0
    row_grp = _lane_group(lax.broadcasted_iota(I32, (SUBLANE, LANE), 0), nb)

    def advance(v):
        return jnp.where(fwd, pltpu.roll(v, nb, 0), pltpu.roll(v, SUBLANE - nb, 0))

    zero = tuple(jnp.zeros((SUBLANE, LANE), F32) for _ in range(gq))
    carry = (zero, zero)
    for _, _, s1, s2, h, _, nc in streams:
        n_tiles = nc // cpt

        def step(t, vs, s1=s1, s2=s2, h=h, n_tiles=n_tiles):
            v1, v2 = list(vs[0]), list(vs[1])
            j = jnp.where(fwd, t, n_tiles - 1 - t)
            rows = pl.ds(pl.multiple_of(j * SUBLANE, SUBLANE), SUBLANE)
            for k in range(gq):
                loc1, loc2 = s1[k, rows, :], s2[k, rows, :]
                entered = v1[k]
                for u in range(cpt):
                    grp = jnp.where(fwd, u, cpt - 1 - u)
                    entered = jnp.where(row_grp == grp, v1[k], entered)
                    n1 = a1[k] * v1[k] + a2[k] * v2[k] + loc1
                    n2 = a1[k] * v2[k] + a3[k] * v1[k] + loc2
                    v1[k], v2[k] = advance(n1), advance(n2)
                h[k, rows, :] = entered
            return tuple(v1), tuple(v2)

        carry = lax.fori_loop(0, n_tiles, step, carry, unroll=2)

    for _, x_s, _, _, h, y_ref, nc in streams:
        if y_ref is None:
            continue
        h_all = jnp.concatenate(
            [jnp.concatenate([h[k, pl.ds(b, nc, stride=nb), :] for b in range(nb)], axis=0) for k in range(gq)],
            axis=1).astype(BF16)
        y = _dot(x_s[...], kbig[...]) + _dot(h_all, cb_ref[...])

        @pl.when(dirn == 0)
        def _():
            for b in range(nb):
                for j in range(ell):
                    y_ref[b, pl.ds(j, nc, stride=ell), :] = y[b * nc:(b + 1) * nc, j * LANE:(j + 1) * LANE]

        @pl.when(dirn == 1)
        def _():
            for b in range(nb):
                for j in range(ell):
                    y_ref[b, pl.ds(j, nc, stride=ell), :] += y[b * nc:(b + 1) * nc, j * LANE:(j + 1) * LANE]


def _s5_tables(a_re, a_im, log_step, b_re, b_im, c_re, c_im):
    ell, hch, p = S5_CHUNK, S5_GROUP_CH, S5_STATE
    lam = lax.complex(a_re.astype(F32), a_im.astype(F32))
    lam_dt = lam * jnp.exp(log_step.astype(F32))[:, :, None]
    a_bar = jnp.exp(lam_dt)
    b_bar = ((a_bar - 1.0) / lam)[:, :, :, None] * lax.complex(b_re.astype(F32), b_im.astype(F32))
    c_mat = lax.complex(c_re.astype(F32), c_im.astype(F32))
    gq = LANE // hch
    nq = S5_GROUPS // gq
    hp = lax.Precision.HIGHEST
    steps = jnp.arange(ell + 1, dtype=F32)
    apow = jnp.exp(lam_dt[:, :, None, :] * steps[None, None, :, None])
    apow_re, apow_im = apow.real, apow.imag
    bt_re, bt_im = jnp.swapaxes(b_bar.real, 2, 3), jnp.swapaxes(b_bar.imag, 2, 3)
    c_re_, c_im_ = c_mat.real, c_mat.imag
    eye = jnp.eye(gq, dtype=F32)

    def cmul(ar, ai, br, bi):
        return ar * br - ai * bi, ar * bi + ai * br

    m_re, m_im = cmul(apow_re[:, :, :ell, None, :], apow_im[:, :, :ell, None, :], bt_re[:, :, None], bt_im[:, :, None])
    kk = (jnp.einsum('dgkip,dgop->dgkio', m_re, c_re_, precision=hp)
          - jnp.einsum('dgkip,dgop->dgkio', m_im, c_im_, precision=hp))
    zeros = jnp.zeros_like(kk[0, :, 1:])
    signed = jnp.stack([jnp.concatenate([zeros, kk[0]], axis=1),
                        jnp.concatenate([jnp.flip(kk[1], axis=1), zeros], axis=1)])
    signed = signed.reshape(2, nq, gq, 2 * ell - 1, hch, hch)
    dblk = jnp.einsum('dqgkio,gh->dqkgiho', signed, eye).reshape(2, nq, 2 * ell - 1, LANE, LANE).astype(BF16)
    i_idx = jnp.arange(ell)
    e_re = jnp.stack([apow_re[0][:, ell - 1 - i_idx], apow_re[1][:, i_idx]])
    e_im = jnp.stack([apow_im[0][:, ell - 1 - i_idx], apow_im[1][:, i_idx]])
    bp_re, bp_im = cmul(e_re[:, :, :, None, :], e_im[:, :, :, None, :], bt_re[:, :, None], bt_im[:, :, None])
    bp_re, bp_im = bp_re.reshape(2, S5_GROUPS, ell * hch, p), bp_im.reshape(2, S5_GROUPS, ell * hch, p)
    b1 = jnp.concatenate([bp_re, bp_im], axis=-1)
    b2 = jnp.concatenate([bp_im, bp_re], axis=-1)
    al_re, al_im = apow_re[:, :, ell, :], apow_im[:, :, ell, :]
    lanes = lambda u, v: jnp.concatenate([u, v], axis=-1).reshape(2, nq, 1, gq * 2 * p)
    a1, a2, a3 = lanes(al_re, al_re), lanes(-al_im, al_im), lanes(al_im, -al_im)
    f_re = jnp.stack([apow_re[0][:, 1 + i_idx], apow_re[1][:, ell - i_idx]])
    f_im = jnp.stack([apow_im[0][:, 1 + i_idx], apow_im[1][:, ell - i_idx]])
    g_re, g_im = cmul(c_re_[:, :, None], c_im_[:, :, None], f_re[:, :, :, None, :], f_im[:, :, :, None, :])
    cp = jnp.concatenate([jnp.transpose(g_re, (0, 1, 4, 2, 3)), -jnp.transpose(g_im, (0, 1, 4, 2, 3))], axis=2)
    cp = cp.reshape(2, nq, gq, 2 * p, ell, hch)
    cbig = jnp.einsum('dqgsjo,gh->dqgsjho', cp, eye).reshape(2, nq, gq * 2 * p, ell * LANE).astype(BF16)
    return dblk, b1, b2, a1, a2, a3, cbig


def _s5_scan(zx, zc, tables, layer, ctx_out):
    b, n, _ = zx.shape
    c = zc.shape[1]
    ell, hch = S5_CHUNK, S5_GROUP_CH
    gq = LANE // hch
    nq = S5_GROUPS // gq
    p2 = 2 * S5_STATE
    ncx, ncc = n // ell, c // ell
    assert n % ell == 0 and c % ell == 0 and ncc % 16 == 0
    assert SUBLANE % b == 0 and b & (b - 1) == 0
    dblk, b1, b2, a1, a2, a3, cbig = tables
    u0 = 0
    sw = gq * p2
    xw = ell * LANE
    a_spec = pl.BlockSpec((None, None, None, 1, sw), lambda q, d: (layer, d, q, 0, 0))
    b_spec = pl.BlockSpec((None, None, gq, ell * hch, p2), lambda q, d: (layer, d, q, 0, 0))
    y_specs = [pl.BlockSpec((b, n, LANE), lambda q, d: (0, 0, q))]
    y_shapes = [jax.ShapeDtypeStruct((b, n, S5_WIDTH), F32)]
    if ctx_out:
        y_specs = [pl.BlockSpec((b, c, LANE), lambda q, d: (0, 0, q))] + y_specs
        y_shapes = [jax.ShapeDtypeStruct((b, c, S5_WIDTH), F32)] + y_shapes
    out = pl.pallas_call(
        functools.partial(_s5_kernel, ctx_out=ctx_out),
        grid=(nq, 2),
        in_specs=[
            pl.BlockSpec((b, c, LANE), lambda q, d: (0, 0, u0 + q)),
            pl.BlockSpec((b, n, LANE), lambda q, d: (0, 0, u0 + q)),
            pl.BlockSpec((None, None, None, 2 * ell - 1, LANE, LANE), lambda q, d: (layer, d, q, 0, 0, 0)),
            b_spec, b_spec, a_spec, a_spec, a_spec,
            pl.BlockSpec((None, None, None, sw, xw), lambda q, d: (layer, d, q, 0, 0)),
        ],
        out_specs=y_specs,
        out_shape=y_shapes,
        scratch_shapes=[
            pltpu.VMEM((xw, xw), BF16), pltpu.VMEM((xw, sw), BF16), pltpu.VMEM((xw, sw), BF16),
            pltpu.VMEM((b * ncc, xw), BF16), pltpu.VMEM((b * ncx, xw), BF16),
            pltpu.VMEM((gq, b * ncc, p2), F32), pltpu.VMEM((gq, b * ncc, p2), F32),
            pltpu.VMEM((gq, b * ncx, p2), F32), pltpu.VMEM((gq, b * ncx, p2), F32),
            pltpu.VMEM((gq, b * ncc, p2), F32), pltpu.VMEM((gq, b * ncx, p2), F32),
        ],
        compiler_params=_cp("parallel", "arbitrary"),
        name="s5_scan",
    )(zc, zx, dblk, b1, b2, a1, a2, a3, cbig)
    return (out[0], out[1]) if ctx_out else (None, out[0])


def _s5_out_kernel(y_ref, u_ref, d_ref, w_ref, o_ref):
    g = jax.nn.gelu(y_ref[...] + d_ref[...] * u_ref[...])
    o_ref[...] = (g * jax.nn.sigmoid(_dot(g.astype(BF16), w_ref[...].astype(BF16)))).astype(o_ref.dtype)


def _s5_out(y, u, d_skip, w_glu):
    b, t, w = y.shape
    tr = _tile(t, 512, 8)
    row = pl.BlockSpec((None, tr, w), lambda bi, i: (bi, i, 0))
    return pl.pallas_call(
        _s5_out_kernel,
        grid=(b, t // tr),
        in_specs=[row, row, pl.BlockSpec((1, w), lambda bi, i: (0, 0)), pl.BlockSpec((w, w), lambda bi, i: (0, 0))],
        out_specs=row,
        out_shape=jax.ShapeDtypeStruct((b, t, w), BF16),
        compiler_params=_cp("parallel", "parallel"),
        name="s5_out",
    )(y, u, d_skip.reshape(1, w).astype(F32), w_glu)


def _merge_kernel(h_ref, ya_ref, yb_ref, yc_ref, yd_ref, wg0_ref, wg1_ref, wg2_ref, wg3_ref, wb_ref, o_ref, wgs_ref, wbs_ref):
    @pl.when(pl.program_id(1) == 0)
    def _():
        for n, wg_ref in enumerate((wg0_ref, wg1_ref, wg2_ref, wg3_ref)):
            wgs_ref[n] = wg_ref[...].astype(BF16)
        wbs_ref[...] = wb_ref[...].astype(BF16)

    h = h_ref[...]
    acc = None
    for n, y_ref in enumerate((ya_ref, yb_ref, yc_ref, yd_ref)):
        term = jax.nn.sigmoid(_dot(h, wgs_ref[n])) * _dot(y_ref[...], wbs_ref[n])
        acc = term if acc is None else acc + term
    o_ref[...] = acc.astype(o_ref.dtype)


def _merge(h, branches, w_in, w_branch, layer):
    m, d = h.shape
    bw = branches[0].shape[1]
    tn = 256
    tm = _tile(m, 1024, 8)
    g0 = N_MIX_IN // tn
    per = d // tn
    assert N_MIX_IN % tn == 0 and d % tn == 0
    wg_specs = [pl.BlockSpec((None, d, tn), lambda j, i, n=n: (layer, 0, g0 + n * per + j)) for n in range(N_BRANCH)]
    y_spec = pl.BlockSpec((tm, bw), lambda j, i: (i, 0))
    return pl.pallas_call(
        _merge_kernel,
        grid=(d // tn, m // tm),
        in_specs=[pl.BlockSpec((tm, d), lambda j, i: (i, 0)), y_spec, y_spec, y_spec, y_spec] + wg_specs
                 + [pl.BlockSpec((None, N_BRANCH, bw, tn), lambda j, i: (layer, 0, 0, j))],
        out_specs=pl.BlockSpec((tm, tn), lambda j, i: (i, j)),
        out_shape=jax.ShapeDtypeStruct((m, d), BF16),
        scratch_shapes=[pltpu.VMEM((N_BRANCH, d, tn), BF16), pltpu.VMEM((N_BRANCH, bw, tn), BF16)],
        compiler_params=_cp("arbitrary", "arbitrary"),
        name="merge",
    )(h, *branches, w_in, w_in, w_in, w_in, w_branch)


def _topk_kernel(lg_ref, slot_ref, aff_ref, tri_ref, *, cap):
    e, t = lg_ref.shape

    @pl.when(pl.program_id(0) == 0)
    def _():
        rows = 256 if t % 256 == 0 else t
        for r0 in range(0, t, rows):
            ri = lax.broadcasted_iota(I32, (rows, t), 0) + r0
            ci = lax.broadcasted_iota(I32, (rows, t), 1)
            tri_ref[r0:r0 + rows, :] = jnp.where(ri < ci, 1.0, 0.0).astype(BF16)

    lg = lg_ref[...]
    ex = jnp.exp(lg - jnp.max(lg, axis=0, keepdims=True))
    aff = ex / jnp.sum(ex, axis=0, keepdims=True)
    aff_ref[...] = aff
    bits = pltpu.bitcast(aff, I32)

    def search(_, carry):
        lo, hi = carry
        mid = lo + lax.shift_right_logical(hi - lo + 1, 1)
        ok = _count(bits >= mid, 1) >= cap
        return jnp.where(ok, mid, lo), jnp.where(ok, hi, mid - 1)

    lo0 = jnp.zeros((e, 1), I32)
    hi0 = jnp.full((e, 1), 0x7F800000, I32)
    thr, _ = lax.fori_loop(0, 32, search, (lo0, hi0))
    gt = bits > thr
    eq = bits == thr
    need = cap - _count(gt, 1)
    tri = tri_ref[...]
    eq_before = _dot(jnp.where(eq, 1.0, 0.0).astype(BF16), tri)
    sel = gt | (eq & (eq_before < need))
    sel_before = _dot(jnp.where(sel, 1.0, 0.0).astype(BF16), tri)
    slot_ref[...] = jnp.where(sel, sel_before.astype(I32), -1)


def _topk(logits_t, b, cap):
    e, bt = logits_t.shape
    t = bt // b
    return pl.pallas_call(
        functools.partial(_topk_kernel, cap=cap),
        grid=(b,),
        in_specs=[pl.BlockSpec((e, t), lambda bi: (0, bi))],
        out_specs=[pl.BlockSpec((None, e, t), lambda bi: (bi, 0, 0)), pl.BlockSpec((None, e, t), lambda bi: (bi, 0, 0))],
        out_shape=[jax.ShapeDtypeStruct((b, e, t), I32), jax.ShapeDtypeStruct((b, e, t), F32)],
        scratch_shapes=[pltpu.VMEM((t, t), BF16)],
        compiler_params=_cp("arbitrary"),
        name="route_topk",
    )(logits_t)


def _gather_kernel(slot_ref, aff_ref, h_ref, xs_ref, gate_ref, *, cap):
    t = h_ref.shape[0]
    pick = lax.broadcasted_iota(I32, (cap, t), 0) == slot_ref[...]
    xs_ref[...] = _dot(jnp.where(pick, 1.0, 0.0).astype(BF16), h_ref[...]).astype(xs_ref.dtype)
    gate_ref[...] = jnp.sum(jnp.where(pick, aff_ref[...], 0.0), axis=1, keepdims=True)


def _gather(slot, aff, h, cap):
    b, e, t = slot.shape
    d = h.shape[2]
    row = pl.BlockSpec((None, None, 1, t), lambda bi, ei: (bi, ei, 0, 0))
    return pl.pallas_call(
        functools.partial(_gather_kernel, cap=cap),
        grid=(b, e),
        in_specs=[row, row, pl.BlockSpec((None, t, d), lambda bi, ei: (bi, 0, 0))],
        out_specs=[pl.BlockSpec((None, cap, d), lambda bi, ei: (ei, bi, 0)),
                   pl.BlockSpec((None, cap, 1), lambda bi, ei: (ei, bi, 0))],
        out_shape=[jax.ShapeDtypeStruct((e, b * cap, d), BF16), jax.ShapeDtypeStruct((e, b * cap, 1), F32)],
        compiler_params=_cp("parallel", "arbitrary"),
        name="moe_gather",
    )(slot.reshape(b, e, 1, t), aff.reshape(b, e, 1, t), h)


def _ffn_up_kernel(*refs, n_streams):
    xs = refs[:n_streams]
    w1_ref, w3_ref = refs[n_streams:n_streams + 2]
    outs = refs[n_streams + 2:]
    w1, w3 = w1_ref[...].astype(BF16), w3_ref[...].astype(BF16)
    for x_ref, o_ref in zip(xs, outs):
        x = x_ref[...]
        a = _dot(x, w1)
        g = _dot(x, w3)
        o_ref[...] = ((a / (1.0 + jnp.exp(-a))) * g).astype(o_ref.dtype)


def _ffn_down_kernel(*refs, n_streams):
    mids = refs[:n_streams]
    gates = refs[n_streams:2 * n_streams]
    w2_ref = refs[2 * n_streams]
    outs = refs[2 * n_streams + 1:]
    w2 = w2_ref[...].astype(BF16)
    for m_ref, gate_ref, o_ref in zip(mids, gates, outs):
        o_ref[...] = (_dot(m_ref[...], w2) * gate_ref[...]).astype(o_ref.dtype)


def _ffn(xs_list, gate_list, w1, w3, w2, layer):
    ns = len(xs_list)
    e, _, d = xs_list[0].shape
    ff = w1.shape[-1]
    tf = _tile(ff, 512)
    tn = _tile(d, 1024)
    rows = [x.shape[1] for x in xs_list]
    mids = pl.pallas_call(
        functools.partial(_ffn_up_kernel, n_streams=ns),
        grid=(e, ff // tf),
        in_specs=[pl.BlockSpec((None, r, d), lambda ei, f: (ei, 0, 0)) for r in rows] + [
            pl.BlockSpec((None, None, d, tf), lambda ei, f: (layer, ei, 0, f)),
            pl.BlockSpec((None, None, d, tf), lambda ei, f: (layer, ei, 0, f)),
        ],
        out_specs=[pl.BlockSpec((None, r, tf), lambda ei, f: (ei, 0, f)) for r in rows],
        out_shape=[jax.ShapeDtypeStruct((e, r, ff), BF16) for r in rows],
        compiler_params=_cp("parallel", "arbitrary"),
        name="moe_ffn_up",
    )(*xs_list, w1, w3)
    return pl.pallas_call(
        functools.partial(_ffn_down_kernel, n_streams=ns),
        grid=(e, d // tn),
        in_specs=[pl.BlockSpec((None, r, ff), lambda ei, j: (ei, 0, 0)) for r in rows]
                 + [pl.BlockSpec((None, r, 1), lambda ei, j: (ei, 0, 0)) for r in rows]
                 + [pl.BlockSpec((None, None, ff, tn), lambda ei, j: (layer, ei, 0, j))],
        out_specs=[pl.BlockSpec((None, r, tn), lambda ei, j: (ei, 0, j)) for r in rows],
        out_shape=[jax.ShapeDtypeStruct((e, r, d), BF16) for r in rows],
        compiler_params=_cp("parallel", "arbitrary"),
        name="moe_ffn_down",
    )(*mids, *gate_list, w2)


def _combine_kernel(slot_ref, y_ref, x_ref, gate_ref, o_ref, pt_ref, *, cap):
    e = slot_ref.shape[1]
    tm = slot_ref.shape[0]
    r0 = pl.multiple_of(pl.program_id(2) * tm, tm)

    @pl.when(pl.program_id(1) == 0)
    def _():
        slot = slot_ref[...]
        lane = lax.broadcasted_iota(I32, (tm, cap), 1)
        for ei in range(e):
            pt_ref[pl.ds(r0, tm), ei * cap:(ei + 1) * cap] = jnp.where(lane == slot[:, ei:ei + 1], 1.0, 0.0).astype(BF16)

    y = y_ref[...]
    acc = _dot(pt_ref[pl.ds(r0, tm), :], y.reshape(e * cap, y.shape[2]))
    o_ref[...] = x_ref[...] + gate_ref[...] * acc


def _combine(slot_te, ys, x, gate, cap):
    b, t, d = x.shape
    e = slot_te.shape[2]
    tm = _tile(t, 512, 8)
    tn = _tile(d, 512)
    return pl.pallas_call(
        functools.partial(_combine_kernel, cap=cap),
        grid=(b, d // tn, t // tm),
        in_specs=[
            pl.BlockSpec((None, tm, e), lambda bi, j, i: (bi, i, 0)),
            pl.BlockSpec((e, cap, tn), lambda bi, j, i: (0, bi, j)),
            pl.BlockSpec((None, tm, tn), lambda bi, j, i: (bi, i, j)),
            pl.BlockSpec((None, 1, tn), lambda bi, j, i: (bi, 0, j)),
        ],
        out_specs=pl.BlockSpec((None, tm, tn), lambda bi, j, i: (bi, i, j)),
        out_shape=jax.ShapeDtypeStruct((b, t, d), F32),
        scratch_shapes=[pltpu.VMEM((t, e * cap), BF16)],
        compiler_params=_cp("parallel", "arbitrary", "arbitrary"),
        name="moe_combine",
    )(slot_te, ys, x, gate)


def kernel(x, c, ctx, c_ctx, ada_w, ada_b, norm1_g, norm2_g, w_in, da_lambda, da_subln_g, wb_sink, na_rpb, s5_a_re, s5_a_im, s5_log_step, s5_b_re, s5_b_im, s5_c_re, s5_c_im, s5_d, s5_glu_w, w_branch, w_out, w_router, w_e1, w_e3, w_e2, final_g):
    b, n, d = x.shape
    n_ctx = ctx.shape[1]
    depth = ada_w.shape[0]
    assert b + 1 <= ADA_ROWS and n % GRID_W == 0

    cs = jnp.zeros((ADA_ROWS, d), F32).at[:b].set(c).at[b].set(c_ctx)
    mods = _ada_mod(cs, ada_w, ada_b)
    rope_tab = _rope_tables(n)
    bias_tab = jax.vmap(_na_bias_table)(na_rpb)
    s5_tab = jax.vmap(_s5_tables)(s5_a_re, s5_a_im, s5_log_step, s5_b_re, s5_b_im, s5_c_re, s5_c_im)

    for l in range(depth):
        with_ctx = l < depth - 1
        mod_x = [mods[l, :b, k * d:(k + 1) * d].reshape(b, 1, d) for k in range(6)]
        mod_c = [jnp.broadcast_to(mods[l, b, k * d:(k + 1) * d], (b, 1, d)) for k in range(6)]

        hx = _norm_mod(x, norm1_g[l], mod_x[0], mod_x[1])
        hc = _norm_mod(ctx, norm1_g[l], mod_c[0], mod_c[1])
        proj = []
        for h, t in ((hx, n), (hc, n_ctx)):
            h2d = h.reshape(b * t, d)
            proj.append([_mm(h2d, w_in, l, col0, ncols, tn, dt).reshape(b, t, ncols) for col0, ncols, tn, dt in MIX_PROJ])
        (za_x, zb_x, zn_x, zs_x), (za_c, zb_c, zn_c, zs_c) = proj
        qa_x = _rope(za_x, IN_WIDTHS[0] + IN_WIDTHS[1], rope_tab)
        qb_x = _rope(zb_x, IN_WIDTHS[3] + IN_WIDTHS[4], rope_tab)

        lam_init = 0.8 - 0.6 * math.exp(-0.3 * l)
        lv = da_lambda[l].astype(F32)
        lam = jnp.exp(jnp.sum(lv[0] * lv[1])) - jnp.exp(jnp.sum(lv[2] * lv[3])) + lam_init

        ya_x = _diff_attn(qa_x, za_c, qa_x, za_x, lam, da_subln_g[l], lam_init, True)
        yb_x = _win_attn(qb_x, zb_c, qb_x, zb_x, wb_sink[l], True)
        yc_x = _na_attn(zn_x, zn_c, zn_x, bias_tab, l, True)
        ys_c, ys_x = _s5_scan(zs_x, zs_c, s5_tab, l, with_ctx)
        yd_x = _s5_out(ys_x, zs_x, s5_d[l], s5_glu_w[l])
        gx = _merge(hx.reshape(b * n, d), [t.reshape(b * n, -1) for t in (ya_x, yb_x, yc_x, yd_x)], w_in, w_branch, l)
        x = _mm_res(gx, w_out, l, x, mod_x[2])

        if with_ctx:
            ya_c = _diff_attn(za_c, za_c, None, None, lam, da_subln_g[l], lam_init, False)
            yb_c = _win_attn(zb_c, zb_c, None, None, wb_sink[l], False)
            yc_c = _na_attn(zn_c, zn_c, None, None, l, False)
            yd_c = _s5_out(ys_c, zs_c, s5_d[l], s5_glu_w[l])
            gc = _merge(hc.reshape(b * n_ctx, d), [t.reshape(b * n_ctx, -1) for t in (ya_c, yb_c, yc_c, yd_c)],
                        w_in, w_branch, l)
            ctx = _mm_res(gc, w_out, l, ctx, mod_c[2])

        w_router_t = jnp.transpose(w_router[l]).astype(F32)
        streams = [(x, mod_x)] + ([(ctx, mod_c)] if with_ctx else [])
        routed = []
        for s, mod in streams:
            t = s.shape[1]
            cap = EC_CAPACITY_FACTOR * t // N_EXPERTS
            h2, logits_t = _norm_router(s, norm2_g[l], mod[3], mod[4], w_router_t)
            slot, aff = _topk(logits_t, b, cap)
            xs, gate = _gather(slot, aff, h2, cap)
            routed.append((jnp.transpose(slot, (0, 2, 1)), xs, gate, cap))
        ys = _ffn([r[1] for r in routed], [r[2] for r in routed], w_e1, w_e3, w_e2, l)
        x = _combine(routed[0][0], ys[0], x, mod_x[5], routed[0][3])
        if with_ctx:
            ctx = _combine(routed[1][0], ys[1], ctx, mod_c[5], routed[1][3])

    return _final_norm(x, final_g)
```

```python
import functools
import math

import jax
import jax.numpy as jnp
from jax import lax
from jax.experimental import pallas as pl
from jax.experimental.pallas import tpu as pltpu

F32 = jnp.float32
BF16 = jnp.bfloat16
I32 = jnp.int32

GRID_W = 64
EPS = 1e-6
NEG_INF = -1e30
LOG2E = math.log2(math.e)
ROPE_BASE = 10000.0

DA_HEADS = 4
DA_QK_DIM = 64
DA_V_DIM = 2 * DA_QK_DIM
WB_HEADS = 8
WB_KV_HEADS = 2
WB_DIM = 64
WB_WINDOW = 128
WB_BLOCK = 128
NA_HEADS = 8
NA_DIM = 64
NA_ROWS = 8
NA_COLS = 16
S5_GROUPS = 32
S5_GROUP_CH = 16
S5_STATE = 64
S5_WIDTH = S5_GROUPS * S5_GROUP_CH
S5_CHUNK = 8
N_BRANCH = 4
BRANCH_WIDTH = 512
N_EXPERTS = 16
EC_CAPACITY_FACTOR = 2

IN_WIDTHS = (
    2 * DA_HEADS * DA_QK_DIM, 2 * DA_HEADS * DA_QK_DIM, DA_HEADS * DA_V_DIM,
    WB_HEADS * WB_DIM, WB_KV_HEADS * WB_DIM, WB_KV_HEADS * WB_DIM,
    NA_HEADS * NA_DIM, NA_HEADS * NA_DIM, NA_HEADS * NA_DIM,
    S5_WIDTH,
)
N_MIX_IN = sum(IN_WIDTHS)
_OFFS = [0]
for _w in IN_WIDTHS:
    _OFFS.append(_OFFS[-1] + _w)
(OFF_DA_Q, OFF_DA_K, OFF_DA_V, OFF_WB_Q, OFF_WB_K, OFF_WB_V, OFF_NA_Q, OFF_NA_K, OFF_NA_V, OFF_S5, _) = _OFFS

MIX_PROJ = (
    (OFF_DA_Q, OFF_WB_Q - OFF_DA_Q, 768, BF16, IN_WIDTHS[0] + IN_WIDTHS[1]),
    (OFF_WB_Q, OFF_NA_Q - OFF_WB_Q, 768, BF16, IN_WIDTHS[3] + IN_WIDTHS[4]),
    (OFF_NA_Q, OFF_S5 - OFF_NA_Q, 768, BF16, 0),
    (OFF_S5, S5_WIDTH, 256, F32, 0),
)

LANE = 128
SUBLANE = 8
ADA_ROWS = 8
VMEM_LIMIT = 56 * 1024 * 1024

NN = (((1,), (0,)), ((), ()))
NT = (((1,), (1,)), ((), ()))


def _cp(*sem):
    return pltpu.CompilerParams(dimension_semantics=sem, vmem_limit_bytes=VMEM_LIMIT)


def _tile(n, pref, mult=LANE):
    if n <= pref:
        return n
    t = (pref // mult) * mult
    while t >= mult:
        if n % t == 0:
            return t
        t -= mult
    return n


def _split(a):
    hi = a.astype(BF16)
    lo = (a - hi.astype(F32)).astype(BF16)
    return hi, lo


def _dot(a, b, dims=NN):
    return lax.dot_general(a, b, dims, preferred_element_type=F32)


def _lane_group(lane, width):
    return lax.shift_right_logical(lane, int(math.log2(width)))


def _count(mask, axis):
    return jnp.sum(jnp.where(mask, 1.0, 0.0), axis=axis, keepdims=True)


def _dot3(a, b, dims=NN):
    ah, al = _split(a)
    bh, bl = _split(b)
    return _dot(ah, bh, dims) + (_dot(ah, bl, dims) + _dot(al, bh, dims))


def _ada_kernel(c_ref, w_ref, b_ref, o_ref):
    c = c_ref[...]
    s = c / (1.0 + jnp.exp(-c))
    o_ref[...] = _dot3(s, w_ref[...]) + b_ref[...]


def _ada_mod(cs, ada_w, ada_b):
    depth, d, n6 = ada_w.shape
    tn = _tile(n6, 512)
    return pl.pallas_call(
        _ada_kernel,
        grid=(depth, n6 // tn),
        in_specs=[
            pl.BlockSpec((ADA_ROWS, d), lambda l, j: (0, 0)),
            pl.BlockSpec((None, d, tn), lambda l, j: (l, 0, j)),
            pl.BlockSpec((None, 1, tn), lambda l, j: (l, 0, j)),
        ],
        out_specs=pl.BlockSpec((None, ADA_ROWS, tn), lambda l, j: (l, 0, j)),
        out_shape=jax.ShapeDtypeStruct((depth, ADA_ROWS, n6), F32),
        compiler_params=_cp("arbitrary", "arbitrary"),
        name="ada_mod",
    )(cs, ada_w, ada_b.reshape(depth, 1, n6))


def _norm_mod_rows(x, g, sh, sc):
    y = x * lax.rsqrt(jnp.mean(x * x, axis=-1, keepdims=True) + EPS) * g
    return y * (1.0 + sc) + sh


def _norm_mod_kernel(x_ref, g_ref, sh_ref, sc_ref, o_ref):
    o_ref[...] = _norm_mod_rows(x_ref[...], g_ref[...], sh_ref[...], sc_ref[...]).astype(o_ref.dtype)


def _norm_mod(x, g, sh, sc):
    b, t, d = x.shape
    tr = _tile(t, 256, 8)
    return pl.pallas_call(
        _norm_mod_kernel,
        grid=(b, t // tr),
        in_specs=[
            pl.BlockSpec((None, tr, d), lambda bi, i: (bi, i, 0)),
            pl.BlockSpec((1, d), lambda bi, i: (0, 0)),
            pl.BlockSpec((None, 1, d), lambda bi, i: (bi, 0, 0)),
            pl.BlockSpec((None, 1, d), lambda bi, i: (bi, 0, 0)),
        ],
        out_specs=pl.BlockSpec((None, tr, d), lambda bi, i: (bi, i, 0)),
        out_shape=jax.ShapeDtypeStruct((b, t, d), BF16),
        compiler_params=_cp("parallel", "parallel"),
        name="norm_mod",
    )(x, g.reshape(1, d), sh, sc)


def _norm_router_kernel(x_ref, g_ref, sh_ref, sc_ref, wrt_ref, h_ref, lg_ref):
    h = _norm_mod_rows(x_ref[...], g_ref[...], sh_ref[...], sc_ref[...])
    h_ref[...] = h.astype(h_ref.dtype)
    lg_ref[...] = _dot3(wrt_ref[...], h, NT)


def _norm_router(x, g, sh, sc, w_router_t):
    b, t, d = x.shape
    e = w_router_t.shape[0]
    tr = _tile(t, 256)
    nt = t // tr
    return pl.pallas_call(
        _norm_router_kernel,
        grid=(b, nt),
        in_specs=[
            pl.BlockSpec((None, tr, d), lambda bi, i: (bi, i, 0)),
            pl.BlockSpec((1, d), lambda bi, i: (0, 0)),
            pl.BlockSpec((None, 1, d), lambda bi, i: (bi, 0, 0)),
            pl.BlockSpec((None, 1, d), lambda bi, i: (bi, 0, 0)),
            pl.BlockSpec((e, d), lambda bi, i: (0, 0)),
        ],
        out_specs=[
            pl.BlockSpec((None, tr, d), lambda bi, i: (bi, i, 0)),
            pl.BlockSpec((e, tr), lambda bi, i: (0, bi * nt + i)),
        ],
        out_shape=[jax.ShapeDtypeStruct((b, t, d), BF16), jax.ShapeDtypeStruct((e, b * t), F32)],
        compiler_params=_cp("parallel", "parallel"),
        name="norm_router",
    )(x, g.reshape(1, d), sh, sc, w_router_t)


def _final_norm_kernel(x_ref, g_ref, o_ref):
    x = x_ref[...]
    o_ref[...] = x * lax.rsqrt(jnp.mean(x * x, axis=-1, keepdims=True) + EPS) * g_ref[...]


def _final_norm(x, g):
    b, t, d = x.shape
    tr = _tile(t, 256, 8)
    return pl.pallas_call(
        _final_norm_kernel,
        grid=(b, t // tr),
        in_specs=[pl.BlockSpec((None, tr, d), lambda bi, i: (bi, i, 0)), pl.BlockSpec((1, d), lambda bi, i: (0, 0))],
        out_specs=pl.BlockSpec((None, tr, d), lambda bi, i: (bi, i, 0)),
        out_shape=jax.ShapeDtypeStruct((b, t, d), F32),
        compiler_params=_cp("parallel", "parallel"),
        name="final_norm",
    )(x, g.reshape(1, d))


def _mm_kernel(a_ref, w_ref, o_ref, wb_ref):
    @pl.when(pl.program_id(1) == 0)
    def _():
        wb_ref[...] = w_ref[...].astype(BF16)

    o_ref[...] = _dot(a_ref[...], wb_ref[...]).astype(o_ref.dtype)


def _mm_rope_kernel(a_ref, w_ref, c_ref, sa_ref, sb_ref, o_ref, wb_ref, *, n_rope):
    @pl.when(pl.program_id(1) == 0)
    def _():
        wb_ref[...] = w_ref[...].astype(BF16)

    z = _dot(a_ref[...], wb_ref[...])
    tn = z.shape[1]
    col0 = pl.program_id(0) * tn
    cos, sa, sb = c_ref[...], sa_ref[...], sb_ref[...]
    for cblk in range(tn // LANE):
        x = z[:, cblk * LANE:(cblk + 1) * LANE]
        y = x * cos + pltpu.roll(x, LANE - 16, 1) * sa + pltpu.roll(x, 16, 1) * sb
        o_ref[:, cblk * LANE:(cblk + 1) * LANE] = jnp.where(col0 + cblk * LANE < n_rope, y, x).astype(o_ref.dtype)


def _mm(a, w, layer, col0, ncols, tn, out_dtype, rope=None):
    m, k = a.shape
    assert ncols % tn == 0 and col0 % tn == 0
    tm = _tile(m, 1024, 8)
    j0 = col0 // tn
    in_specs = [
        pl.BlockSpec((tm, k), lambda j, i: (i, 0)),
        pl.BlockSpec((None, k, tn), lambda j, i: (layer, 0, j0 + j)),
    ]
    args = [a, w]
    body = _mm_kernel
    if rope is not None:
        tables, n_rope, seq = rope
        assert seq % tm == 0
        per_seq = seq // tm
        in_specs += [pl.BlockSpec((tm, LANE), lambda j, i: (i % per_seq, 0))] * 3
        args += list(tables)
        body = functools.partial(_mm_rope_kernel, n_rope=n_rope)
    return pl.pallas_call(
        body,
        grid=(ncols // tn, m // tm),
        in_specs=in_specs,
        out_specs=pl.BlockSpec((tm, tn), lambda j, i: (i, j)),
        out_shape=jax.ShapeDtypeStruct((m, ncols), out_dtype),
        scratch_shapes=[pltpu.VMEM((k, tn), BF16)],
        compiler_params=_cp("arbitrary", "arbitrary"),
        name="mm_in",
    )(*args)


def _mm_res_kernel(a_ref, w_ref, x_ref, gate_ref, o_ref, wb_ref):
    @pl.when(pl.program_id(1) == 0)
    def _():
        wb_ref[...] = w_ref[...].astype(BF16)

    o_ref[...] = x_ref[...] + gate_ref[...] * _dot(a_ref[...], wb_ref[...])


def _mm_res(a, w, layer, x, gate):
    b, t, n = x.shape
    m, k = a.shape
    tn = _tile(n, 1024)
    tm = _tile(t, 1024, 8)
    per_b = t // tm
    out = pl.pallas_call(
        _mm_res_kernel,
        grid=(n // tn, m // tm),
        in_specs=[
            pl.BlockSpec((tm, k), lambda j, i: (i, 0)),
            pl.BlockSpec((None, k, tn), lambda j, i: (layer, 0, j)),
            pl.BlockSpec((tm, tn), lambda j, i: (i, j)),
            pl.BlockSpec((None, 1, tn), lambda j, i: (i // per_b, 0, j)),
        ],
        out_specs=pl.BlockSpec((tm, tn), lambda j, i: (i, j)),
        out_shape=jax.ShapeDtypeStruct((m, n), F32),
        scratch_shapes=[pltpu.VMEM((k, tn), BF16)],
        compiler_params=_cp("arbitrary", "arbitrary"),
        name="mm_out_res",
    )(a, w, x.reshape(m, n), gate)
    return out.reshape(b, t, n)


def _rope_tables(n):
    pos = jnp.arange(n)
    rows, cols = (pos // GRID_W).astype(F32), (pos % GRID_W).astype(F32)
    half = DA_QK_DIM // 2
    inv_freq = jnp.power(ROPE_BASE, -jnp.arange(0, half, 2, dtype=F32) / half)
    ang_r, ang_c = rows[:, None] * inv_freq[None, :], cols[:, None] * inv_freq[None, :]
    cos64 = jnp.concatenate([jnp.cos(ang_r)] * 2 + [jnp.cos(ang_c)] * 2, axis=-1)
    sin64 = jnp.concatenate([jnp.sin(ang_r)] * 2 + [jnp.sin(ang_c)] * 2, axis=-1)
    cos, sin = jnp.tile(cos64, (1, 2)), jnp.tile(sin64, (1, 2))
    first = (jnp.arange(LANE) % 32) < 16
    return cos, jnp.where(first, -sin, 0.0), jnp.where(first, 0.0, sin)


def _diff_attn_kernel(*refs, has_x, post_scale):
    if has_x:
        lam_ref, g_ref, q_ref, kc_ref, vc_ref, kx_ref, vx_ref, o_ref = refs
    else:
        lam_ref, g_ref, q_ref, kc_ref, vc_ref, o_ref = refs
    q = q_ref[...].astype(F32) * (DA_QK_DIM ** -0.5 * LOG2E)
    lane = lax.broadcasted_iota(I32, q.shape, 1)
    if has_x:
        k = jnp.concatenate([kc_ref[...], kx_ref[...]], axis=0)
        v = jnp.concatenate([vc_ref[...], vx_ref[...]], axis=0)
    else:
        k, v = kc_ref[...], vc_ref[...]

    nk = k.shape[0]
    kc_len = _tile(nk, 768)

    def attend(qm):
        m = l = acc = None
        for k0 in range(0, nk, kc_len):
            s = _dot(qm, k[k0:k0 + kc_len], NT)
            m_c = jnp.max(s, axis=-1, keepdims=True)
            if m is None:
                m = m_c
                p = jnp.exp2(s - m)
                l = jnp.sum(p, axis=-1, keepdims=True)
                acc = _dot(p.astype(BF16), v[k0:k0 + kc_len])
            else:
                m_new = jnp.maximum(m, m_c)
                alpha = jnp.exp2(m - m_new)
                p = jnp.exp2(s - m_new)
                l = alpha * l + jnp.sum(p, axis=-1, keepdims=True)
                acc = alpha * acc + _dot(p.astype(BF16), v[k0:k0 + kc_len])
                m = m_new
        return acc / l

    o = attend(jnp.where(lane < DA_QK_DIM, q, 0.0).astype(BF16)) \
        - lam_ref[...] * attend(jnp.where(lane >= DA_QK_DIM, q, 0.0).astype(BF16))
    y = o * lax.rsqrt(jnp.mean(o * o, axis=-1, keepdims=True) + EPS) * g_ref[...] * post_scale
    o_ref[...] = y.astype(o_ref.dtype)


def _diff_attn(q_arr, zc, kx_arr, vx_arr, lam, subln_g, lam_init, has_x):
    b, tq_all, _ = q_arr.shape
    c = zc.shape[1]
    tq = _tile(tq_all, 512, 8)
    q_blk0, kc_blk0, vc_blk0 = 0, IN_WIDTHS[0] // LANE, 2 * IN_WIDTHS[0] // LANE
    kx_blk0 = kc_blk0
    in_specs = [
        pl.BlockSpec((1, 1), lambda bi, h, i: (0, 0)),
        pl.BlockSpec((1, DA_V_DIM), lambda bi, h, i: (0, 0)),
        pl.BlockSpec((None, tq, LANE), lambda bi, h, i: (bi, i, q_blk0 + h)),
        pl.BlockSpec((None, c, LANE), lambda bi, h, i: (bi, 0, kc_blk0 + h)),
        pl.BlockSpec((None, c, LANE), lambda bi, h, i: (bi, 0, vc_blk0 + h)),
    ]
    args = [lam.reshape(1, 1), subln_g.reshape(1, DA_V_DIM), q_arr, zc, zc]
    if has_x:
        n = kx_arr.shape[1]
        in_specs += [
            pl.BlockSpec((None, n, LANE), lambda bi, h, i: (bi, 0, kx_blk0 + h)),
            pl.BlockSpec((None, n, LANE), lambda bi, h, i: (bi, 0, vc_blk0 + h)),
        ]
        args += [kx_arr, vx_arr]
    return pl.pallas_call(
        functools.partial(_diff_attn_kernel, has_x=has_x, post_scale=1.0 - lam_init),
        grid=(b, DA_HEADS, tq_all // tq),
        in_specs=in_specs,
        out_specs=pl.BlockSpec((None, tq, LANE), lambda bi, h, i: (bi, i, h)),
        out_shape=jax.ShapeDtypeStruct((b, tq_all, DA_HEADS * DA_V_DIM), BF16),
        compiler_params=_cp("parallel", "parallel", "arbitrary"),
        name="diff_attn_x" if has_x else "diff_attn_c",
    )(*args)


def _group_cols(x, g):
    x = x.astype(F32)
    lane = lax.broadcasted_iota(I32, x.shape, 1)
    own = jnp.where(_lane_group(lane, WB_DIM) == g, x, pltpu.roll(x, WB_DIM, 1))
    return jnp.concatenate([own, own], axis=1).astype(BF16)


def _win_attn_kernel(*refs, has_x, n_tok):
    if has_x:
        sink_ref, q_ref, kc_ref, vc_ref, kp_ref, k0_ref, kn_ref, vp_ref, v0_ref, vn_ref, o_ref = refs
    else:
        sink_ref, q_ref, kc_ref, vc_ref, o_ref = refs
    i = pl.program_id(1)
    rep = WB_HEADS // WB_KV_HEADS
    gw = rep * WB_DIM
    q_all = q_ref[...].astype(F32) * (WB_DIM ** -0.5 * LOG2E)
    blk = q_all.shape[0]
    lane = lax.broadcasted_iota(I32, (blk, gw), 1)
    if has_x:
        k_in = jnp.concatenate([kp_ref[...], k0_ref[...], kn_ref[...], kc_ref[...]], axis=0)
        v_in = jnp.concatenate([vp_ref[...], v0_ref[...], vn_ref[...], vc_ref[...]], axis=0)
        q_pos = i * blk + (lax.broadcasted_iota(I32, (rep * blk, 3 * blk), 0) & (blk - 1))
        k_pos = (i - 1) * blk + lax.broadcasted_iota(I32, (rep * blk, 3 * blk), 1)
        valid = (jnp.abs(q_pos - k_pos) <= WB_WINDOW) & (k_pos >= 0) & (k_pos < n_tok)
    else:
        k_in, v_in = kc_ref[...], vc_ref[...]
    outs = []
    for g in range(WB_KV_HEADS):
        q = q_all[:, g * gw:(g + 1) * gw]
        k, v = _group_cols(k_in, g), _group_cols(v_in, g)
        mine = [_lane_group(lane, WB_DIM) == r for r in range(rep)]
        qs = jnp.concatenate([jnp.where(mine[r], q, 0.0) for r in range(rep)], axis=0).astype(BF16)
        sink = jnp.concatenate([jnp.broadcast_to(sink_ref[:, (g * rep + r) * WB_DIM:(g * rep + r) * WB_DIM + 1], (blk, 1))
                                for r in range(rep)], axis=0)
        s = _dot(qs, k, NT)
        if has_x:
            s = jnp.concatenate([jnp.where(valid, s[:, :3 * blk], NEG_INF), s[:, 3 * blk:]], axis=1)
        m = jnp.maximum(jnp.max(s, axis=-1, keepdims=True), sink)
        pr = jnp.exp2(s - m)
        l = jnp.sum(pr, axis=-1, keepdims=True) + jnp.exp2(sink - m)
        o = _dot(pr.astype(BF16), v) / l
        acc = jnp.zeros(q.shape, F32)
        for r in range(rep):
            acc = acc + jnp.where(mine[r], o[r * blk:(r + 1) * blk], 0.0)
        outs.append(acc)
    o_ref[...] = jnp.concatenate(outs, axis=1).astype(o_ref.dtype)


def _win_attn(q_arr, zc, kx_arr, vx_arr, sink, has_x):
    b, tq_all, _ = q_arr.shape
    c = zc.shape[1]
    blk = WB_BLOCK
    nb = tq_all // blk
    qw = WB_HEADS * WB_DIM
    sink_arr = jnp.repeat(sink.astype(F32) * LOG2E, WB_DIM).reshape(1, qw)
    k_blk, v_blk = qw // LANE, qw // LANE + 1
    in_specs = [
        pl.BlockSpec((1, qw), lambda bi, i: (0, 0)),
        pl.BlockSpec((None, blk, qw), lambda bi, i: (bi, i, 0)),
        pl.BlockSpec((None, c, LANE), lambda bi, i: (bi, 0, k_blk)),
        pl.BlockSpec((None, c, LANE), lambda bi, i: (bi, 0, v_blk)),
    ]
    args = [sink_arr, q_arr, zc, zc]
    n_tok = 0
    if has_x:
        n_tok = kx_arr.shape[1]
        prev = lambda i: jnp.maximum(i - 1, 0)
        nxt = lambda i: jnp.minimum(i + 1, nb - 1)
        for arr, cb in ((kx_arr, k_blk), (vx_arr, v_blk)):
            in_specs += [
                pl.BlockSpec((None, blk, LANE), lambda bi, i, cb=cb: (bi, prev(i), cb)),
                pl.BlockSpec((None, blk, LANE), lambda bi, i, cb=cb: (bi, i, cb)),
                pl.BlockSpec((None, blk, LANE), lambda bi, i, cb=cb: (bi, nxt(i), cb)),
            ]
            args += [arr, arr, arr]
    return pl.pallas_call(
        functools.partial(_win_attn_kernel, has_x=has_x, n_tok=n_tok),
        grid=(b, nb),
        in_specs=in_specs,
        out_specs=pl.BlockSpec((None, blk, qw), lambda bi, i: (bi, i, 0)),
        out_shape=jax.ShapeDtypeStruct((b, tq_all, qw), BF16),
        compiler_params=_cp("parallel", "arbitrary"),
        name="win_attn_x" if has_x else "win_attn_c",
    )(*args)


NA_QROWS = 4
NA_KROWS = 12


def _na_kernel(*refs, has_x, n_blocks):
    if has_x:
        q_ref, kc_ref, vc_ref, k_ref, v_ref, t2_ref, o_ref = refs
    else:
        q_ref, kc_ref, vc_ref, o_ref = refs
    tq = q_ref.shape[0]
    lane = lax.broadcasted_iota(I32, (tq, LANE), 1)
    per = LANE // NA_DIM
    if has_x:
        p = pl.program_id(1)
        is_first, is_last = p == 0, p == n_blocks - 1
        n_rows = n_blocks * NA_QROWS
        start = jnp.where(is_first, 0, jnp.where(is_last, n_rows - NA_KROWS, p * NA_QROWS - NA_ROWS // 2))
        off = pl.multiple_of(start * GRID_W, GRID_W)
        nk = NA_KROWS * GRID_W
        row_i = lax.broadcasted_iota(I32, (tq, nk), 0)
        key_i = lax.broadcasted_iota(I32, (tq, nk), 1)
        qi, col = _lane_group(row_i, GRID_W), row_i & (GRID_W - 1)
        kp, w = _lane_group(key_i, GRID_W), key_i & (GRID_W - 1)
        lo = jnp.where(is_first, 0, jnp.where(is_last, NA_KROWS - NA_ROWS, qi))
        col_start = jnp.clip(col - NA_COLS // 2, 0, GRID_W - NA_COLS)
        valid = (w >= col_start) & (w < col_start + NA_COLS) & (kp >= lo) & (kp < lo + NA_ROWS)
        n_pair = t2_ref.shape[1]
        first_pair = [jnp.where(is_first, NA_ROWS - 1 - i, jnp.where(is_last, -1 - i, NA_ROWS // 2 - 1 - i))
                      for i in range(NA_QROWS)]
    for j in range(NA_HEADS // per):
        cols = slice(j * LANE, (j + 1) * LANE)
        q = q_ref[:, cols].astype(F32) * (NA_DIM ** -0.5 * LOG2E)
        if has_x:
            k = jnp.concatenate([k_ref[pl.ds(off, nk), cols], kc_ref[:, cols]], axis=0)
            v = jnp.concatenate([v_ref[pl.ds(off, nk), cols], vc_ref[:, cols]], axis=0)
        else:
            k, v = kc_ref[:, cols], vc_ref[:, cols]
        acc = jnp.zeros(q.shape, F32)
        for hh in range(per):
            mine = _lane_group(lane, NA_DIM) == hh
            qh = jnp.where(mine, q, 0.0).astype(BF16)
            s = _dot(qh, k, NT)
            if has_x:
                head = j * per + hh
                bias = jnp.concatenate([
                    jnp.concatenate([t2_ref[head, jnp.clip(first_pair[i] + 2 * m, 0, n_pair - 1)]
                                     for m in range(NA_KROWS // 2)], axis=1)
                    for i in range(NA_QROWS)], axis=0)
                s = jnp.concatenate([jnp.where(valid, s[:, :nk] + bias, NEG_INF), s[:, nk:]], axis=1)
            m = jnp.max(s, axis=-1, keepdims=True)
            pr = jnp.exp2(s - m)
            l = jnp.sum(pr, axis=-1, keepdims=True)
            acc = acc + jnp.where(mine, _dot(pr.astype(BF16), v) / l, 0.0)
        o_ref[:, cols] = acc.astype(o_ref.dtype)


def _na_bias_table(rpb):
    r = rpb.astype(F32) * LOG2E
    edge = GRID_W - NA_COLS
    ext = jnp.concatenate([jnp.repeat(r[..., :1], edge, -1), r, jnp.repeat(r[..., -1:], edge, -1)], axis=-1)
    by_col = jnp.stack([ext[..., GRID_W - 1 - c:2 * GRID_W - 1 - c] for c in range(GRID_W)], axis=-2)
    return jnp.concatenate([by_col[:, :-1], by_col[:, 1:]], axis=-1)


def _na_attn(q_arr, zc, zx, bias_tab, layer, has_x):
    b, tq_all, _ = q_arr.shape
    c = zc.shape[1]
    hw = NA_HEADS * NA_DIM
    tq = NA_QROWS * GRID_W if has_x else _tile(tq_all, 256, 8)
    nq = tq_all // tq
    in_specs = [
        pl.BlockSpec((None, tq, hw), lambda bi, r: (bi, r, 0)),
        pl.BlockSpec((None, c, hw), lambda bi, r: (bi, 0, 1)),
        pl.BlockSpec((None, c, hw), lambda bi, r: (bi, 0, 2)),
    ]
    args = [q_arr, zc, zc]
    if has_x:
        n = zx.shape[1]
        assert tq_all % tq == 0 and nq * NA_QROWS >= NA_KROWS
        in_specs += [
            pl.BlockSpec((None, n, hw), lambda bi, r: (bi, 0, 1)),
            pl.BlockSpec((None, n, hw), lambda bi, r: (bi, 0, 2)),
            pl.BlockSpec((None,) + bias_tab.shape[1:], lambda bi, r: (layer, 0, 0, 0, 0)),
        ]
        args += [zx, zx, bias_tab]
    return pl.pallas_call(
        functools.partial(_na_kernel, has_x=has_x, n_blocks=nq),
        grid=(b, nq),
        in_specs=in_specs,
        out_specs=pl.BlockSpec((None, tq, hw), lambda bi, r: (bi, r, 0)),
        out_shape=jax.ShapeDtypeStruct((b, tq_all, hw), BF16),
        compiler_params=_cp("parallel", "arbitrary"),
        name="na_attn_x" if has_x else "na_attn_c",
    )(*args)


def _s5_kernel(*refs, ctx_out):
    if ctx_out:
        (uc_ref, ux_ref, d_ref, b1_ref, b2_ref, a1_ref, a2_ref, a3_ref, cp_ref, scat_ref, yc_ref, yx_ref,
         kbig, bb1, bb2, cb_ref, xc, xx, s1c, s2c, s1x, s2x, hc, hx) = refs
    else:
        (uc_ref, ux_ref, d_ref, b1_ref, b2_ref, a1_ref, a2_ref, a3_ref, cp_ref, scat_ref, yx_ref,
         kbig, bb1, bb2, cb_ref, xc, xx, s1c, s2c, s1x, s2x, hc, hx) = refs
        yc_ref = None
    dirn = pl.program_id(1)
    ell, hch = S5_CHUNK, S5_GROUP_CH
    gq = LANE // hch
    nb = ux_ref.shape[0]
    streams = ((uc_ref, xc, s1c, s2c, hc, yc_ref, uc_ref.shape[1] // ell),
               (ux_ref, xx, s1x, s2x, hx, yx_ref, ux_ref.shape[1] // ell))

    for i in range(ell):
        for j in range(ell):
            kbig[i * LANE:(i + 1) * LANE, j * LANE:(j + 1) * LANE] = d_ref[j - i + ell - 1]
    bb1[...] = jnp.zeros(bb1.shape, BF16)
    bb2[...] = jnp.zeros(bb2.shape, BF16)
    for i in range(ell):
        for gl in range(gq):
            r0 = i * LANE + gl * hch
            bb1[r0:r0 + hch, gl * LANE:(gl + 1) * LANE] = b1_ref[gl, i * hch:(i + 1) * hch, :].astype(BF16)
            bb2[r0:r0 + hch, gl * LANE:(gl + 1) * LANE] = b2_ref[gl, i * hch:(i + 1) * hch, :].astype(BF16)
    for gl in range(gq):
        cb_ref[gl * LANE:(gl + 1) * LANE, :] = _dot(cp_ref[gl], scat_ref[gl]).astype(BF16)

    @pl.when(dirn == 0)
    def _():
        for u_ref, x_s, _, _, _, _, nc in streams:
            for b in range(nb):
                for i in range(ell):
                    x_s[b * nc:(b + 1) * nc, i * LANE:(i + 1) * LANE] = u_ref[b, pl.ds(i, nc, stride=ell), :].astype(BF16)

    cpt = SUBLANE // nb
    for _, x_s, s1, s2, _, _, nc in streams:
        loc1 = _dot(x_s[...], bb1[...])
        loc2 = _dot(x_s[...], bb2[...])
        for k in range(gq):
            for b in range(nb):
                s1[k, pl.ds(b, nc, stride=nb), :] = loc1[b * nc:(b + 1) * nc, k * LANE:(k + 1) * LANE]
                s2[k, pl.ds(b, nc, stride=nb), :] = loc2[b * nc:(b + 1) * nc, k * LANE:(k + 1) * LANE]

    def lane_blocks(a_ref):
        return [jnp.broadcast_to(a_ref[:, k * LANE:(k + 1) * LANE], (SUBLANE, LANE)) for k in range(gq)]

    a1, a2, a3 = lane_blocks(a1_ref), lane_blocks(a2_ref), lane_blocks(a3_ref)
    fwd = dirn == 0
    row_grp = _lane_group(lax.broadcasted_iota(I32, (SUBLANE, LANE), 0), nb)

    def advance(v):
        return jnp.where(fwd, pltpu.roll(v, nb, 0), pltpu.roll(v, SUBLANE - nb, 0))

    zero = tuple(jnp.zeros((SUBLANE, LANE), F32) for _ in range(gq))
    carry = (zero, zero)
    for _, _, s1, s2, h, _, nc in streams:
        n_tiles = nc // cpt

        def step(t, vs, s1=s1, s2=s2, h=h, n_tiles=n_tiles):
            v1, v2 = list(vs[0]), list(vs[1])
            j = jnp.where(fwd, t, n_tiles - 1 - t)
            rows = pl.ds(pl.multiple_of(j * SUBLANE, SUBLANE), SUBLANE)
            for k in range(gq):
                loc1, loc2 = s1[k, rows, :], s2[k, rows, :]
                entered = v1[k]
                for u in range(cpt):
                    grp = jnp.where(fwd, u, cpt - 1 - u)
                    entered = jnp.where(row_grp == grp, v1[k], entered)
                    n1 = a1[k] * v1[k] + a2[k] * v2[k] + loc1
                    n2 = a1[k] * v2[k] + a3[k] * v1[k] + loc2
                    v1[k], v2[k] = advance(n1), advance(n2)
                h[k, rows, :] = entered
            return tuple(v1), tuple(v2)

        carry = lax.fori_loop(0, n_tiles, step, carry, unroll=2)

    for _, x_s, _, _, h, y_ref, nc in streams:
        if y_ref is None:
            continue
        h_all = jnp.concatenate(
            [jnp.concatenate([h[k, pl.ds(b, nc, stride=nb), :] for b in range(nb)], axis=0) for k in range(gq)],
            axis=1).astype(BF16)
        y = _dot(x_s[...], kbig[...]) + _dot(h_all, cb_ref[...])

        @pl.when(dirn == 0)
        def _():
            for b in range(nb):
                for j in range(ell):
                    y_ref[b, pl.ds(j, nc, stride=ell), :] = y[b * nc:(b + 1) * nc, j * LANE:(j + 1) * LANE]

        @pl.when(dirn == 1)
        def _():
            for b in range(nb):
                for j in range(ell):
                    y_ref[b, pl.ds(j, nc, stride=ell), :] += y[b * nc:(b + 1) * nc, j * LANE:(j + 1) * LANE]


def _s5_tables(a_re, a_im, log_step, b_re, b_im, c_re, c_im):
    ell, hch, p = S5_CHUNK, S5_GROUP_CH, S5_STATE
    lam = lax.complex(a_re.astype(F32), a_im.astype(F32))
    lam_dt = lam * jnp.exp(log_step.astype(F32))[:, :, None]
    a_bar = jnp.exp(lam_dt)
    b_bar = ((a_bar - 1.0) / lam)[:, :, :, None] * lax.complex(b_re.astype(F32), b_im.astype(F32))
    c_mat = lax.complex(c_re.astype(F32), c_im.astype(F32))
    gq = LANE // hch
    nq = S5_GROUPS // gq
    hp = lax.Precision.HIGHEST
    steps = jnp.arange(ell + 1, dtype=F32)
    apow = jnp.exp(lam_dt[:, :, None, :] * steps[None, None, :, None])
    apow_re, apow_im = apow.real, apow.imag
    bt_re, bt_im = jnp.swapaxes(b_bar.real, 2, 3), jnp.swapaxes(b_bar.imag, 2, 3)
    c_re_, c_im_ = c_mat.real, c_mat.imag
    eye = jnp.eye(gq, dtype=F32)

    def cmul(ar, ai, br, bi):
        return ar * br - ai * bi, ar * bi + ai * br

    m_re, m_im = cmul(apow_re[:, :, :ell, None, :], apow_im[:, :, :ell, None, :], bt_re[:, :, None], bt_im[:, :, None])
    kk = (jnp.einsum('dgkip,dgop->dgkio', m_re, c_re_, precision=hp)
          - jnp.einsum('dgkip,dgop->dgkio', m_im, c_im_, precision=hp))
    zeros = jnp.zeros_like(kk[0, :, 1:])
    signed = jnp.stack([jnp.concatenate([zeros, kk[0]], axis=1),
                        jnp.concatenate([jnp.flip(kk[1], axis=1), zeros], axis=1)])
    signed = signed.reshape(2, nq, gq, 2 * ell - 1, hch, hch)
    dblk = jnp.einsum('dqgkio,gh->dqkgiho', signed, eye).reshape(2, nq, 2 * ell - 1, LANE, LANE).astype(BF16)
    i_idx = jnp.arange(ell)
    e_re = jnp.stack([apow_re[0][:, ell - 1 - i_idx], apow_re[1][:, i_idx]])
    e_im = jnp.stack([apow_im[0][:, ell - 1 - i_idx], apow_im[1][:, i_idx]])
    bp_re, bp_im = cmul(e_re[:, :, :, None, :], e_im[:, :, :, None, :], bt_re[:, :, None], bt_im[:, :, None])
    bp_re, bp_im = bp_re.reshape(2, S5_GROUPS, ell * hch, p), bp_im.reshape(2, S5_GROUPS, ell * hch, p)
    b1 = jnp.concatenate([bp_re, bp_im], axis=-1)
    b2 = jnp.concatenate([bp_im, bp_re], axis=-1)
    al_re, al_im = apow_re[:, :, ell, :], apow_im[:, :, ell, :]
    lanes = lambda u, v: jnp.concatenate([u, v], axis=-1).reshape(2, nq, 1, gq * 2 * p)
    a1, a2, a3 = lanes(al_re, al_re), lanes(-al_im, al_im), lanes(al_im, -al_im)
    f_re = jnp.stack([apow_re[0][:, 1 + i_idx], apow_re[1][:, ell - i_idx]])
    f_im = jnp.stack([apow_im[0][:, 1 + i_idx], apow_im[1][:, ell - i_idx]])
    g_re, g_im = cmul(c_re_[:, :, None], c_im_[:, :, None], f_re[:, :, :, None, :], f_im[:, :, :, None, :])
    cp = jnp.concatenate([jnp.transpose(g_re, (0, 1, 4, 2, 3)), -jnp.transpose(g_im, (0, 1, 4, 2, 3))], axis=2)
    cp = cp.reshape(2, S5_GROUPS, 2 * p, ell * hch).astype(BF16)
    return dblk, b1, b2, a1, a2, a3, cp


def _s5_out_scatter():
    ell, hch = S5_CHUNK, S5_GROUP_CH
    gq = LANE // hch
    src = jnp.arange(ell * hch)
    dst = (src // hch)[None, :] * LANE + jnp.arange(gq)[:, None] * hch + (src % hch)[None, :]
    return (dst[:, :, None] == jnp.arange(ell * LANE)[None, None, :]).astype(BF16)


def _s5_scan(zx, zc, tables, layer, ctx_out):
    b, n, _ = zx.shape
    c = zc.shape[1]
    ell, hch = S5_CHUNK, S5_GROUP_CH
    gq = LANE // hch
    nq = S5_GROUPS // gq
    p2 = 2 * S5_STATE
    ncx, ncc = n // ell, c // ell
    assert n % ell == 0 and c % ell == 0 and ncc % 16 == 0
    assert SUBLANE % b == 0 and b & (b - 1) == 0
    dblk, b1, b2, a1, a2, a3, cp = tables
    u0 = 0
    sw = gq * p2
    xw = ell * LANE
    a_spec = pl.BlockSpec((None, None, None, 1, sw), lambda q, d: (layer, d, q, 0, 0))
    b_spec = pl.BlockSpec((None, None, gq, ell * hch, p2), lambda q, d: (layer, d, q, 0, 0))
    y_specs = [pl.BlockSpec((b, n, LANE), lambda q, d: (0, 0, q))]
    y_shapes = [jax.ShapeDtypeStruct((b, n, S5_WIDTH), F32)]
    if ctx_out:
        y_specs = [pl.BlockSpec((b, c, LANE), lambda q, d: (0, 0, q))] + y_specs
        y_shapes = [jax.ShapeDtypeStruct((b, c, S5_WIDTH), F32)] + y_shapes
    out = pl.pallas_call(
        functools.partial(_s5_kernel, ctx_out=ctx_out),
        grid=(nq, 2),
        in_specs=[
            pl.BlockSpec((b, c, LANE), lambda q, d: (0, 0, u0 + q)),
            pl.BlockSpec((b, n, LANE), lambda q, d: (0, 0, u0 + q)),
            pl.BlockSpec((None, None, None, 2 * ell - 1, LANE, LANE), lambda q, d: (layer, d, q, 0, 0, 0)),
            b_spec, b_spec, a_spec, a_spec, a_spec,
            pl.BlockSpec((None, None, gq, p2, ell * hch), lambda q, d: (layer, d, q, 0, 0)),
            pl.BlockSpec((gq, ell * hch, xw), lambda q, d: (0, 0, 0)),
        ],
        out_specs=y_specs,
        out_shape=y_shapes,
        scratch_shapes=[
            pltpu.VMEM((xw, xw), BF16), pltpu.VMEM((xw, sw), BF16), pltpu.VMEM((xw, sw), BF16),
            pltpu.VMEM((sw, xw), BF16),
            pltpu.VMEM((b * ncc, xw), BF16), pltpu.VMEM((b * ncx, xw), BF16),
            pltpu.VMEM((gq, b * ncc, p2), F32), pltpu.VMEM((gq, b * ncc, p2), F32),
            pltpu.VMEM((gq, b * ncx, p2), F32), pltpu.VMEM((gq, b * ncx, p2), F32),
            pltpu.VMEM((gq, b * ncc, p2), F32), pltpu.VMEM((gq, b * ncx, p2), F32),
        ],
        compiler_params=_cp("parallel", "arbitrary"),
        name="s5_scan",
    )(zc, zx, dblk, b1, b2, a1, a2, a3, cp, _s5_out_scatter())
    return (out[0], out[1]) if ctx_out else (None, out[0])


def _s5_out_kernel(y_ref, u_ref, d_ref, w_ref, o_ref):
    g = jax.nn.gelu(y_ref[...] + d_ref[...] * u_ref[...])
    o_ref[...] = (g * jax.nn.sigmoid(_dot(g.astype(BF16), w_ref[...].astype(BF16)))).astype(o_ref.dtype)


def _s5_out(y, u, d_skip, w_glu):
    b, t, w = y.shape
    tr = _tile(t, 512, 8)
    row = pl.BlockSpec((None, tr, w), lambda bi, i: (bi, i, 0))
    return pl.pallas_call(
        _s5_out_kernel,
        grid=(b, t // tr),
        in_specs=[row, row, pl.BlockSpec((1, w), lambda bi, i: (0, 0)), pl.BlockSpec((w, w), lambda bi, i: (0, 0))],
        out_specs=row,
        out_shape=jax.ShapeDtypeStruct((b, t, w), BF16),
        compiler_params=_cp("parallel", "parallel"),
        name="s5_out",
    )(y, u, d_skip.reshape(1, w).astype(F32), w_glu)


def _merge_kernel(h_ref, ya_ref, yb_ref, yc_ref, yd_ref, wg0_ref, wg1_ref, wg2_ref, wg3_ref, wb_ref, o_ref, wgs_ref, wbs_ref):
    @pl.when(pl.program_id(1) == 0)
    def _():
        for n, wg_ref in enumerate((wg0_ref, wg1_ref, wg2_ref, wg3_ref)):
            wgs_ref[n] = wg_ref[...].astype(BF16)
        wbs_ref[...] = wb_ref[...].astype(BF16)

    h = h_ref[...]
    acc = None
    for n, y_ref in enumerate((ya_ref, yb_ref, yc_ref, yd_ref)):
        term = jax.nn.sigmoid(_dot(h, wgs_ref[n])) * _dot(y_ref[...], wbs_ref[n])
        acc = term if acc is None else acc + term
    o_ref[...] = acc.astype(o_ref.dtype)


def _merge(h, branches, w_in, w_branch, layer):
    m, d = h.shape
    bw = branches[0].shape[1]
    tn = 256
    tm = _tile(m, 1024, 8)
    g0 = N_MIX_IN // tn
    per = d // tn
    assert N_MIX_IN % tn == 0 and d % tn == 0
    wg_specs = [pl.BlockSpec((None, d, tn), lambda j, i, n=n: (layer, 0, g0 + n * per + j)) for n in range(N_BRANCH)]
    y_spec = pl.BlockSpec((tm, bw), lambda j, i: (i, 0))
    return pl.pallas_call(
        _merge_kernel,
        grid=(d // tn, m // tm),
        in_specs=[pl.BlockSpec((tm, d), lambda j, i: (i, 0)), y_spec, y_spec, y_spec, y_spec] + wg_specs
                 + [pl.BlockSpec((None, N_BRANCH, bw, tn), lambda j, i: (layer, 0, 0, j))],
        out_specs=pl.BlockSpec((tm, tn), lambda j, i: (i, j)),
        out_shape=jax.ShapeDtypeStruct((m, d), BF16),
        scratch_shapes=[pltpu.VMEM((N_BRANCH, d, tn), BF16), pltpu.VMEM((N_BRANCH, bw, tn), BF16)],
        compiler_params=_cp("arbitrary", "arbitrary"),
        name="merge",
    )(h, *branches, w_in, w_in, w_in, w_in, w_branch)


def _topk_kernel(lg_ref, slot_ref, aff_ref, tri_ref, *, cap):
    e, t = lg_ref.shape

    @pl.when(pl.program_id(0) == 0)
    def _():
        rows = 256 if t % 256 == 0 else t
        for r0 in range(0, t, rows):
            ri = lax.broadcasted_iota(I32, (rows, t), 0) + r0
            ci = lax.broadcasted_iota(I32, (rows, t), 1)
            tri_ref[r0:r0 + rows, :] = jnp.where(ri < ci, 1.0, 0.0).astype(BF16)

    lg = lg_ref[...]
    ex = jnp.exp(lg - jnp.max(lg, axis=0, keepdims=True))
    aff = ex / jnp.sum(ex, axis=0, keepdims=True)
    aff_ref[...] = aff
    bits = pltpu.bitcast(aff, I32)

    def search(_, carry):
        lo, hi = carry
        mid = lo + lax.shift_right_logical(hi - lo + 1, 1)
        ok = _count(bits >= mid, 1) >= cap
        return jnp.where(ok, mid, lo), jnp.where(ok, hi, mid - 1)

    lo0 = jnp.zeros((e, 1), I32)
    hi0 = jnp.full((e, 1), 0x7F800000, I32)
    thr, _ = lax.fori_loop(0, 32, search, (lo0, hi0))
    gt = bits > thr
    eq = bits == thr
    need = cap - _count(gt, 1)
    tri = tri_ref[...]
    eq_before = _dot(jnp.where(eq, 1.0, 0.0).astype(BF16), tri)
    sel = gt | (eq & (eq_before < need))
    sel_before = _dot(jnp.where(sel, 1.0, 0.0).astype(BF16), tri)
    slot_ref[...] = jnp.where(sel, sel_before.astype(I32), -1)


def _topk(logits_t, b, cap):
    e, bt = logits_t.shape
    t = bt // b
    return pl.pallas_call(
        functools.partial(_topk_kernel, cap=cap),
        grid=(b,),
        in_specs=[pl.BlockSpec((e, t), lambda bi: (0, bi))],
        out_specs=[pl.BlockSpec((None, e, t), lambda bi: (bi, 0, 0)), pl.BlockSpec((None, e, t), lambda bi: (bi, 0, 0))],
        out_shape=[jax.ShapeDtypeStruct((b, e, t), I32), jax.ShapeDtypeStruct((b, e, t), F32)],
        scratch_shapes=[pltpu.VMEM((t, t), BF16)],
        compiler_params=_cp("arbitrary"),
        name="route_topk",
    )(logits_t)


def _gather_kernel(slot_ref, aff_ref, h_ref, xs_ref, gate_ref, *, cap):
    t = h_ref.shape[0]
    pick = lax.broadcasted_iota(I32, (cap, t), 0) == slot_ref[...]
    xs_ref[...] = _dot(jnp.where(pick, 1.0, 0.0).astype(BF16), h_ref[...]).astype(xs_ref.dtype)
    gate_ref[...] = jnp.sum(jnp.where(pick, aff_ref[...], 0.0), axis=1, keepdims=True)


def _gather(slot, aff, h, cap):
    b, e, t = slot.shape
    d = h.shape[2]
    row = pl.BlockSpec((None, None, 1, t), lambda bi, ei: (bi, ei, 0, 0))
    return pl.pallas_call(
        functools.partial(_gather_kernel, cap=cap),
        grid=(b, e),
        in_specs=[row, row, pl.BlockSpec((None, t, d), lambda bi, ei: (bi, 0, 0))],
        out_specs=[pl.BlockSpec((None, cap, d), lambda bi, ei: (ei, bi, 0)),
                   pl.BlockSpec((None, cap, 1), lambda bi, ei: (ei, bi, 0))],
        out_shape=[jax.ShapeDtypeStruct((e, b * cap, d), BF16), jax.ShapeDtypeStruct((e, b * cap, 1), F32)],
        compiler_params=_cp("parallel", "arbitrary"),
        name="moe_gather",
    )(slot.reshape(b, e, 1, t), aff.reshape(b, e, 1, t), h)


def _ffn_up_kernel(*refs, n_streams):
    xs = refs[:n_streams]
    w1_ref, w3_ref = refs[n_streams:n_streams + 2]
    outs = refs[n_streams + 2:]
    w1, w3 = w1_ref[...].astype(BF16), w3_ref[...].astype(BF16)
    for x_ref, o_ref in zip(xs, outs):
        x = x_ref[...]
        a = _dot(x, w1)
        g = _dot(x, w3)
        o_ref[...] = ((a / (1.0 + jnp.exp(-a))) * g).astype(o_ref.dtype)


def _ffn_down_kernel(*refs, n_streams):
    mids = refs[:n_streams]
    gates = refs[n_streams:2 * n_streams]
    w2_ref = refs[2 * n_streams]
    outs = refs[2 * n_streams + 1:]
    w2 = w2_ref[...].astype(BF16)
    for m_ref, gate_ref, o_ref in zip(mids, gates, outs):
        o_ref[...] = (_dot(m_ref[...], w2) * gate_ref[...]).astype(o_ref.dtype)


def _ffn(xs_list, gate_list, w1, w3, w2, layer):
    ns = len(xs_list)
    e, _, d = xs_list[0].shape
    ff = w1.shape[-1]
    tf = _tile(ff, 512)
    tn = _tile(d, 1024)
    rows = [x.shape[1] for x in xs_list]
    mids = pl.pallas_call(
        functools.partial(_ffn_up_kernel, n_streams=ns),
        grid=(e, ff // tf),
        in_specs=[pl.BlockSpec((None, r, d), lambda ei, f: (ei, 0, 0)) for r in rows] + [
            pl.BlockSpec((None, None, d, tf), lambda ei, f: (layer, ei, 0, f)),
            pl.BlockSpec((None, None, d, tf), lambda ei, f: (layer, ei, 0, f)),
        ],
        out_specs=[pl.BlockSpec((None, r, tf), lambda ei, f: (ei, 0, f)) for r in rows],
        out_shape=[jax.ShapeDtypeStruct((e, r, ff), BF16) for r in rows],
        compiler_params=_cp("parallel", "arbitrary"),
        name="moe_ffn_up",
    )(*xs_list, w1, w3)
    return pl.pallas_call(
        functools.partial(_ffn_down_kernel, n_streams=ns),
        grid=(e, d // tn),
        in_specs=[pl.BlockSpec((None, r, ff), lambda ei, j: (ei, 0, 0)) for r in rows]
                 + [pl.BlockSpec((None, r, 1), lambda ei, j: (ei, 0, 0)) for r in rows]
                 + [pl.BlockSpec((None, None, ff, tn), lambda ei, j: (layer, ei, 0, j))],
        out_specs=[pl.BlockSpec((None, r, tn), lambda ei, j: (ei, 0, j)) for r in rows],
        out_shape=[jax.ShapeDtypeStruct((e, r, d), BF16) for r in rows],
        compiler_params=_cp("parallel", "arbitrary"),
        name="moe_ffn_down",
    )(*mids, *gate_list, w2)


def _combine_kernel(slot_ref, y_ref, x_ref, gate_ref, o_ref, pt_ref, *, cap):
    e = slot_ref.shape[1]
    tm = slot_ref.shape[0]
    r0 = pl.multiple_of(pl.program_id(2) * tm, tm)

    @pl.when(pl.program_id(1) == 0)
    def _():
        slot = slot_ref[...]
        lane = lax.broadcasted_iota(I32, (tm, cap), 1)
        for ei in range(e):
            pt_ref[pl.ds(r0, tm), ei * cap:(ei + 1) * cap] = jnp.where(lane == slot[:, ei:ei + 1], 1.0, 0.0).astype(BF16)

    y = y_ref[...]
    acc = _dot(pt_ref[pl.ds(r0, tm), :], y.reshape(e * cap, y.shape[2]))
    o_ref[...] = x_ref[...] + gate_ref[...] * acc


def _combine(slot_te, ys, x, gate, cap):
    b, t, d = x.shape
    e = slot_te.shape[2]
    tm = _tile(t, 512, 8)
    tn = _tile(d, 512)
    return pl.pallas_call(
        functools.partial(_combine_kernel, cap=cap),
        grid=(b, d // tn, t // tm),
        in_specs=[
            pl.BlockSpec((None, tm, e), lambda bi, j, i: (bi, i, 0)),
            pl.BlockSpec((e, cap, tn), lambda bi, j, i: (0, bi, j)),
            pl.BlockSpec((None, tm, tn), lambda bi, j, i: (bi, i, j)),
            pl.BlockSpec((None, 1, tn), lambda bi, j, i: (bi, 0, j)),
        ],
        out_specs=pl.BlockSpec((None, tm, tn), lambda bi, j, i: (bi, i, j)),
        out_shape=jax.ShapeDtypeStruct((b, t, d), F32),
        scratch_shapes=[pltpu.VMEM((t, e * cap), BF16)],
        compiler_params=_cp("parallel", "arbitrary", "arbitrary"),
        name="moe_combine",
    )(slot_te, ys, x, gate)


def kernel(x, c, ctx, c_ctx, ada_w, ada_b, norm1_g, norm2_g, w_in, da_lambda, da_subln_g, wb_sink, na_rpb, s5_a_re, s5_a_im, s5_log_step, s5_b_re, s5_b_im, s5_c_re, s5_c_im, s5_d, s5_glu_w, w_branch, w_out, w_router, w_e1, w_e3, w_e2, final_g):
    b, n, d = x.shape
    n_ctx = ctx.shape[1]
    depth = ada_w.shape[0]
    assert b + 1 <= ADA_ROWS and n % GRID_W == 0

    cs = jnp.zeros((ADA_ROWS, d), F32).at[:b].set(c).at[b].set(c_ctx)
    mods = _ada_mod(cs, ada_w, ada_b)
    rope_tab = _rope_tables(n)
    bias_tab = jax.vmap(_na_bias_table)(na_rpb)
    s5_tab = jax.vmap(_s5_tables)(s5_a_re, s5_a_im, s5_log_step, s5_b_re, s5_b_im, s5_c_re, s5_c_im)

    for l in range(depth):
        with_ctx = l < depth - 1
        mod_x = [mods[l, :b, k * d:(k + 1) * d].reshape(b, 1, d) for k in range(6)]
        mod_c = [jnp.broadcast_to(mods[l, b, k * d:(k + 1) * d], (b, 1, d)) for k in range(6)]

        hx = _norm_mod(x, norm1_g[l], mod_x[0], mod_x[1])
        hc = _norm_mod(ctx, norm1_g[l], mod_c[0], mod_c[1])
        proj = []
        for h, t, positions in ((hx, n, True), (hc, n_ctx, False)):
            h2d = h.reshape(b * t, d)
            proj.append([
                _mm(h2d, w_in, l, col0, ncols, tn, dt,
                    rope=(rope_tab, n_rope, t) if positions and n_rope else None).reshape(b, t, ncols)
                for col0, ncols, tn, dt, n_rope in MIX_PROJ])
        (za_x, zb_x, zn_x, zs_x), (za_c, zb_c, zn_c, zs_c) = proj

        lam_init = 0.8 - 0.6 * math.exp(-0.3 * l)
        lv = da_lambda[l].astype(F32)
        lam = jnp.exp(jnp.sum(lv[0] * lv[1])) - jnp.exp(jnp.sum(lv[2] * lv[3])) + lam_init

        ya_x = _diff_attn(za_x, za_c, za_x, za_x, lam, da_subln_g[l], lam_init, True)
        yb_x = _win_attn(zb_x, zb_c, zb_x, zb_x, wb_sink[l], True)
        yc_x = _na_attn(zn_x, zn_c, zn_x, bias_tab, l, True)
        ys_c, ys_x = _s5_scan(zs_x, zs_c, s5_tab, l, with_ctx)
        yd_x = _s5_out(ys_x, zs_x, s5_d[l], s5_glu_w[l])
        gx = _merge(hx.reshape(b * n, d), [t.reshape(b * n, -1) for t in (ya_x, yb_x, yc_x, yd_x)], w_in, w_branch, l)
        x = _mm_res(gx, w_out, l, x, mod_x[2])

        if with_ctx:
            ya_c = _diff_attn(za_c, za_c, None, None, lam, da_subln_g[l], lam_init, False)
            yb_c = _win_attn(zb_c, zb_c, None, None, wb_sink[l], False)
            yc_c = _na_attn(zn_c, zn_c, None, None, l, False)
            yd_c = _s5_out(ys_c, zs_c, s5_d[l], s5_glu_w[l])
            gc = _merge(hc.reshape(b * n_ctx, d), [t.reshape(b * n_ctx, -1) for t in (ya_c, yb_c, yc_c, yd_c)],
                        w_in, w_branch, l)
            ctx = _mm_res(gc, w_out, l, ctx, mod_c[2])

        w_router_t = jnp.transpose(w_router[l]).astype(F32)
        streams = [(x, mod_x)] + ([(ctx, mod_c)] if with_ctx else [])
        routed = []
        for s, mod in streams:
            t = s.shape[1]
            cap = EC_CAPACITY_FACTOR * t // N_EXPERTS
            h2, logits_t = _norm_router(s, norm2_g[l], mod[3], mod[4], w_router_t)
            slot, aff = _topk(logits_t, b, cap)
            xs, gate = _gather(slot, aff, h2, cap)
            routed.append((jnp.transpose(slot, (0, 2, 1)), xs, gate, cap))
        ys = _ffn([r[1] for r in routed], [r[2] for r in routed], w_e1, w_e3, w_e2, l)
        x = _combine(routed[0][0], ys[0], x, mod_x[5], routed[0][3])
        if with_ctx:
            ctx = _combine(routed[1][0], ys[1], ctx, mod_c[5], routed[1][3])

    return _final_norm(x, final_g)
```

```python
import functools
import math

import jax
import jax.numpy as jnp
from jax import lax
from jax.experimental import pallas as pl
from jax.experimental.pallas import tpu as pltpu

F32 = jnp.float32
BF16 = jnp.bfloat16
I32 = jnp.int32

GRID_W = 64
EPS = 1e-6
NEG_INF = -1e30
LOG2E = math.log2(math.e)
ROPE_BASE = 10000.0

DA_HEADS = 4
DA_QK_DIM = 64
DA_V_DIM = 2 * DA_QK_DIM
WB_HEADS = 8
WB_KV_HEADS = 2
WB_DIM = 64
WB_WINDOW = 128
WB_BLOCK = 128
NA_HEADS = 8
NA_DIM = 64
NA_ROWS = 8
NA_COLS = 16
S5_GROUPS = 32
S5_GROUP_CH = 16
S5_STATE = 64
S5_WIDTH = S5_GROUPS * S5_GROUP_CH
S5_CHUNK = 8
N_BRANCH = 4
BRANCH_WIDTH = 512
N_EXPERTS = 16
EC_CAPACITY_FACTOR = 2

IN_WIDTHS = (
    2 * DA_HEADS * DA_QK_DIM, 2 * DA_HEADS * DA_QK_DIM, DA_HEADS * DA_V_DIM,
    WB_HEADS * WB_DIM, WB_KV_HEADS * WB_DIM, WB_KV_HEADS * WB_DIM,
    NA_HEADS * NA_DIM, NA_HEADS * NA_DIM, NA_HEADS * NA_DIM,
    S5_WIDTH,
)
N_MIX_IN = sum(IN_WIDTHS)
_OFFS = [0]
for _w in IN_WIDTHS:
    _OFFS.append(_OFFS[-1] + _w)
(OFF_DA_Q, OFF_DA_K, OFF_DA_V, OFF_WB_Q, OFF_WB_K, OFF_WB_V, OFF_NA_Q, OFF_NA_K, OFF_NA_V, OFF_S5, _) = _OFFS

MIX_PROJ = (
    (OFF_DA_Q, OFF_WB_Q - OFF_DA_Q, 768, BF16, IN_WIDTHS[0] + IN_WIDTHS[1]),
    (OFF_WB_Q, OFF_NA_Q - OFF_WB_Q, 768, BF16, IN_WIDTHS[3] + IN_WIDTHS[4]),
    (OFF_NA_Q, OFF_S5 - OFF_NA_Q, 768, BF16, 0),
    (OFF_S5, S5_WIDTH, 256, F32, 0),
)

LANE = 128
SUBLANE = 8
ADA_ROWS = 8
VMEM_LIMIT = 56 * 1024 * 1024

NN = (((1,), (0,)), ((), ()))
NT = (((1,), (1,)), ((), ()))


def _cp(*sem):
    return pltpu.CompilerParams(dimension_semantics=sem, vmem_limit_bytes=VMEM_LIMIT)


def _tile(n, pref, mult=LANE):
    if n <= pref:
        return n
    t = (pref // mult) * mult
    while t >= mult:
        if n % t == 0:
            return t
        t -= mult
    return n


def _split(a):
    hi = a.astype(BF16)
    lo = (a - hi.astype(F32)).astype(BF16)
    return hi, lo


def _dot(a, b, dims=NN):
    return lax.dot_general(a, b, dims, preferred_element_type=F32)


def _lane_group(lane, width):
    return lax.shift_right_logical(lane, int(math.log2(width)))


def _count(mask, axis):
    return jnp.sum(jnp.where(mask, 1.0, 0.0), axis=axis, keepdims=True)


def _dot3(a, b, dims=NN):
    ah, al = _split(a)
    bh, bl = _split(b)
    return _dot(ah, bh, dims) + (_dot(ah, bl, dims) + _dot(al, bh, dims))


def _ada_kernel(c_ref, w_ref, b_ref, o_ref):
    c = c_ref[...]
    s = c / (1.0 + jnp.exp(-c))
    o_ref[...] = _dot3(s, w_ref[...]) + b_ref[...]


def _ada_mod(cs, ada_w, ada_b):
    depth, d, n6 = ada_w.shape
    tn = _tile(n6, 512)
    return pl.pallas_call(
        _ada_kernel,
        grid=(depth, n6 // tn),
        in_specs=[
            pl.BlockSpec((ADA_ROWS, d), lambda l, j: (0, 0)),
            pl.BlockSpec((None, d, tn), lambda l, j: (l, 0, j)),
            pl.BlockSpec((None, 1, tn), lambda l, j: (l, 0, j)),
        ],
        out_specs=pl.BlockSpec((None, ADA_ROWS, tn), lambda l, j: (l, 0, j)),
        out_shape=jax.ShapeDtypeStruct((depth, ADA_ROWS, n6), F32),
        compiler_params=_cp("arbitrary", "arbitrary"),
        name="ada_mod",
    )(cs, ada_w, ada_b.reshape(depth, 1, n6))


def _norm_mod_rows(x, g, sh, sc):
    y = x * lax.rsqrt(jnp.mean(x * x, axis=-1, keepdims=True) + EPS) * g
    return y * (1.0 + sc) + sh


def _norm_mod_kernel(x_ref, g_ref, sh_ref, sc_ref, o_ref):
    o_ref[...] = _norm_mod_rows(x_ref[...], g_ref[...], sh_ref[...], sc_ref[...]).astype(o_ref.dtype)


def _norm_mod(x, g, sh, sc):
    b, t, d = x.shape
    tr = _tile(t, 512, 8)
    return pl.pallas_call(
        _norm_mod_kernel,
        grid=(b, t // tr),
        in_specs=[
            pl.BlockSpec((None, tr, d), lambda bi, i: (bi, i, 0)),
            pl.BlockSpec((1, d), lambda bi, i: (0, 0)),
            pl.BlockSpec((None, 1, d), lambda bi, i: (bi, 0, 0)),
            pl.BlockSpec((None, 1, d), lambda bi, i: (bi, 0, 0)),
        ],
        out_specs=pl.BlockSpec((None, tr, d), lambda bi, i: (bi, i, 0)),
        out_shape=jax.ShapeDtypeStruct((b, t, d), BF16),
        compiler_params=_cp("parallel", "parallel"),
        name="norm_mod",
    )(x, g.reshape(1, d), sh, sc)


def _pack_halves(h):
    half = h.shape[1] // 2
    bits = pltpu.bitcast(h.astype(BF16).astype(F32), jnp.uint32)
    return bits[:, half:] | lax.shift_right_logical(bits[:, :half], jnp.uint32(16))


def _unpack_halves(p):
    lo = pltpu.bitcast(lax.shift_left(p, jnp.uint32(16)), F32)
    hi = pltpu.bitcast(p & jnp.uint32(0xFFFF0000), F32)
    return jnp.concatenate([lo, hi], axis=1).astype(BF16)


def _norm_router_kernel(x_ref, g_ref, sh_ref, sc_ref, wrt_ref, h_ref, lg_ref):
    h = _norm_mod_rows(x_ref[...], g_ref[...], sh_ref[...], sc_ref[...])
    h_ref[...] = _pack_halves(h)
    lg_ref[...] = _dot3(wrt_ref[...], h, NT)


def _norm_router(x, g, sh, sc, w_router_t):
    b, t, d = x.shape
    e = w_router_t.shape[0]
    tr = _tile(t, 512)
    nt = t // tr
    return pl.pallas_call(
        _norm_router_kernel,
        grid=(b, nt),
        in_specs=[
            pl.BlockSpec((None, tr, d), lambda bi, i: (bi, i, 0)),
            pl.BlockSpec((1, d), lambda bi, i: (0, 0)),
            pl.BlockSpec((None, 1, d), lambda bi, i: (bi, 0, 0)),
            pl.BlockSpec((None, 1, d), lambda bi, i: (bi, 0, 0)),
            pl.BlockSpec((e, d), lambda bi, i: (0, 0)),
        ],
        out_specs=[
            pl.BlockSpec((None, tr, d // 2), lambda bi, i: (bi, i, 0)),
            pl.BlockSpec((e, tr), lambda bi, i: (0, bi * nt + i)),
        ],
        out_shape=[jax.ShapeDtypeStruct((b, t, d // 2), jnp.uint32), jax.ShapeDtypeStruct((e, b * t), F32)],
        compiler_params=_cp("parallel", "parallel"),
        name="norm_router",
    )(x, g.reshape(1, d), sh, sc, w_router_t)


def _final_norm_kernel(x_ref, g_ref, o_ref):
    x = x_ref[...]
    o_ref[...] = x * lax.rsqrt(jnp.mean(x * x, axis=-1, keepdims=True) + EPS) * g_ref[...]


def _final_norm(x, g):
    b, t, d = x.shape
    tr = _tile(t, 512, 8)
    return pl.pallas_call(
        _final_norm_kernel,
        grid=(b, t // tr),
        in_specs=[pl.BlockSpec((None, tr, d), lambda bi, i: (bi, i, 0)), pl.BlockSpec((1, d), lambda bi, i: (0, 0))],
        out_specs=pl.BlockSpec((None, tr, d), lambda bi, i: (bi, i, 0)),
        out_shape=jax.ShapeDtypeStruct((b, t, d), F32),
        compiler_params=_cp("parallel", "parallel"),
        name="final_norm",
    )(x, g.reshape(1, d))


def _mm_kernel(a_ref, w_ref, o_ref, wb_ref):
    @pl.when(pl.program_id(1) == 0)
    def _():
        wb_ref[...] = w_ref[...].astype(BF16)

    o_ref[...] = _dot(a_ref[...], wb_ref[...]).astype(o_ref.dtype)


def _mm_rope_kernel(a_ref, w_ref, c_ref, sa_ref, sb_ref, o_ref, wb_ref, *, n_rope):
    @pl.when(pl.program_id(1) == 0)
    def _():
        wb_ref[...] = w_ref[...].astype(BF16)

    z = _dot(a_ref[...], wb_ref[...])
    tn = z.shape[1]
    col0 = pl.program_id(0) * tn
    cos, sa, sb = c_ref[...], sa_ref[...], sb_ref[...]
    for cblk in range(tn // LANE):
        x = z[:, cblk * LANE:(cblk + 1) * LANE]
        y = x * cos + pltpu.roll(x, LANE - 16, 1) * sa + pltpu.roll(x, 16, 1) * sb
        o_ref[:, cblk * LANE:(cblk + 1) * LANE] = jnp.where(col0 + cblk * LANE < n_rope, y, x).astype(o_ref.dtype)


def _mm(a, w, layer, col0, ncols, tn, out_dtype, rope=None):
    m, k = a.shape
    assert ncols % tn == 0 and col0 % tn == 0
    tm = _tile(m, 1024, 8)
    j0 = col0 // tn
    in_specs = [
        pl.BlockSpec((tm, k), lambda j, i: (i, 0)),
        pl.BlockSpec((None, k, tn), lambda j, i: (layer, 0, j0 + j)),
    ]
    args = [a, w]
    body = _mm_kernel
    if rope is not None:
        tables, n_rope, seq = rope
        assert seq % tm == 0
        per_seq = seq // tm
        in_specs += [pl.BlockSpec((tm, LANE), lambda j, i: (i % per_seq, 0))] * 3
        args += list(tables)
        body = functools.partial(_mm_rope_kernel, n_rope=n_rope)
    return pl.pallas_call(
        body,
        grid=(ncols // tn, m // tm),
        in_specs=in_specs,
        out_specs=pl.BlockSpec((tm, tn), lambda j, i: (i, j)),
        out_shape=jax.ShapeDtypeStruct((m, ncols), out_dtype),
        scratch_shapes=[pltpu.VMEM((k, tn), BF16)],
        compiler_params=_cp("arbitrary", "arbitrary"),
        name="mm_in",
    )(*args)


def _mm_res_kernel(a_ref, w_ref, x_ref, gate_ref, o_ref, wb_ref):
    @pl.when(pl.program_id(1) == 0)
    def _():
        wb_ref[...] = w_ref[...].astype(BF16)

    o_ref[...] = x_ref[...] + gate_ref[...] * _dot(a_ref[...], wb_ref[...])


def _mm_res(a, w, layer, x, gate):
    b, t, n = x.shape
    m, k = a.shape
    tn = _tile(n, 1024)
    tm = _tile(t, 1024, 8)
    per_b = t // tm
    out = pl.pallas_call(
        _mm_res_kernel,
        grid=(n // tn, m // tm),
        in_specs=[
            pl.BlockSpec((tm, k), lambda j, i: (i, 0)),
            pl.BlockSpec((None, k, tn), lambda j, i: (layer, 0, j)),
            pl.BlockSpec((tm, tn), lambda j, i: (i, j)),
            pl.BlockSpec((None, 1, tn), lambda j, i: (i // per_b, 0, j)),
        ],
        out_specs=pl.BlockSpec((tm, tn), lambda j, i: (i, j)),
        out_shape=jax.ShapeDtypeStruct((m, n), F32),
        scratch_shapes=[pltpu.VMEM((k, tn), BF16)],
        compiler_params=_cp("arbitrary", "arbitrary"),
        name="mm_out_res",
    )(a, w, x.reshape(m, n), gate)
    return out.reshape(b, t, n)


def _rope_tables(n):
    pos = jnp.arange(n)
    rows, cols = (pos // GRID_W).astype(F32), (pos % GRID_W).astype(F32)
    half = DA_QK_DIM // 2
    inv_freq = jnp.power(ROPE_BASE, -jnp.arange(0, half, 2, dtype=F32) / half)
    ang_r, ang_c = rows[:, None] * inv_freq[None, :], cols[:, None] * inv_freq[None, :]
    cos64 = jnp.concatenate([jnp.cos(ang_r)] * 2 + [jnp.cos(ang_c)] * 2, axis=-1)
    sin64 = jnp.concatenate([jnp.sin(ang_r)] * 2 + [jnp.sin(ang_c)] * 2, axis=-1)
    cos, sin = jnp.tile(cos64, (1, 2)), jnp.tile(sin64, (1, 2))
    first = (jnp.arange(LANE) % 32) < 16
    return cos, jnp.where(first, -sin, 0.0), jnp.where(first, 0.0, sin)


def _diff_attn_kernel(*refs, has_x, post_scale):
    if has_x:
        lam_ref, g_ref, q_ref, kc_ref, vc_ref, kx_ref, vx_ref, o_ref = refs
    else:
        lam_ref, g_ref, q_ref, kc_ref, vc_ref, o_ref = refs
    q = q_ref[...].astype(F32) * (DA_QK_DIM ** -0.5 * LOG2E)
    lane = lax.broadcasted_iota(I32, q.shape, 1)
    if has_x:
        k = jnp.concatenate([kc_ref[...], kx_ref[...]], axis=0)
        v = jnp.concatenate([vc_ref[...], vx_ref[...]], axis=0)
    else:
        k, v = kc_ref[...], vc_ref[...]

    nk = k.shape[0]
    kc_len = _tile(nk, 768)

    def attend(qm):
        m = l = acc = None
        for k0 in range(0, nk, kc_len):
            s = _dot(qm, k[k0:k0 + kc_len], NT)
            m_c = jnp.max(s, axis=-1, keepdims=True)
            if m is None:
                m = m_c
                p = jnp.exp2(s - m)
                l = jnp.sum(p, axis=-1, keepdims=True)
                acc = _dot(p.astype(BF16), v[k0:k0 + kc_len])
            else:
                m_new = jnp.maximum(m, m_c)
                alpha = jnp.exp2(m - m_new)
                p = jnp.exp2(s - m_new)
                l = alpha * l + jnp.sum(p, axis=-1, keepdims=True)
                acc = alpha * acc + _dot(p.astype(BF16), v[k0:k0 + kc_len])
                m = m_new
        return acc / l

    o = attend(jnp.where(lane < DA_QK_DIM, q, 0.0).astype(BF16)) \
        - lam_ref[...] * attend(jnp.where(lane >= DA_QK_DIM, q, 0.0).astype(BF16))
    y = o * lax.rsqrt(jnp.mean(o * o, axis=-1, keepdims=True) + EPS) * g_ref[...] * post_scale
    o_ref[...] = y.astype(o_ref.dtype)


def _diff_attn(q_arr, zc, kx_arr, vx_arr, lam, subln_g, lam_init, has_x):
    b, tq_all, _ = q_arr.shape
    c = zc.shape[1]
    tq = _tile(tq_all, 512, 8)
    q_blk0, kc_blk0, vc_blk0 = 0, IN_WIDTHS[0] // LANE, 2 * IN_WIDTHS[0] // LANE
    kx_blk0 = kc_blk0
    in_specs = [
        pl.BlockSpec((1, 1), lambda bi, h, i: (0, 0)),
        pl.BlockSpec((1, DA_V_DIM), lambda bi, h, i: (0, 0)),
        pl.BlockSpec((None, tq, LANE), lambda bi, h, i: (bi, i, q_blk0 + h)),
        pl.BlockSpec((None, c, LANE), lambda bi, h, i: (bi, 0, kc_blk0 + h)),
        pl.BlockSpec((None, c, LANE), lambda bi, h, i: (bi, 0, vc_blk0 + h)),
    ]
    args = [lam.reshape(1, 1), subln_g.reshape(1, DA_V_DIM), q_arr, zc, zc]
    if has_x:
        n = kx_arr.shape[1]
        in_specs += [
            pl.BlockSpec((None, n, LANE), lambda bi, h, i: (bi, 0, kx_blk0 + h)),
            pl.BlockSpec((None, n, LANE), lambda bi, h, i: (bi, 0, vc_blk0 + h)),
        ]
        args += [kx_arr, vx_arr]
    return pl.pallas_call(
        functools.partial(_diff_attn_kernel, has_x=has_x, post_scale=1.0 - lam_init),
        grid=(b, DA_HEADS, tq_all // tq),
        in_specs=in_specs,
        out_specs=pl.BlockSpec((None, tq, LANE), lambda bi, h, i: (bi, i, h)),
        out_shape=jax.ShapeDtypeStruct((b, tq_all, DA_HEADS * DA_V_DIM), BF16),
        compiler_params=_cp("parallel", "parallel", "arbitrary"),
        name="diff_attn_x" if has_x else "diff_attn_c",
    )(*args)


def _group_cols(x, g):
    x = x.astype(F32)
    lane = lax.broadcasted_iota(I32, x.shape, 1)
    own = jnp.where(_lane_group(lane, WB_DIM) == g, x, pltpu.roll(x, WB_DIM, 1))
    return jnp.concatenate([own, own], axis=1).astype(BF16)


def _win_attn_kernel(*refs, has_x, n_tok):
    if has_x:
        sink_ref, q_ref, kc_ref, vc_ref, kp_ref, k0_ref, kn_ref, vp_ref, v0_ref, vn_ref, o_ref = refs
    else:
        sink_ref, q_ref, kc_ref, vc_ref, o_ref = refs
    i = pl.program_id(1)
    rep = WB_HEADS // WB_KV_HEADS
    gw = rep * WB_DIM
    q_all = q_ref[...].astype(F32) * (WB_DIM ** -0.5 * LOG2E)
    blk = q_all.shape[0]
    lane = lax.broadcasted_iota(I32, (blk, gw), 1)
    if has_x:
        k_in = jnp.concatenate([kp_ref[...], k0_ref[...], kn_ref[...], kc_ref[...]], axis=0)
        v_in = jnp.concatenate([vp_ref[...], v0_ref[...], vn_ref[...], vc_ref[...]], axis=0)
        q_pos = i * blk + (lax.broadcasted_iota(I32, (rep * blk, 3 * blk), 0) & (blk - 1))
        k_pos = (i - 1) * blk + lax.broadcasted_iota(I32, (rep * blk, 3 * blk), 1)
        valid = (jnp.abs(q_pos - k_pos) <= WB_WINDOW) & (k_pos >= 0) & (k_pos < n_tok)
    else:
        k_in, v_in = kc_ref[...], vc_ref[...]
    outs = []
    for g in range(WB_KV_HEADS):
        q = q_all[:, g * gw:(g + 1) * gw]
        k, v = _group_cols(k_in, g), _group_cols(v_in, g)
        mine = [_lane_group(lane, WB_DIM) == r for r in range(rep)]
        qs = jnp.concatenate([jnp.where(mine[r], q, 0.0) for r in range(rep)], axis=0).astype(BF16)
        sink = jnp.concatenate([jnp.broadcast_to(sink_ref[:, (g * rep + r) * WB_DIM:(g * rep + r) * WB_DIM + 1], (blk, 1))
                                for r in range(rep)], axis=0)
        s = _dot(qs, k, NT)
        if has_x:
            s = jnp.concatenate([jnp.where(valid, s[:, :3 * blk], NEG_INF), s[:, 3 * blk:]], axis=1)
        m = jnp.maximum(jnp.max(s, axis=-1, keepdims=True), sink)
        pr = jnp.exp2(s - m)
        l = jnp.sum(pr, axis=-1, keepdims=True) + jnp.exp2(sink - m)
        o = _dot(pr.astype(BF16), v) / l
        acc = jnp.zeros(q.shape, F32)
        for r in range(rep):
            acc = acc + jnp.where(mine[r], o[r * blk:(r + 1) * blk], 0.0)
        outs.append(acc)
    o_ref[...] = jnp.concatenate(outs, axis=1).astype(o_ref.dtype)


def _win_attn(q_arr, zc, kx_arr, vx_arr, sink, has_x):
    b, tq_all, _ = q_arr.shape
    c = zc.shape[1]
    blk = WB_BLOCK
    nb = tq_all // blk
    qw = WB_HEADS * WB_DIM
    sink_arr = jnp.repeat(sink.astype(F32) * LOG2E, WB_DIM).reshape(1, qw)
    k_blk, v_blk = qw // LANE, qw // LANE + 1
    in_specs = [
        pl.BlockSpec((1, qw), lambda bi, i: (0, 0)),
        pl.BlockSpec((None, blk, qw), lambda bi, i: (bi, i, 0)),
        pl.BlockSpec((None, c, LANE), lambda bi, i: (bi, 0, k_blk)),
        pl.BlockSpec((None, c, LANE), lambda bi, i: (bi, 0, v_blk)),
    ]
    args = [sink_arr, q_arr, zc, zc]
    n_tok = 0
    if has_x:
        n_tok = kx_arr.shape[1]
        prev = lambda i: jnp.maximum(i - 1, 0)
        nxt = lambda i: jnp.minimum(i + 1, nb - 1)
        for arr, cb in ((kx_arr, k_blk), (vx_arr, v_blk)):
            in_specs += [
                pl.BlockSpec((None, blk, LANE), lambda bi, i, cb=cb: (bi, prev(i), cb)),
                pl.BlockSpec((None, blk, LANE), lambda bi, i, cb=cb: (bi, i, cb)),
                pl.BlockSpec((None, blk, LANE), lambda bi, i, cb=cb: (bi, nxt(i), cb)),
            ]
            args += [arr, arr, arr]
    return pl.pallas_call(
        functools.partial(_win_attn_kernel, has_x=has_x, n_tok=n_tok),
        grid=(b, nb),
        in_specs=in_specs,
        out_specs=pl.BlockSpec((None, blk, qw), lambda bi, i: (bi, i, 0)),
        out_shape=jax.ShapeDtypeStruct((b, tq_all, qw), BF16),
        compiler_params=_cp("parallel", "arbitrary"),
        name="win_attn_x" if has_x else "win_attn_c",
    )(*args)


NA_QROWS = 4
NA_KROWS = 12


def _na_kernel(*refs, has_x, n_blocks):
    if has_x:
        q_ref, kc_ref, vc_ref, k_ref, v_ref, t2_ref, o_ref = refs
    else:
        q_ref, kc_ref, vc_ref, o_ref = refs
    tq = q_ref.shape[0]
    lane = lax.broadcasted_iota(I32, (tq, LANE), 1)
    per = LANE // NA_DIM
    if has_x:
        p = pl.program_id(1)
        is_first, is_last = p == 0, p == n_blocks - 1
        n_rows = n_blocks * NA_QROWS
        start = jnp.where(is_first, 0, jnp.where(is_last, n_rows - NA_KROWS, p * NA_QROWS - NA_ROWS // 2))
        off = pl.multiple_of(start * GRID_W, GRID_W)
        nk = NA_KROWS * GRID_W
        row_i = lax.broadcasted_iota(I32, (tq, nk), 0)
        key_i = lax.broadcasted_iota(I32, (tq, nk), 1)
        qi, col = _lane_group(row_i, GRID_W), row_i & (GRID_W - 1)
        kp, w = _lane_group(key_i, GRID_W), key_i & (GRID_W - 1)
        lo = jnp.where(is_first, 0, jnp.where(is_last, NA_KROWS - NA_ROWS, qi))
        col_start = jnp.clip(col - NA_COLS // 2, 0, GRID_W - NA_COLS)
        valid = (w >= col_start) & (w < col_start + NA_COLS) & (kp >= lo) & (kp < lo + NA_ROWS)
        n_pair = t2_ref.shape[1]
        first_pair = [jnp.where(is_first, NA_ROWS - 1 - i, jnp.where(is_last, -1 - i, NA_ROWS // 2 - 1 - i))
                      for i in range(NA_QROWS)]
    for j in range(NA_HEADS // per):
        cols = slice(j * LANE, (j + 1) * LANE)
        q = q_ref[:, cols].astype(F32) * (NA_DIM ** -0.5 * LOG2E)
        if has_x:
            k = jnp.concatenate([k_ref[pl.ds(off, nk), cols], kc_ref[:, cols]], axis=0)
            v = jnp.concatenate([v_ref[pl.ds(off, nk), cols], vc_ref[:, cols]], axis=0)
        else:
            k, v = kc_ref[:, cols], vc_ref[:, cols]
        acc = jnp.zeros(q.shape, F32)
        for hh in range(per):
            mine = _lane_group(lane, NA_DIM) == hh
            qh = jnp.where(mine, q, 0.0).astype(BF16)
            s = _dot(qh, k, NT)
            if has_x:
                head = j * per + hh
                bias = jnp.concatenate([
                    jnp.concatenate([t2_ref[head, jnp.clip(first_pair[i] + 2 * m, 0, n_pair - 1)]
                                     for m in range(NA_KROWS // 2)], axis=1)
                    for i in range(NA_QROWS)], axis=0)
                s = jnp.concatenate([jnp.where(valid, s[:, :nk] + bias, NEG_INF), s[:, nk:]], axis=1)
            m = jnp.max(s, axis=-1, keepdims=True)
            pr = jnp.exp2(s - m)
            l = jnp.sum(pr, axis=-1, keepdims=True)
            acc = acc + jnp.where(mine, _dot(pr.astype(BF16), v) / l, 0.0)
        o_ref[:, cols] = acc.astype(o_ref.dtype)


def _na_bias_table(rpb):
    r = rpb.astype(F32) * LOG2E
    edge = GRID_W - NA_COLS
    ext = jnp.concatenate([jnp.repeat(r[..., :1], edge, -1), r, jnp.repeat(r[..., -1:], edge, -1)], axis=-1)
    by_col = jnp.stack([ext[..., GRID_W - 1 - c:2 * GRID_W - 1 - c] for c in range(GRID_W)], axis=-2)
    return jnp.concatenate([by_col[:, :-1], by_col[:, 1:]], axis=-1)


def _na_attn(q_arr, zc, zx, bias_tab, layer, has_x):
    b, tq_all, _ = q_arr.shape
    c = zc.shape[1]
    hw = NA_HEADS * NA_DIM
    tq = NA_QROWS * GRID_W if has_x else _tile(tq_all, 256, 8)
    nq = tq_all // tq
    in_specs = [
        pl.BlockSpec((None, tq, hw), lambda bi, r: (bi, r, 0)),
        pl.BlockSpec((None, c, hw), lambda bi, r: (bi, 0, 1)),
        pl.BlockSpec((None, c, hw), lambda bi, r: (bi, 0, 2)),
    ]
    args = [q_arr, zc, zc]
    if has_x:
        n = zx.shape[1]
        assert tq_all % tq == 0 and nq * NA_QROWS >= NA_KROWS
        in_specs += [
            pl.BlockSpec((None, n, hw), lambda bi, r: (bi, 0, 1)),
            pl.BlockSpec((None, n, hw), lambda bi, r: (bi, 0, 2)),
            pl.BlockSpec((None,) + bias_tab.shape[1:], lambda bi, r: (layer, 0, 0, 0, 0)),
        ]
        args += [zx, zx, bias_tab]
    return pl.pallas_call(
        functools.partial(_na_kernel, has_x=has_x, n_blocks=nq),
        grid=(b, nq),
        in_specs=in_specs,
        out_specs=pl.BlockSpec((None, tq, hw), lambda bi, r: (bi, r, 0)),
        out_shape=jax.ShapeDtypeStruct((b, tq_all, hw), BF16),
        compiler_params=_cp("parallel", "arbitrary"),
        name="na_attn_x" if has_x else "na_attn_c",
    )(*args)


def _s5_kernel(*refs, ctx_out):
    if ctx_out:
        (uc_ref, ux_ref, d_ref, b1_ref, b2_ref, a1_ref, a2_ref, a3_ref, cp_ref, scat_ref, yc_ref, yx_ref,
         kbig, bb1, bb2, cb_ref, xc, xx, s1c, s2c, s1x, s2x, hc, hx) = refs
    else:
        (uc_ref, ux_ref, d_ref, b1_ref, b2_ref, a1_ref, a2_ref, a3_ref, cp_ref, scat_ref, yx_ref,
         kbig, bb1, bb2, cb_ref, xc, xx, s1c, s2c, s1x, s2x, hc, hx) = refs
        yc_ref = None
    dirn = pl.program_id(1)
    ell, hch = S5_CHUNK, S5_GROUP_CH
    gq = LANE // hch
    nb = ux_ref.shape[0]
    streams = ((uc_ref, xc, s1c, s2c, hc, yc_ref, uc_ref.shape[1] // ell),
               (ux_ref, xx, s1x, s2x, hx, yx_ref, ux_ref.shape[1] // ell))

    for i in range(ell):
        for j in range(ell):
            kbig[i * LANE:(i + 1) * LANE, j * LANE:(j + 1) * LANE] = d_ref[j - i + ell - 1]
    bb1[...] = jnp.zeros(bb1.shape, BF16)
    bb2[...] = jnp.zeros(bb2.shape, BF16)
    for i in range(ell):
        for gl in range(gq):
            r0 = i * LANE + gl * hch
            bb1[r0:r0 + hch, gl * LANE:(gl + 1) * LANE] = b1_ref[gl, i * hch:(i + 1) * hch, :].astype(BF16)
            bb2[r0:r0 + hch, gl * LANE:(gl + 1) * LANE] = b2_ref[gl, i * hch:(i + 1) * hch, :].astype(BF16)
    for gl in range(gq):
        cb_ref[gl * LANE:(gl + 1) * LANE, :] = _dot(cp_ref[gl], scat_ref[gl]).astype(BF16)

    @pl.when(dirn == 0)
    def _():
        for u_ref, x_s, _, _, _, _, nc in streams:
            for b in range(nb):
                for i in range(ell):
                    x_s[b * nc:(b + 1) * nc, i * LANE:(i + 1) * LANE] = u_ref[b, pl.ds(i, nc, stride=ell), :].astype(BF16)

    cpt = SUBLANE // nb
    for _, x_s, s1, s2, _, _, nc in streams:
        loc1 = _dot(x_s[...], bb1[...])
        loc2 = _dot(x_s[...], bb2[...])
        for k in range(gq):
            for b in range(nb):
                s1[k, pl.ds(b, nc, stride=nb), :] = loc1[b * nc:(b + 1) * nc, k * LANE:(k + 1) * LANE]
                s2[k, pl.ds(b, nc, stride=nb), :] = loc2[b * nc:(b + 1) * nc, k * LANE:(k + 1) * LANE]

    def lane_blocks(a_ref):
        return [jnp.broadcast_to(a_ref[:, k * LANE:(k + 1) * LANE], (SUBLANE, LANE)) for k in range(gq)]

    a1, a2, a3 = lane_blocks(a1_ref), lane_blocks(a2_ref), lane_blocks(a3_ref)
    fwd = dirn == 0
    row_grp = _lane_group(lax.broadcasted_iota(I32, (SUBLANE, LANE), 0), nb)

    def advance(v):
        return jnp.where(fwd, pltpu.roll(v, nb, 0), pltpu.roll(v, SUBLANE - nb, 0))

    zero = tuple(jnp.zeros((SUBLANE, LANE), F32) for _ in range(gq))
    carry = (zero, zero)
    for _, _, s1, s2, h, _, nc in streams:
        n_tiles = nc // cpt

        def step(t, vs, s1=s1, s2=s2, h=h, n_tiles=n_tiles):
            v1, v2 = list(vs[0]), list(vs[1])
            j = jnp.where(fwd, t, n_tiles - 1 - t)
            rows = pl.ds(pl.multiple_of(j * SUBLANE, SUBLANE), SUBLANE)
            for k in range(gq):
                loc1, loc2 = s1[k, rows, :], s2[k, rows, :]
                entered = v1[k]
                for u in range(cpt):
                    grp = jnp.where(fwd, u, cpt - 1 - u)
                    entered = jnp.where(row_grp == grp, v1[k], entered)
                    n1 = a1[k] * v1[k] + a2[k] * v2[k] + loc1
                    n2 = a1[k] * v2[k] + a3[k] * v1[k] + loc2
                    v1[k], v2[k] = advance(n1), advance(n2)
                h[k, rows, :] = entered
            return tuple(v1), tuple(v2)

        carry = lax.fori_loop(0, n_tiles, step, carry, unroll=2)

    for _, x_s, _, _, h, y_ref, nc in streams:
        if y_ref is None:
            continue
        h_all = jnp.concatenate(
            [jnp.concatenate([h[k, pl.ds(b, nc, stride=nb), :] for b in range(nb)], axis=0) for k in range(gq)],
            axis=1).astype(BF16)
        y = _dot(x_s[...], kbig[...]) + _dot(h_all, cb_ref[...])

        @pl.when(dirn == 0)
        def _():
            for b in range(nb):
                for j in range(ell):
                    y_ref[b, pl.ds(j, nc, stride=ell), :] = y[b * nc:(b + 1) * nc, j * LANE:(j + 1) * LANE]

        @pl.when(dirn == 1)
        def _():
            for b in range(nb):
                for j in range(ell):
                    y_ref[b, pl.ds(j, nc, stride=ell), :] += y[b * nc:(b + 1) * nc, j * LANE:(j + 1) * LANE]


def _s5_tables(a_re, a_im, log_step, b_re, b_im, c_re, c_im):
    ell, hch, p = S5_CHUNK, S5_GROUP_CH, S5_STATE
    lam = lax.complex(a_re.astype(F32), a_im.astype(F32))
    lam_dt = lam * jnp.exp(log_step.astype(F32))[:, :, None]
    a_bar = jnp.exp(lam_dt)
    b_bar = ((a_bar - 1.0) / lam)[:, :, :, None] * lax.complex(b_re.astype(F32), b_im.astype(F32))
    c_mat = lax.complex(c_re.astype(F32), c_im.astype(F32))
    gq = LANE // hch
    nq = S5_GROUPS // gq
    hp = lax.Precision.HIGHEST
    steps = jnp.arange(ell + 1, dtype=F32)
    apow = jnp.exp(lam_dt[:, :, None, :] * steps[None, None, :, None])
    apow_re, apow_im = apow.real, apow.imag
    bt_re, bt_im = jnp.swapaxes(b_bar.real, 2, 3), jnp.swapaxes(b_bar.imag, 2, 3)
    c_re_, c_im_ = c_mat.real, c_mat.imag
    eye = jnp.eye(gq, dtype=F32)

    def cmul(ar, ai, br, bi):
        return ar * br - ai * bi, ar * bi + ai * br

    m_re, m_im = cmul(apow_re[:, :, :ell, None, :], apow_im[:, :, :ell, None, :], bt_re[:, :, None], bt_im[:, :, None])
    kk = (jnp.einsum('dgkip,dgop->dgkio', m_re, c_re_, precision=hp)
          - jnp.einsum('dgkip,dgop->dgkio', m_im, c_im_, precision=hp))
    zeros = jnp.zeros_like(kk[0, :, 1:])
    signed = jnp.stack([jnp.concatenate([zeros, kk[0]], axis=1),
                        jnp.concatenate([jnp.flip(kk[1], axis=1), zeros], axis=1)])
    signed = signed.reshape(2, nq, gq, 2 * ell - 1, hch, hch)
    dblk = jnp.einsum('dqgkio,gh->dqkgiho', signed, eye).reshape(2, nq, 2 * ell - 1, LANE, LANE).astype(BF16)
    i_idx = jnp.arange(ell)
    e_re = jnp.stack([apow_re[0][:, ell - 1 - i_idx], apow_re[1][:, i_idx]])
    e_im = jnp.stack([apow_im[0][:, ell - 1 - i_idx], apow_im[1][:, i_idx]])
    bp_re, bp_im = cmul(e_re[:, :, :, None, :], e_im[:, :, :, None, :], bt_re[:, :, None], bt_im[:, :, None])
    bp_re, bp_im = bp_re.reshape(2, S5_GROUPS, ell * hch, p), bp_im.reshape(2, S5_GROUPS, ell * hch, p)
    b1 = jnp.concatenate([bp_re, bp_im], axis=-1)
    b2 = jnp.concatenate([bp_im, bp_re], axis=-1)
    al_re, al_im = apow_re[:, :, ell, :], apow_im[:, :, ell, :]
    lanes = lambda u, v: jnp.concatenate([u, v], axis=-1).reshape(2, nq, 1, gq * 2 * p)
    a1, a2, a3 = lanes(al_re, al_re), lanes(-al_im, al_im), lanes(al_im, -al_im)
    f_re = jnp.stack([apow_re[0][:, 1 + i_idx], apow_re[1][:, ell - i_idx]])
    f_im = jnp.stack([apow_im[0][:, 1 + i_idx], apow_im[1][:, ell - i_idx]])
    g_re, g_im = cmul(c_re_[:, :, None], c_im_[:, :, None], f_re[:, :, :, None, :], f_im[:, :, :, None, :])
    cp = jnp.concatenate([jnp.transpose(g_re, (0, 1, 4, 2, 3)), -jnp.transpose(g_im, (0, 1, 4, 2, 3))], axis=2)
    cp = cp.reshape(2, S5_GROUPS, 2 * p, ell * hch).astype(BF16)
    return dblk, b1, b2, a1, a2, a3, cp


def _s5_out_scatter():
    ell, hch = S5_CHUNK, S5_GROUP_CH
    gq = LANE // hch
    src = jnp.arange(ell * hch)
    dst = (src // hch)[None, :] * LANE + jnp.arange(gq)[:, None] * hch + (src % hch)[None, :]
    return (dst[:, :, None] == jnp.arange(ell * LANE)[None, None, :]).astype(BF16)


def _s5_scan(zx, zc, tables, layer, ctx_out):
    b, n, _ = zx.shape
    c = zc.shape[1]
    ell, hch = S5_CHUNK, S5_GROUP_CH
    gq = LANE // hch
    nq = S5_GROUPS // gq
    p2 = 2 * S5_STATE
    ncx, ncc = n // ell, c // ell
    assert n % ell == 0 and c % ell == 0 and ncc % 16 == 0
    assert SUBLANE % b == 0 and b & (b - 1) == 0
    dblk, b1, b2, a1, a2, a3, cp = tables
    u0 = 0
    sw = gq * p2
    xw = ell * LANE
    a_spec = pl.BlockSpec((None, None, None, 1, sw), lambda q, d: (layer, d, q, 0, 0))
    b_spec = pl.BlockSpec((None, None, gq, ell * hch, p2), lambda q, d: (layer, d, q, 0, 0))
    y_specs = [pl.BlockSpec((b, n, LANE), lambda q, d: (0, 0, q))]
    y_shapes = [jax.ShapeDtypeStruct((b, n, S5_WIDTH), F32)]
    if ctx_out:
        y_specs = [pl.BlockSpec((b, c, LANE), lambda q, d: (0, 0, q))] + y_specs
        y_shapes = [jax.ShapeDtypeStruct((b, c, S5_WIDTH), F32)] + y_shapes
    out = pl.pallas_call(
        functools.partial(_s5_kernel, ctx_out=ctx_out),
        grid=(nq, 2),
        in_specs=[
            pl.BlockSpec((b, c, LANE), lambda q, d: (0, 0, u0 + q)),
            pl.BlockSpec((b, n, LANE), lambda q, d: (0, 0, u0 + q)),
            pl.BlockSpec((None, None, None, 2 * ell - 1, LANE, LANE), lambda q, d: (layer, d, q, 0, 0, 0)),
            b_spec, b_spec, a_spec, a_spec, a_spec,
            pl.BlockSpec((None, None, gq, p2, ell * hch), lambda q, d: (layer, d, q, 0, 0)),
            pl.BlockSpec((gq, ell * hch, xw), lambda q, d: (0, 0, 0)),
        ],
        out_specs=y_specs,
        out_shape=y_shapes,
        scratch_shapes=[
            pltpu.VMEM((xw, xw), BF16), pltpu.VMEM((xw, sw), BF16), pltpu.VMEM((xw, sw), BF16),
            pltpu.VMEM((sw, xw), BF16),
            pltpu.VMEM((b * ncc, xw), BF16), pltpu.VMEM((b * ncx, xw), BF16),
            pltpu.VMEM((gq, b * ncc, p2), F32), pltpu.VMEM((gq, b * ncc, p2), F32),
            pltpu.VMEM((gq, b * ncx, p2), F32), pltpu.VMEM((gq, b * ncx, p2), F32),
            pltpu.VMEM((gq, b * ncc, p2), F32), pltpu.VMEM((gq, b * ncx, p2), F32),
        ],
        compiler_params=_cp("parallel", "arbitrary"),
        name="s5_scan",
    )(zc, zx, dblk, b1, b2, a1, a2, a3, cp, _s5_out_scatter())
    return (out[0], out[1]) if ctx_out else (None, out[0])


def _s5_out_kernel(y_ref, u_ref, d_ref, w_ref, o_ref):
    g = jax.nn.gelu(y_ref[...] + d_ref[...] * u_ref[...])
    o_ref[...] = (g * jax.nn.sigmoid(_dot(g.astype(BF16), w_ref[...].astype(BF16)))).astype(o_ref.dtype)


def _s5_out(y, u, d_skip, w_glu):
    b, t, w = y.shape
    tr = _tile(t, 512, 8)
    row = pl.BlockSpec((None, tr, w), lambda bi, i: (bi, i, 0))
    return pl.pallas_call(
        _s5_out_kernel,
        grid=(b, t // tr),
        in_specs=[row, row, pl.BlockSpec((1, w), lambda bi, i: (0, 0)), pl.BlockSpec((w, w), lambda bi, i: (0, 0))],
        out_specs=row,
        out_shape=jax.ShapeDtypeStruct((b, t, w), BF16),
        compiler_params=_cp("parallel", "parallel"),
        name="s5_out",
    )(y, u, d_skip.reshape(1, w).astype(F32), w_glu)


def _merge_kernel(h_ref, ya_ref, yb_ref, yc_ref, yd_ref, wg0_ref, wg1_ref, wg2_ref, wg3_ref, wb_ref, o_ref, wgs_ref, wbs_ref):
    @pl.when(pl.program_id(1) == 0)
    def _():
        for n, wg_ref in enumerate((wg0_ref, wg1_ref, wg2_ref, wg3_ref)):
            wgs_ref[n] = wg_ref[...].astype(BF16)
        wbs_ref[...] = wb_ref[...].astype(BF16)

    h = h_ref[...]
    acc = None
    for n, y_ref in enumerate((ya_ref, yb_ref, yc_ref, yd_ref)):
        term = jax.nn.sigmoid(_dot(h, wgs_ref[n])) * _dot(y_ref[...], wbs_ref[n])
        acc = term if acc is None else acc + term
    o_ref[...] = acc.astype(o_ref.dtype)


def _merge(h, branches, w_in, w_branch, layer):
    m, d = h.shape
    bw = branches[0].shape[1]
    tn = 256
    tm = _tile(m, 1024, 8)
    g0 = N_MIX_IN // tn
    per = d // tn
    assert N_MIX_IN % tn == 0 and d % tn == 0
    wg_specs = [pl.BlockSpec((None, d, tn), lambda j, i, n=n: (layer, 0, g0 + n * per + j)) for n in range(N_BRANCH)]
    y_spec = pl.BlockSpec((tm, bw), lambda j, i: (i, 0))
    return pl.pallas_call(
        _merge_kernel,
        grid=(d // tn, m // tm),
        in_specs=[pl.BlockSpec((tm, d), lambda j, i: (i, 0)), y_spec, y_spec, y_spec, y_spec] + wg_specs
                 + [pl.BlockSpec((None, N_BRANCH, bw, tn), lambda j, i: (layer, 0, 0, j))],
        out_specs=pl.BlockSpec((tm, tn), lambda j, i: (i, j)),
        out_shape=jax.ShapeDtypeStruct((m, d), BF16),
        scratch_shapes=[pltpu.VMEM((N_BRANCH, d, tn), BF16), pltpu.VMEM((N_BRANCH, bw, tn), BF16)],
        compiler_params=_cp("arbitrary", "arbitrary"),
        name="merge",
    )(h, *branches, w_in, w_in, w_in, w_in, w_branch)


def _topk_kernel(lg_ref, slot_ref, idx_ref, gate_ref, tri_ref, *, cap):
    e, t = lg_ref.shape

    @pl.when(pl.program_id(0) == 0)
    def _():
        rows = 256 if t % 256 == 0 else t
        for r0 in range(0, t, rows):
            ri = lax.broadcasted_iota(I32, (rows, t), 0) + r0
            ci = lax.broadcasted_iota(I32, (rows, t), 1)
            tri_ref[r0:r0 + rows, :] = jnp.where(ri < ci, 1.0, 0.0).astype(BF16)

    lg = lg_ref[...]
    ex = jnp.exp(lg - jnp.max(lg, axis=0, keepdims=True))
    aff = ex / jnp.sum(ex, axis=0, keepdims=True)
    bits = pltpu.bitcast(aff, I32)

    def search(_, carry):
        lo, hi = carry
        mid = lo + lax.shift_right_logical(hi - lo + 1, 1)
        ok = _count(bits >= mid, 1) >= cap
        return jnp.where(ok, mid, lo), jnp.where(ok, hi, mid - 1)

    lo0 = jnp.zeros((e, 1), I32)
    hi0 = jnp.full((e, 1), 0x7F800000, I32)
    thr, _ = lax.fori_loop(0, 32, search, (lo0, hi0))
    gt = bits > thr
    eq = bits == thr
    need = cap - _count(gt, 1)
    tri = tri_ref[...]
    eq_before = _dot(jnp.where(eq, 1.0, 0.0).astype(BF16), tri)
    sel = gt | (eq & (eq_before < need))
    sel_before = _dot(jnp.where(sel, 1.0, 0.0).astype(BF16), tri)
    slot = jnp.where(sel, sel_before.astype(I32), -1)
    slot_ref[...] = slot

    tok = lax.broadcasted_iota(I32, (SUBLANE, t), 1)
    row = lax.broadcasted_iota(I32, (SUBLANE, t), 0)
    tok_hi, tok_lo = lax.shift_right_logical(tok, 6).astype(F32), (tok & 63).astype(F32)
    slot_iota = lax.broadcasted_iota(I32, (cap, t), 0)
    for ei in range(e):
        a = jnp.broadcast_to(aff[ei:ei + 1, :], (SUBLANE, t))
        a_hi = a.astype(BF16).astype(F32)
        a_mid = (a - a_hi).astype(BF16).astype(F32)
        a_lo = a - a_hi - a_mid
        feats = jnp.where(row == 0, tok_hi, jnp.where(row == 1, tok_lo, jnp.where(
            row == 2, a_hi, jnp.where(row == 3, a_mid, jnp.where(row == 4, a_lo, 0.0))))).astype(BF16)
        pick = jnp.where(slot_iota == slot[ei:ei + 1, :], 1.0, 0.0).astype(BF16)
        res = _dot(feats, pick, NT)
        idx_ref[ei:ei + 1, :] = (res[0:1] * 64.0 + res[1:2]).astype(I32)
        gate_ref[ei:ei + 1, :] = res[2:3] + res[3:4] + res[4:5]


def _topk(logits_t, b, cap):
    e, bt = logits_t.shape
    t = bt // b
    per_b = lambda w: pl.BlockSpec((None, e, w), lambda bi: (bi, 0, 0))
    return pl.pallas_call(
        functools.partial(_topk_kernel, cap=cap),
        grid=(b,),
        in_specs=[pl.BlockSpec((e, t), lambda bi: (0, bi))],
        out_specs=[per_b(t), per_b(cap), per_b(cap)],
        out_shape=[jax.ShapeDtypeStruct((b, e, t), I32), jax.ShapeDtypeStruct((b, e, cap), I32),
                   jax.ShapeDtypeStruct((b, e, cap), F32)],
        scratch_shapes=[pltpu.VMEM((t, t), BF16)],
        compiler_params=_cp("arbitrary"),
        name="route_topk",
    )(logits_t)


def _ffn_up_kernel(*refs, n_streams, caps, n_f):
    ns = n_streams
    idx_refs, h_refs = refs[:ns], refs[ns:2 * ns]
    w1_ref, w3_ref = refs[2 * ns:2 * ns + 2]
    outs = refs[2 * ns + 2:3 * ns + 2]
    scratch = refs[3 * ns + 2:]
    gbufs, xss, sems = scratch[:ns], scratch[ns:2 * ns], scratch[2 * ns:3 * ns]
    e, f = pl.program_id(0), pl.program_id(1)
    n_e = pl.num_programs(0)
    slot = lax.rem(e, 2)

    for idx_ref, h_ref, gbuf, xs, sem, cap in zip(idx_refs, h_refs, gbufs, xss, sems, caps):
        rows = gbuf.shape[1]
        part = rows // n_f
        shift = int(math.log2(cap))

        def row_copy(expert, buf_slot, r):
            tok = idx_ref[expert, r]
            return pltpu.make_async_copy(h_ref.at[lax.shift_right_logical(r, shift), tok],
                                         gbuf.at[buf_slot, r], sem.at[buf_slot])

        @pl.when((e == 0) & (f == 0))
        def _():
            def start(r, carry):
                row_copy(0, 0, r).start()
                return carry
            lax.fori_loop(0, rows, start, 0)

        @pl.when(e + 1 < n_e)
        def _():
            for i in range(part):
                row_copy(e + 1, 1 - slot, f * part + i).start()

        @pl.when(f == 0)
        def _():
            pltpu.make_async_copy(gbuf.at[slot], gbuf.at[slot], sem.at[slot]).wait()
            xs[...] = _unpack_halves(gbuf[slot])

    w1, w3 = w1_ref[...].astype(BF16), w3_ref[...].astype(BF16)
    for xs, o_ref in zip(xss, outs):
        x = xs[...]
        a = _dot(x, w1)
        g = _dot(x, w3)
        o_ref[...] = ((a / (1.0 + jnp.exp(-a))) * g).astype(o_ref.dtype)


def _ffn_down_kernel(*refs, n_streams):
    mids = refs[:n_streams]
    gates = refs[n_streams:2 * n_streams]
    w2_ref = refs[2 * n_streams]
    outs = refs[2 * n_streams + 1:]
    w2 = w2_ref[...].astype(BF16)
    for m_ref, gate_ref, o_ref in zip(mids, gates, outs):
        o_ref[...] = (_dot(m_ref[...], w2) * gate_ref[...]).astype(o_ref.dtype)


def _ffn(idx_list, h_list, gate_list, caps, w1, w3, w2, layer):
    ns = len(idx_list)
    e = idx_list[0].shape[0]
    d = 2 * h_list[0].shape[2]
    ff = w1.shape[-1]
    tf = _tile(ff, 512)
    tn = _tile(d, 1024)
    n_f = ff // tf
    rows = [idx.shape[1] for idx in idx_list]
    assert all(r % n_f == 0 and cap & (cap - 1) == 0 for r, cap in zip(rows, caps))
    hbm = pl.BlockSpec(memory_space=pl.ANY)
    mids = pl.pallas_call(
        functools.partial(_ffn_up_kernel, n_streams=ns, caps=tuple(caps), n_f=n_f),
        grid_spec=pltpu.PrefetchScalarGridSpec(
            num_scalar_prefetch=ns,
            grid=(e, n_f),
            in_specs=[hbm] * ns + [
                pl.BlockSpec((None, None, d, tf), lambda ei, f, *_: (layer, ei, 0, f)),
                pl.BlockSpec((None, None, d, tf), lambda ei, f, *_: (layer, ei, 0, f)),
            ],
            out_specs=[pl.BlockSpec((None, r, tf), lambda ei, f, *_: (ei, 0, f)) for r in rows],
            scratch_shapes=[pltpu.VMEM((2, r, d // 2), jnp.uint32) for r in rows]
                           + [pltpu.VMEM((r, d), BF16) for r in rows]
                           + [pltpu.SemaphoreType.DMA((2,)) for _ in rows],
        ),
        out_shape=[jax.ShapeDtypeStruct((e, r, ff), BF16) for r in rows],
        compiler_params=_cp("arbitrary", "arbitrary"),
        name="moe_ffn_up",
    )(*idx_list, *h_list, w1, w3)
    return pl.pallas_call(
        functools.partial(_ffn_down_kernel, n_streams=ns),
        grid=(e, d // tn),
        in_specs=[pl.BlockSpec((None, r, ff), lambda ei, j: (ei, 0, 0)) for r in rows]
                 + [pl.BlockSpec((None, r, 1), lambda ei, j: (ei, 0, 0)) for r in rows]
                 + [pl.BlockSpec((None, None, ff, tn), lambda ei, j: (layer, ei, 0, j))],
        out_specs=[pl.BlockSpec((None, r, tn), lambda ei, j: (ei, 0, j)) for r in rows],
        out_shape=[jax.ShapeDtypeStruct((e, r, d), BF16) for r in rows],
        compiler_params=_cp("parallel", "arbitrary"),
        name="moe_ffn_down",
    )(*mids, *gate_list, w2)


def _combine_kernel(slot_ref, y_ref, x_ref, gate_ref, o_ref, pt_ref, *, cap):
    e = slot_ref.shape[1]
    tm = slot_ref.shape[0]
    r0 = pl.multiple_of(pl.program_id(2) * tm, tm)

    @pl.when(pl.program_id(1) == 0)
    def _():
        slot = slot_ref[...]
        lane = lax.broadcasted_iota(I32, (tm, cap), 1)
        for ei in range(e):
            pt_ref[pl.ds(r0, tm), ei * cap:(ei + 1) * cap] = jnp.where(lane == slot[:, ei:ei + 1], 1.0, 0.0).astype(BF16)

    y = y_ref[...]
    acc = _dot(pt_ref[pl.ds(r0, tm), :], y.reshape(e * cap, y.shape[2]))
    o_ref[...] = x_ref[...] + gate_ref[...] * acc


def _combine(slot_te, ys, x, gate, cap):
    b, t, d = x.shape
    e = slot_te.shape[2]
    tm = _tile(t, 512, 8)
    tn = _tile(d, 512)
    return pl.pallas_call(
        functools.partial(_combine_kernel, cap=cap),
        grid=(b, d // tn, t // tm),
        in_specs=[
            pl.BlockSpec((None, tm, e), lambda bi, j, i: (bi, i, 0)),
            pl.BlockSpec((e, cap, tn), lambda bi, j, i: (0, bi, j)),
            pl.BlockSpec((None, tm, tn), lambda bi, j, i: (bi, i, j)),
            pl.BlockSpec((None, 1, tn), lambda bi, j, i: (bi, 0, j)),
        ],
        out_specs=pl.BlockSpec((None, tm, tn), lambda bi, j, i: (bi, i, j)),
        out_shape=jax.ShapeDtypeStruct((b, t, d), F32),
        scratch_shapes=[pltpu.VMEM((t, e * cap), BF16)],
        compiler_params=_cp("parallel", "arbitrary", "arbitrary"),
        name="moe_combine",
    )(slot_te, ys, x, gate)


def kernel(x, c, ctx, c_ctx, ada_w, ada_b, norm1_g, norm2_g, w_in, da_lambda, da_subln_g, wb_sink, na_rpb, s5_a_re, s5_a_im, s5_log_step, s5_b_re, s5_b_im, s5_c_re, s5_c_im, s5_d, s5_glu_w, w_branch, w_out, w_router, w_e1, w_e3, w_e2, final_g):
    b, n, d = x.shape
    n_ctx = ctx.shape[1]
    depth = ada_w.shape[0]
    assert b + 1 <= ADA_ROWS and n % GRID_W == 0

    cs = jnp.zeros((ADA_ROWS, d), F32).at[:b].set(c).at[b].set(c_ctx)
    mods = _ada_mod(cs, ada_w, ada_b)
    rope_tab = _rope_tables(n)
    bias_tab = jax.vmap(_na_bias_table)(na_rpb)
    s5_tab = jax.vmap(_s5_tables)(s5_a_re, s5_a_im, s5_log_step, s5_b_re, s5_b_im, s5_c_re, s5_c_im)

    for l in range(depth):
        with_ctx = l < depth - 1
        mod_x = [mods[l, :b, k * d:(k + 1) * d].reshape(b, 1, d) for k in range(6)]
        mod_c = [jnp.broadcast_to(mods[l, b, k * d:(k + 1) * d], (b, 1, d)) for k in range(6)]

        hx = _norm_mod(x, norm1_g[l], mod_x[0], mod_x[1])
        hc = _norm_mod(ctx, norm1_g[l], mod_c[0], mod_c[1])
        proj = []
        for h, t, positions in ((hx, n, True), (hc, n_ctx, False)):
            h2d = h.reshape(b * t, d)
            proj.append([
                _mm(h2d, w_in, l, col0, ncols, tn, dt,
                    rope=(rope_tab, n_rope, t) if positions and n_rope else None).reshape(b, t, ncols)
                for col0, ncols, tn, dt, n_rope in MIX_PROJ])
        (za_x, zb_x, zn_x, zs_x), (za_c, zb_c, zn_c, zs_c) = proj

        lam_init = 0.8 - 0.6 * math.exp(-0.3 * l)
        lv = da_lambda[l].astype(F32)
        lam = jnp.exp(jnp.sum(lv[0] * lv[1])) - jnp.exp(jnp.sum(lv[2] * lv[3])) + lam_init

        ya_x = _diff_attn(za_x, za_c, za_x, za_x, lam, da_subln_g[l], lam_init, True)
        yb_x = _win_attn(zb_x, zb_c, zb_x, zb_x, wb_sink[l], True)
        yc_x = _na_attn(zn_x, zn_c, zn_x, bias_tab, l, True)
        ys_c, ys_x = _s5_scan(zs_x, zs_c, s5_tab, l, with_ctx)
        yd_x = _s5_out(ys_x, zs_x, s5_d[l], s5_glu_w[l])
        gx = _merge(hx.reshape(b * n, d), [t.reshape(b * n, -1) for t in (ya_x, yb_x, yc_x, yd_x)], w_in, w_branch, l)
        x = _mm_res(gx, w_out, l, x, mod_x[2])

        if with_ctx:
            ya_c = _diff_attn(za_c, za_c, None, None, lam, da_subln_g[l], lam_init, False)
            yb_c = _win_attn(zb_c, zb_c, None, None, wb_sink[l], False)
            yc_c = _na_attn(zn_c, zn_c, None, None, l, False)
            yd_c = _s5_out(ys_c, zs_c, s5_d[l], s5_glu_w[l])
            gc = _merge(hc.reshape(b * n_ctx, d), [t.reshape(b * n_ctx, -1) for t in (ya_c, yb_c, yc_c, yd_c)],
                        w_in, w_branch, l)
            ctx = _mm_res(gc, w_out, l, ctx, mod_c[2])

        w_router_t = jnp.transpose(w_router[l]).astype(F32)
        streams = [(x, mod_x)] + ([(ctx, mod_c)] if with_ctx else [])
        routed = []
        for s, mod in streams:
            t = s.shape[1]
            cap = EC_CAPACITY_FACTOR * t // N_EXPERTS
            h2p, logits_t = _norm_router(s, norm2_g[l], mod[3], mod[4], w_router_t)
            slot, idx, gate = _topk(logits_t, b, cap)
            by_expert = lambda v: jnp.transpose(v, (1, 0, 2)).reshape(N_EXPERTS, b * cap)
            routed.append((jnp.transpose(slot, (0, 2, 1)), by_expert(idx), h2p, by_expert(gate)[:, :, None], cap))
        ys = _ffn([r[1] for r in routed], [r[2] for r in routed], [r[3] for r in routed], [r[4] for r in routed],
                  w_e1, w_e3, w_e2, l)
        x = _combine(routed[0][0], ys[0], x, mod_x[5], routed[0][4])
        if with_ctx:
            ctx = _combine(routed[1][0], ys[1], ctx, mod_c[5], routed[1][4])

    return _final_norm(x, final_g)
```

```python
import functools
import math

import jax
import jax.numpy as jnp
from jax import lax
from jax.experimental import pallas as pl
from jax.experimental.pallas import tpu as pltpu

F32 = jnp.float32
BF16 = jnp.bfloat16
I32 = jnp.int32

GRID_W = 64
EPS = 1e-6
NEG_INF = -1e30
LOG2E = math.log2(math.e)
ROPE_BASE = 10000.0

DA_HEADS = 4
DA_QK_DIM = 64
DA_V_DIM = 2 * DA_QK_DIM
WB_HEADS = 8
WB_KV_HEADS = 2
WB_DIM = 64
WB_WINDOW = 128
WB_BLOCK = 128
NA_HEADS = 8
NA_DIM = 64
NA_ROWS = 8
NA_COLS = 16
S5_GROUPS = 32
S5_GROUP_CH = 16
S5_STATE = 64
S5_WIDTH = S5_GROUPS * S5_GROUP_CH
S5_CHUNK = 8
N_BRANCH = 4
BRANCH_WIDTH = 512
N_EXPERTS = 16
EC_CAPACITY_FACTOR = 2

IN_WIDTHS = (
    2 * DA_HEADS * DA_QK_DIM, 2 * DA_HEADS * DA_QK_DIM, DA_HEADS * DA_V_DIM,
    WB_HEADS * WB_DIM, WB_KV_HEADS * WB_DIM, WB_KV_HEADS * WB_DIM,
    NA_HEADS * NA_DIM, NA_HEADS * NA_DIM, NA_HEADS * NA_DIM,
    S5_WIDTH,
)
N_MIX_IN = sum(IN_WIDTHS)
_OFFS = [0]
for _w in IN_WIDTHS:
    _OFFS.append(_OFFS[-1] + _w)
(OFF_DA_Q, OFF_DA_K, OFF_DA_V, OFF_WB_Q, OFF_WB_K, OFF_WB_V, OFF_NA_Q, OFF_NA_K, OFF_NA_V, OFF_S5, _) = _OFFS

MIX_PROJ = (
    (OFF_DA_Q, OFF_WB_Q - OFF_DA_Q, 768, BF16, IN_WIDTHS[0] + IN_WIDTHS[1]),
    (OFF_WB_Q, OFF_NA_Q - OFF_WB_Q, 768, BF16, IN_WIDTHS[3] + IN_WIDTHS[4]),
    (OFF_NA_Q, OFF_S5 - OFF_NA_Q, 768, BF16, 0),
    (OFF_S5, S5_WIDTH, 256, F32, 0),
)

LANE = 128
SUBLANE = 8
ADA_ROWS = 8
VMEM_LIMIT = 56 * 1024 * 1024

NN = (((1,), (0,)), ((), ()))
NT = (((1,), (1,)), ((), ()))


def _cp(*sem):
    return pltpu.CompilerParams(dimension_semantics=sem, vmem_limit_bytes=VMEM_LIMIT)


def _tile(n, pref, mult=LANE):
    if n <= pref:
        return n
    t = (pref // mult) * mult
    while t >= mult:
        if n % t == 0:
            return t
        t -= mult
    return n


def _split(a):
    hi = a.astype(BF16)
    lo = (a - hi.astype(F32)).astype(BF16)
    return hi, lo


def _dot(a, b, dims=NN):
    return lax.dot_general(a, b, dims, preferred_element_type=F32)


def _lane_group(lane, width):
    return lax.shift_right_logical(lane, int(math.log2(width)))


def _count(mask, axis):
    return jnp.sum(jnp.where(mask, 1.0, 0.0), axis=axis, keepdims=True)


def _dot3(a, b, dims=NN):
    ah, al = _split(a)
    bh, bl = _split(b)
    return _dot(ah, bh, dims) + (_dot(ah, bl, dims) + _dot(al, bh, dims))


def _ada_kernel(c_ref, w_ref, b_ref, o_ref):
    c = c_ref[...]
    s = c / (1.0 + jnp.exp(-c))
    o_ref[...] = _dot3(s, w_ref[...]) + b_ref[...]


def _ada_mod(cs, ada_w, ada_b):
    depth, d, n6 = ada_w.shape
    tn = _tile(n6, 512)
    return pl.pallas_call(
        _ada_kernel,
        grid=(depth, n6 // tn),
        in_specs=[
            pl.BlockSpec((ADA_ROWS, d), lambda l, j: (0, 0)),
            pl.BlockSpec((None, d, tn), lambda l, j: (l, 0, j)),
            pl.BlockSpec((None, 1, tn), lambda l, j: (l, 0, j)),
        ],
        out_specs=pl.BlockSpec((None, ADA_ROWS, tn), lambda l, j: (l, 0, j)),
        out_shape=jax.ShapeDtypeStruct((depth, ADA_ROWS, n6), F32),
        compiler_params=_cp("arbitrary", "arbitrary"),
        name="ada_mod",
    )(cs, ada_w, ada_b.reshape(depth, 1, n6))


def _norm_mod_rows(x, g, sh, sc):
    y = x * lax.rsqrt(jnp.mean(x * x, axis=-1, keepdims=True) + EPS) * g
    return y * (1.0 + sc) + sh


def _norm_mod_kernel(x_ref, g_ref, sh_ref, sc_ref, o_ref):
    o_ref[...] = _norm_mod_rows(x_ref[...], g_ref[...], sh_ref[...], sc_ref[...]).astype(o_ref.dtype)


def _norm_mod(x, g, sh, sc):
    b, t, d = x.shape
    tr = _tile(t, 512, 8)
    return pl.pallas_call(
        _norm_mod_kernel,
        grid=(b, t // tr),
        in_specs=[
            pl.BlockSpec((None, tr, d), lambda bi, i: (bi, i, 0)),
            pl.BlockSpec((1, d), lambda bi, i: (0, 0)),
            pl.BlockSpec((None, 1, d), lambda bi, i: (bi, 0, 0)),
            pl.BlockSpec((None, 1, d), lambda bi, i: (bi, 0, 0)),
        ],
        out_specs=pl.BlockSpec((None, tr, d), lambda bi, i: (bi, i, 0)),
        out_shape=jax.ShapeDtypeStruct((b, t, d), BF16),
        compiler_params=_cp("parallel", "parallel"),
        name="norm_mod",
    )(x, g.reshape(1, d), sh, sc)


def _pack_halves(h):
    half = h.shape[1] // 2
    bits = pltpu.bitcast(h.astype(BF16).astype(F32), jnp.uint32)
    return bits[:, half:] | lax.shift_right_logical(bits[:, :half], jnp.uint32(16))


def _unpack_halves(p):
    lo = pltpu.bitcast(lax.shift_left(p, jnp.uint32(16)), F32)
    hi = pltpu.bitcast(p & jnp.uint32(0xFFFF0000), F32)
    return jnp.concatenate([lo, hi], axis=1).astype(BF16)


def _norm_router_kernel(x_ref, g_ref, sh_ref, sc_ref, wrt_ref, h_ref, lg_ref):
    h = _norm_mod_rows(x_ref[...], g_ref[...], sh_ref[...], sc_ref[...])
    h_ref[...] = _pack_halves(h)
    lg_ref[...] = _dot3(wrt_ref[...], h, NT)


def _norm_router(x, g, sh, sc, w_router_t):
    b, t, d = x.shape
    e = w_router_t.shape[0]
    tr = _tile(t, 512)
    nt = t // tr
    return pl.pallas_call(
        _norm_router_kernel,
        grid=(b, nt),
        in_specs=[
            pl.BlockSpec((None, tr, d), lambda bi, i: (bi, i, 0)),
            pl.BlockSpec((1, d), lambda bi, i: (0, 0)),
            pl.BlockSpec((None, 1, d), lambda bi, i: (bi, 0, 0)),
            pl.BlockSpec((None, 1, d), lambda bi, i: (bi, 0, 0)),
            pl.BlockSpec((e, d), lambda bi, i: (0, 0)),
        ],
        out_specs=[
            pl.BlockSpec((None, tr, d // 2), lambda bi, i: (bi, i, 0)),
            pl.BlockSpec((e, tr), lambda bi, i: (0, bi * nt + i)),
        ],
        out_shape=[jax.ShapeDtypeStruct((b, t, d // 2), jnp.uint32), jax.ShapeDtypeStruct((e, b * t), F32)],
        compiler_params=_cp("parallel", "parallel"),
        name="norm_router",
    )(x, g.reshape(1, d), sh, sc, w_router_t)


def _final_norm_kernel(x_ref, g_ref, o_ref):
    x = x_ref[...]
    o_ref[...] = x * lax.rsqrt(jnp.mean(x * x, axis=-1, keepdims=True) + EPS) * g_ref[...]


def _final_norm(x, g):
    b, t, d = x.shape
    tr = _tile(t, 512, 8)
    return pl.pallas_call(
        _final_norm_kernel,
        grid=(b, t // tr),
        in_specs=[pl.BlockSpec((None, tr, d), lambda bi, i: (bi, i, 0)), pl.BlockSpec((1, d), lambda bi, i: (0, 0))],
        out_specs=pl.BlockSpec((None, tr, d), lambda bi, i: (bi, i, 0)),
        out_shape=jax.ShapeDtypeStruct((b, t, d), F32),
        compiler_params=_cp("parallel", "parallel"),
        name="final_norm",
    )(x, g.reshape(1, d))


def _mm_kernel(a_ref, w_ref, o_ref, wb_ref):
    @pl.when(pl.program_id(1) == 0)
    def _():
        wb_ref[...] = w_ref[...].astype(BF16)

    o_ref[...] = _dot(a_ref[...], wb_ref[...]).astype(o_ref.dtype)


def _mm_rope_kernel(a_ref, w_ref, c_ref, sa_ref, sb_ref, o_ref, wb_ref, *, n_rope):
    @pl.when(pl.program_id(1) == 0)
    def _():
        wb_ref[...] = w_ref[...].astype(BF16)

    z = _dot(a_ref[...], wb_ref[...])
    tn = z.shape[1]
    col0 = pl.program_id(0) * tn
    cos, sa, sb = c_ref[...], sa_ref[...], sb_ref[...]
    for cblk in range(tn // LANE):
        x = z[:, cblk * LANE:(cblk + 1) * LANE]
        y = x * cos + pltpu.roll(x, LANE - 16, 1) * sa + pltpu.roll(x, 16, 1) * sb
        o_ref[:, cblk * LANE:(cblk + 1) * LANE] = jnp.where(col0 + cblk * LANE < n_rope, y, x).astype(o_ref.dtype)


def _mm(a, w, layer, col0, ncols, tn, out_dtype, rope=None):
    m, k = a.shape
    assert ncols % tn == 0 and col0 % tn == 0
    tm = _tile(m, 1024, 8)
    j0 = col0 // tn
    in_specs = [
        pl.BlockSpec((tm, k), lambda j, i: (i, 0)),
        pl.BlockSpec((None, k, tn), lambda j, i: (layer, 0, j0 + j)),
    ]
    args = [a, w]
    body = _mm_kernel
    if rope is not None:
        tables, n_rope, seq = rope
        assert seq % tm == 0
        per_seq = seq // tm
        in_specs += [pl.BlockSpec((tm, LANE), lambda j, i: (i % per_seq, 0))] * 3
        args += list(tables)
        body = functools.partial(_mm_rope_kernel, n_rope=n_rope)
    return pl.pallas_call(
        body,
        grid=(ncols // tn, m // tm),
        in_specs=in_specs,
        out_specs=pl.BlockSpec((tm, tn), lambda j, i: (i, j)),
        out_shape=jax.ShapeDtypeStruct((m, ncols), out_dtype),
        scratch_shapes=[pltpu.VMEM((k, tn), BF16)],
        compiler_params=_cp("arbitrary", "arbitrary"),
        name="mm_in",
    )(*args)


def _mm_res_kernel(a_ref, w_ref, x_ref, gate_ref, o_ref, wb_ref):
    @pl.when(pl.program_id(1) == 0)
    def _():
        wb_ref[...] = w_ref[...].astype(BF16)

    o_ref[...] = x_ref[...] + gate_ref[...] * _dot(a_ref[...], wb_ref[...])


def _mm_res(a, w, layer, x, gate):
    b, t, n = x.shape
    m, k = a.shape
    tn = _tile(n, 1024)
    tm = _tile(t, 1024, 8)
    per_b = t // tm
    out = pl.pallas_call(
        _mm_res_kernel,
        grid=(n // tn, m // tm),
        in_specs=[
            pl.BlockSpec((tm, k), lambda j, i: (i, 0)),
            pl.BlockSpec((None, k, tn), lambda j, i: (layer, 0, j)),
            pl.BlockSpec((tm, tn), lambda j, i: (i, j)),
            pl.BlockSpec((None, 1, tn), lambda j, i: (i // per_b, 0, j)),
        ],
        out_specs=pl.BlockSpec((tm, tn), lambda j, i: (i, j)),
        out_shape=jax.ShapeDtypeStruct((m, n), F32),
        scratch_shapes=[pltpu.VMEM((k, tn), BF16)],
        compiler_params=_cp("arbitrary", "arbitrary"),
        name="mm_out_res",
    )(a, w, x.reshape(m, n), gate)
    return out.reshape(b, t, n)


def _rope_tables(n):
    pos = jnp.arange(n)
    rows, cols = (pos // GRID_W).astype(F32), (pos % GRID_W).astype(F32)
    half = DA_QK_DIM // 2
    inv_freq = jnp.power(ROPE_BASE, -jnp.arange(0, half, 2, dtype=F32) / half)
    ang_r, ang_c = rows[:, None] * inv_freq[None, :], cols[:, None] * inv_freq[None, :]
    cos64 = jnp.concatenate([jnp.cos(ang_r)] * 2 + [jnp.cos(ang_c)] * 2, axis=-1)
    sin64 = jnp.concatenate([jnp.sin(ang_r)] * 2 + [jnp.sin(ang_c)] * 2, axis=-1)
    cos, sin = jnp.tile(cos64, (1, 2)), jnp.tile(sin64, (1, 2))
    first = (jnp.arange(LANE) % 32) < 16
    return cos, jnp.where(first, -sin, 0.0), jnp.where(first, 0.0, sin)


def _diff_attn_kernel(*refs, has_x, post_scale):
    if has_x:
        lam_ref, g_ref, q_ref, kc_ref, vc_ref, kx_ref, vx_ref, o_ref = refs
    else:
        lam_ref, g_ref, q_ref, kc_ref, vc_ref, o_ref = refs
    q = q_ref[...].astype(F32) * (DA_QK_DIM ** -0.5 * LOG2E)
    lane = lax.broadcasted_iota(I32, q.shape, 1)
    if has_x:
        k = jnp.concatenate([kc_ref[...], kx_ref[...]], axis=0)
        v = jnp.concatenate([vc_ref[...], vx_ref[...]], axis=0)
    else:
        k, v = kc_ref[...], vc_ref[...]

    nk = k.shape[0]
    kc_len = _tile(nk, 768)

    def attend(qm):
        m = l = acc = None
        for k0 in range(0, nk, kc_len):
            s = _dot(qm, k[k0:k0 + kc_len], NT)
            m_c = jnp.max(s, axis=-1, keepdims=True)
            if m is None:
                m = m_c
                p = jnp.exp2(s - m)
                l = jnp.sum(p, axis=-1, keepdims=True)
                acc = _dot(p.astype(BF16), v[k0:k0 + kc_len])
            else:
                m_new = jnp.maximum(m, m_c)
                alpha = jnp.exp2(m - m_new)
                p = jnp.exp2(s - m_new)
                l = alpha * l + jnp.sum(p, axis=-1, keepdims=True)
                acc = alpha * acc + _dot(p.astype(BF16), v[k0:k0 + kc_len])
                m = m_new
        return acc / l

    o = attend(jnp.where(lane < DA_QK_DIM, q, 0.0).astype(BF16)) \
        - lam_ref[...] * attend(jnp.where(lane >= DA_QK_DIM, q, 0.0).astype(BF16))
    y = o * lax.rsqrt(jnp.mean(o * o, axis=-1, keepdims=True) + EPS) * g_ref[...] * post_scale
    o_ref[...] = y.astype(o_ref.dtype)


def _diff_attn(q_arr, zc, kx_arr, vx_arr, lam, subln_g, lam_init, has_x):
    b, tq_all, _ = q_arr.shape
    c = zc.shape[1]
    tq = _tile(tq_all, 512, 8)
    q_blk0, kc_blk0, vc_blk0 = 0, IN_WIDTHS[0] // LANE, 2 * IN_WIDTHS[0] // LANE
    kx_blk0 = kc_blk0
    in_specs = [
        pl.BlockSpec((1, 1), lambda bi, h, i: (0, 0)),
        pl.BlockSpec((1, DA_V_DIM), lambda bi, h, i: (0, 0)),
        pl.BlockSpec((None, tq, LANE), lambda bi, h, i: (bi, i, q_blk0 + h)),
        pl.BlockSpec((None, c, LANE), lambda bi, h, i: (bi, 0, kc_blk0 + h)),
        pl.BlockSpec((None, c, LANE), lambda bi, h, i: (bi, 0, vc_blk0 + h)),
    ]
    args = [lam.reshape(1, 1), subln_g.reshape(1, DA_V_DIM), q_arr, zc, zc]
    if has_x:
        n = kx_arr.shape[1]
        in_specs += [
            pl.BlockSpec((None, n, LANE), lambda bi, h, i: (bi, 0, kx_blk0 + h)),
            pl.BlockSpec((None, n, LANE), lambda bi, h, i: (bi, 0, vc_blk0 + h)),
        ]
        args += [kx_arr, vx_arr]
    return pl.pallas_call(
        functools.partial(_diff_attn_kernel, has_x=has_x, post_scale=1.0 - lam_init),
        grid=(b, DA_HEADS, tq_all // tq),
        in_specs=in_specs,
        out_specs=pl.BlockSpec((None, tq, LANE), lambda bi, h, i: (bi, i, h)),
        out_shape=jax.ShapeDtypeStruct((b, tq_all, DA_HEADS * DA_V_DIM), BF16),
        compiler_params=_cp("parallel", "parallel", "arbitrary"),
        name="diff_attn_x" if has_x else "diff_attn_c",
    )(*args)


def _group_cols(x, g):
    x = x.astype(F32)
    lane = lax.broadcasted_iota(I32, x.shape, 1)
    own = jnp.where(_lane_group(lane, WB_DIM) == g, x, pltpu.roll(x, WB_DIM, 1))
    return jnp.concatenate([own, own], axis=1).astype(BF16)


def _win_attn_kernel(*refs, has_x, n_tok):
    if has_x:
        sink_ref, q_ref, kc_ref, vc_ref, kp_ref, k0_ref, kn_ref, vp_ref, v0_ref, vn_ref, o_ref = refs
    else:
        sink_ref, q_ref, kc_ref, vc_ref, o_ref = refs
    i = pl.program_id(1)
    rep = WB_HEADS // WB_KV_HEADS
    gw = rep * WB_DIM
    q_all = q_ref[...].astype(F32) * (WB_DIM ** -0.5 * LOG2E)
    blk = q_all.shape[0]
    lane = lax.broadcasted_iota(I32, (blk, gw), 1)
    if has_x:
        k_in = jnp.concatenate([kp_ref[...], k0_ref[...], kn_ref[...], kc_ref[...]], axis=0)
        v_in = jnp.concatenate([vp_ref[...], v0_ref[...], vn_ref[...], vc_ref[...]], axis=0)
        q_pos = i * blk + (lax.broadcasted_iota(I32, (rep * blk, 3 * blk), 0) & (blk - 1))
        k_pos = (i - 1) * blk + lax.broadcasted_iota(I32, (rep * blk, 3 * blk), 1)
        valid = (jnp.abs(q_pos - k_pos) <= WB_WINDOW) & (k_pos >= 0) & (k_pos < n_tok)
    else:
        k_in, v_in = kc_ref[...], vc_ref[...]
    outs = []
    for g in range(WB_KV_HEADS):
        q = q_all[:, g * gw:(g + 1) * gw]
        k, v = _group_cols(k_in, g), _group_cols(v_in, g)
        mine = [_lane_group(lane, WB_DIM) == r for r in range(rep)]
        qs = jnp.concatenate([jnp.where(mine[r], q, 0.0) for r in range(rep)], axis=0).astype(BF16)
        sink = jnp.concatenate([jnp.broadcast_to(sink_ref[:, (g * rep + r) * WB_DIM:(g * rep + r) * WB_DIM + 1], (blk, 1))
                                for r in range(rep)], axis=0)
        s = _dot(qs, k, NT)
        if has_x:
            s = jnp.concatenate([jnp.where(valid, s[:, :3 * blk], NEG_INF), s[:, 3 * blk:]], axis=1)
        m = jnp.maximum(jnp.max(s, axis=-1, keepdims=True), sink)
        pr = jnp.exp2(s - m)
        l = jnp.sum(pr, axis=-1, keepdims=True) + jnp.exp2(sink - m)
        o = _dot(pr.astype(BF16), v) / l
        acc = jnp.zeros(q.shape, F32)
        for r in range(rep):
            acc = acc + jnp.where(mine[r], o[r * blk:(r + 1) * blk], 0.0)
        outs.append(acc)
    o_ref[...] = jnp.concatenate(outs, axis=1).astype(o_ref.dtype)


def _win_attn(q_arr, zc, kx_arr, vx_arr, sink, has_x):
    b, tq_all, _ = q_arr.shape
    c = zc.shape[1]
    blk = WB_BLOCK
    nb = tq_all // blk
    qw = WB_HEADS * WB_DIM
    sink_arr = jnp.repeat(sink.astype(F32) * LOG2E, WB_DIM).reshape(1, qw)
    k_blk, v_blk = qw // LANE, qw // LANE + 1
    in_specs = [
        pl.BlockSpec((1, qw), lambda bi, i: (0, 0)),
        pl.BlockSpec((None, blk, qw), lambda bi, i: (bi, i, 0)),
        pl.BlockSpec((None, c, LANE), lambda bi, i: (bi, 0, k_blk)),
        pl.BlockSpec((None, c, LANE), lambda bi, i: (bi, 0, v_blk)),
    ]
    args = [sink_arr, q_arr, zc, zc]
    n_tok = 0
    if has_x:
        n_tok = kx_arr.shape[1]
        prev = lambda i: jnp.maximum(i - 1, 0)
        nxt = lambda i: jnp.minimum(i + 1, nb - 1)
        for arr, cb in ((kx_arr, k_blk), (vx_arr, v_blk)):
            in_specs += [
                pl.BlockSpec((None, blk, LANE), lambda bi, i, cb=cb: (bi, prev(i), cb)),
                pl.BlockSpec((None, blk, LANE), lambda bi, i, cb=cb: (bi, i, cb)),
                pl.BlockSpec((None, blk, LANE), lambda bi, i, cb=cb: (bi, nxt(i), cb)),
            ]
            args += [arr, arr, arr]
    return pl.pallas_call(
        functools.partial(_win_attn_kernel, has_x=has_x, n_tok=n_tok),
        grid=(b, nb),
        in_specs=in_specs,
        out_specs=pl.BlockSpec((None, blk, qw), lambda bi, i: (bi, i, 0)),
        out_shape=jax.ShapeDtypeStruct((b, tq_all, qw), BF16),
        compiler_params=_cp("parallel", "arbitrary"),
        name="win_attn_x" if has_x else "win_attn_c",
    )(*args)


NA_QROWS = 4
NA_KROWS = 12


def _na_kernel(*refs, has_x, n_blocks):
    if has_x:
        q_ref, kc_ref, vc_ref, k_ref, v_ref, t2_ref, o_ref = refs
    else:
        q_ref, kc_ref, vc_ref, o_ref = refs
    tq = q_ref.shape[0]
    lane = lax.broadcasted_iota(I32, (tq, LANE), 1)
    per = LANE // NA_DIM
    if has_x:
        p = pl.program_id(1)
        is_first, is_last = p == 0, p == n_blocks - 1
        n_rows = n_blocks * NA_QROWS
        start = jnp.where(is_first, 0, jnp.where(is_last, n_rows - NA_KROWS, p * NA_QROWS - NA_ROWS // 2))
        off = pl.multiple_of(start * GRID_W, GRID_W)
        nk = NA_KROWS * GRID_W
        row_i = lax.broadcasted_iota(I32, (tq, nk), 0)
        key_i = lax.broadcasted_iota(I32, (tq, nk), 1)
        qi, col = _lane_group(row_i, GRID_W), row_i & (GRID_W - 1)
        kp, w = _lane_group(key_i, GRID_W), key_i & (GRID_W - 1)
        lo = jnp.where(is_first, 0, jnp.where(is_last, NA_KROWS - NA_ROWS, qi))
        col_start = jnp.clip(col - NA_COLS // 2, 0, GRID_W - NA_COLS)
        valid = (w >= col_start) & (w < col_start + NA_COLS) & (kp >= lo) & (kp < lo + NA_ROWS)
        n_pair = t2_ref.shape[1]
        first_pair = [jnp.where(is_first, NA_ROWS - 1 - i, jnp.where(is_last, -1 - i, NA_ROWS // 2 - 1 - i))
                      for i in range(NA_QROWS)]
    for j in range(NA_HEADS // per):
        cols = slice(j * LANE, (j + 1) * LANE)
        q = q_ref[:, cols].astype(F32) * (NA_DIM ** -0.5 * LOG2E)
        if has_x:
            k = jnp.concatenate([k_ref[pl.ds(off, nk), cols], kc_ref[:, cols]], axis=0)
            v = jnp.concatenate([v_ref[pl.ds(off, nk), cols], vc_ref[:, cols]], axis=0)
        else:
            k, v = kc_ref[:, cols], vc_ref[:, cols]
        acc = jnp.zeros(q.shape, F32)
        for hh in range(per):
            mine = _lane_group(lane, NA_DIM) == hh
            qh = jnp.where(mine, q, 0.0).astype(BF16)
            s = _dot(qh, k, NT)
            if has_x:
                head = j * per + hh
                bias = jnp.concatenate([
                    jnp.concatenate([t2_ref[head, jnp.clip(first_pair[i] + 2 * m, 0, n_pair - 1)]
                                     for m in range(NA_KROWS // 2)], axis=1)
                    for i in range(NA_QROWS)], axis=0)
                s = jnp.concatenate([jnp.where(valid, s[:, :nk] + bias, NEG_INF), s[:, nk:]], axis=1)
            m = jnp.max(s, axis=-1, keepdims=True)
            pr = jnp.exp2(s - m)
            l = jnp.sum(pr, axis=-1, keepdims=True)
            acc = acc + jnp.where(mine, _dot(pr.astype(BF16), v) / l, 0.0)
        o_ref[:, cols] = acc.astype(o_ref.dtype)


def _na_bias_table(rpb):
    r = rpb.astype(F32) * LOG2E
    edge = GRID_W - NA_COLS
    ext = jnp.concatenate([jnp.repeat(r[..., :1], edge, -1), r, jnp.repeat(r[..., -1:], edge, -1)], axis=-1)
    by_col = jnp.stack([ext[..., GRID_W - 1 - c:2 * GRID_W - 1 - c] for c in range(GRID_W)], axis=-2)
    return jnp.concatenate([by_col[:, :-1], by_col[:, 1:]], axis=-1)


def _na_attn(q_arr, zc, zx, bias_tab, layer, has_x):
    b, tq_all, _ = q_arr.shape
    c = zc.shape[1]
    hw = NA_HEADS * NA_DIM
    tq = NA_QROWS * GRID_W if has_x else _tile(tq_all, 256, 8)
    nq = tq_all // tq
    in_specs = [
        pl.BlockSpec((None, tq, hw), lambda bi, r: (bi, r, 0)),
        pl.BlockSpec((None, c, hw), lambda bi, r: (bi, 0, 1)),
        pl.BlockSpec((None, c, hw), lambda bi, r: (bi, 0, 2)),
    ]
    args = [q_arr, zc, zc]
    if has_x:
        n = zx.shape[1]
        assert tq_all % tq == 0 and nq * NA_QROWS >= NA_KROWS
        in_specs += [
            pl.BlockSpec((None, n, hw), lambda bi, r: (bi, 0, 1)),
            pl.BlockSpec((None, n, hw), lambda bi, r: (bi, 0, 2)),
            pl.BlockSpec((None,) + bias_tab.shape[1:], lambda bi, r: (layer, 0, 0, 0, 0)),
        ]
        args += [zx, zx, bias_tab]
    return pl.pallas_call(
        functools.partial(_na_kernel, has_x=has_x, n_blocks=nq),
        grid=(b, nq),
        in_specs=in_specs,
        out_specs=pl.BlockSpec((None, tq, hw), lambda bi, r: (bi, r, 0)),
        out_shape=jax.ShapeDtypeStruct((b, tq_all, hw), BF16),
        compiler_params=_cp("parallel", "arbitrary"),
        name="na_attn_x" if has_x else "na_attn_c",
    )(*args)


def _s5_kernel(*refs, ctx_out):
    if ctx_out:
        (uc_ref, ux_ref, d_ref, b1_ref, b2_ref, a1_ref, a2_ref, a3_ref, cp_ref, scat_ref, yc_ref, yx_ref,
         kbig, bb1, bb2, cb_ref, xc, xx, s1c, s2c, s1x, s2x, hc, hx) = refs
    else:
        (uc_ref, ux_ref, d_ref, b1_ref, b2_ref, a1_ref, a2_ref, a3_ref, cp_ref, scat_ref, yx_ref,
         kbig, bb1, bb2, cb_ref, xc, xx, s1c, s2c, s1x, s2x, hc, hx) = refs
        yc_ref = None
    dirn = pl.program_id(1)
    ell, hch = S5_CHUNK, S5_GROUP_CH
    gq = LANE // hch
    nb = ux_ref.shape[0]
    streams = ((uc_ref, xc, s1c, s2c, hc, yc_ref, uc_ref.shape[1] // ell),
               (ux_ref, xx, s1x, s2x, hx, yx_ref, ux_ref.shape[1] // ell))

    for i in range(ell):
        for j in range(ell):
            kbig[i * LANE:(i + 1) * LANE, j * LANE:(j + 1) * LANE] = d_ref[j - i + ell - 1]
    bb1[...] = jnp.zeros(bb1.shape, BF16)
    bb2[...] = jnp.zeros(bb2.shape, BF16)
    for i in range(ell):
        for gl in range(gq):
            r0 = i * LANE + gl * hch
            bb1[r0:r0 + hch, gl * LANE:(gl + 1) * LANE] = b1_ref[gl, i * hch:(i + 1) * hch, :].astype(BF16)
            bb2[r0:r0 + hch, gl * LANE:(gl + 1) * LANE] = b2_ref[gl, i * hch:(i + 1) * hch, :].astype(BF16)
    for gl in range(gq):
        cb_ref[gl * LANE:(gl + 1) * LANE, :] = _dot(cp_ref[gl], scat_ref[gl]).astype(BF16)

    @pl.when(dirn == 0)
    def _():
        for u_ref, x_s, _, _, _, _, nc in streams:
            for b in range(nb):
                for i in range(ell):
                    x_s[b * nc:(b + 1) * nc, i * LANE:(i + 1) * LANE] = u_ref[b, pl.ds(i, nc, stride=ell), :].astype(BF16)

    cpt = SUBLANE // nb
    for _, x_s, s1, s2, _, _, nc in streams:
        loc1 = _dot(x_s[...], bb1[...])
        loc2 = _dot(x_s[...], bb2[...])
        for k in range(gq):
            for b in range(nb):
                s1[k, pl.ds(b, nc, stride=nb), :] = loc1[b * nc:(b + 1) * nc, k * LANE:(k + 1) * LANE]
                s2[k, pl.ds(b, nc, stride=nb), :] = loc2[b * nc:(b + 1) * nc, k * LANE:(k + 1) * LANE]

    def lane_blocks(a_ref):
        return [jnp.broadcast_to(a_ref[:, k * LANE:(k + 1) * LANE], (SUBLANE, LANE)) for k in range(gq)]

    a1, a2, a3 = lane_blocks(a1_ref), lane_blocks(a2_ref), lane_blocks(a3_ref)
    fwd = dirn == 0
    row_grp = _lane_group(lax.broadcasted_iota(I32, (SUBLANE, LANE), 0), nb)

    def advance(v):
        return jnp.where(fwd, pltpu.roll(v, nb, 0), pltpu.roll(v, SUBLANE - nb, 0))

    zero = tuple(jnp.zeros((SUBLANE, LANE), F32) for _ in range(gq))
    carry = (zero, zero)
    for _, _, s1, s2, h, _, nc in streams:
        n_tiles = nc // cpt

        def step(t, vs, s1=s1, s2=s2, h=h, n_tiles=n_tiles):
            v1, v2 = list(vs[0]), list(vs[1])
            j = jnp.where(fwd, t, n_tiles - 1 - t)
            rows = pl.ds(pl.multiple_of(j * SUBLANE, SUBLANE), SUBLANE)
            for k in range(gq):
                loc1, loc2 = s1[k, rows, :], s2[k, rows, :]
                entered = v1[k]
                for u in range(cpt):
                    grp = jnp.where(fwd, u, cpt - 1 - u)
                    entered = jnp.where(row_grp == grp, v1[k], entered)
                    n1 = a1[k] * v1[k] + a2[k] * v2[k] + loc1
                    n2 = a1[k] * v2[k] + a3[k] * v1[k] + loc2
                    v1[k], v2[k] = advance(n1), advance(n2)
                h[k, rows, :] = entered
            return tuple(v1), tuple(v2)

        carry = lax.fori_loop(0, n_tiles, step, carry, unroll=2)

    for _, x_s, _, _, h, y_ref, nc in streams:
        if y_ref is None:
            continue
        h_all = jnp.concatenate(
            [jnp.concatenate([h[k, pl.ds(b, nc, stride=nb), :] for b in range(nb)], axis=0) for k in range(gq)],
            axis=1).astype(BF16)
        y = _dot(x_s[...], kbig[...]) + _dot(h_all, cb_ref[...])

        @pl.when(dirn == 0)
        def _():
            for b in range(nb):
                for j in range(ell):
                    y_ref[b, pl.ds(j, nc, stride=ell), :] = y[b * nc:(b + 1) * nc, j * LANE:(j + 1) * LANE]

        @pl.when(dirn == 1)
        def _():
            for b in range(nb):
                for j in range(ell):
                    y_ref[b, pl.ds(j, nc, stride=ell), :] += y[b * nc:(b + 1) * nc, j * LANE:(j + 1) * LANE]


def _s5_tables(a_re, a_im, log_step, b_re, b_im, c_re, c_im):
    ell, hch, p = S5_CHUNK, S5_GROUP_CH, S5_STATE
    lam = lax.complex(a_re.astype(F32), a_im.astype(F32))
    lam_dt = lam * jnp.exp(log_step.astype(F32))[:, :, None]
    a_bar = jnp.exp(lam_dt)
    b_bar = ((a_bar - 1.0) / lam)[:, :, :, None] * lax.complex(b_re.astype(F32), b_im.astype(F32))
    c_mat = lax.complex(c_re.astype(F32), c_im.astype(F32))
    gq = LANE // hch
    nq = S5_GROUPS // gq
    hp = lax.Precision.HIGHEST
    steps = jnp.arange(ell + 1, dtype=F32)
    apow = jnp.exp(lam_dt[:, :, None, :] * steps[None, None, :, None])
    apow_re, apow_im = apow.real, apow.imag
    bt_re, bt_im = jnp.swapaxes(b_bar.real, 2, 3), jnp.swapaxes(b_bar.imag, 2, 3)
    c_re_, c_im_ = c_mat.real, c_mat.imag
    eye = jnp.eye(gq, dtype=F32)

    def cmul(ar, ai, br, bi):
        return ar * br - ai * bi, ar * bi + ai * br

    m_re, m_im = cmul(apow_re[:, :, :ell, None, :], apow_im[:, :, :ell, None, :], bt_re[:, :, None], bt_im[:, :, None])
    kk = (jnp.einsum('dgkip,dgop->dgkio', m_re, c_re_, precision=hp)
          - jnp.einsum('dgkip,dgop->dgkio', m_im, c_im_, precision=hp))
    zeros = jnp.zeros_like(kk[0, :, 1:])
    signed = jnp.stack([jnp.concatenate([zeros, kk[0]], axis=1),
                        jnp.concatenate([jnp.flip(kk[1], axis=1), zeros], axis=1)])
    signed = signed.reshape(2, nq, gq, 2 * ell - 1, hch, hch)
    dblk = jnp.einsum('dqgkio,gh->dqkgiho', signed, eye).reshape(2, nq, 2 * ell - 1, LANE, LANE).astype(BF16)
    i_idx = jnp.arange(ell)
    e_re = jnp.stack([apow_re[0][:, ell - 1 - i_idx], apow_re[1][:, i_idx]])
    e_im = jnp.stack([apow_im[0][:, ell - 1 - i_idx], apow_im[1][:, i_idx]])
    bp_re, bp_im = cmul(e_re[:, :, :, None, :], e_im[:, :, :, None, :], bt_re[:, :, None], bt_im[:, :, None])
    bp_re, bp_im = bp_re.reshape(2, S5_GROUPS, ell * hch, p), bp_im.reshape(2, S5_GROUPS, ell * hch, p)
    b1 = jnp.concatenate([bp_re, bp_im], axis=-1)
    b2 = jnp.concatenate([bp_im, bp_re], axis=-1)
    al_re, al_im = apow_re[:, :, ell, :], apow_im[:, :, ell, :]
    lanes = lambda u, v: jnp.concatenate([u, v], axis=-1).reshape(2, nq, 1, gq * 2 * p)
    a1, a2, a3 = lanes(al_re, al_re), lanes(-al_im, al_im), lanes(al_im, -al_im)
    f_re = jnp.stack([apow_re[0][:, 1 + i_idx], apow_re[1][:, ell - i_idx]])
    f_im = jnp.stack([apow_im[0][:, 1 + i_idx], apow_im[1][:, ell - i_idx]])
    g_re, g_im = cmul(c_re_[:, :, None], c_im_[:, :, None], f_re[:, :, :, None, :], f_im[:, :, :, None, :])
    cp = jnp.concatenate([jnp.transpose(g_re, (0, 1, 4, 2, 3)), -jnp.transpose(g_im, (0, 1, 4, 2, 3))], axis=2)
    cp = cp.reshape(2, S5_GROUPS, 2 * p, ell * hch).astype(BF16)
    return dblk, b1, b2, a1, a2, a3, cp


def _s5_out_scatter():
    ell, hch = S5_CHUNK, S5_GROUP_CH
    gq = LANE // hch
    src = jnp.arange(ell * hch)
    dst = (src // hch)[None, :] * LANE + jnp.arange(gq)[:, None] * hch + (src % hch)[None, :]
    return (dst[:, :, None] == jnp.arange(ell * LANE)[None, None, :]).astype(BF16)


def _s5_scan(zx, zc, tables, layer, ctx_out):
    b, n, _ = zx.shape
    c = zc.shape[1]
    ell, hch = S5_CHUNK, S5_GROUP_CH
    gq = LANE // hch
    nq = S5_GROUPS // gq
    p2 = 2 * S5_STATE
    ncx, ncc = n // ell, c // ell
    assert n % ell == 0 and c % ell == 0 and ncc % 16 == 0
    assert SUBLANE % b == 0 and b & (b - 1) == 0
    dblk, b1, b2, a1, a2, a3, cp = tables
    u0 = 0
    sw = gq * p2
    xw = ell * LANE
    a_spec = pl.BlockSpec((None, None, None, 1, sw), lambda q, d: (layer, d, q, 0, 0))
    b_spec = pl.BlockSpec((None, None, gq, ell * hch, p2), lambda q, d: (layer, d, q, 0, 0))
    y_specs = [pl.BlockSpec((b, n, LANE), lambda q, d: (0, 0, q))]
    y_shapes = [jax.ShapeDtypeStruct((b, n, S5_WIDTH), F32)]
    if ctx_out:
        y_specs = [pl.BlockSpec((b, c, LANE), lambda q, d: (0, 0, q))] + y_specs
        y_shapes = [jax.ShapeDtypeStruct((b, c, S5_WIDTH), F32)] + y_shapes
    out = pl.pallas_call(
        functools.partial(_s5_kernel, ctx_out=ctx_out),
        grid=(nq, 2),
        in_specs=[
            pl.BlockSpec((b, c, LANE), lambda q, d: (0, 0, u0 + q)),
            pl.BlockSpec((b, n, LANE), lambda q, d: (0, 0, u0 + q)),
            pl.BlockSpec((None, None, None, 2 * ell - 1, LANE, LANE), lambda q, d: (layer, d, q, 0, 0, 0)),
            b_spec, b_spec, a_spec, a_spec, a_spec,
            pl.BlockSpec((None, None, gq, p2, ell * hch), lambda q, d: (layer, d, q, 0, 0)),
            pl.BlockSpec((gq, ell * hch, xw), lambda q, d: (0, 0, 0)),
        ],
        out_specs=y_specs,
        out_shape=y_shapes,
        scratch_shapes=[
            pltpu.VMEM((xw, xw), BF16), pltpu.VMEM((xw, sw), BF16), pltpu.VMEM((xw, sw), BF16),
            pltpu.VMEM((sw, xw), BF16),
            pltpu.VMEM((b * ncc, xw), BF16), pltpu.VMEM((b * ncx, xw), BF16),
            pltpu.VMEM((gq, b * ncc, p2), F32), pltpu.VMEM((gq, b * ncc, p2), F32),
            pltpu.VMEM((gq, b * ncx, p2), F32), pltpu.VMEM((gq, b * ncx, p2), F32),
            pltpu.VMEM((gq, b * ncc, p2), F32), pltpu.VMEM((gq, b * ncx, p2), F32),
        ],
        compiler_params=_cp("parallel", "arbitrary"),
        name="s5_scan",
    )(zc, zx, dblk, b1, b2, a1, a2, a3, cp, _s5_out_scatter())
    return (out[0], out[1]) if ctx_out else (None, out[0])


def _s5_out_kernel(y_ref, u_ref, d_ref, w_ref, o_ref):
    g = jax.nn.gelu(y_ref[...] + d_ref[...] * u_ref[...])
    o_ref[...] = (g * jax.nn.sigmoid(_dot(g.astype(BF16), w_ref[...].astype(BF16)))).astype(o_ref.dtype)


def _s5_out(y, u, d_skip, w_glu):
    b, t, w = y.shape
    tr = _tile(t, 512, 8)
    row = pl.BlockSpec((None, tr, w), lambda bi, i: (bi, i, 0))
    return pl.pallas_call(
        _s5_out_kernel,
        grid=(b, t // tr),
        in_specs=[row, row, pl.BlockSpec((1, w), lambda bi, i: (0, 0)), pl.BlockSpec((w, w), lambda bi, i: (0, 0))],
        out_specs=row,
        out_shape=jax.ShapeDtypeStruct((b, t, w), BF16),
        compiler_params=_cp("parallel", "parallel"),
        name="s5_out",
    )(y, u, d_skip.reshape(1, w).astype(F32), w_glu)


def _merge_kernel(h_ref, ya_ref, yb_ref, yc_ref, yd_ref, wg0_ref, wg1_ref, wg2_ref, wg3_ref, wb_ref, o_ref, wgs_ref, wbs_ref):
    @pl.when(pl.program_id(1) == 0)
    def _():
        for n, wg_ref in enumerate((wg0_ref, wg1_ref, wg2_ref, wg3_ref)):
            wgs_ref[n] = wg_ref[...].astype(BF16)
        wbs_ref[...] = wb_ref[...].astype(BF16)

    h = h_ref[...]
    acc = None
    for n, y_ref in enumerate((ya_ref, yb_ref, yc_ref, yd_ref)):
        term = jax.nn.sigmoid(_dot(h, wgs_ref[n])) * _dot(y_ref[...], wbs_ref[n])
        acc = term if acc is None else acc + term
    o_ref[...] = acc.astype(o_ref.dtype)


def _merge(h, branches, w_in, w_branch, layer):
    m, d = h.shape
    bw = branches[0].shape[1]
    tn = 256
    tm = _tile(m, 1024, 8)
    g0 = N_MIX_IN // tn
    per = d // tn
    assert N_MIX_IN % tn == 0 and d % tn == 0
    wg_specs = [pl.BlockSpec((None, d, tn), lambda j, i, n=n: (layer, 0, g0 + n * per + j)) for n in range(N_BRANCH)]
    y_spec = pl.BlockSpec((tm, bw), lambda j, i: (i, 0))
    return pl.pallas_call(
        _merge_kernel,
        grid=(d // tn, m // tm),
        in_specs=[pl.BlockSpec((tm, d), lambda j, i: (i, 0)), y_spec, y_spec, y_spec, y_spec] + wg_specs
                 + [pl.BlockSpec((None, N_BRANCH, bw, tn), lambda j, i: (layer, 0, 0, j))],
        out_specs=pl.BlockSpec((tm, tn), lambda j, i: (i, j)),
        out_shape=jax.ShapeDtypeStruct((m, d), BF16),
        scratch_shapes=[pltpu.VMEM((N_BRANCH, d, tn), BF16), pltpu.VMEM((N_BRANCH, bw, tn), BF16)],
        compiler_params=_cp("arbitrary", "arbitrary"),
        name="merge",
    )(h, *branches, w_in, w_in, w_in, w_in, w_branch)


def _topk_kernel(lg_ref, slot_ref, idx_ref, gate_ref, tri_ref, *, cap):
    e, t = lg_ref.shape

    @pl.when(pl.program_id(0) == 0)
    def _():
        rows = 256 if t % 256 == 0 else t
        for r0 in range(0, t, rows):
            ri = lax.broadcasted_iota(I32, (rows, t), 0) + r0
            ci = lax.broadcasted_iota(I32, (rows, t), 1)
            tri_ref[r0:r0 + rows, :] = jnp.where(ri < ci, 1.0, 0.0).astype(BF16)

    lg = lg_ref[...]
    ex = jnp.exp(lg - jnp.max(lg, axis=0, keepdims=True))
    aff = ex / jnp.sum(ex, axis=0, keepdims=True)
    bits = pltpu.bitcast(aff, I32)

    def search(_, carry):
        lo, hi = carry
        mid = lo + lax.shift_right_logical(hi - lo + 1, 1)
        ok = _count(bits >= mid, 1) >= cap
        return jnp.where(ok, mid, lo), jnp.where(ok, hi, mid - 1)

    lo0 = jnp.zeros((e, 1), I32)
    hi0 = jnp.full((e, 1), 0x7F800000, I32)
    thr, _ = lax.fori_loop(0, 32, search, (lo0, hi0))
    gt = bits > thr
    eq = bits == thr
    need = cap - _count(gt, 1)
    tri = tri_ref[...]
    eq_before = _dot(jnp.where(eq, 1.0, 0.0).astype(BF16), tri)
    sel = gt | (eq & (eq_before < need))
    sel_before = _dot(jnp.where(sel, 1.0, 0.0).astype(BF16), tri)
    slot = jnp.where(sel, sel_before.astype(I32), -1)
    slot_ref[...] = slot

    tok = lax.broadcasted_iota(I32, (SUBLANE, t), 1)
    row = lax.broadcasted_iota(I32, (SUBLANE, t), 0)
    tok_hi, tok_lo = lax.shift_right_logical(tok, 6).astype(F32), (tok & 63).astype(F32)
    slot_iota = lax.broadcasted_iota(I32, (cap, t), 0)
    for ei in range(e):
        a = jnp.broadcast_to(aff[ei:ei + 1, :], (SUBLANE, t))
        a_hi = a.astype(BF16).astype(F32)
        a_mid = (a - a_hi).astype(BF16).astype(F32)
        a_lo = a - a_hi - a_mid
        feats = jnp.where(row == 0, tok_hi, jnp.where(row == 1, tok_lo, jnp.where(
            row == 2, a_hi, jnp.where(row == 3, a_mid, jnp.where(row == 4, a_lo, 0.0))))).astype(BF16)
        pick = jnp.where(slot_iota == slot[ei:ei + 1, :], 1.0, 0.0).astype(BF16)
        res = _dot(feats, pick, NT)
        idx_ref[ei:ei + 1, :] = (res[0:1] * 64.0 + res[1:2]).astype(I32)
        gate_ref[ei:ei + 1, :] = res[2:3] + res[3:4] + res[4:5]


def _topk(logits_t, b, cap):
    e, bt = logits_t.shape
    t = bt // b
    per_b = lambda w: pl.BlockSpec((None, e, w), lambda bi: (bi, 0, 0))
    return pl.pallas_call(
        functools.partial(_topk_kernel, cap=cap),
        grid=(b,),
        in_specs=[pl.BlockSpec((e, t), lambda bi: (0, bi))],
        out_specs=[per_b(t), per_b(cap), per_b(cap)],
        out_shape=[jax.ShapeDtypeStruct((b, e, t), I32), jax.ShapeDtypeStruct((b, e, cap), I32),
                   jax.ShapeDtypeStruct((b, e, cap), F32)],
        scratch_shapes=[pltpu.VMEM((t, t), BF16)],
        compiler_params=_cp("arbitrary"),
        name="route_topk",
    )(logits_t)


def _ffn_up_kernel(*refs, n_streams, n_f):
    ns = n_streams
    idx_refs, h_refs = refs[:ns], refs[ns:2 * ns]
    w1_ref, w3_ref = refs[2 * ns:2 * ns + 2]
    outs = refs[2 * ns + 2:3 * ns + 2]
    scratch = refs[3 * ns + 2:]
    gbufs, xss, sems = scratch[:ns], scratch[ns:2 * ns], scratch[2 * ns:3 * ns]
    e, f = pl.program_id(0), pl.program_id(1)
    n_e = pl.num_programs(0)
    slot = lax.rem(e, 2)

    for idx_ref, h_ref, gbuf, xs, sem in zip(idx_refs, h_refs, gbufs, xss, sems):
        rows = gbuf.shape[1]

        @pl.when((e == 0) & (f == 0))
        def _():
            def start(r, carry):
                pltpu.make_async_copy(h_ref.at[idx_ref[0, r]], gbuf.at[0, r], sem.at[0]).start()
                return carry
            lax.fori_loop(0, rows, start, 0)

        @pl.when(f == 0)
        def _():
            pltpu.make_async_copy(gbuf.at[slot], gbuf.at[slot], sem.at[slot]).wait()
            xs[...] = _unpack_halves(gbuf[slot])

    nxt = jnp.minimum(e + 1, n_e - 1)
    w1, w3 = w1_ref[...].astype(BF16), w3_ref[...].astype(BF16)
    for idx_ref, h_ref, gbuf, xs, sem, o_ref in zip(idx_refs, h_refs, gbufs, xss, sems, outs):
        x = xs[...]
        a = _dot(x, w1)
        g = _dot(x, w3)
        part = gbuf.shape[1] // n_f
        for i in range(part):
            r = f * part + i
            pltpu.make_async_copy(h_ref.at[idx_ref[nxt, r]], gbuf.at[1 - slot, r], sem.at[1 - slot]).start()
        o_ref[...] = ((a / (1.0 + jnp.exp(-a))) * g).astype(o_ref.dtype)

    @pl.when((e == n_e - 1) & (f == n_f - 1))
    def _():
        for gbuf, sem in zip(gbufs, sems):
            pltpu.make_async_copy(gbuf.at[1 - slot], gbuf.at[1 - slot], sem.at[1 - slot]).wait()


def _ffn_down_kernel(*refs, n_streams):
    mids = refs[:n_streams]
    gates = refs[n_streams:2 * n_streams]
    w2_ref = refs[2 * n_streams]
    outs = refs[2 * n_streams + 1:]
    w2 = w2_ref[...].astype(BF16)
    for m_ref, gate_ref, o_ref in zip(mids, gates, outs):
        o_ref[...] = (_dot(m_ref[...], w2) * gate_ref[...]).astype(o_ref.dtype)


def _ffn(idx_list, h_list, gate_list, w1, w3, w2, layer):
    ns = len(idx_list)
    e = idx_list[0].shape[0]
    d = 2 * h_list[0].shape[1]
    ff = w1.shape[-1]
    tf = _tile(ff, 512)
    tn = _tile(d, 1024)
    n_f = ff // tf
    rows = [idx.shape[1] for idx in idx_list]
    assert all(r % n_f == 0 for r in rows)
    hbm = pl.BlockSpec(memory_space=pl.ANY)
    mids = pl.pallas_call(
        functools.partial(_ffn_up_kernel, n_streams=ns, n_f=n_f),
        grid_spec=pltpu.PrefetchScalarGridSpec(
            num_scalar_prefetch=ns,
            grid=(e, n_f),
            in_specs=[hbm] * ns + [
                pl.BlockSpec((None, None, d, tf), lambda ei, f, *_: (layer, ei, 0, f)),
                pl.BlockSpec((None, None, d, tf), lambda ei, f, *_: (layer, ei, 0, f)),
            ],
            out_specs=[pl.BlockSpec((None, r, tf), lambda ei, f, *_: (ei, 0, f)) for r in rows],
            scratch_shapes=[pltpu.VMEM((2, r, d // 2), jnp.uint32) for r in rows]
                           + [pltpu.VMEM((r, d), BF16) for r in rows]
                           + [pltpu.SemaphoreType.DMA((2,)) for _ in rows],
        ),
        out_shape=[jax.ShapeDtypeStruct((e, r, ff), BF16) for r in rows],
        compiler_params=_cp("arbitrary", "arbitrary"),
        name="moe_ffn_up",
    )(*idx_list, *h_list, w1, w3)
    return pl.pallas_call(
        functools.partial(_ffn_down_kernel, n_streams=ns),
        grid=(e, d // tn),
        in_specs=[pl.BlockSpec((None, r, ff), lambda ei, j: (ei, 0, 0)) for r in rows]
                 + [pl.BlockSpec((None, r, 1), lambda ei, j: (ei, 0, 0)) for r in rows]
                 + [pl.BlockSpec((None, None, ff, tn), lambda ei, j: (layer, ei, 0, j))],
        out_specs=[pl.BlockSpec((None, r, tn), lambda ei, j: (ei, 0, j)) for r in rows],
        out_shape=[jax.ShapeDtypeStruct((e, r, d), BF16) for r in rows],
        compiler_params=_cp("parallel", "arbitrary"),
        name="moe_ffn_down",
    )(*mids, *gate_list, w2)


def _combine_kernel(slot_ref, y_ref, x_ref, gate_ref, o_ref, pt_ref, *, cap):
    e = slot_ref.shape[1]
    tm = slot_ref.shape[0]
    r0 = pl.multiple_of(pl.program_id(2) * tm, tm)

    @pl.when(pl.program_id(1) == 0)
    def _():
        slot = slot_ref[...]
        lane = lax.broadcasted_iota(I32, (tm, cap), 1)
        for ei in range(e):
            pt_ref[pl.ds(r0, tm), ei * cap:(ei + 1) * cap] = jnp.where(lane == slot[:, ei:ei + 1], 1.0, 0.0).astype(BF16)

    y = y_ref[...]
    acc = _dot(pt_ref[pl.ds(r0, tm), :], y.reshape(e * cap, y.shape[2]))
    o_ref[...] = x_ref[...] + gate_ref[...] * acc


def _combine(slot_te, ys, x, gate, cap):
    b, t, d = x.shape
    e = slot_te.shape[2]
    tm = _tile(t, 512, 8)
    tn = _tile(d, 512)
    return pl.pallas_call(
        functools.partial(_combine_kernel, cap=cap),
        grid=(b, d // tn, t // tm),
        in_specs=[
            pl.BlockSpec((None, tm, e), lambda bi, j, i: (bi, i, 0)),
            pl.BlockSpec((e, cap, tn), lambda bi, j, i: (0, bi, j)),
            pl.BlockSpec((None, tm, tn), lambda bi, j, i: (bi, i, j)),
            pl.BlockSpec((None, 1, tn), lambda bi, j, i: (bi, 0, j)),
        ],
        out_specs=pl.BlockSpec((None, tm, tn), lambda bi, j, i: (bi, i, j)),
        out_shape=jax.ShapeDtypeStruct((b, t, d), F32),
        scratch_shapes=[pltpu.VMEM((t, e * cap), BF16)],
        compiler_params=_cp("parallel", "arbitrary", "arbitrary"),
        name="moe_combine",
    )(slot_te, ys, x, gate)


def kernel(x, c, ctx, c_ctx, ada_w, ada_b, norm1_g, norm2_g, w_in, da_lambda, da_subln_g, wb_sink, na_rpb, s5_a_re, s5_a_im, s5_log_step, s5_b_re, s5_b_im, s5_c_re, s5_c_im, s5_d, s5_glu_w, w_branch, w_out, w_router, w_e1, w_e3, w_e2, final_g):
    b, n, d = x.shape
    n_ctx = ctx.shape[1]
    depth = ada_w.shape[0]
    assert b + 1 <= ADA_ROWS and n % GRID_W == 0

    cs = jnp.zeros((ADA_ROWS, d), F32).at[:b].set(c).at[b].set(c_ctx)
    mods = _ada_mod(cs, ada_w, ada_b)
    rope_tab = _rope_tables(n)
    bias_tab = jax.vmap(_na_bias_table)(na_rpb)
    s5_tab = jax.vmap(_s5_tables)(s5_a_re, s5_a_im, s5_log_step, s5_b_re, s5_b_im, s5_c_re, s5_c_im)

    for l in range(depth):
        with_ctx = l < depth - 1
        mod_x = [mods[l, :b, k * d:(k + 1) * d].reshape(b, 1, d) for k in range(6)]
        mod_c = [jnp.broadcast_to(mods[l, b, k * d:(k + 1) * d], (b, 1, d)) for k in range(6)]

        hx = _norm_mod(x, norm1_g[l], mod_x[0], mod_x[1])
        hc = _norm_mod(ctx, norm1_g[l], mod_c[0], mod_c[1])
        proj = []
        for h, t, positions in ((hx, n, True), (hc, n_ctx, False)):
            h2d = h.reshape(b * t, d)
            proj.append([
                _mm(h2d, w_in, l, col0, ncols, tn, dt,
                    rope=(rope_tab, n_rope, t) if positions and n_rope else None).reshape(b, t, ncols)
                for col0, ncols, tn, dt, n_rope in MIX_PROJ])
        (za_x, zb_x, zn_x, zs_x), (za_c, zb_c, zn_c, zs_c) = proj

        lam_init = 0.8 - 0.6 * math.exp(-0.3 * l)
        lv = da_lambda[l].astype(F32)
        lam = jnp.exp(jnp.sum(lv[0] * lv[1])) - jnp.exp(jnp.sum(lv[2] * lv[3])) + lam_init

        ya_x = _diff_attn(za_x, za_c, za_x, za_x, lam, da_subln_g[l], lam_init, True)
        yb_x = _win_attn(zb_x, zb_c, zb_x, zb_x, wb_sink[l], True)
        yc_x = _na_attn(zn_x, zn_c, zn_x, bias_tab, l, True)
        ys_c, ys_x = _s5_scan(zs_x, zs_c, s5_tab, l, with_ctx)
        yd_x = _s5_out(ys_x, zs_x, s5_d[l], s5_glu_w[l])
        gx = _merge(hx.reshape(b * n, d), [t.reshape(b * n, -1) for t in (ya_x, yb_x, yc_x, yd_x)], w_in, w_branch, l)
        x = _mm_res(gx, w_out, l, x, mod_x[2])

        if with_ctx:
            ya_c = _diff_attn(za_c, za_c, None, None, lam, da_subln_g[l], lam_init, False)
            yb_c = _win_attn(zb_c, zb_c, None, None, wb_sink[l], False)
            yc_c = _na_attn(zn_c, zn_c, None, None, l, False)
            yd_c = _s5_out(ys_c, zs_c, s5_d[l], s5_glu_w[l])
            gc = _merge(hc.reshape(b * n_ctx, d), [t.reshape(b * n_ctx, -1) for t in (ya_c, yb_c, yc_c, yd_c)],
                        w_in, w_branch, l)
            ctx = _mm_res(gc, w_out, l, ctx, mod_c[2])

        w_router_t = jnp.transpose(w_router[l]).astype(F32)
        streams = [(x, mod_x)] + ([(ctx, mod_c)] if with_ctx else [])
        routed = []
        for s, mod in streams:
            t = s.shape[1]
            cap = EC_CAPACITY_FACTOR * t // N_EXPERTS
            h2p, logits_t = _norm_router(s, norm2_g[l], mod[3], mod[4], w_router_t)
            slot, idx, gate = _topk(logits_t, b, cap)
            by_expert = lambda v: jnp.transpose(v, (1, 0, 2)).reshape(N_EXPERTS, b * cap)
            flat_row = idx + (jnp.arange(b, dtype=I32) * t)[:, None, None]
            routed.append((jnp.transpose(slot, (0, 2, 1)), by_expert(flat_row), h2p.reshape(b * t, d // 2),
                           by_expert(gate)[:, :, None], cap))
        ys = _ffn([r[1] for r in routed], [r[2] for r in routed], [r[3] for r in routed], w_e1, w_e3, w_e2, l)
        x = _combine(routed[0][0], ys[0], x, mod_x[5], routed[0][4])
        if with_ctx:
            ctx = _combine(routed[1][0], ys[1], ctx, mod_c[5], routed[1][4])

    return _final_norm(x, final_g)
```

```python
import functools
import math

import jax
import jax.numpy as jnp
from jax import lax
from jax.experimental import pallas as pl
from jax.experimental.pallas import tpu as pltpu

F32 = jnp.float32
BF16 = jnp.bfloat16
I32 = jnp.int32

GRID_W = 64
EPS = 1e-6
NEG_INF = -1e30
LOG2E = math.log2(math.e)
ROPE_BASE = 10000.0

DA_HEADS = 4
DA_QK_DIM = 64
DA_V_DIM = 2 * DA_QK_DIM
WB_HEADS = 8
WB_KV_HEADS = 2
WB_DIM = 64
WB_WINDOW = 128
WB_BLOCK = 128
NA_HEADS = 8
NA_DIM = 64
NA_ROWS = 8
NA_COLS = 16
S5_GROUPS = 32
S5_GROUP_CH = 16
S5_STATE = 64
S5_WIDTH = S5_GROUPS * S5_GROUP_CH
S5_CHUNK = 8
N_BRANCH = 4
BRANCH_WIDTH = 512
N_EXPERTS = 16
EC_CAPACITY_FACTOR = 2

IN_WIDTHS = (
    2 * DA_HEADS * DA_QK_DIM, 2 * DA_HEADS * DA_QK_DIM, DA_HEADS * DA_V_DIM,
    WB_HEADS * WB_DIM, WB_KV_HEADS * WB_DIM, WB_KV_HEADS * WB_DIM,
    NA_HEADS * NA_DIM, NA_HEADS * NA_DIM, NA_HEADS * NA_DIM,
    S5_WIDTH,
)
N_MIX_IN = sum(IN_WIDTHS)
_OFFS = [0]
for _w in IN_WIDTHS:
    _OFFS.append(_OFFS[-1] + _w)
(OFF_DA_Q, OFF_DA_K, OFF_DA_V, OFF_WB_Q, OFF_WB_K, OFF_WB_V, OFF_NA_Q, OFF_NA_K, OFF_NA_V, OFF_S5, _) = _OFFS

MIX_PROJ = (
    (OFF_DA_Q, OFF_WB_Q - OFF_DA_Q, 768, BF16, IN_WIDTHS[0] + IN_WIDTHS[1]),
    (OFF_WB_Q, OFF_NA_Q - OFF_WB_Q, 768, BF16, IN_WIDTHS[3] + IN_WIDTHS[4]),
    (OFF_NA_Q, OFF_S5 - OFF_NA_Q, 768, BF16, 0),
    (OFF_S5, S5_WIDTH, 256, F32, 0),
)

LANE = 128
SUBLANE = 8
ADA_ROWS = 8
VMEM_LIMIT = 56 * 1024 * 1024

NN = (((1,), (0,)), ((), ()))
NT = (((1,), (1,)), ((), ()))


def _cp(*sem):
    return pltpu.CompilerParams(dimension_semantics=sem, vmem_limit_bytes=VMEM_LIMIT)


def _tile(n, pref, mult=LANE):
    if n <= pref:
        return n
    t = (pref // mult) * mult
    while t >= mult:
        if n % t == 0:
            return t
        t -= mult
    return n


def _split(a):
    hi = a.astype(BF16)
    lo = (a - hi.astype(F32)).astype(BF16)
    return hi, lo


def _dot(a, b, dims=NN):
    return lax.dot_general(a, b, dims, preferred_element_type=F32)


def _lane_group(lane, width):
    return lax.shift_right_logical(lane, int(math.log2(width)))


def _count(mask, axis):
    return jnp.sum(jnp.where(mask, 1.0, 0.0), axis=axis, keepdims=True)


def _dot3(a, b, dims=NN):
    ah, al = _split(a)
    bh, bl = _split(b)
    return _dot(ah, bh, dims) + (_dot(ah, bl, dims) + _dot(al, bh, dims))


def _ada_kernel(c_ref, w_ref, b_ref, o_ref):
    c = c_ref[...]
    s = c / (1.0 + jnp.exp(-c))
    o_ref[...] = _dot3(s, w_ref[...]) + b_ref[...]


def _ada_mod(cs, ada_w, ada_b):
    depth, d, n6 = ada_w.shape
    tn = _tile(n6, 512)
    return pl.pallas_call(
        _ada_kernel,
        grid=(depth, n6 // tn),
        in_specs=[
            pl.BlockSpec((ADA_ROWS, d), lambda l, j: (0, 0)),
            pl.BlockSpec((None, d, tn), lambda l, j: (l, 0, j)),
            pl.BlockSpec((None, 1, tn), lambda l, j: (l, 0, j)),
        ],
        out_specs=pl.BlockSpec((None, ADA_ROWS, tn), lambda l, j: (l, 0, j)),
        out_shape=jax.ShapeDtypeStruct((depth, ADA_ROWS, n6), F32),
        compiler_params=_cp("arbitrary", "arbitrary"),
        name="ada_mod",
    )(cs, ada_w, ada_b.reshape(depth, 1, n6))


def _norm_mod_rows(x, g, sh, sc):
    y = x * lax.rsqrt(jnp.mean(x * x, axis=-1, keepdims=True) + EPS) * g
    return y * (1.0 + sc) + sh


def _norm_mod_kernel(x_ref, g_ref, sh_ref, sc_ref, o_ref):
    o_ref[...] = _norm_mod_rows(x_ref[...], g_ref[...], sh_ref[...], sc_ref[...]).astype(o_ref.dtype)


def _norm_mod(x, g, sh, sc):
    b, t, d = x.shape
    tr = _tile(t, 512, 8)
    return pl.pallas_call(
        _norm_mod_kernel,
        grid=(b, t // tr),
        in_specs=[
            pl.BlockSpec((None, tr, d), lambda bi, i: (bi, i, 0)),
            pl.BlockSpec((1, d), lambda bi, i: (0, 0)),
            pl.BlockSpec((None, 1, d), lambda bi, i: (bi, 0, 0)),
            pl.BlockSpec((None, 1, d), lambda bi, i: (bi, 0, 0)),
        ],
        out_specs=pl.BlockSpec((None, tr, d), lambda bi, i: (bi, i, 0)),
        out_shape=jax.ShapeDtypeStruct((b, t, d), BF16),
        compiler_params=_cp("parallel", "parallel"),
        name="norm_mod",
    )(x, g.reshape(1, d), sh, sc)


def _pack_halves(h):
    half = h.shape[1] // 2
    bits = pltpu.bitcast(h.astype(BF16).astype(F32), jnp.uint32)
    return bits[:, half:] | lax.shift_right_logical(bits[:, :half], jnp.uint32(16))


def _unpack_halves(p):
    lo = pltpu.bitcast(lax.shift_left(p, jnp.uint32(16)), F32)
    hi = pltpu.bitcast(p & jnp.uint32(0xFFFF0000), F32)
    return jnp.concatenate([lo, hi], axis=1).astype(BF16)


def _norm_router_kernel(x_ref, g_ref, sh_ref, sc_ref, wrt_ref, h_ref, lg_ref):
    h = _norm_mod_rows(x_ref[...], g_ref[...], sh_ref[...], sc_ref[...])
    h_ref[...] = _pack_halves(h)
    lg_ref[...] = _dot3(wrt_ref[...], h, NT)


def _norm_router(x, g, sh, sc, w_router_t):
    b, t, d = x.shape
    e = w_router_t.shape[0]
    tr = _tile(t, 512)
    nt = t // tr
    return pl.pallas_call(
        _norm_router_kernel,
        grid=(b, nt),
        in_specs=[
            pl.BlockSpec((None, tr, d), lambda bi, i: (bi, i, 0)),
            pl.BlockSpec((1, d), lambda bi, i: (0, 0)),
            pl.BlockSpec((None, 1, d), lambda bi, i: (bi, 0, 0)),
            pl.BlockSpec((None, 1, d), lambda bi, i: (bi, 0, 0)),
            pl.BlockSpec((e, d), lambda bi, i: (0, 0)),
        ],
        out_specs=[
            pl.BlockSpec((None, tr, d // 2), lambda bi, i: (bi, i, 0)),
            pl.BlockSpec((e, tr), lambda bi, i: (0, bi * nt + i)),
        ],
        out_shape=[jax.ShapeDtypeStruct((b, t, d // 2), jnp.uint32), jax.ShapeDtypeStruct((e, b * t), F32)],
        compiler_params=_cp("parallel", "parallel"),
        name="norm_router",
    )(x, g.reshape(1, d), sh, sc, w_router_t)


def _final_norm_kernel(x_ref, g_ref, o_ref):
    x = x_ref[...]
    o_ref[...] = x * lax.rsqrt(jnp.mean(x * x, axis=-1, keepdims=True) + EPS) * g_ref[...]


def _final_norm(x, g):
    b, t, d = x.shape
    tr = _tile(t, 512, 8)
    return pl.pallas_call(
        _final_norm_kernel,
        grid=(b, t // tr),
        in_specs=[pl.BlockSpec((None, tr, d), lambda bi, i: (bi, i, 0)), pl.BlockSpec((1, d), lambda bi, i: (0, 0))],
        out_specs=pl.BlockSpec((None, tr, d), lambda bi, i: (bi, i, 0)),
        out_shape=jax.ShapeDtypeStruct((b, t, d), F32),
        compiler_params=_cp("parallel", "parallel"),
        name="final_norm",
    )(x, g.reshape(1, d))


def _mm_kernel(a_ref, w_ref, o_ref, wb_ref):
    @pl.when(pl.program_id(1) == 0)
    def _():
        wb_ref[...] = w_ref[...].astype(BF16)

    o_ref[...] = _dot(a_ref[...], wb_ref[...]).astype(o_ref.dtype)


def _mm_rope_kernel(a_ref, w_ref, c_ref, sa_ref, sb_ref, o_ref, wb_ref, *, n_rope):
    @pl.when(pl.program_id(1) == 0)
    def _():
        wb_ref[...] = w_ref[...].astype(BF16)

    z = _dot(a_ref[...], wb_ref[...])
    tn = z.shape[1]
    col0 = pl.program_id(0) * tn
    cos, sa, sb = c_ref[...], sa_ref[...], sb_ref[...]
    for cblk in range(tn // LANE):
        x = z[:, cblk * LANE:(cblk + 1) * LANE]
        y = x * cos + pltpu.roll(x, LANE - 16, 1) * sa + pltpu.roll(x, 16, 1) * sb
        o_ref[:, cblk * LANE:(cblk + 1) * LANE] = jnp.where(col0 + cblk * LANE < n_rope, y, x).astype(o_ref.dtype)


def _mm(a, w, layer, col0, ncols, tn, out_dtype, rope=None):
    m, k = a.shape
    assert ncols % tn == 0 and col0 % tn == 0
    tm = _tile(m, 1024, 8)
    j0 = col0 // tn
    in_specs = [
        pl.BlockSpec((tm, k), lambda j, i: (i, 0)),
        pl.BlockSpec((None, k, tn), lambda j, i: (layer, 0, j0 + j)),
    ]
    args = [a, w]
    body = _mm_kernel
    if rope is not None:
        tables, n_rope, seq = rope
        assert seq % tm == 0
        per_seq = seq // tm
        in_specs += [pl.BlockSpec((tm, LANE), lambda j, i: (i % per_seq, 0))] * 3
        args += list(tables)
        body = functools.partial(_mm_rope_kernel, n_rope=n_rope)
    return pl.pallas_call(
        body,
        grid=(ncols // tn, m // tm),
        in_specs=in_specs,
        out_specs=pl.BlockSpec((tm, tn), lambda j, i: (i, j)),
        out_shape=jax.ShapeDtypeStruct((m, ncols), out_dtype),
        scratch_shapes=[pltpu.VMEM((k, tn), BF16)],
        compiler_params=_cp("arbitrary", "arbitrary"),
        name="mm_in",
    )(*args)


def _mm_res_kernel(a_ref, w_ref, x_ref, gate_ref, o_ref, wb_ref):
    @pl.when(pl.program_id(1) == 0)
    def _():
        wb_ref[...] = w_ref[...].astype(BF16)

    o_ref[...] = x_ref[...] + gate_ref[...] * _dot(a_ref[...], wb_ref[...])


def _mm_res(a, w, layer, x, gate):
    b, t, n = x.shape
    m, k = a.shape
    tn = _tile(n, 1024)
    tm = _tile(t, 1024, 8)
    per_b = t // tm
    out = pl.pallas_call(
        _mm_res_kernel,
        grid=(n // tn, m // tm),
        in_specs=[
            pl.BlockSpec((tm, k), lambda j, i: (i, 0)),
            pl.BlockSpec((None, k, tn), lambda j, i: (layer, 0, j)),
            pl.BlockSpec((tm, tn), lambda j, i: (i, j)),
            pl.BlockSpec((None, 1, tn), lambda j, i: (i // per_b, 0, j)),
        ],
        out_specs=pl.BlockSpec((tm, tn), lambda j, i: (i, j)),
        out_shape=jax.ShapeDtypeStruct((m, n), F32),
        scratch_shapes=[pltpu.VMEM((k, tn), BF16)],
        compiler_params=_cp("arbitrary", "arbitrary"),
        name="mm_out_res",
    )(a, w, x.reshape(m, n), gate)
    return out.reshape(b, t, n)


def _rope_tables(n):
    pos = jnp.arange(n)
    rows, cols = (pos // GRID_W).astype(F32), (pos % GRID_W).astype(F32)
    half = DA_QK_DIM // 2
    inv_freq = jnp.power(ROPE_BASE, -jnp.arange(0, half, 2, dtype=F32) / half)
    ang_r, ang_c = rows[:, None] * inv_freq[None, :], cols[:, None] * inv_freq[None, :]
    cos64 = jnp.concatenate([jnp.cos(ang_r)] * 2 + [jnp.cos(ang_c)] * 2, axis=-1)
    sin64 = jnp.concatenate([jnp.sin(ang_r)] * 2 + [jnp.sin(ang_c)] * 2, axis=-1)
    cos, sin = jnp.tile(cos64, (1, 2)), jnp.tile(sin64, (1, 2))
    first = (jnp.arange(LANE) % 32) < 16
    return cos, jnp.where(first, -sin, 0.0), jnp.where(first, 0.0, sin)


def _diff_attn_kernel(*refs, has_x, post_scale):
    if has_x:
        lam_ref, g_ref, q_ref, kc_ref, vc_ref, kx_ref, vx_ref, o_ref = refs
    else:
        lam_ref, g_ref, q_ref, kc_ref, vc_ref, o_ref = refs
    q = q_ref[...].astype(F32) * (DA_QK_DIM ** -0.5 * LOG2E)
    lane = lax.broadcasted_iota(I32, q.shape, 1)
    if has_x:
        k = jnp.concatenate([kc_ref[...], kx_ref[...]], axis=0)
        v = jnp.concatenate([vc_ref[...], vx_ref[...]], axis=0)
    else:
        k, v = kc_ref[...], vc_ref[...]

    nk = k.shape[0]
    kc_len = _tile(nk, 768)

    def attend(qm):
        m = l = acc = None
        for k0 in range(0, nk, kc_len):
            s = _dot(qm, k[k0:k0 + kc_len], NT)
            m_c = jnp.max(s, axis=-1, keepdims=True)
            if m is None:
                m = m_c
                p = jnp.exp2(s - m)
                l = jnp.sum(p, axis=-1, keepdims=True)
                acc = _dot(p.astype(BF16), v[k0:k0 + kc_len])
            else:
                m_new = jnp.maximum(m, m_c)
                alpha = jnp.exp2(m - m_new)
                p = jnp.exp2(s - m_new)
                l = alpha * l + jnp.sum(p, axis=-1, keepdims=True)
                acc = alpha * acc + _dot(p.astype(BF16), v[k0:k0 + kc_len])
                m = m_new
        return acc / l

    o = attend(jnp.where(lane < DA_QK_DIM, q, 0.0).astype(BF16)) \
        - lam_ref[...] * attend(jnp.where(lane >= DA_QK_DIM, q, 0.0).astype(BF16))
    y = o * lax.rsqrt(jnp.mean(o * o, axis=-1, keepdims=True) + EPS) * g_ref[...] * post_scale
    o_ref[...] = y.astype(o_ref.dtype)


def _diff_attn(q_arr, zc, kx_arr, vx_arr, lam, subln_g, lam_init, has_x):
    b, tq_all, _ = q_arr.shape
    c = zc.shape[1]
    tq = _tile(tq_all, 1024, 8)
    q_blk0, kc_blk0, vc_blk0 = 0, IN_WIDTHS[0] // LANE, 2 * IN_WIDTHS[0] // LANE
    kx_blk0 = kc_blk0
    in_specs = [
        pl.BlockSpec((1, 1), lambda bi, h, i: (0, 0)),
        pl.BlockSpec((1, DA_V_DIM), lambda bi, h, i: (0, 0)),
        pl.BlockSpec((None, tq, LANE), lambda bi, h, i: (bi, i, q_blk0 + h)),
        pl.BlockSpec((None, c, LANE), lambda bi, h, i: (bi, 0, kc_blk0 + h)),
        pl.BlockSpec((None, c, LANE), lambda bi, h, i: (bi, 0, vc_blk0 + h)),
    ]
    args = [lam.reshape(1, 1), subln_g.reshape(1, DA_V_DIM), q_arr, zc, zc]
    if has_x:
        n = kx_arr.shape[1]
        in_specs += [
            pl.BlockSpec((None, n, LANE), lambda bi, h, i: (bi, 0, kx_blk0 + h)),
            pl.BlockSpec((None, n, LANE), lambda bi, h, i: (bi, 0, vc_blk0 + h)),
        ]
        args += [kx_arr, vx_arr]
    return pl.pallas_call(
        functools.partial(_diff_attn_kernel, has_x=has_x, post_scale=1.0 - lam_init),
        grid=(b, DA_HEADS, tq_all // tq),
        in_specs=in_specs,
        out_specs=pl.BlockSpec((None, tq, LANE), lambda bi, h, i: (bi, i, h)),
        out_shape=jax.ShapeDtypeStruct((b, tq_all, DA_HEADS * DA_V_DIM), BF16),
        compiler_params=_cp("parallel", "parallel", "arbitrary"),
        name="diff_attn_x" if has_x else "diff_attn_c",
    )(*args)


def _group_cols(x, g):
    x = x.astype(F32)
    lane = lax.broadcasted_iota(I32, x.shape, 1)
    own = jnp.where(_lane_group(lane, WB_DIM) == g, x, pltpu.roll(x, WB_DIM, 1))
    return jnp.concatenate([own, own], axis=1).astype(BF16)


def _win_attn_kernel(*refs, has_x, n_tok):
    if has_x:
        sink_ref, q_ref, kc_ref, vc_ref, kp_ref, k0_ref, kn_ref, vp_ref, v0_ref, vn_ref, o_ref = refs
    else:
        sink_ref, q_ref, kc_ref, vc_ref, o_ref = refs
    i = pl.program_id(1)
    rep = WB_HEADS // WB_KV_HEADS
    gw = rep * WB_DIM
    q_all = q_ref[...].astype(F32) * (WB_DIM ** -0.5 * LOG2E)
    blk = q_all.shape[0]
    lane = lax.broadcasted_iota(I32, (blk, gw), 1)
    if has_x:
        k_in = jnp.concatenate([kp_ref[...], k0_ref[...], kn_ref[...], kc_ref[...]], axis=0)
        v_in = jnp.concatenate([vp_ref[...], v0_ref[...], vn_ref[...], vc_ref[...]], axis=0)
        q_pos = i * blk + (lax.broadcasted_iota(I32, (rep * blk, 3 * blk), 0) & (blk - 1))
        k_pos = (i - 1) * blk + lax.broadcasted_iota(I32, (rep * blk, 3 * blk), 1)
        valid = (jnp.abs(q_pos - k_pos) <= WB_WINDOW) & (k_pos >= 0) & (k_pos < n_tok)
    else:
        k_in, v_in = kc_ref[...], vc_ref[...]
    outs = []
    for g in range(WB_KV_HEADS):
        q = q_all[:, g * gw:(g + 1) * gw]
        k, v = _group_cols(k_in, g), _group_cols(v_in, g)
        mine = [_lane_group(lane, WB_DIM) == r for r in range(rep)]
        qs = jnp.concatenate([jnp.where(mine[r], q, 0.0) for r in range(rep)], axis=0).astype(BF16)
        sink = jnp.concatenate([jnp.broadcast_to(sink_ref[:, (g * rep + r) * WB_DIM:(g * rep + r) * WB_DIM + 1], (blk, 1))
                                for r in range(rep)], axis=0)
        s = _dot(qs, k, NT)
        if has_x:
            s = jnp.concatenate([jnp.where(valid, s[:, :3 * blk], NEG_INF), s[:, 3 * blk:]], axis=1)
        m = jnp.maximum(jnp.max(s, axis=-1, keepdims=True), sink)
        pr = jnp.exp2(s - m)
        l = jnp.sum(pr, axis=-1, keepdims=True) + jnp.exp2(sink - m)
        o = _dot(pr.astype(BF16), v) / l
        acc = jnp.zeros(q.shape, F32)
        for r in range(rep):
            acc = acc + jnp.where(mine[r], o[r * blk:(r + 1) * blk], 0.0)
        outs.append(acc)
    o_ref[...] = jnp.concatenate(outs, axis=1).astype(o_ref.dtype)


def _win_attn(q_arr, zc, kx_arr, vx_arr, sink, has_x):
    b, tq_all, _ = q_arr.shape
    c = zc.shape[1]
    blk = WB_BLOCK
    nb = tq_all // blk
    qw = WB_HEADS * WB_DIM
    sink_arr = jnp.repeat(sink.astype(F32) * LOG2E, WB_DIM).reshape(1, qw)
    k_blk, v_blk = qw // LANE, qw // LANE + 1
    in_specs = [
        pl.BlockSpec((1, qw), lambda bi, i: (0, 0)),
        pl.BlockSpec((None, blk, qw), lambda bi, i: (bi, i, 0)),
        pl.BlockSpec((None, c, LANE), lambda bi, i: (bi, 0, k_blk)),
        pl.BlockSpec((None, c, LANE), lambda bi, i: (bi, 0, v_blk)),
    ]
    args = [sink_arr, q_arr, zc, zc]
    n_tok = 0
    if has_x:
        n_tok = kx_arr.shape[1]
        prev = lambda i: jnp.maximum(i - 1, 0)
        nxt = lambda i: jnp.minimum(i + 1, nb - 1)
        for arr, cb in ((kx_arr, k_blk), (vx_arr, v_blk)):
            in_specs += [
                pl.BlockSpec((None, blk, LANE), lambda bi, i, cb=cb: (bi, prev(i), cb)),
                pl.BlockSpec((None, blk, LANE), lambda bi, i, cb=cb: (bi, i, cb)),
                pl.BlockSpec((None, blk, LANE), lambda bi, i, cb=cb: (bi, nxt(i), cb)),
            ]
            args += [arr, arr, arr]
    return pl.pallas_call(
        functools.partial(_win_attn_kernel, has_x=has_x, n_tok=n_tok),
        grid=(b, nb),
        in_specs=in_specs,
        out_specs=pl.BlockSpec((None, blk, qw), lambda bi, i: (bi, i, 0)),
        out_shape=jax.ShapeDtypeStruct((b, tq_all, qw), BF16),
        compiler_params=_cp("parallel", "arbitrary"),
        name="win_attn_x" if has_x else "win_attn_c",
    )(*args)


NA_QROWS = 4
NA_KROWS = 12


def _na_kernel(*refs, has_x, n_blocks):
    if has_x:
        q_ref, kc_ref, vc_ref, k_ref, v_ref, t2_ref, o_ref = refs
    else:
        q_ref, kc_ref, vc_ref, o_ref = refs
    tq = q_ref.shape[0]
    lane = lax.broadcasted_iota(I32, (tq, LANE), 1)
    per = LANE // NA_DIM
    if has_x:
        p = pl.program_id(1)
        is_first, is_last = p == 0, p == n_blocks - 1
        n_rows = n_blocks * NA_QROWS
        start = jnp.where(is_first, 0, jnp.where(is_last, n_rows - NA_KROWS, p * NA_QROWS - NA_ROWS // 2))
        off = pl.multiple_of(start * GRID_W, GRID_W)
        nk = NA_KROWS * GRID_W
        row_i = lax.broadcasted_iota(I32, (tq, nk), 0)
        key_i = lax.broadcasted_iota(I32, (tq, nk), 1)
        qi, col = _lane_group(row_i, GRID_W), row_i & (GRID_W - 1)
        kp, w = _lane_group(key_i, GRID_W), key_i & (GRID_W - 1)
        lo = jnp.where(is_first, 0, jnp.where(is_last, NA_KROWS - NA_ROWS, qi))
        col_start = jnp.clip(col - NA_COLS // 2, 0, GRID_W - NA_COLS)
        valid = (w >= col_start) & (w < col_start + NA_COLS) & (kp >= lo) & (kp < lo + NA_ROWS)
        n_pair = t2_ref.shape[1]
        first_pair = [jnp.where(is_first, NA_ROWS - 1 - i, jnp.where(is_last, -1 - i, NA_ROWS // 2 - 1 - i))
                      for i in range(NA_QROWS)]
    for j in range(NA_HEADS // per):
        cols = slice(j * LANE, (j + 1) * LANE)
        q = q_ref[:, cols].astype(F32) * (NA_DIM ** -0.5 * LOG2E)
        if has_x:
            k = jnp.concatenate([k_ref[pl.ds(off, nk), cols], kc_ref[:, cols]], axis=0)
            v = jnp.concatenate([v_ref[pl.ds(off, nk), cols], vc_ref[:, cols]], axis=0)
        else:
            k, v = kc_ref[:, cols], vc_ref[:, cols]
        acc = jnp.zeros(q.shape, F32)
        for hh in range(per):
            mine = _lane_group(lane, NA_DIM) == hh
            qh = jnp.where(mine, q, 0.0).astype(BF16)
            s = _dot(qh, k, NT)
            if has_x:
                head = j * per + hh
                bias = jnp.concatenate([
                    jnp.concatenate([t2_ref[head, jnp.clip(first_pair[i] + 2 * m, 0, n_pair - 1)]
                                     for m in range(NA_KROWS // 2)], axis=1)
                    for i in range(NA_QROWS)], axis=0)
                s = jnp.concatenate([jnp.where(valid, s[:, :nk] + bias, NEG_INF), s[:, nk:]], axis=1)
            m = jnp.max(s, axis=-1, keepdims=True)
            pr = jnp.exp2(s - m)
            l = jnp.sum(pr, axis=-1, keepdims=True)
            acc = acc + jnp.where(mine, _dot(pr.astype(BF16), v) / l, 0.0)
        o_ref[:, cols] = acc.astype(o_ref.dtype)


def _na_bias_table(rpb):
    r = rpb.astype(F32) * LOG2E
    edge = GRID_W - NA_COLS
    ext = jnp.concatenate([jnp.repeat(r[..., :1], edge, -1), r, jnp.repeat(r[..., -1:], edge, -1)], axis=-1)
    by_col = jnp.stack([ext[..., GRID_W - 1 - c:2 * GRID_W - 1 - c] for c in range(GRID_W)], axis=-2)
    return jnp.concatenate([by_col[:, :-1], by_col[:, 1:]], axis=-1)


def _na_attn(q_arr, zc, zx, bias_tab, layer, has_x):
    b, tq_all, _ = q_arr.shape
    c = zc.shape[1]
    hw = NA_HEADS * NA_DIM
    tq = NA_QROWS * GRID_W if has_x else _tile(tq_all, 256, 8)
    nq = tq_all // tq
    in_specs = [
        pl.BlockSpec((None, tq, hw), lambda bi, r: (bi, r, 0)),
        pl.BlockSpec((None, c, hw), lambda bi, r: (bi, 0, 1)),
        pl.BlockSpec((None, c, hw), lambda bi, r: (bi, 0, 2)),
    ]
    args = [q_arr, zc, zc]
    if has_x:
        n = zx.shape[1]
        assert tq_all % tq == 0 and nq * NA_QROWS >= NA_KROWS
        in_specs += [
            pl.BlockSpec((None, n, hw), lambda bi, r: (bi, 0, 1)),
            pl.BlockSpec((None, n, hw), lambda bi, r: (bi, 0, 2)),
            pl.BlockSpec((None,) + bias_tab.shape[1:], lambda bi, r: (layer, 0, 0, 0, 0)),
        ]
        args += [zx, zx, bias_tab]
    return pl.pallas_call(
        functools.partial(_na_kernel, has_x=has_x, n_blocks=nq),
        grid=(b, nq),
        in_specs=in_specs,
        out_specs=pl.BlockSpec((None, tq, hw), lambda bi, r: (bi, r, 0)),
        out_shape=jax.ShapeDtypeStruct((b, tq_all, hw), BF16),
        compiler_params=_cp("parallel", "arbitrary"),
        name="na_attn_x" if has_x else "na_attn_c",
    )(*args)


def _s5_kernel(*refs, ctx_out):
    if ctx_out:
        (uc_ref, ux_ref, d_ref, b1_ref, b2_ref, a1_ref, a2_ref, a3_ref, cp_ref, scat_ref, yc_ref, yx_ref,
         kbig, bb1, bb2, cb_ref, xc, xx, s1c, s2c, s1x, s2x, hc, hx) = refs
    else:
        (uc_ref, ux_ref, d_ref, b1_ref, b2_ref, a1_ref, a2_ref, a3_ref, cp_ref, scat_ref, yx_ref,
         kbig, bb1, bb2, cb_ref, xc, xx, s1c, s2c, s1x, s2x, hc, hx) = refs
        yc_ref = None
    dirn = pl.program_id(1)
    ell, hch = S5_CHUNK, S5_GROUP_CH
    gq = LANE // hch
    nb = ux_ref.shape[0]
    streams = ((uc_ref, xc, s1c, s2c, hc, yc_ref, uc_ref.shape[1] // ell),
               (ux_ref, xx, s1x, s2x, hx, yx_ref, ux_ref.shape[1] // ell))

    for i in range(ell):
        for j in range(ell):
            kbig[i * LANE:(i + 1) * LANE, j * LANE:(j + 1) * LANE] = d_ref[j - i + ell - 1]
    bb1[...] = jnp.zeros(bb1.shape, BF16)
    bb2[...] = jnp.zeros(bb2.shape, BF16)
    for i in range(ell):
        for gl in range(gq):
            r0 = i * LANE + gl * hch
            bb1[r0:r0 + hch, gl * LANE:(gl + 1) * LANE] = b1_ref[gl, i * hch:(i + 1) * hch, :].astype(BF16)
            bb2[r0:r0 + hch, gl * LANE:(gl + 1) * LANE] = b2_ref[gl, i * hch:(i + 1) * hch, :].astype(BF16)
    for gl in range(gq):
        cb_ref[gl * LANE:(gl + 1) * LANE, :] = _dot(cp_ref[gl], scat_ref[gl]).astype(BF16)

    @pl.when(dirn == 0)
    def _():
        for u_ref, x_s, _, _, _, _, nc in streams:
            for b in range(nb):
                for i in range(ell):
                    x_s[b * nc:(b + 1) * nc, i * LANE:(i + 1) * LANE] = u_ref[b, pl.ds(i, nc, stride=ell), :].astype(BF16)

    cpt = SUBLANE // nb
    for _, x_s, s1, s2, _, _, nc in streams:
        loc1 = _dot(x_s[...], bb1[...])
        loc2 = _dot(x_s[...], bb2[...])
        for k in range(gq):
            for b in range(nb):
                s1[k, pl.ds(b, nc, stride=nb), :] = loc1[b * nc:(b + 1) * nc, k * LANE:(k + 1) * LANE]
                s2[k, pl.ds(b, nc, stride=nb), :] = loc2[b * nc:(b + 1) * nc, k * LANE:(k + 1) * LANE]

    def lane_blocks(a_ref):
        return [jnp.broadcast_to(a_ref[:, k * LANE:(k + 1) * LANE], (SUBLANE, LANE)) for k in range(gq)]

    a1, a2, a3 = lane_blocks(a1_ref), lane_blocks(a2_ref), lane_blocks(a3_ref)
    fwd = dirn == 0
    row_grp = _lane_group(lax.broadcasted_iota(I32, (SUBLANE, LANE), 0), nb)

    def advance(v):
        return jnp.where(fwd, pltpu.roll(v, nb, 0), pltpu.roll(v, SUBLANE - nb, 0))

    zero = tuple(jnp.zeros((SUBLANE, LANE), F32) for _ in range(gq))
    carry = (zero, zero)
    for _, _, s1, s2, h, _, nc in streams:
        n_tiles = nc // cpt

        def step(t, vs, s1=s1, s2=s2, h=h, n_tiles=n_tiles):
            v1, v2 = list(vs[0]), list(vs[1])
            j = jnp.where(fwd, t, n_tiles - 1 - t)
            rows = pl.ds(pl.multiple_of(j * SUBLANE, SUBLANE), SUBLANE)
            for k in range(gq):
                loc1, loc2 = s1[k, rows, :], s2[k, rows, :]
                entered = v1[k]
                for u in range(cpt):
                    grp = jnp.where(fwd, u, cpt - 1 - u)
                    entered = jnp.where(row_grp == grp, v1[k], entered)
                    n1 = a1[k] * v1[k] + a2[k] * v2[k] + loc1
                    n2 = a1[k] * v2[k] + a3[k] * v1[k] + loc2
                    v1[k], v2[k] = advance(n1), advance(n2)
                h[k, rows, :] = entered
            return tuple(v1), tuple(v2)

        carry = lax.fori_loop(0, n_tiles, step, carry, unroll=2)

    for _, x_s, _, _, h, y_ref, nc in streams:
        if y_ref is None:
            continue
        h_all = jnp.concatenate(
            [jnp.concatenate([h[k, pl.ds(b, nc, stride=nb), :] for b in range(nb)], axis=0) for k in range(gq)],
            axis=1).astype(BF16)
        y = _dot(x_s[...], kbig[...]) + _dot(h_all, cb_ref[...])

        @pl.when(dirn == 0)
        def _():
            for b in range(nb):
                for j in range(ell):
                    y_ref[b, pl.ds(j, nc, stride=ell), :] = y[b * nc:(b + 1) * nc, j * LANE:(j + 1) * LANE]

        @pl.when(dirn == 1)
        def _():
            for b in range(nb):
                for j in range(ell):
                    y_ref[b, pl.ds(j, nc, stride=ell), :] += y[b * nc:(b + 1) * nc, j * LANE:(j + 1) * LANE]


def _s5_tables(a_re, a_im, log_step, b_re, b_im, c_re, c_im):
    ell, hch, p = S5_CHUNK, S5_GROUP_CH, S5_STATE
    lam = lax.complex(a_re.astype(F32), a_im.astype(F32))
    lam_dt = lam * jnp.exp(log_step.astype(F32))[:, :, None]
    a_bar = jnp.exp(lam_dt)
    b_bar = ((a_bar - 1.0) / lam)[:, :, :, None] * lax.complex(b_re.astype(F32), b_im.astype(F32))
    c_mat = lax.complex(c_re.astype(F32), c_im.astype(F32))
    gq = LANE // hch
    nq = S5_GROUPS // gq
    hp = lax.Precision.HIGHEST
    steps = jnp.arange(ell + 1, dtype=F32)
    apow = jnp.exp(lam_dt[:, :, None, :] * steps[None, None, :, None])
    apow_re, apow_im = apow.real, apow.imag
    bt_re, bt_im = jnp.swapaxes(b_bar.real, 2, 3), jnp.swapaxes(b_bar.imag, 2, 3)
    c_re_, c_im_ = c_mat.real, c_mat.imag

    def cmul(ar, ai, br, bi):
        return ar * br - ai * bi, ar * bi + ai * br

    m_re, m_im = cmul(apow_re[:, :, :ell, None, :], apow_im[:, :, :ell, None, :], bt_re[:, :, None], bt_im[:, :, None])
    kk = (jnp.einsum('dgkip,dgop->dgkio', m_re, c_re_, precision=hp)
          - jnp.einsum('dgkip,dgop->dgkio', m_im, c_im_, precision=hp))
    zeros = jnp.zeros_like(kk[0, :, 1:])
    signed = jnp.stack([jnp.concatenate([zeros, kk[0]], axis=1),
                        jnp.concatenate([jnp.flip(kk[1], axis=1), zeros], axis=1)])
    signed = signed.reshape(2, nq, gq, 2 * ell - 1, hch, hch).astype(BF16)
    dblk = jnp.concatenate(
        [jnp.pad(signed[:, :, g], ((0, 0), (0, 0), (0, 0), (0, 0), (g * hch, LANE - (g + 1) * hch))) for g in range(gq)],
        axis=3)
    i_idx = jnp.arange(ell)
    e_re = jnp.stack([apow_re[0][:, ell - 1 - i_idx], apow_re[1][:, i_idx]])
    e_im = jnp.stack([apow_im[0][:, ell - 1 - i_idx], apow_im[1][:, i_idx]])
    bp_re, bp_im = cmul(e_re[:, :, :, None, :], e_im[:, :, :, None, :], bt_re[:, :, None], bt_im[:, :, None])
    bp_re, bp_im = bp_re.reshape(2, S5_GROUPS, ell * hch, p), bp_im.reshape(2, S5_GROUPS, ell * hch, p)
    b1 = jnp.concatenate([bp_re, bp_im], axis=-1)
    b2 = jnp.concatenate([bp_im, bp_re], axis=-1)
    al_re, al_im = apow_re[:, :, ell, :], apow_im[:, :, ell, :]
    lanes = lambda u, v: jnp.concatenate([u, v], axis=-1).reshape(2, nq, 1, gq * 2 * p)
    a1, a2, a3 = lanes(al_re, al_re), lanes(-al_im, al_im), lanes(al_im, -al_im)
    f_re = jnp.stack([apow_re[0][:, 1 + i_idx], apow_re[1][:, ell - i_idx]])
    f_im = jnp.stack([apow_im[0][:, 1 + i_idx], apow_im[1][:, ell - i_idx]])
    g_re, g_im = cmul(c_re_[:, :, None], c_im_[:, :, None], f_re[:, :, :, None, :], f_im[:, :, :, None, :])
    cp = jnp.concatenate([jnp.transpose(g_re, (0, 1, 4, 2, 3)), -jnp.transpose(g_im, (0, 1, 4, 2, 3))], axis=2)
    cp = cp.reshape(2, S5_GROUPS, 2 * p, ell * hch).astype(BF16)
    return dblk, b1, b2, a1, a2, a3, cp


def _s5_out_scatter():
    ell, hch = S5_CHUNK, S5_GROUP_CH
    gq = LANE // hch
    src = jnp.arange(ell * hch)
    dst = (src // hch)[None, :] * LANE + jnp.arange(gq)[:, None] * hch + (src % hch)[None, :]
    return (dst[:, :, None] == jnp.arange(ell * LANE)[None, None, :]).astype(BF16)


def _s5_scan(zx, zc, tables, layer, ctx_out):
    b, n, _ = zx.shape
    c = zc.shape[1]
    ell, hch = S5_CHUNK, S5_GROUP_CH
    gq = LANE // hch
    nq = S5_GROUPS // gq
    p2 = 2 * S5_STATE
    ncx, ncc = n // ell, c // ell
    assert n % ell == 0 and c % ell == 0 and ncc % 16 == 0
    assert SUBLANE % b == 0 and b & (b - 1) == 0
    dblk, b1, b2, a1, a2, a3, cp = tables
    u0 = 0
    sw = gq * p2
    xw = ell * LANE
    a_spec = pl.BlockSpec((None, None, None, 1, sw), lambda q, d: (layer, d, q, 0, 0))
    b_spec = pl.BlockSpec((None, None, gq, ell * hch, p2), lambda q, d: (layer, d, q, 0, 0))
    y_specs = [pl.BlockSpec((b, n, LANE), lambda q, d: (0, 0, q))]
    y_shapes = [jax.ShapeDtypeStruct((b, n, S5_WIDTH), F32)]
    if ctx_out:
        y_specs = [pl.BlockSpec((b, c, LANE), lambda q, d: (0, 0, q))] + y_specs
        y_shapes = [jax.ShapeDtypeStruct((b, c, S5_WIDTH), F32)] + y_shapes
    out = pl.pallas_call(
        functools.partial(_s5_kernel, ctx_out=ctx_out),
        grid=(nq, 2),
        in_specs=[
            pl.BlockSpec((b, c, LANE), lambda q, d: (0, 0, u0 + q)),
            pl.BlockSpec((b, n, LANE), lambda q, d: (0, 0, u0 + q)),
            pl.BlockSpec((None, None, None, 2 * ell - 1, LANE, LANE), lambda q, d: (layer, d, q, 0, 0, 0)),
            b_spec, b_spec, a_spec, a_spec, a_spec,
            pl.BlockSpec((None, None, gq, p2, ell * hch), lambda q, d: (layer, d, q, 0, 0)),
            pl.BlockSpec((gq, ell * hch, xw), lambda q, d: (0, 0, 0)),
        ],
        out_specs=y_specs,
        out_shape=y_shapes,
        scratch_shapes=[
            pltpu.VMEM((xw, xw), BF16), pltpu.VMEM((xw, sw), BF16), pltpu.VMEM((xw, sw), BF16),
            pltpu.VMEM((sw, xw), BF16),
            pltpu.VMEM((b * ncc, xw), BF16), pltpu.VMEM((b * ncx, xw), BF16),
            pltpu.VMEM((gq, b * ncc, p2), F32), pltpu.VMEM((gq, b * ncc, p2), F32),
            pltpu.VMEM((gq, b * ncx, p2), F32), pltpu.VMEM((gq, b * ncx, p2), F32),
            pltpu.VMEM((gq, b * ncc, p2), F32), pltpu.VMEM((gq, b * ncx, p2), F32),
        ],
        compiler_params=_cp("parallel", "arbitrary"),
        name="s5_scan",
    )(zc, zx, dblk, b1, b2, a1, a2, a3, cp, _s5_out_scatter())
    return (out[0], out[1]) if ctx_out else (None, out[0])


def _s5_out_kernel(y_ref, u_ref, d_ref, w_ref, o_ref):
    g = jax.nn.gelu(y_ref[...] + d_ref[...] * u_ref[...])
    o_ref[...] = (g * jax.nn.sigmoid(_dot(g.astype(BF16), w_ref[...].astype(BF16)))).astype(o_ref.dtype)


def _s5_out(y, u, d_skip, w_glu):
    b, t, w = y.shape
    tr = _tile(t, 512, 8)
    row = pl.BlockSpec((None, tr, w), lambda bi, i: (bi, i, 0))
    return pl.pallas_call(
        _s5_out_kernel,
        grid=(b, t // tr),
        in_specs=[row, row, pl.BlockSpec((1, w), lambda bi, i: (0, 0)), pl.BlockSpec((w, w), lambda bi, i: (0, 0))],
        out_specs=row,
        out_shape=jax.ShapeDtypeStruct((b, t, w), BF16),
        compiler_params=_cp("parallel", "parallel"),
        name="s5_out",
    )(y, u, d_skip.reshape(1, w).astype(F32), w_glu)


def _merge_kernel(h_ref, ya_ref, yb_ref, yc_ref, yd_ref, wg0_ref, wg1_ref, wg2_ref, wg3_ref, wb_ref, o_ref, wgs_ref, wbs_ref):
    @pl.when(pl.program_id(1) == 0)
    def _():
        for n, wg_ref in enumerate((wg0_ref, wg1_ref, wg2_ref, wg3_ref)):
            wgs_ref[n] = wg_ref[...].astype(BF16)
        wbs_ref[...] = wb_ref[...].astype(BF16)

    h = h_ref[...]
    acc = None
    for n, y_ref in enumerate((ya_ref, yb_ref, yc_ref, yd_ref)):
        term = jax.nn.sigmoid(_dot(h, wgs_ref[n])) * _dot(y_ref[...], wbs_ref[n])
        acc = term if acc is None else acc + term
    o_ref[...] = acc.astype(o_ref.dtype)


def _merge(h, branches, w_in, w_branch, layer):
    m, d = h.shape
    bw = branches[0].shape[1]
    tn = 256
    tm = _tile(m, 1024, 8)
    g0 = N_MIX_IN // tn
    per = d // tn
    assert N_MIX_IN % tn == 0 and d % tn == 0
    wg_specs = [pl.BlockSpec((None, d, tn), lambda j, i, n=n: (layer, 0, g0 + n * per + j)) for n in range(N_BRANCH)]
    y_spec = pl.BlockSpec((tm, bw), lambda j, i: (i, 0))
    return pl.pallas_call(
        _merge_kernel,
        grid=(d // tn, m // tm),
        in_specs=[pl.BlockSpec((tm, d), lambda j, i: (i, 0)), y_spec, y_spec, y_spec, y_spec] + wg_specs
                 + [pl.BlockSpec((None, N_BRANCH, bw, tn), lambda j, i: (layer, 0, 0, j))],
        out_specs=pl.BlockSpec((tm, tn), lambda j, i: (i, j)),
        out_shape=jax.ShapeDtypeStruct((m, d), BF16),
        scratch_shapes=[pltpu.VMEM((N_BRANCH, d, tn), BF16), pltpu.VMEM((N_BRANCH, bw, tn), BF16)],
        compiler_params=_cp("arbitrary", "arbitrary"),
        name="merge",
    )(h, *branches, w_in, w_in, w_in, w_in, w_branch)


def _topk_kernel(lg_ref, slot_ref, idx_ref, gate_ref, tri_ref, *, cap):
    e, t = lg_ref.shape

    @pl.when(pl.program_id(0) == 0)
    def _():
        rows = 256 if t % 256 == 0 else t
        for r0 in range(0, t, rows):
            ri = lax.broadcasted_iota(I32, (rows, t), 0) + r0
            ci = lax.broadcasted_iota(I32, (rows, t), 1)
            tri_ref[r0:r0 + rows, :] = jnp.where(ri < ci, 1.0, 0.0).astype(BF16)

    lg = lg_ref[...]
    ex = jnp.exp(lg - jnp.max(lg, axis=0, keepdims=True))
    aff = ex / jnp.sum(ex, axis=0, keepdims=True)
    bits = pltpu.bitcast(aff, I32)

    def search(_, carry):
        lo, hi = carry
        mid = lo + lax.shift_right_logical(hi - lo + 1, 1)
        ok = _count(bits >= mid, 1) >= cap
        return jnp.where(ok, mid, lo), jnp.where(ok, hi, mid - 1)

    lo0 = jnp.zeros((e, 1), I32)
    hi0 = jnp.full((e, 1), 0x7F800000, I32)
    thr, _ = lax.fori_loop(0, 32, search, (lo0, hi0))
    gt = bits > thr
    eq = bits == thr
    need = cap - _count(gt, 1)
    tri = tri_ref[...]
    eq_before = _dot(jnp.where(eq, 1.0, 0.0).astype(BF16), tri)
    sel = gt | (eq & (eq_before < need))
    sel_before = _dot(jnp.where(sel, 1.0, 0.0).astype(BF16), tri)
    slot = jnp.where(sel, sel_before.astype(I32), -1)
    slot_ref[...] = slot

    tok = lax.broadcasted_iota(I32, (SUBLANE, t), 1)
    row = lax.broadcasted_iota(I32, (SUBLANE, t), 0)
    tok_hi, tok_lo = lax.shift_right_logical(tok, 6).astype(F32), (tok & 63).astype(F32)
    slot_iota = lax.broadcasted_iota(I32, (cap, t), 0)
    for ei in range(e):
        a = jnp.broadcast_to(aff[ei:ei + 1, :], (SUBLANE, t))
        a_hi = a.astype(BF16).astype(F32)
        a_mid = (a - a_hi).astype(BF16).astype(F32)
        a_lo = a - a_hi - a_mid
        feats = jnp.where(row == 0, tok_hi, jnp.where(row == 1, tok_lo, jnp.where(
            row == 2, a_hi, jnp.where(row == 3, a_mid, jnp.where(row == 4, a_lo, 0.0))))).astype(BF16)
        pick = jnp.where(slot_iota == slot[ei:ei + 1, :], 1.0, 0.0).astype(BF16)
        res = _dot(feats, pick, NT)
        idx_ref[ei:ei + 1, :] = (res[0:1] * 64.0 + res[1:2]).astype(I32)
        gate_ref[ei:ei + 1, :] = res[2:3] + res[3:4] + res[4:5]


def _topk(logits_t, b, cap):
    e, bt = logits_t.shape
    t = bt // b
    per_b = lambda w: pl.BlockSpec((None, e, w), lambda bi: (bi, 0, 0))
    return pl.pallas_call(
        functools.partial(_topk_kernel, cap=cap),
        grid=(b,),
        in_specs=[pl.BlockSpec((e, t), lambda bi: (0, bi))],
        out_specs=[per_b(t), per_b(cap), per_b(cap)],
        out_shape=[jax.ShapeDtypeStruct((b, e, t), I32), jax.ShapeDtypeStruct((b, e, cap), I32),
                   jax.ShapeDtypeStruct((b, e, cap), F32)],
        scratch_shapes=[pltpu.VMEM((t, t), BF16)],
        compiler_params=_cp("arbitrary"),
        name="route_topk",
    )(logits_t)


def _ffn_up_kernel(*refs, n_streams, n_f):
    ns = n_streams
    idx_refs, h_refs = refs[:ns], refs[ns:2 * ns]
    w1_ref, w3_ref = refs[2 * ns:2 * ns + 2]
    outs = refs[2 * ns + 2:3 * ns + 2]
    scratch = refs[3 * ns + 2:]
    gbufs, xss, sems = scratch[:ns], scratch[ns:2 * ns], scratch[2 * ns:3 * ns]
    e, f = pl.program_id(0), pl.program_id(1)
    n_e = pl.num_programs(0)
    slot = lax.rem(e, 2)

    for idx_ref, h_ref, gbuf, xs, sem in zip(idx_refs, h_refs, gbufs, xss, sems):
        rows = gbuf.shape[1]

        @pl.when((e == 0) & (f == 0))
        def _():
            def start(r, carry):
                pltpu.make_async_copy(h_ref.at[idx_ref[0, r]], gbuf.at[0, r], sem.at[0]).start()
                return carry
            lax.fori_loop(0, rows, start, 0)

        @pl.when(f == 0)
        def _():
            pltpu.make_async_copy(gbuf.at[slot], gbuf.at[slot], sem.at[slot]).wait()
            xs[...] = _unpack_halves(gbuf[slot])

    nxt = jnp.minimum(e + 1, n_e - 1)
    w1, w3 = w1_ref[...].astype(BF16), w3_ref[...].astype(BF16)
    for idx_ref, h_ref, gbuf, xs, sem, o_ref in zip(idx_refs, h_refs, gbufs, xss, sems, outs):
        x = xs[...]
        a = _dot(x, w1)
        g = _dot(x, w3)
        part = gbuf.shape[1] // n_f
        for i in range(part):
            r = f * part + i
            pltpu.make_async_copy(h_ref.at[idx_ref[nxt, r]], gbuf.at[1 - slot, r], sem.at[1 - slot]).start()
        o_ref[...] = ((a / (1.0 + jnp.exp(-a))) * g).astype(o_ref.dtype)

    @pl.when((e == n_e - 1) & (f == n_f - 1))
    def _():
        for gbuf, sem in zip(gbufs, sems):
            pltpu.make_async_copy(gbuf.at[1 - slot], gbuf.at[1 - slot], sem.at[1 - slot]).wait()


def _ffn_down_kernel(*refs, n_streams):
    mids = refs[:n_streams]
    gates = refs[n_streams:2 * n_streams]
    w2_ref = refs[2 * n_streams]
    outs = refs[2 * n_streams + 1:]
    w2 = w2_ref[...].astype(BF16)
    for m_ref, gate_ref, o_ref in zip(mids, gates, outs):
        o_ref[...] = (_dot(m_ref[...], w2) * gate_ref[...]).astype(o_ref.dtype)


def _ffn(idx_list, h_list, gate_list, w1, w3, w2, layer):
    ns = len(idx_list)
    e = idx_list[0].shape[0]
    d = 2 * h_list[0].shape[1]
    ff = w1.shape[-1]
    tf = _tile(ff, 512)
    tn = _tile(d, 1024)
    n_f = ff // tf
    rows = [idx.shape[1] for idx in idx_list]
    assert all(r % n_f == 0 for r in rows)
    hbm = pl.BlockSpec(memory_space=pl.ANY)
    mids = pl.pallas_call(
        functools.partial(_ffn_up_kernel, n_streams=ns, n_f=n_f),
        grid_spec=pltpu.PrefetchScalarGridSpec(
            num_scalar_prefetch=ns,
            grid=(e, n_f),
            in_specs=[hbm] * ns + [
                pl.BlockSpec((None, None, d, tf), lambda ei, f, *_: (layer, ei, 0, f)),
                pl.BlockSpec((None, None, d, tf), lambda ei, f, *_: (layer, ei, 0, f)),
            ],
            out_specs=[pl.BlockSpec((None, r, tf), lambda ei, f, *_: (ei, 0, f)) for r in rows],
            scratch_shapes=[pltpu.VMEM((2, r, d // 2), jnp.uint32) for r in rows]
                           + [pltpu.VMEM((r, d), BF16) for r in rows]
                           + [pltpu.SemaphoreType.DMA((2,)) for _ in rows],
        ),
        out_shape=[jax.ShapeDtypeStruct((e, r, ff), BF16) for r in rows],
        compiler_params=_cp("arbitrary", "arbitrary"),
        name="moe_ffn_up",
    )(*idx_list, *h_list, w1, w3)
    return pl.pallas_call(
        functools.partial(_ffn_down_kernel, n_streams=ns),
        grid=(e, d // tn),
        in_specs=[pl.BlockSpec((None, r, ff), lambda ei, j: (ei, 0, 0)) for r in rows]
                 + [pl.BlockSpec((None, r, 1), lambda ei, j: (ei, 0, 0)) for r in rows]
                 + [pl.BlockSpec((None, None, ff, tn), lambda ei, j: (layer, ei, 0, j))],
        out_specs=[pl.BlockSpec((None, r, tn), lambda ei, j: (ei, 0, j)) for r in rows],
        out_shape=[jax.ShapeDtypeStruct((e, r, d), BF16) for r in rows],
        compiler_params=_cp("parallel", "arbitrary"),
        name="moe_ffn_down",
    )(*mids, *gate_list, w2)


def _combine_kernel(slot_ref, y_ref, x_ref, gate_ref, o_ref, pt_ref, *, cap):
    e = slot_ref.shape[1]
    tm = slot_ref.shape[0]
    r0 = pl.multiple_of(pl.program_id(2) * tm, tm)

    @pl.when(pl.program_id(1) == 0)
    def _():
        slot = slot_ref[...]
        lane = lax.broadcasted_iota(I32, (tm, cap), 1)
        for ei in range(e):
            pt_ref[pl.ds(r0, tm), ei * cap:(ei + 1) * cap] = jnp.where(lane == slot[:, ei:ei + 1], 1.0, 0.0).astype(BF16)

    y = y_ref[...]
    acc = _dot(pt_ref[pl.ds(r0, tm), :], y.reshape(e * cap, y.shape[2]))
    o_ref[...] = x_ref[...] + gate_ref[...] * acc


def _combine(slot_te, ys, x, gate, cap):
    b, t, d = x.shape
    e = slot_te.shape[2]
    tm = _tile(t, 512, 8)
    tn = _tile(d, 512)
    return pl.pallas_call(
        functools.partial(_combine_kernel, cap=cap),
        grid=(b, d // tn, t // tm),
        in_specs=[
            pl.BlockSpec((None, tm, e), lambda bi, j, i: (bi, i, 0)),
            pl.BlockSpec((e, cap, tn), lambda bi, j, i: (0, bi, j)),
            pl.BlockSpec((None, tm, tn), lambda bi, j, i: (bi, i, j)),
            pl.BlockSpec((None, 1, tn), lambda bi, j, i: (bi, 0, j)),
        ],
        out_specs=pl.BlockSpec((None, tm, tn), lambda bi, j, i: (bi, i, j)),
        out_shape=jax.ShapeDtypeStruct((b, t, d), F32),
        scratch_shapes=[pltpu.VMEM((t, e * cap), BF16)],
        compiler_params=_cp("parallel", "arbitrary", "arbitrary"),
        name="moe_combine",
    )(slot_te, ys, x, gate)


def kernel(x, c, ctx, c_ctx, ada_w, ada_b, norm1_g, norm2_g, w_in, da_lambda, da_subln_g, wb_sink, na_rpb, s5_a_re, s5_a_im, s5_log_step, s5_b_re, s5_b_im, s5_c_re, s5_c_im, s5_d, s5_glu_w, w_branch, w_out, w_router, w_e1, w_e3, w_e2, final_g):
    b, n, d = x.shape
    n_ctx = ctx.shape[1]
    depth = ada_w.shape[0]
    assert b + 1 <= ADA_ROWS and n % GRID_W == 0

    cs = jnp.zeros((ADA_ROWS, d), F32).at[:b].set(c).at[b].set(c_ctx)
    mods = _ada_mod(cs, ada_w, ada_b)
    rope_tab = _rope_tables(n)
    bias_tab = jax.vmap(_na_bias_table)(na_rpb)
    s5_tab = jax.vmap(_s5_tables)(s5_a_re, s5_a_im, s5_log_step, s5_b_re, s5_b_im, s5_c_re, s5_c_im)

    for l in range(depth):
        with_ctx = l < depth - 1
        mod_x = [mods[l, :b, k * d:(k + 1) * d].reshape(b, 1, d) for k in range(6)]
        mod_c = [jnp.broadcast_to(mods[l, b, k * d:(k + 1) * d], (b, 1, d)) for k in range(6)]

        hx = _norm_mod(x, norm1_g[l], mod_x[0], mod_x[1])
        hc = _norm_mod(ctx, norm1_g[l], mod_c[0], mod_c[1])
        proj = []
        for h, t, positions in ((hx, n, True), (hc, n_ctx, False)):
            h2d = h.reshape(b * t, d)
            proj.append([
                _mm(h2d, w_in, l, col0, ncols, tn, dt,
                    rope=(rope_tab, n_rope, t) if positions and n_rope else None).reshape(b, t, ncols)
                for col0, ncols, tn, dt, n_rope in MIX_PROJ])
        (za_x, zb_x, zn_x, zs_x), (za_c, zb_c, zn_c, zs_c) = proj

        lam_init = 0.8 - 0.6 * math.exp(-0.3 * l)
        lv = da_lambda[l].astype(F32)
        lam = jnp.exp(jnp.sum(lv[0] * lv[1])) - jnp.exp(jnp.sum(lv[2] * lv[3])) + lam_init

        ya_x = _diff_attn(za_x, za_c, za_x, za_x, lam, da_subln_g[l], lam_init, True)
        yb_x = _win_attn(zb_x, zb_c, zb_x, zb_x, wb_sink[l], True)
        yc_x = _na_attn(zn_x, zn_c, zn_x, bias_tab, l, True)
        ys_c, ys_x = _s5_scan(zs_x, zs_c, s5_tab, l, with_ctx)
        yd_x = _s5_out(ys_x, zs_x, s5_d[l], s5_glu_w[l])
        gx = _merge(hx.reshape(b * n, d), [t.reshape(b * n, -1) for t in (ya_x, yb_x, yc_x, yd_x)], w_in, w_branch, l)
        x = _mm_res(gx, w_out, l, x, mod_x[2])

        if with_ctx:
            ya_c = _diff_attn(za_c, za_c, None, None, lam, da_subln_g[l], lam_init, False)
            yb_c = _win_attn(zb_c, zb_c, None, None, wb_sink[l], False)
            yc_c = _na_attn(zn_c, zn_c, None, None, l, False)
            yd_c = _s5_out(ys_c, zs_c, s5_d[l], s5_glu_w[l])
            gc = _merge(hc.reshape(b * n_ctx, d), [t.reshape(b * n_ctx, -1) for t in (ya_c, yb_c, yc_c, yd_c)],
                        w_in, w_branch, l)
            ctx = _mm_res(gc, w_out, l, ctx, mod_c[2])

        w_router_t = jnp.transpose(w_router[l]).astype(F32)
        streams = [(x, mod_x)] + ([(ctx, mod_c)] if with_ctx else [])
        routed = []
        for s, mod in streams:
            t = s.shape[1]
            cap = EC_CAPACITY_FACTOR * t // N_EXPERTS
            h2p, logits_t = _norm_router(s, norm2_g[l], mod[3], mod[4], w_router_t)
            slot, idx, gate = _topk(logits_t, b, cap)
            by_expert = lambda v: jnp.transpose(v, (1, 0, 2)).reshape(N_EXPERTS, b * cap)
            flat_row = idx + (jnp.arange(b, dtype=I32) * t)[:, None, None]
            routed.append((jnp.transpose(slot, (0, 2, 1)), by_expert(flat_row), h2p.reshape(b * t, d // 2),
                           by_expert(gate)[:, :, None], cap))
        ys = _ffn([r[1] for r in routed], [r[2] for r in routed], [r[3] for r in routed], w_e1, w_e3, w_e2, l)
        x = _combine(routed[0][0], ys[0], x, mod_x[5], routed[0][4])
        if with_ctx:
            ctx = _combine(routed[1][0], ys[1], ctx, mod_c[5], routed[1][4])

    return _final_norm(x, final_g)
```

```python
import functools
import math

import jax
import jax.numpy as jnp
from jax import lax
from jax.experimental import pallas as pl
from jax.experimental.pallas import tpu as pltpu

F32 = jnp.float32
BF16 = jnp.bfloat16
I32 = jnp.int32

GRID_W = 64
EPS = 1e-6
NEG_INF = -1e30
LOG2E = math.log2(math.e)
ROPE_BASE = 10000.0

DA_HEADS = 4
DA_QK_DIM = 64
DA_V_DIM = 2 * DA_QK_DIM
WB_HEADS = 8
WB_KV_HEADS = 2
WB_DIM = 64
WB_WINDOW = 128
WB_BLOCK = 128
NA_HEADS = 8
NA_DIM = 64
NA_ROWS = 8
NA_COLS = 16
S5_GROUPS = 32
S5_GROUP_CH = 16
S5_STATE = 64
S5_WIDTH = S5_GROUPS * S5_GROUP_CH
S5_CHUNK = 8
N_BRANCH = 4
BRANCH_WIDTH = 512
N_EXPERTS = 16
EC_CAPACITY_FACTOR = 2

IN_WIDTHS = (
    2 * DA_HEADS * DA_QK_DIM, 2 * DA_HEADS * DA_QK_DIM, DA_HEADS * DA_V_DIM,
    WB_HEADS * WB_DIM, WB_KV_HEADS * WB_DIM, WB_KV_HEADS * WB_DIM,
    NA_HEADS * NA_DIM, NA_HEADS * NA_DIM, NA_HEADS * NA_DIM,
    S5_WIDTH,
)
N_MIX_IN = sum(IN_WIDTHS)
_OFFS = [0]
for _w in IN_WIDTHS:
    _OFFS.append(_OFFS[-1] + _w)
(OFF_DA_Q, OFF_DA_K, OFF_DA_V, OFF_WB_Q, OFF_WB_K, OFF_WB_V, OFF_NA_Q, OFF_NA_K, OFF_NA_V, OFF_S5, _) = _OFFS

MIX_PROJ = (
    (OFF_DA_Q, OFF_WB_Q - OFF_DA_Q, 768, BF16, IN_WIDTHS[0] + IN_WIDTHS[1]),
    (OFF_WB_Q, OFF_NA_Q - OFF_WB_Q, 768, BF16, IN_WIDTHS[3] + IN_WIDTHS[4]),
    (OFF_NA_Q, OFF_S5 - OFF_NA_Q, 768, BF16, 0),
    (OFF_S5, S5_WIDTH, 256, F32, 0),
)

LANE = 128
SUBLANE = 8
ADA_ROWS = 8
VMEM_LIMIT = 56 * 1024 * 1024

NN = (((1,), (0,)), ((), ()))
NT = (((1,), (1,)), ((), ()))


def _cp(*sem):
    return pltpu.CompilerParams(dimension_semantics=sem, vmem_limit_bytes=VMEM_LIMIT)


def _tile(n, pref, mult=LANE):
    if n <= pref:
        return n
    t = (pref // mult) * mult
    while t >= mult:
        if n % t == 0:
            return t
        t -= mult
    return n


def _split(a):
    hi = a.astype(BF16)
    lo = (a - hi.astype(F32)).astype(BF16)
    return hi, lo


def _dot(a, b, dims=NN):
    return lax.dot_general(a, b, dims, preferred_element_type=F32)


def _lane_group(lane, width):
    return lax.shift_right_logical(lane, int(math.log2(width)))


def _count(mask, axis):
    return jnp.sum(jnp.where(mask, 1.0, 0.0), axis=axis, keepdims=True)


def _dot3(a, b, dims=NN):
    ah, al = _split(a)
    bh, bl = _split(b)
    return _dot(ah, bh, dims) + (_dot(ah, bl, dims) + _dot(al, bh, dims))


def _ada_kernel(c_ref, w_ref, b_ref, o_ref):
    c = c_ref[...]
    s = c / (1.0 + jnp.exp(-c))
    o_ref[...] = _dot3(s, w_ref[...]) + b_ref[...]


def _ada_mod(cs, ada_w, ada_b):
    depth, d, n6 = ada_w.shape
    tn = _tile(n6, 512)
    return pl.pallas_call(
        _ada_kernel,
        grid=(depth, n6 // tn),
        in_specs=[
            pl.BlockSpec((ADA_ROWS, d), lambda l, j: (0, 0)),
            pl.BlockSpec((None, d, tn), lambda l, j: (l, 0, j)),
            pl.BlockSpec((None, 1, tn), lambda l, j: (l, 0, j)),
        ],
        out_specs=pl.BlockSpec((None, ADA_ROWS, tn), lambda l, j: (l, 0, j)),
        out_shape=jax.ShapeDtypeStruct((depth, ADA_ROWS, n6), F32),
        compiler_params=_cp("arbitrary", "arbitrary"),
        name="ada_mod",
    )(cs, ada_w, ada_b.reshape(depth, 1, n6))


def _norm_mod_rows(x, g, sh, sc):
    y = x * lax.rsqrt(jnp.mean(x * x, axis=-1, keepdims=True) + EPS) * g
    return y * (1.0 + sc) + sh


def _norm_mod_kernel(x_ref, g_ref, sh_ref, sc_ref, o_ref):
    o_ref[...] = _norm_mod_rows(x_ref[...], g_ref[...], sh_ref[...], sc_ref[...]).astype(o_ref.dtype)


def _norm_mod(x, g, sh, sc):
    b, t, d = x.shape
    tr = _tile(t, 1024, 8)
    return pl.pallas_call(
        _norm_mod_kernel,
        grid=(b, t // tr),
        in_specs=[
            pl.BlockSpec((None, tr, d), lambda bi, i: (bi, i, 0)),
            pl.BlockSpec((1, d), lambda bi, i: (0, 0)),
            pl.BlockSpec((None, 1, d), lambda bi, i: (bi, 0, 0)),
            pl.BlockSpec((None, 1, d), lambda bi, i: (bi, 0, 0)),
        ],
        out_specs=pl.BlockSpec((None, tr, d), lambda bi, i: (bi, i, 0)),
        out_shape=jax.ShapeDtypeStruct((b, t, d), BF16),
        compiler_params=_cp("parallel", "parallel"),
        name="norm_mod",
    )(x, g.reshape(1, d), sh, sc)


def _pack_halves(h):
    half = h.shape[1] // 2
    bits = pltpu.bitcast(h.astype(BF16).astype(F32), jnp.uint32)
    return bits[:, half:] | lax.shift_right_logical(bits[:, :half], jnp.uint32(16))


def _unpack_halves(p):
    lo = pltpu.bitcast(lax.shift_left(p, jnp.uint32(16)), F32)
    hi = pltpu.bitcast(p & jnp.uint32(0xFFFF0000), F32)
    return jnp.concatenate([lo, hi], axis=1).astype(BF16)


def _norm_router_kernel(x_ref, g_ref, sh_ref, sc_ref, wrt_ref, h_ref, lg_ref):
    h = _norm_mod_rows(x_ref[...], g_ref[...], sh_ref[...], sc_ref[...])
    h_ref[...] = _pack_halves(h)
    lg_ref[...] = _dot3(wrt_ref[...], h, NT)


def _norm_router(x, g, sh, sc, w_router_t):
    b, t, d = x.shape
    e = w_router_t.shape[0]
    tr = _tile(t, 512)
    nt = t // tr
    return pl.pallas_call(
        _norm_router_kernel,
        grid=(b, nt),
        in_specs=[
            pl.BlockSpec((None, tr, d), lambda bi, i: (bi, i, 0)),
            pl.BlockSpec((1, d), lambda bi, i: (0, 0)),
            pl.BlockSpec((None, 1, d), lambda bi, i: (bi, 0, 0)),
            pl.BlockSpec((None, 1, d), lambda bi, i: (bi, 0, 0)),
            pl.BlockSpec((e, d), lambda bi, i: (0, 0)),
        ],
        out_specs=[
            pl.BlockSpec((None, tr, d // 2), lambda bi, i: (bi, i, 0)),
            pl.BlockSpec((e, tr), lambda bi, i: (0, bi * nt + i)),
        ],
        out_shape=[jax.ShapeDtypeStruct((b, t, d // 2), jnp.uint32), jax.ShapeDtypeStruct((e, b * t), F32)],
        compiler_params=_cp("parallel", "parallel"),
        name="norm_router",
    )(x, g.reshape(1, d), sh, sc, w_router_t)


def _final_norm_kernel(x_ref, g_ref, o_ref):
    x = x_ref[...]
    o_ref[...] = x * lax.rsqrt(jnp.mean(x * x, axis=-1, keepdims=True) + EPS) * g_ref[...]


def _final_norm(x, g):
    b, t, d = x.shape
    tr = _tile(t, 1024, 8)
    return pl.pallas_call(
        _final_norm_kernel,
        grid=(b, t // tr),
        in_specs=[pl.BlockSpec((None, tr, d), lambda bi, i: (bi, i, 0)), pl.BlockSpec((1, d), lambda bi, i: (0, 0))],
        out_specs=pl.BlockSpec((None, tr, d), lambda bi, i: (bi, i, 0)),
        out_shape=jax.ShapeDtypeStruct((b, t, d), F32),
        compiler_params=_cp("parallel", "parallel"),
        name="final_norm",
    )(x, g.reshape(1, d))


def _mm_kernel(a_ref, w_ref, o_ref, wb_ref):
    @pl.when(pl.program_id(1) == 0)
    def _():
        wb_ref[...] = w_ref[...].astype(BF16)

    o_ref[...] = _dot(a_ref[...], wb_ref[...]).astype(o_ref.dtype)


def _mm_rope_kernel(a_ref, w_ref, c_ref, sa_ref, sb_ref, o_ref, wb_ref, *, n_rope):
    @pl.when(pl.program_id(1) == 0)
    def _():
        wb_ref[...] = w_ref[...].astype(BF16)

    z = _dot(a_ref[...], wb_ref[...])
    tn = z.shape[1]
    col0 = pl.program_id(0) * tn
    cos, sa, sb = c_ref[...], sa_ref[...], sb_ref[...]
    for cblk in range(tn // LANE):
        x = z[:, cblk * LANE:(cblk + 1) * LANE]
        y = x * cos + pltpu.roll(x, LANE - 16, 1) * sa + pltpu.roll(x, 16, 1) * sb
        o_ref[:, cblk * LANE:(cblk + 1) * LANE] = jnp.where(col0 + cblk * LANE < n_rope, y, x).astype(o_ref.dtype)


def _mm(a, w, layer, col0, ncols, tn, out_dtype, rope=None):
    m, k = a.shape
    assert ncols % tn == 0 and col0 % tn == 0
    tm = _tile(m, 1024, 8)
    j0 = col0 // tn
    in_specs = [
        pl.BlockSpec((tm, k), lambda j, i: (i, 0)),
        pl.BlockSpec((None, k, tn), lambda j, i: (layer, 0, j0 + j)),
    ]
    args = [a, w]
    body = _mm_kernel
    if rope is not None:
        tables, n_rope, seq = rope
        assert seq % tm == 0
        per_seq = seq // tm
        in_specs += [pl.BlockSpec((tm, LANE), lambda j, i: (i % per_seq, 0))] * 3
        args += list(tables)
        body = functools.partial(_mm_rope_kernel, n_rope=n_rope)
    return pl.pallas_call(
        body,
        grid=(ncols // tn, m // tm),
        in_specs=in_specs,
        out_specs=pl.BlockSpec((tm, tn), lambda j, i: (i, j)),
        out_shape=jax.ShapeDtypeStruct((m, ncols), out_dtype),
        scratch_shapes=[pltpu.VMEM((k, tn), BF16)],
        compiler_params=_cp("arbitrary", "arbitrary"),
        name="mm_in",
    )(*args)


def _mm_res_kernel(a_ref, w_ref, x_ref, gate_ref, o_ref, wb_ref):
    @pl.when(pl.program_id(1) == 0)
    def _():
        wb_ref[...] = w_ref[...].astype(BF16)

    o_ref[...] = x_ref[...] + gate_ref[...] * _dot(a_ref[...], wb_ref[...])


def _mm_res(a, w, layer, x, gate):
    b, t, n = x.shape
    m, k = a.shape
    tn = _tile(n, 1024)
    tm = _tile(t, 1024, 8)
    per_b = t // tm
    out = pl.pallas_call(
        _mm_res_kernel,
        grid=(n // tn, m // tm),
        in_specs=[
            pl.BlockSpec((tm, k), lambda j, i: (i, 0)),
            pl.BlockSpec((None, k, tn), lambda j, i: (layer, 0, j)),
            pl.BlockSpec((tm, tn), lambda j, i: (i, j)),
            pl.BlockSpec((None, 1, tn), lambda j, i: (i // per_b, 0, j)),
        ],
        out_specs=pl.BlockSpec((tm, tn), lambda j, i: (i, j)),
        out_shape=jax.ShapeDtypeStruct((m, n), F32),
        scratch_shapes=[pltpu.VMEM((k, tn), BF16)],
        compiler_params=_cp("arbitrary", "arbitrary"),
        name="mm_out_res",
    )(a, w, x.reshape(m, n), gate)
    return out.reshape(b, t, n)


def _rope_tables(n):
    pos = jnp.arange(n)
    rows, cols = (pos // GRID_W).astype(F32), (pos % GRID_W).astype(F32)
    half = DA_QK_DIM // 2
    inv_freq = jnp.power(ROPE_BASE, -jnp.arange(0, half, 2, dtype=F32) / half)
    ang_r, ang_c = rows[:, None] * inv_freq[None, :], cols[:, None] * inv_freq[None, :]
    cos64 = jnp.concatenate([jnp.cos(ang_r)] * 2 + [jnp.cos(ang_c)] * 2, axis=-1)
    sin64 = jnp.concatenate([jnp.sin(ang_r)] * 2 + [jnp.sin(ang_c)] * 2, axis=-1)
    cos, sin = jnp.tile(cos64, (1, 2)), jnp.tile(sin64, (1, 2))
    first = (jnp.arange(LANE) % 32) < 16
    return cos, jnp.where(first, -sin, 0.0), jnp.where(first, 0.0, sin)


def _diff_attn_kernel(*refs, has_x, post_scale):
    if has_x:
        lam_ref, g_ref, q_ref, kc_ref, vc_ref, kx_ref, vx_ref, o_ref = refs
    else:
        lam_ref, g_ref, q_ref, kc_ref, vc_ref, o_ref = refs
    q = q_ref[...].astype(F32) * (DA_QK_DIM ** -0.5 * LOG2E)
    lane = lax.broadcasted_iota(I32, q.shape, 1)
    if has_x:
        k = jnp.concatenate([kc_ref[...], kx_ref[...]], axis=0)
        v = jnp.concatenate([vc_ref[...], vx_ref[...]], axis=0)
    else:
        k, v = kc_ref[...], vc_ref[...]

    nk = k.shape[0]
    kc_len = _tile(nk, 768)

    def attend(qm):
        m = l = acc = None
        for k0 in range(0, nk, kc_len):
            s = _dot(qm, k[k0:k0 + kc_len], NT)
            m_c = jnp.max(s, axis=-1, keepdims=True)
            if m is None:
                m = m_c
                p = jnp.exp2(s - m)
                l = jnp.sum(p, axis=-1, keepdims=True)
                acc = _dot(p.astype(BF16), v[k0:k0 + kc_len])
            else:
                m_new = jnp.maximum(m, m_c)
                alpha = jnp.exp2(m - m_new)
                p = jnp.exp2(s - m_new)
                l = alpha * l + jnp.sum(p, axis=-1, keepdims=True)
                acc = alpha * acc + _dot(p.astype(BF16), v[k0:k0 + kc_len])
                m = m_new
        return acc / l

    o = attend(jnp.where(lane < DA_QK_DIM, q, 0.0).astype(BF16)) \
        - lam_ref[...] * attend(jnp.where(lane >= DA_QK_DIM, q, 0.0).astype(BF16))
    y = o * lax.rsqrt(jnp.mean(o * o, axis=-1, keepdims=True) + EPS) * g_ref[...] * post_scale
    o_ref[...] = y.astype(o_ref.dtype)


def _diff_attn(q_arr, zc, kx_arr, vx_arr, lam, subln_g, lam_init, has_x):
    b, tq_all, _ = q_arr.shape
    c = zc.shape[1]
    tq = _tile(tq_all, 1024, 8)
    q_blk0, kc_blk0, vc_blk0 = 0, IN_WIDTHS[0] // LANE, 2 * IN_WIDTHS[0] // LANE
    kx_blk0 = kc_blk0
    in_specs = [
        pl.BlockSpec((1, 1), lambda bi, h, i: (0, 0)),
        pl.BlockSpec((1, DA_V_DIM), lambda bi, h, i: (0, 0)),
        pl.BlockSpec((None, tq, LANE), lambda bi, h, i: (bi, i, q_blk0 + h)),
        pl.BlockSpec((None, c, LANE), lambda bi, h, i: (bi, 0, kc_blk0 + h)),
        pl.BlockSpec((None, c, LANE), lambda bi, h, i: (bi, 0, vc_blk0 + h)),
    ]
    args = [lam.reshape(1, 1), subln_g.reshape(1, DA_V_DIM), q_arr, zc, zc]
    if has_x:
        n = kx_arr.shape[1]
        in_specs += [
            pl.BlockSpec((None, n, LANE), lambda bi, h, i: (bi, 0, kx_blk0 + h)),
            pl.BlockSpec((None, n, LANE), lambda bi, h, i: (bi, 0, vc_blk0 + h)),
        ]
        args += [kx_arr, vx_arr]
    return pl.pallas_call(
        functools.partial(_diff_attn_kernel, has_x=has_x, post_scale=1.0 - lam_init),
        grid=(b, DA_HEADS, tq_all // tq),
        in_specs=in_specs,
        out_specs=pl.BlockSpec((None, tq, LANE), lambda bi, h, i: (bi, i, h)),
        out_shape=jax.ShapeDtypeStruct((b, tq_all, DA_HEADS * DA_V_DIM), BF16),
        compiler_params=_cp("parallel", "parallel", "arbitrary"),
        name="diff_attn_x" if has_x else "diff_attn_c",
    )(*args)


def _group_cols(x, g):
    x = x.astype(F32)
    lane = lax.broadcasted_iota(I32, x.shape, 1)
    own = jnp.where(_lane_group(lane, WB_DIM) == g, x, pltpu.roll(x, WB_DIM, 1))
    return jnp.concatenate([own, own], axis=1).astype(BF16)


def _win_attn_kernel(*refs, has_x, n_tok):
    if has_x:
        sink_ref, q_ref, kc_ref, vc_ref, kp_ref, k0_ref, kn_ref, vp_ref, v0_ref, vn_ref, o_ref = refs
    else:
        sink_ref, q_ref, kc_ref, vc_ref, o_ref = refs
    blk = WB_BLOCK
    nqb = q_ref.shape[0] // blk
    if has_x:
        k_lat = jnp.concatenate([kp_ref[...], k0_ref[...], kn_ref[...]], axis=0)
        v_lat = jnp.concatenate([vp_ref[...], v0_ref[...], vn_ref[...]], axis=0)
    for qb in range(nqb):
        rows = slice(qb * blk, (qb + 1) * blk)
        if has_x:
            k_in = jnp.concatenate([k_lat[qb * blk:(qb + 3) * blk], kc_ref[...]], axis=0)
            v_in = jnp.concatenate([v_lat[qb * blk:(qb + 3) * blk], vc_ref[...]], axis=0)
        else:
            k_in, v_in = kc_ref[...], vc_ref[...]
        o_ref[rows, :] = _win_block(sink_ref, q_ref[rows, :], k_in, v_in, pl.program_id(1) * nqb + qb,
                                    has_x, n_tok).astype(o_ref.dtype)


def _win_block(sink_ref, q_blk, k_in, v_in, i, has_x, n_tok):
    rep = WB_HEADS // WB_KV_HEADS
    gw = rep * WB_DIM
    q_all = q_blk.astype(F32) * (WB_DIM ** -0.5 * LOG2E)
    blk = q_all.shape[0]
    lane = lax.broadcasted_iota(I32, (blk, gw), 1)
    if has_x:
        q_pos = i * blk + (lax.broadcasted_iota(I32, (rep * blk, 3 * blk), 0) & (blk - 1))
        k_pos = (i - 1) * blk + lax.broadcasted_iota(I32, (rep * blk, 3 * blk), 1)
        valid = (jnp.abs(q_pos - k_pos) <= WB_WINDOW) & (k_pos >= 0) & (k_pos < n_tok)
    outs = []
    for g in range(WB_KV_HEADS):
        q = q_all[:, g * gw:(g + 1) * gw]
        k, v = _group_cols(k_in, g), _group_cols(v_in, g)
        mine = [_lane_group(lane, WB_DIM) == r for r in range(rep)]
        qs = jnp.concatenate([jnp.where(mine[r], q, 0.0) for r in range(rep)], axis=0).astype(BF16)
        sink = jnp.concatenate([jnp.broadcast_to(sink_ref[:, (g * rep + r) * WB_DIM:(g * rep + r) * WB_DIM + 1], (blk, 1))
                                for r in range(rep)], axis=0)
        s = _dot(qs, k, NT)
        if has_x:
            s = jnp.concatenate([jnp.where(valid, s[:, :3 * blk], NEG_INF), s[:, 3 * blk:]], axis=1)
        m = jnp.maximum(jnp.max(s, axis=-1, keepdims=True), sink)
        pr = jnp.exp2(s - m)
        l = jnp.sum(pr, axis=-1, keepdims=True) + jnp.exp2(sink - m)
        o = _dot(pr.astype(BF16), v) / l
        acc = jnp.zeros(q.shape, F32)
        for r in range(rep):
            acc = acc + jnp.where(mine[r], o[r * blk:(r + 1) * blk], 0.0)
        outs.append(acc)
    return jnp.concatenate(outs, axis=1)


def _win_attn(q_arr, zc, kx_arr, vx_arr, sink, has_x):
    b, tq_all, _ = q_arr.shape
    c = zc.shape[1]
    blk = WB_BLOCK
    nb = tq_all // blk
    qw = WB_HEADS * WB_DIM
    sink_arr = jnp.repeat(sink.astype(F32) * LOG2E, WB_DIM).reshape(1, qw)
    k_blk, v_blk = qw // LANE, qw // LANE + 1
    nqb = 2 if has_x and nb % 2 == 0 else 1
    tq = nqb * blk
    in_specs = [
        pl.BlockSpec((1, qw), lambda bi, i: (0, 0)),
        pl.BlockSpec((None, tq, qw), lambda bi, i: (bi, i, 0)),
        pl.BlockSpec((None, c, LANE), lambda bi, i: (bi, 0, k_blk)),
        pl.BlockSpec((None, c, LANE), lambda bi, i: (bi, 0, v_blk)),
    ]
    args = [sink_arr, q_arr, zc, zc]
    n_tok = 0
    if has_x:
        n_tok = kx_arr.shape[1]
        prev = lambda i: jnp.maximum(i * nqb - 1, 0)
        nxt = lambda i: jnp.minimum((i + 1) * nqb, nb - 1)
        for arr, cb in ((kx_arr, k_blk), (vx_arr, v_blk)):
            in_specs += [
                pl.BlockSpec((None, blk, LANE), lambda bi, i, cb=cb: (bi, prev(i), cb)),
                pl.BlockSpec((None, tq, LANE), lambda bi, i, cb=cb: (bi, i, cb)),
                pl.BlockSpec((None, blk, LANE), lambda bi, i, cb=cb: (bi, nxt(i), cb)),
            ]
            args += [arr, arr, arr]
    return pl.pallas_call(
        functools.partial(_win_attn_kernel, has_x=has_x, n_tok=n_tok),
        grid=(b, tq_all // tq),
        in_specs=in_specs,
        out_specs=pl.BlockSpec((None, tq, qw), lambda bi, i: (bi, i, 0)),
        out_shape=jax.ShapeDtypeStruct((b, tq_all, qw), BF16),
        compiler_params=_cp("parallel", "arbitrary"),
        name="win_attn_x" if has_x else "win_attn_c",
    )(*args)


NA_QROWS = 4
NA_KROWS = 12


def _na_kernel(*refs, has_x, n_blocks):
    if has_x:
        q_ref, kc_ref, vc_ref, k_ref, v_ref, t2_ref, o_ref = refs
    else:
        q_ref, kc_ref, vc_ref, o_ref = refs
    tq = q_ref.shape[0]
    lane = lax.broadcasted_iota(I32, (tq, LANE), 1)
    per = LANE // NA_DIM
    if has_x:
        p = pl.program_id(1)
        is_first, is_last = p == 0, p == n_blocks - 1
        n_rows = n_blocks * NA_QROWS
        start = jnp.where(is_first, 0, jnp.where(is_last, n_rows - NA_KROWS, p * NA_QROWS - NA_ROWS // 2))
        off = pl.multiple_of(start * GRID_W, GRID_W)
        nk = NA_KROWS * GRID_W
        row_i = lax.broadcasted_iota(I32, (tq, nk), 0)
        key_i = lax.broadcasted_iota(I32, (tq, nk), 1)
        qi, col = _lane_group(row_i, GRID_W), row_i & (GRID_W - 1)
        kp, w = _lane_group(key_i, GRID_W), key_i & (GRID_W - 1)
        lo = jnp.where(is_first, 0, jnp.where(is_last, NA_KROWS - NA_ROWS, qi))
        col_start = jnp.clip(col - NA_COLS // 2, 0, GRID_W - NA_COLS)
        valid = (w >= col_start) & (w < col_start + NA_COLS) & (kp >= lo) & (kp < lo + NA_ROWS)
        n_pair = t2_ref.shape[1]
        first_pair = [jnp.where(is_first, NA_ROWS - 1 - i, jnp.where(is_last, -1 - i, NA_ROWS // 2 - 1 - i))
                      for i in range(NA_QROWS)]
    for j in range(NA_HEADS // per):
        cols = slice(j * LANE, (j + 1) * LANE)
        q = q_ref[:, cols].astype(F32) * (NA_DIM ** -0.5 * LOG2E)
        if has_x:
            k = jnp.concatenate([k_ref[pl.ds(off, nk), cols], kc_ref[:, cols]], axis=0)
            v = jnp.concatenate([v_ref[pl.ds(off, nk), cols], vc_ref[:, cols]], axis=0)
        else:
            k, v = kc_ref[:, cols], vc_ref[:, cols]
        acc = jnp.zeros(q.shape, F32)
        for hh in range(per):
            mine = _lane_group(lane, NA_DIM) == hh
            qh = jnp.where(mine, q, 0.0).astype(BF16)
            s = _dot(qh, k, NT)
            if has_x:
                head = j * per + hh
                bias = jnp.concatenate([
                    jnp.concatenate([t2_ref[head, jnp.clip(first_pair[i] + 2 * m, 0, n_pair - 1)]
                                     for m in range(NA_KROWS // 2)], axis=1)
                    for i in range(NA_QROWS)], axis=0)
                s = jnp.concatenate([jnp.where(valid, s[:, :nk] + bias, NEG_INF), s[:, nk:]], axis=1)
            m = jnp.max(s, axis=-1, keepdims=True)
            pr = jnp.exp2(s - m)
            l = jnp.sum(pr, axis=-1, keepdims=True)
            acc = acc + jnp.where(mine, _dot(pr.astype(BF16), v) / l, 0.0)
        o_ref[:, cols] = acc.astype(o_ref.dtype)


def _na_bias_table(rpb):
    r = rpb.astype(F32) * LOG2E
    edge = GRID_W - NA_COLS
    ext = jnp.concatenate([jnp.repeat(r[..., :1], edge, -1), r, jnp.repeat(r[..., -1:], edge, -1)], axis=-1)
    by_col = jnp.stack([ext[..., GRID_W - 1 - c:2 * GRID_W - 1 - c] for c in range(GRID_W)], axis=-2)
    return jnp.concatenate([by_col[:, :-1], by_col[:, 1:]], axis=-1)


def _na_attn(q_arr, zc, zx, bias_tab, layer, has_x):
    b, tq_all, _ = q_arr.shape
    c = zc.shape[1]
    hw = NA_HEADS * NA_DIM
    tq = NA_QROWS * GRID_W if has_x else _tile(tq_all, 256, 8)
    nq = tq_all // tq
    in_specs = [
        pl.BlockSpec((None, tq, hw), lambda bi, r: (bi, r, 0)),
        pl.BlockSpec((None, c, hw), lambda bi, r: (bi, 0, 1)),
        pl.BlockSpec((None, c, hw), lambda bi, r: (bi, 0, 2)),
    ]
    args = [q_arr, zc, zc]
    if has_x:
        n = zx.shape[1]
        assert tq_all % tq == 0 and nq * NA_QROWS >= NA_KROWS
        in_specs += [
            pl.BlockSpec((None, n, hw), lambda bi, r: (bi, 0, 1)),
            pl.BlockSpec((None, n, hw), lambda bi, r: (bi, 0, 2)),
            pl.BlockSpec((None,) + bias_tab.shape[1:], lambda bi, r: (layer, 0, 0, 0, 0)),
        ]
        args += [zx, zx, bias_tab]
    return pl.pallas_call(
        functools.partial(_na_kernel, has_x=has_x, n_blocks=nq),
        grid=(b, nq),
        in_specs=in_specs,
        out_specs=pl.BlockSpec((None, tq, hw), lambda bi, r: (bi, r, 0)),
        out_shape=jax.ShapeDtypeStruct((b, tq_all, hw), BF16),
        compiler_params=_cp("parallel", "arbitrary"),
        name="na_attn_x" if has_x else "na_attn_c",
    )(*args)


def _s5_kernel(*refs, ctx_out):
    if ctx_out:
        (uc_ref, ux_ref, d_ref, b1_ref, b2_ref, a1_ref, a2_ref, a3_ref, cp_ref, scat_ref, yc_ref, yx_ref,
         kbig, bb1, bb2, cb_ref, xc, xx, s1c, s2c, s1x, s2x, hc, hx) = refs
    else:
        (uc_ref, ux_ref, d_ref, b1_ref, b2_ref, a1_ref, a2_ref, a3_ref, cp_ref, scat_ref, yx_ref,
         kbig, bb1, bb2, cb_ref, xc, xx, s1c, s2c, s1x, s2x, hc, hx) = refs
        yc_ref = None
    dirn = pl.program_id(1)
    ell, hch = S5_CHUNK, S5_GROUP_CH
    gq = LANE // hch
    nb = ux_ref.shape[0]
    streams = ((uc_ref, xc, s1c, s2c, hc, yc_ref, uc_ref.shape[1] // ell),
               (ux_ref, xx, s1x, s2x, hx, yx_ref, ux_ref.shape[1] // ell))

    for i in range(ell):
        for j in range(ell):
            kbig[i * LANE:(i + 1) * LANE, j * LANE:(j + 1) * LANE] = d_ref[j - i + ell - 1]
    bb1[...] = jnp.zeros(bb1.shape, BF16)
    bb2[...] = jnp.zeros(bb2.shape, BF16)
    for i in range(ell):
        for gl in range(gq):
            r0 = i * LANE + gl * hch
            bb1[r0:r0 + hch, gl * LANE:(gl + 1) * LANE] = b1_ref[gl, i * hch:(i + 1) * hch, :].astype(BF16)
            bb2[r0:r0 + hch, gl * LANE:(gl + 1) * LANE] = b2_ref[gl, i * hch:(i + 1) * hch, :].astype(BF16)
    for gl in range(gq):
        cb_ref[gl * LANE:(gl + 1) * LANE, :] = _dot(cp_ref[gl], scat_ref[gl]).astype(BF16)

    @pl.when(dirn == 0)
    def _():
        for u_ref, x_s, _, _, _, _, nc in streams:
            for b in range(nb):
                for i in range(ell):
                    x_s[b * nc:(b + 1) * nc, i * LANE:(i + 1) * LANE] = u_ref[b, pl.ds(i, nc, stride=ell), :].astype(BF16)

    cpt = SUBLANE // nb
    for _, x_s, s1, s2, _, _, nc in streams:
        loc1 = _dot(x_s[...], bb1[...])
        loc2 = _dot(x_s[...], bb2[...])
        for k in range(gq):
            for b in range(nb):
                s1[k, pl.ds(b, nc, stride=nb), :] = loc1[b * nc:(b + 1) * nc, k * LANE:(k + 1) * LANE]
                s2[k, pl.ds(b, nc, stride=nb), :] = loc2[b * nc:(b + 1) * nc, k * LANE:(k + 1) * LANE]

    def lane_blocks(a_ref):
        return [jnp.broadcast_to(a_ref[:, k * LANE:(k + 1) * LANE], (SUBLANE, LANE)) for k in range(gq)]

    a1, a2, a3 = lane_blocks(a1_ref), lane_blocks(a2_ref), lane_blocks(a3_ref)
    fwd = dirn == 0
    row_grp = _lane_group(lax.broadcasted_iota(I32, (SUBLANE, LANE), 0), nb)

    def advance(v):
        return jnp.where(fwd, pltpu.roll(v, nb, 0), pltpu.roll(v, SUBLANE - nb, 0))

    zero = tuple(jnp.zeros((SUBLANE, LANE), F32) for _ in range(gq))
    carry = (zero, zero)
    for _, _, s1, s2, h, _, nc in streams:
        n_tiles = nc // cpt

        def step(t, vs, s1=s1, s2=s2, h=h, n_tiles=n_tiles):
            v1, v2 = list(vs[0]), list(vs[1])
            j = jnp.where(fwd, t, n_tiles - 1 - t)
            rows = pl.ds(pl.multiple_of(j * SUBLANE, SUBLANE), SUBLANE)
            for k in range(gq):
                loc1, loc2 = s1[k, rows, :], s2[k, rows, :]
                entered = v1[k]
                for u in range(cpt):
                    grp = jnp.where(fwd, u, cpt - 1 - u)
                    entered = jnp.where(row_grp == grp, v1[k], entered)
                    n1 = a1[k] * v1[k] + a2[k] * v2[k] + loc1
                    n2 = a1[k] * v2[k] + a3[k] * v1[k] + loc2
                    v1[k], v2[k] = advance(n1), advance(n2)
                h[k, rows, :] = entered
            return tuple(v1), tuple(v2)

        carry = lax.fori_loop(0, n_tiles, step, carry, unroll=2)

    for _, x_s, _, _, h, y_ref, nc in streams:
        if y_ref is None:
            continue
        h_all = jnp.concatenate(
            [jnp.concatenate([h[k, pl.ds(b, nc, stride=nb), :] for b in range(nb)], axis=0) for k in range(gq)],
            axis=1).astype(BF16)
        y = _dot(x_s[...], kbig[...]) + _dot(h_all, cb_ref[...])

        @pl.when(dirn == 0)
        def _():
            for b in range(nb):
                for j in range(ell):
                    y_ref[b, pl.ds(j, nc, stride=ell), :] = y[b * nc:(b + 1) * nc, j * LANE:(j + 1) * LANE]

        @pl.when(dirn == 1)
        def _():
            for b in range(nb):
                for j in range(ell):
                    y_ref[b, pl.ds(j, nc, stride=ell), :] += y[b * nc:(b + 1) * nc, j * LANE:(j + 1) * LANE]


def _s5_tables(a_re, a_im, log_step, b_re, b_im, c_re, c_im):
    ell, hch, p = S5_CHUNK, S5_GROUP_CH, S5_STATE
    lam = lax.complex(a_re.astype(F32), a_im.astype(F32))
    lam_dt = lam * jnp.exp(log_step.astype(F32))[:, :, None]
    a_bar = jnp.exp(lam_dt)
    b_bar = ((a_bar - 1.0) / lam)[:, :, :, None] * lax.complex(b_re.astype(F32), b_im.astype(F32))
    c_mat = lax.complex(c_re.astype(F32), c_im.astype(F32))
    gq = LANE // hch
    nq = S5_GROUPS // gq
    hp = lax.Precision.HIGHEST
    steps = jnp.arange(ell + 1, dtype=F32)
    apow = jnp.exp(lam_dt[:, :, None, :] * steps[None, None, :, None])
    apow_re, apow_im = apow.real, apow.imag
    bt_re, bt_im = jnp.swapaxes(b_bar.real, 2, 3), jnp.swapaxes(b_bar.imag, 2, 3)
    c_re_, c_im_ = c_mat.real, c_mat.imag

    def cmul(ar, ai, br, bi):
        return ar * br - ai * bi, ar * bi + ai * br

    m_re, m_im = cmul(apow_re[:, :, :ell, None, :], apow_im[:, :, :ell, None, :], bt_re[:, :, None], bt_im[:, :, None])
    kk = (jnp.einsum('dgkip,dgop->dgkio', m_re, c_re_, precision=hp)
          - jnp.einsum('dgkip,dgop->dgkio', m_im, c_im_, precision=hp))
    zeros = jnp.zeros_like(kk[0, :, 1:])
    signed = jnp.stack([jnp.concatenate([zeros, kk[0]], axis=1),
                        jnp.concatenate([jnp.flip(kk[1], axis=1), zeros], axis=1)])
    signed = signed.reshape(2, nq, gq, 2 * ell - 1, hch, hch).astype(BF16)
    dblk = jnp.concatenate(
        [jnp.pad(signed[:, :, g], ((0, 0), (0, 0), (0, 0), (0, 0), (g * hch, LANE - (g + 1) * hch))) for g in range(gq)],
        axis=3)
    i_idx = jnp.arange(ell)
    e_re = jnp.stack([apow_re[0][:, ell - 1 - i_idx], apow_re[1][:, i_idx]])
    e_im = jnp.stack([apow_im[0][:, ell - 1 - i_idx], apow_im[1][:, i_idx]])
    bp_re, bp_im = cmul(e_re[:, :, :, None, :], e_im[:, :, :, None, :], bt_re[:, :, None], bt_im[:, :, None])
    bp_re, bp_im = bp_re.reshape(2, S5_GROUPS, ell * hch, p), bp_im.reshape(2, S5_GROUPS, ell * hch, p)
    b1 = jnp.concatenate([bp_re, bp_im], axis=-1)
    b2 = jnp.concatenate([bp_im, bp_re], axis=-1)
    al_re, al_im = apow_re[:, :, ell, :], apow_im[:, :, ell, :]
    lanes = lambda u, v: jnp.concatenate([u, v], axis=-1).reshape(2, nq, 1, gq * 2 * p)
    a1, a2, a3 = lanes(al_re, al_re), lanes(-al_im, al_im), lanes(al_im, -al_im)
    f_re = jnp.stack([apow_re[0][:, 1 + i_idx], apow_re[1][:, ell - i_idx]])
    f_im = jnp.stack([apow_im[0][:, 1 + i_idx], apow_im[1][:, ell - i_idx]])
    g_re, g_im = cmul(c_re_[:, :, None], c_im_[:, :, None], f_re[:, :, :, None, :], f_im[:, :, :, None, :])
    cp = jnp.concatenate([jnp.transpose(g_re, (0, 1, 4, 2, 3)), -jnp.transpose(g_im, (0, 1, 4, 2, 3))], axis=2)
    cp = cp.reshape(2, S5_GROUPS, 2 * p, ell * hch).astype(BF16)
    return dblk, b1, b2, a1, a2, a3, cp


def _s5_out_scatter():
    ell, hch = S5_CHUNK, S5_GROUP_CH
    gq = LANE // hch
    src = jnp.arange(ell * hch)
    dst = (src // hch)[None, :] * LANE + jnp.arange(gq)[:, None] * hch + (src % hch)[None, :]
    return (dst[:, :, None] == jnp.arange(ell * LANE)[None, None, :]).astype(BF16)


def _s5_scan(zx, zc, tables, layer, ctx_out):
    b, n, _ = zx.shape
    c = zc.shape[1]
    ell, hch = S5_CHUNK, S5_GROUP_CH
    gq = LANE // hch
    nq = S5_GROUPS // gq
    p2 = 2 * S5_STATE
    ncx, ncc = n // ell, c // ell
    assert n % ell == 0 and c % ell == 0 and ncc % 16 == 0
    assert SUBLANE % b == 0 and b & (b - 1) == 0
    dblk, b1, b2, a1, a2, a3, cp = tables
    u0 = 0
    sw = gq * p2
    xw = ell * LANE
    a_spec = pl.BlockSpec((None, None, None, 1, sw), lambda q, d: (layer, d, q, 0, 0))
    b_spec = pl.BlockSpec((None, None, gq, ell * hch, p2), lambda q, d: (layer, d, q, 0, 0))
    y_specs = [pl.BlockSpec((b, n, LANE), lambda q, d: (0, 0, q))]
    y_shapes = [jax.ShapeDtypeStruct((b, n, S5_WIDTH), F32)]
    if ctx_out:
        y_specs = [pl.BlockSpec((b, c, LANE), lambda q, d: (0, 0, q))] + y_specs
        y_shapes = [jax.ShapeDtypeStruct((b, c, S5_WIDTH), F32)] + y_shapes
    out = pl.pallas_call(
        functools.partial(_s5_kernel, ctx_out=ctx_out),
        grid=(nq, 2),
        in_specs=[
            pl.BlockSpec((b, c, LANE), lambda q, d: (0, 0, u0 + q)),
            pl.BlockSpec((b, n, LANE), lambda q, d: (0, 0, u0 + q)),
            pl.BlockSpec((None, None, None, 2 * ell - 1, LANE, LANE), lambda q, d: (layer, d, q, 0, 0, 0)),
            b_spec, b_spec, a_spec, a_spec, a_spec,
            pl.BlockSpec((None, None, gq, p2, ell * hch), lambda q, d: (layer, d, q, 0, 0)),
            pl.BlockSpec((gq, ell * hch, xw), lambda q, d: (0, 0, 0)),
        ],
        out_specs=y_specs,
        out_shape=y_shapes,
        scratch_shapes=[
            pltpu.VMEM((xw, xw), BF16), pltpu.VMEM((xw, sw), BF16), pltpu.VMEM((xw, sw), BF16),
            pltpu.VMEM((sw, xw), BF16),
            pltpu.VMEM((b * ncc, xw), BF16), pltpu.VMEM((b * ncx, xw), BF16),
            pltpu.VMEM((gq, b * ncc, p2), F32), pltpu.VMEM((gq, b * ncc, p2), F32),
            pltpu.VMEM((gq, b * ncx, p2), F32), pltpu.VMEM((gq, b * ncx, p2), F32),
            pltpu.VMEM((gq, b * ncc, p2), F32), pltpu.VMEM((gq, b * ncx, p2), F32),
        ],
        compiler_params=_cp("parallel", "arbitrary"),
        name="s5_scan",
    )(zc, zx, dblk, b1, b2, a1, a2, a3, cp, _s5_out_scatter())
    return (out[0], out[1]) if ctx_out else (None, out[0])


def _s5_out_kernel(y_ref, u_ref, d_ref, w_ref, o_ref):
    g = jax.nn.gelu(y_ref[...] + d_ref[...] * u_ref[...])
    o_ref[...] = (g * jax.nn.sigmoid(_dot(g.astype(BF16), w_ref[...].astype(BF16)))).astype(o_ref.dtype)


def _s5_out(y, u, d_skip, w_glu):
    b, t, w = y.shape
    tr = _tile(t, 512, 8)
    row = pl.BlockSpec((None, tr, w), lambda bi, i: (bi, i, 0))
    return pl.pallas_call(
        _s5_out_kernel,
        grid=(b, t // tr),
        in_specs=[row, row, pl.BlockSpec((1, w), lambda bi, i: (0, 0)), pl.BlockSpec((w, w), lambda bi, i: (0, 0))],
        out_specs=row,
        out_shape=jax.ShapeDtypeStruct((b, t, w), BF16),
        compiler_params=_cp("parallel", "parallel"),
        name="s5_out",
    )(y, u, d_skip.reshape(1, w).astype(F32), w_glu)


def _merge_kernel(h_ref, ya_ref, yb_ref, yc_ref, yd_ref, wg0_ref, wg1_ref, wg2_ref, wg3_ref, wb_ref, o_ref, wgs_ref, wbs_ref):
    @pl.when(pl.program_id(1) == 0)
    def _():
        for n, wg_ref in enumerate((wg0_ref, wg1_ref, wg2_ref, wg3_ref)):
            wgs_ref[n] = wg_ref[...].astype(BF16)
        wbs_ref[...] = wb_ref[...].astype(BF16)

    h = h_ref[...]
    acc = None
    for n, y_ref in enumerate((ya_ref, yb_ref, yc_ref, yd_ref)):
        term = jax.nn.sigmoid(_dot(h, wgs_ref[n])) * _dot(y_ref[...], wbs_ref[n])
        acc = term if acc is None else acc + term
    o_ref[...] = acc.astype(o_ref.dtype)


def _merge(h, branches, w_in, w_branch, layer):
    m, d = h.shape
    bw = branches[0].shape[1]
    tn = 256
    tm = _tile(m, 1024, 8)
    g0 = N_MIX_IN // tn
    per = d // tn
    assert N_MIX_IN % tn == 0 and d % tn == 0
    wg_specs = [pl.BlockSpec((None, d, tn), lambda j, i, n=n: (layer, 0, g0 + n * per + j)) for n in range(N_BRANCH)]
    y_spec = pl.BlockSpec((tm, bw), lambda j, i: (i, 0))
    return pl.pallas_call(
        _merge_kernel,
        grid=(d // tn, m // tm),
        in_specs=[pl.BlockSpec((tm, d), lambda j, i: (i, 0)), y_spec, y_spec, y_spec, y_spec] + wg_specs
                 + [pl.BlockSpec((None, N_BRANCH, bw, tn), lambda j, i: (layer, 0, 0, j))],
        out_specs=pl.BlockSpec((tm, tn), lambda j, i: (i, j)),
        out_shape=jax.ShapeDtypeStruct((m, d), BF16),
        scratch_shapes=[pltpu.VMEM((N_BRANCH, d, tn), BF16), pltpu.VMEM((N_BRANCH, bw, tn), BF16)],
        compiler_params=_cp("arbitrary", "arbitrary"),
        name="merge",
    )(h, *branches, w_in, w_in, w_in, w_in, w_branch)


def _topk_kernel(lg_ref, slot_ref, idx_ref, gate_ref, tri_ref, *, cap):
    e, t = lg_ref.shape

    @pl.when(pl.program_id(0) == 0)
    def _():
        rows = 256 if t % 256 == 0 else t
        for r0 in range(0, t, rows):
            ri = lax.broadcasted_iota(I32, (rows, t), 0) + r0
            ci = lax.broadcasted_iota(I32, (rows, t), 1)
            tri_ref[r0:r0 + rows, :] = jnp.where(ri < ci, 1.0, 0.0).astype(BF16)

    lg = lg_ref[...]
    ex = jnp.exp(lg - jnp.max(lg, axis=0, keepdims=True))
    aff = ex / jnp.sum(ex, axis=0, keepdims=True)
    bits = pltpu.bitcast(aff, I32)

    def search(_, carry):
        lo, hi = carry
        mid = lo + lax.shift_right_logical(hi - lo + 1, 1)
        ok = _count(bits >= mid, 1) >= cap
        return jnp.where(ok, mid, lo), jnp.where(ok, hi, mid - 1)

    lo0 = jnp.zeros((e, 1), I32)
    hi0 = jnp.full((e, 1), 0x7F800000, I32)
    thr, _ = lax.fori_loop(0, 32, search, (lo0, hi0))
    gt = bits > thr
    eq = bits == thr
    need = cap - _count(gt, 1)
    tri = tri_ref[...]
    eq_before = _dot(jnp.where(eq, 1.0, 0.0).astype(BF16), tri)
    sel = gt | (eq & (eq_before < need))
    sel_before = _dot(jnp.where(sel, 1.0, 0.0).astype(BF16), tri)
    slot = jnp.where(sel, sel_before.astype(I32), -1)
    slot_ref[...] = slot

    tok = lax.broadcasted_iota(I32, (SUBLANE, t), 1)
    row = lax.broadcasted_iota(I32, (SUBLANE, t), 0)
    tok_hi, tok_lo = lax.shift_right_logical(tok, 6).astype(F32), (tok & 63).astype(F32)
    slot_iota = lax.broadcasted_iota(I32, (cap, t), 0)
    for ei in range(e):
        a = jnp.broadcast_to(aff[ei:ei + 1, :], (SUBLANE, t))
        a_hi = a.astype(BF16).astype(F32)
        a_mid = (a - a_hi).astype(BF16).astype(F32)
        a_lo = a - a_hi - a_mid
        feats = jnp.where(row == 0, tok_hi, jnp.where(row == 1, tok_lo, jnp.where(
            row == 2, a_hi, jnp.where(row == 3, a_mid, jnp.where(row == 4, a_lo, 0.0))))).astype(BF16)
        pick = jnp.where(slot_iota == slot[ei:ei + 1, :], 1.0, 0.0).astype(BF16)
        res = _dot(feats, pick, NT)
        idx_ref[ei:ei + 1, :] = (res[0:1] * 64.0 + res[1:2]).astype(I32)
        gate_ref[ei:ei + 1, :] = res[2:3] + res[3:4] + res[4:5]


def _topk(logits_t, b, cap):
    e, bt = logits_t.shape
    t = bt // b
    per_b = lambda w: pl.BlockSpec((None, e, w), lambda bi: (bi, 0, 0))
    return pl.pallas_call(
        functools.partial(_topk_kernel, cap=cap),
        grid=(b,),
        in_specs=[pl.BlockSpec((e, t), lambda bi: (0, bi))],
        out_specs=[per_b(t), per_b(cap), per_b(cap)],
        out_shape=[jax.ShapeDtypeStruct((b, e, t), I32), jax.ShapeDtypeStruct((b, e, cap), I32),
                   jax.ShapeDtypeStruct((b, e, cap), F32)],
        scratch_shapes=[pltpu.VMEM((t, t), BF16)],
        compiler_params=_cp("arbitrary"),
        name="route_topk",
    )(logits_t)


def _ffn_up_kernel(*refs, n_streams, n_f):
    ns = n_streams
    idx_refs, h_refs = refs[:ns], refs[ns:2 * ns]
    w1_ref, w3_ref = refs[2 * ns:2 * ns + 2]
    outs = refs[2 * ns + 2:3 * ns + 2]
    scratch = refs[3 * ns + 2:]
    gbufs, xss, sems = scratch[:ns], scratch[ns:2 * ns], scratch[2 * ns:3 * ns]
    e, f = pl.program_id(0), pl.program_id(1)
    n_e = pl.num_programs(0)
    slot = lax.rem(e, 2)

    for idx_ref, h_ref, gbuf, xs, sem in zip(idx_refs, h_refs, gbufs, xss, sems):
        rows = gbuf.shape[1]

        @pl.when((e == 0) & (f == 0))
        def _():
            def start(r, carry):
                pltpu.make_async_copy(h_ref.at[idx_ref[0, r]], gbuf.at[0, r], sem.at[0]).start()
                return carry
            lax.fori_loop(0, rows, start, 0)

        @pl.when(f == 0)
        def _():
            pltpu.make_async_copy(gbuf.at[slot], gbuf.at[slot], sem.at[slot]).wait()
            xs[...] = _unpack_halves(gbuf[slot])

    nxt = jnp.minimum(e + 1, n_e - 1)
    w1, w3 = w1_ref[...].astype(BF16), w3_ref[...].astype(BF16)
    for idx_ref, h_ref, gbuf, xs, sem, o_ref in zip(idx_refs, h_refs, gbufs, xss, sems, outs):
        x = xs[...]
        a = _dot(x, w1)
        g = _dot(x, w3)
        part = gbuf.shape[1] // n_f
        for i in range(part):
            r = f * part + i
            pltpu.make_async_copy(h_ref.at[idx_ref[nxt, r]], gbuf.at[1 - slot, r], sem.at[1 - slot]).start()
        o_ref[...] = ((a / (1.0 + jnp.exp(-a))) * g).astype(o_ref.dtype)

    @pl.when((e == n_e - 1) & (f == n_f - 1))
    def _():
        for gbuf, sem in zip(gbufs, sems):
            pltpu.make_async_copy(gbuf.at[1 - slot], gbuf.at[1 - slot], sem.at[1 - slot]).wait()


def _ffn_down_kernel(*refs, n_streams):
    mids = refs[:n_streams]
    gates = refs[n_streams:2 * n_streams]
    w2_ref = refs[2 * n_streams]
    outs = refs[2 * n_streams + 1:]
    w2 = w2_ref[...].astype(BF16)
    for m_ref, gate_ref, o_ref in zip(mids, gates, outs):
        o_ref[...] = (_dot(m_ref[...], w2) * gate_ref[...]).astype(o_ref.dtype)


def _ffn(idx_list, h_list, gate_list, w1, w3, w2, layer):
    ns = len(idx_list)
    e = idx_list[0].shape[0]
    d = 2 * h_list[0].shape[1]
    ff = w1.shape[-1]
    tf = _tile(ff, 512)
    tn = _tile(d, 1024)
    n_f = ff // tf
    rows = [idx.shape[1] for idx in idx_list]
    assert all(r % n_f == 0 for r in rows)
    hbm = pl.BlockSpec(memory_space=pl.ANY)
    mids = pl.pallas_call(
        functools.partial(_ffn_up_kernel, n_streams=ns, n_f=n_f),
        grid_spec=pltpu.PrefetchScalarGridSpec(
            num_scalar_prefetch=ns,
            grid=(e, n_f),
            in_specs=[hbm] * ns + [
                pl.BlockSpec((None, None, d, tf), lambda ei, f, *_: (layer, ei, 0, f)),
                pl.BlockSpec((None, None, d, tf), lambda ei, f, *_: (layer, ei, 0, f)),
            ],
            out_specs=[pl.BlockSpec((None, r, tf), lambda ei, f, *_: (ei, 0, f)) for r in rows],
            scratch_shapes=[pltpu.VMEM((2, r, d // 2), jnp.uint32) for r in rows]
                           + [pltpu.VMEM((r, d), BF16) for r in rows]
                           + [pltpu.SemaphoreType.DMA((2,)) for _ in rows],
        ),
        out_shape=[jax.ShapeDtypeStruct((e, r, ff), BF16) for r in rows],
        compiler_params=_cp("arbitrary", "arbitrary"),
        name="moe_ffn_up",
    )(*idx_list, *h_list, w1, w3)
    return pl.pallas_call(
        functools.partial(_ffn_down_kernel, n_streams=ns),
        grid=(e, d // tn),
        in_specs=[pl.BlockSpec((None, r, ff), lambda ei, j: (ei, 0, 0)) for r in rows]
                 + [pl.BlockSpec((None, r, 1), lambda ei, j: (ei, 0, 0)) for r in rows]
                 + [pl.BlockSpec((None, None, ff, tn), lambda ei, j: (layer, ei, 0, j))],
        out_specs=[pl.BlockSpec((None, r, tn), lambda ei, j: (ei, 0, j)) for r in rows],
        out_shape=[jax.ShapeDtypeStruct((e, r, d), BF16) for r in rows],
        compiler_params=_cp("parallel", "arbitrary"),
        name="moe_ffn_down",
    )(*mids, *gate_list, w2)


def _combine_kernel(slot_ref, y_ref, x_ref, gate_ref, o_ref, pt_ref, *, cap):
    e = slot_ref.shape[1]
    tm = slot_ref.shape[0]
    r0 = pl.multiple_of(pl.program_id(2) * tm, tm)

    @pl.when(pl.program_id(1) == 0)
    def _():
        slot = slot_ref[...]
        lane = lax.broadcasted_iota(I32, (tm, cap), 1)
        for ei in range(e):
            pt_ref[pl.ds(r0, tm), ei * cap:(ei + 1) * cap] = jnp.where(lane == slot[:, ei:ei + 1], 1.0, 0.0).astype(BF16)

    y = y_ref[...]
    acc = _dot(pt_ref[pl.ds(r0, tm), :], y.reshape(e * cap, y.shape[2]))
    o_ref[...] = x_ref[...] + gate_ref[...] * acc


def _combine(slot_te, ys, x, gate, cap):
    b, t, d = x.shape
    e = slot_te.shape[2]
    tm = _tile(t, 1024, 8)
    tn = _tile(d, 512)
    return pl.pallas_call(
        functools.partial(_combine_kernel, cap=cap),
        grid=(b, d // tn, t // tm),
        in_specs=[
            pl.BlockSpec((None, tm, e), lambda bi, j, i: (bi, i, 0)),
            pl.BlockSpec((e, cap, tn), lambda bi, j, i: (0, bi, j)),
            pl.BlockSpec((None, tm, tn), lambda bi, j, i: (bi, i, j)),
            pl.BlockSpec((None, 1, tn), lambda bi, j, i: (bi, 0, j)),
        ],
        out_specs=pl.BlockSpec((None, tm, tn), lambda bi, j, i: (bi, i, j)),
        out_shape=jax.ShapeDtypeStruct((b, t, d), F32),
        scratch_shapes=[pltpu.VMEM((t, e * cap), BF16)],
        compiler_params=_cp("parallel", "arbitrary", "arbitrary"),
        name="moe_combine",
    )(slot_te, ys, x, gate)


def kernel(x, c, ctx, c_ctx, ada_w, ada_b, norm1_g, norm2_g, w_in, da_lambda, da_subln_g, wb_sink, na_rpb, s5_a_re, s5_a_im, s5_log_step, s5_b_re, s5_b_im, s5_c_re, s5_c_im, s5_d, s5_glu_w, w_branch, w_out, w_router, w_e1, w_e3, w_e2, final_g):
    b, n, d = x.shape
    n_ctx = ctx.shape[1]
    depth = ada_w.shape[0]
    assert b + 1 <= ADA_ROWS and n % GRID_W == 0

    cs = jnp.zeros((ADA_ROWS, d), F32).at[:b].set(c).at[b].set(c_ctx)
    mods = _ada_mod(cs, ada_w, ada_b)
    rope_tab = _rope_tables(n)
    bias_tab = jax.vmap(_na_bias_table)(na_rpb)
    s5_tab = jax.vmap(_s5_tables)(s5_a_re, s5_a_im, s5_log_step, s5_b_re, s5_b_im, s5_c_re, s5_c_im)

    for l in range(depth):
        with_ctx = l < depth - 1
        mod_x = [mods[l, :b, k * d:(k + 1) * d].reshape(b, 1, d) for k in range(6)]
        mod_c = [jnp.broadcast_to(mods[l, b, k * d:(k + 1) * d], (b, 1, d)) for k in range(6)]

        hx = _norm_mod(x, norm1_g[l], mod_x[0], mod_x[1])
        hc = _norm_mod(ctx, norm1_g[l], mod_c[0], mod_c[1])
        proj = []
        for h, t, positions in ((hx, n, True), (hc, n_ctx, False)):
            h2d = h.reshape(b * t, d)
            proj.append([
                _mm(h2d, w_in, l, col0, ncols, tn, dt,
                    rope=(rope_tab, n_rope, t) if positions and n_rope else None).reshape(b, t, ncols)
                for col0, ncols, tn, dt, n_rope in MIX_PROJ])
        (za_x, zb_x, zn_x, zs_x), (za_c, zb_c, zn_c, zs_c) = proj

        lam_init = 0.8 - 0.6 * math.exp(-0.3 * l)
        lv = da_lambda[l].astype(F32)
        lam = jnp.exp(jnp.sum(lv[0] * lv[1])) - jnp.exp(jnp.sum(lv[2] * lv[3])) + lam_init

        ya_x = _diff_attn(za_x, za_c, za_x, za_x, lam, da_subln_g[l], lam_init, True)
        yb_x = _win_attn(zb_x, zb_c, zb_x, zb_x, wb_sink[l], True)
        yc_x = _na_attn(zn_x, zn_c, zn_x, bias_tab, l, True)
        ys_c, ys_x = _s5_scan(zs_x, zs_c, s5_tab, l, with_ctx)
        yd_x = _s5_out(ys_x, zs_x, s5_d[l], s5_glu_w[l])
        gx = _merge(hx.reshape(b * n, d), [t.reshape(b * n, -1) for t in (ya_x, yb_x, yc_x, yd_x)], w_in, w_branch, l)
        x = _mm_res(gx, w_out, l, x, mod_x[2])

        if with_ctx:
            ya_c = _diff_attn(za_c, za_c, None, None, lam, da_subln_g[l], lam_init, False)
            yb_c = _win_attn(zb_c, zb_c, None, None, wb_sink[l], False)
            yc_c = _na_attn(zn_c, zn_c, None, None, l, False)
            yd_c = _s5_out(ys_c, zs_c, s5_d[l], s5_glu_w[l])
            gc = _merge(hc.reshape(b * n_ctx, d), [t.reshape(b * n_ctx, -1) for t in (ya_c, yb_c, yc_c, yd_c)],
                        w_in, w_branch, l)
            ctx = _mm_res(gc, w_out, l, ctx, mod_c[2])

        w_router_t = jnp.transpose(w_router[l]).astype(F32)
        streams = [(x, mod_x)] + ([(ctx, mod_c)] if with_ctx else [])
        routed = []
        for s, mod in streams:
            t = s.shape[1]
            cap = EC_CAPACITY_FACTOR * t // N_EXPERTS
            h2p, logits_t = _norm_router(s, norm2_g[l], mod[3], mod[4], w_router_t)
            slot, idx, gate = _topk(logits_t, b, cap)
            by_expert = lambda v: jnp.transpose(v, (1, 0, 2)).reshape(N_EXPERTS, b * cap)
            flat_row = idx + (jnp.arange(b, dtype=I32) * t)[:, None, None]
            routed.append((jnp.transpose(slot, (0, 2, 1)), by_expert(flat_row), h2p.reshape(b * t, d // 2),
                           by_expert(gate)[:, :, None], cap))
        ys = _ffn([r[1] for r in routed], [r[2] for r in routed], [r[3] for r in routed], w_e1, w_e3, w_e2, l)
        x = _combine(routed[0][0], ys[0], x, mod_x[5], routed[0][4])
        if with_ctx:
            ctx = _combine(routed[1][0], ys[1], ctx, mod_c[5], routed[1][4])

    return _final_norm(x, final_g)
```

```python
import functools
import math

import jax
import jax.numpy as jnp
from jax import lax
from jax.experimental import pallas as pl
from jax.experimental.pallas import tpu as pltpu

F32 = jnp.float32
BF16 = jnp.bfloat16
I32 = jnp.int32

GRID_W = 64
EPS = 1e-6
NEG_INF = -1e30
LOG2E = math.log2(math.e)
ROPE_BASE = 10000.0

DA_HEADS = 4
DA_QK_DIM = 64
DA_V_DIM = 2 * DA_QK_DIM
WB_HEADS = 8
WB_KV_HEADS = 2
WB_DIM = 64
WB_WINDOW = 128
WB_BLOCK = 128
NA_HEADS = 8
NA_DIM = 64
NA_ROWS = 8
NA_COLS = 16
S5_GROUPS = 32
S5_GROUP_CH = 16
S5_STATE = 64
S5_WIDTH = S5_GROUPS * S5_GROUP_CH
S5_CHUNK = 8
N_BRANCH = 4
BRANCH_WIDTH = 512
N_EXPERTS = 16
EC_CAPACITY_FACTOR = 2

IN_WIDTHS = (
    2 * DA_HEADS * DA_QK_DIM, 2 * DA_HEADS * DA_QK_DIM, DA_HEADS * DA_V_DIM,
    WB_HEADS * WB_DIM, WB_KV_HEADS * WB_DIM, WB_KV_HEADS * WB_DIM,
    NA_HEADS * NA_DIM, NA_HEADS * NA_DIM, NA_HEADS * NA_DIM,
    S5_WIDTH,
)
N_MIX_IN = sum(IN_WIDTHS)
_OFFS = [0]
for _w in IN_WIDTHS:
    _OFFS.append(_OFFS[-1] + _w)
(OFF_DA_Q, OFF_DA_K, OFF_DA_V, OFF_WB_Q, OFF_WB_K, OFF_WB_V, OFF_NA_Q, OFF_NA_K, OFF_NA_V, OFF_S5, _) = _OFFS

MIX_PROJ = (
    (OFF_DA_Q, OFF_WB_Q - OFF_DA_Q, 1536, BF16, IN_WIDTHS[0] + IN_WIDTHS[1]),
    (OFF_WB_Q, OFF_NA_Q - OFF_WB_Q, 768, BF16, IN_WIDTHS[3] + IN_WIDTHS[4]),
    (OFF_NA_Q, OFF_S5 - OFF_NA_Q, 768, BF16, 0),
    (OFF_S5, S5_WIDTH, 256, F32, 0),
)

LANE = 128
SUBLANE = 8
ADA_ROWS = 8
VMEM_LIMIT = 56 * 1024 * 1024

NN = (((1,), (0,)), ((), ()))
NT = (((1,), (1,)), ((), ()))


def _cp(*sem):
    return pltpu.CompilerParams(dimension_semantics=sem, vmem_limit_bytes=VMEM_LIMIT)


def _tile(n, pref, mult=LANE):
    if n <= pref:
        return n
    t = (pref // mult) * mult
    while t >= mult:
        if n % t == 0:
            return t
        t -= mult
    return n


def _split(a):
    hi = a.astype(BF16)
    lo = (a - hi.astype(F32)).astype(BF16)
    return hi, lo


def _dot(a, b, dims=NN):
    return lax.dot_general(a, b, dims, preferred_element_type=F32)


def _lane_group(lane, width):
    return lax.shift_right_logical(lane, int(math.log2(width)))


def _count(mask, axis):
    return jnp.sum(jnp.where(mask, 1.0, 0.0), axis=axis, keepdims=True)


def _dot3(a, b, dims=NN):
    ah, al = _split(a)
    bh, bl = _split(b)
    return _dot(ah, bh, dims) + (_dot(ah, bl, dims) + _dot(al, bh, dims))


def _ada_kernel(c_ref, w_ref, b_ref, o_ref):
    c = c_ref[...]
    s = c / (1.0 + jnp.exp(-c))
    o_ref[...] = _dot3(s, w_ref[...]) + b_ref[...]


def _ada_mod(cs, ada_w, ada_b):
    depth, d, n6 = ada_w.shape
    tn = _tile(n6, 512)
    return pl.pallas_call(
        _ada_kernel,
        grid=(depth, n6 // tn),
        in_specs=[
            pl.BlockSpec((ADA_ROWS, d), lambda l, j: (0, 0)),
            pl.BlockSpec((None, d, tn), lambda l, j: (l, 0, j)),
            pl.BlockSpec((None, 1, tn), lambda l, j: (l, 0, j)),
        ],
        out_specs=pl.BlockSpec((None, ADA_ROWS, tn), lambda l, j: (l, 0, j)),
        out_shape=jax.ShapeDtypeStruct((depth, ADA_ROWS, n6), F32),
        compiler_params=_cp("arbitrary", "arbitrary"),
        name="ada_mod",
    )(cs, ada_w, ada_b.reshape(depth, 1, n6))


def _norm_mod_rows(x, g, sh, sc):
    y = x * lax.rsqrt(jnp.mean(x * x, axis=-1, keepdims=True) + EPS) * g
    return y * (1.0 + sc) + sh


def _norm_mod_kernel(x_ref, g_ref, sh_ref, sc_ref, o_ref):
    o_ref[...] = _norm_mod_rows(x_ref[...], g_ref[...], sh_ref[...], sc_ref[...]).astype(o_ref.dtype)


def _norm_mod(x, g, sh, sc):
    b, t, d = x.shape
    tr = _tile(t, 1024, 8)
    return pl.pallas_call(
        _norm_mod_kernel,
        grid=(b, t // tr),
        in_specs=[
            pl.BlockSpec((None, tr, d), lambda bi, i: (bi, i, 0)),
            pl.BlockSpec((1, d), lambda bi, i: (0, 0)),
            pl.BlockSpec((None, 1, d), lambda bi, i: (bi, 0, 0)),
            pl.BlockSpec((None, 1, d), lambda bi, i: (bi, 0, 0)),
        ],
        out_specs=pl.BlockSpec((None, tr, d), lambda bi, i: (bi, i, 0)),
        out_shape=jax.ShapeDtypeStruct((b, t, d), BF16),
        compiler_params=_cp("parallel", "parallel"),
        name="norm_mod",
    )(x, g.reshape(1, d), sh, sc)


def _pack_halves(h):
    half = h.shape[1] // 2
    bits = pltpu.bitcast(h.astype(BF16).astype(F32), jnp.uint32)
    return bits[:, half:] | lax.shift_right_logical(bits[:, :half], jnp.uint32(16))


def _unpack_halves(p):
    lo = pltpu.bitcast(lax.shift_left(p, jnp.uint32(16)), F32)
    hi = pltpu.bitcast(p & jnp.uint32(0xFFFF0000), F32)
    return jnp.concatenate([lo, hi], axis=1).astype(BF16)


def _norm_router_kernel(x_ref, g_ref, sh_ref, sc_ref, wrt_ref, h_ref, lg_ref):
    h = _norm_mod_rows(x_ref[...], g_ref[...], sh_ref[...], sc_ref[...])
    h_ref[...] = _pack_halves(h)
    lg_ref[...] = _dot3(wrt_ref[...], h, NT)


def _norm_router(x, g, sh, sc, w_router_t):
    b, t, d = x.shape
    e = w_router_t.shape[0]
    tr = _tile(t, 512)
    nt = t // tr
    return pl.pallas_call(
        _norm_router_kernel,
        grid=(b, nt),
        in_specs=[
            pl.BlockSpec((None, tr, d), lambda bi, i: (bi, i, 0)),
            pl.BlockSpec((1, d), lambda bi, i: (0, 0)),
            pl.BlockSpec((None, 1, d), lambda bi, i: (bi, 0, 0)),
            pl.BlockSpec((None, 1, d), lambda bi, i: (bi, 0, 0)),
            pl.BlockSpec((e, d), lambda bi, i: (0, 0)),
        ],
        out_specs=[
            pl.BlockSpec((None, tr, d // 2), lambda bi, i: (bi, i, 0)),
            pl.BlockSpec((e, tr), lambda bi, i: (0, bi * nt + i)),
        ],
        out_shape=[jax.ShapeDtypeStruct((b, t, d // 2), jnp.uint32), jax.ShapeDtypeStruct((e, b * t), F32)],
        compiler_params=_cp("parallel", "parallel"),
        name="norm_router",
    )(x, g.reshape(1, d), sh, sc, w_router_t)


def _final_norm_kernel(x_ref, g_ref, o_ref):
    x = x_ref[...]
    o_ref[...] = x * lax.rsqrt(jnp.mean(x * x, axis=-1, keepdims=True) + EPS) * g_ref[...]


def _final_norm(x, g):
    b, t, d = x.shape
    tr = _tile(t, 1024, 8)
    return pl.pallas_call(
        _final_norm_kernel,
        grid=(b, t // tr),
        in_specs=[pl.BlockSpec((None, tr, d), lambda bi, i: (bi, i, 0)), pl.BlockSpec((1, d), lambda bi, i: (0, 0))],
        out_specs=pl.BlockSpec((None, tr, d), lambda bi, i: (bi, i, 0)),
        out_shape=jax.ShapeDtypeStruct((b, t, d), F32),
        compiler_params=_cp("parallel", "parallel"),
        name="final_norm",
    )(x, g.reshape(1, d))


def _mm_kernel(a_ref, w_ref, o_ref, wb_ref):
    @pl.when(pl.program_id(1) == 0)
    def _():
        wb_ref[...] = w_ref[...].astype(BF16)

    o_ref[...] = _dot(a_ref[...], wb_ref[...]).astype(o_ref.dtype)


def _mm_rope_kernel(a_ref, w_ref, c_ref, sa_ref, sb_ref, o_ref, wb_ref, *, n_rope):
    @pl.when(pl.program_id(1) == 0)
    def _():
        wb_ref[...] = w_ref[...].astype(BF16)

    z = _dot(a_ref[...], wb_ref[...])
    tn = z.shape[1]
    col0 = pl.program_id(0) * tn
    cos, sa, sb = c_ref[...], sa_ref[...], sb_ref[...]
    for cblk in range(tn // LANE):
        x = z[:, cblk * LANE:(cblk + 1) * LANE]
        y = x * cos + pltpu.roll(x, LANE - 16, 1) * sa + pltpu.roll(x, 16, 1) * sb
        o_ref[:, cblk * LANE:(cblk + 1) * LANE] = jnp.where(col0 + cblk * LANE < n_rope, y, x).astype(o_ref.dtype)


def _mm(a, w, layer, col0, ncols, tn, out_dtype, rope=None):
    m, k = a.shape
    assert ncols % tn == 0 and col0 % tn == 0
    tm = _tile(m, 1024, 8)
    j0 = col0 // tn
    in_specs = [
        pl.BlockSpec((tm, k), lambda j, i: (i, 0)),
        pl.BlockSpec((None, k, tn), lambda j, i: (layer, 0, j0 + j)),
    ]
    args = [a, w]
    body = _mm_kernel
    if rope is not None:
        tables, n_rope, seq = rope
        assert seq % tm == 0
        per_seq = seq // tm
        in_specs += [pl.BlockSpec((tm, LANE), lambda j, i: (i % per_seq, 0))] * 3
        args += list(tables)
        body = functools.partial(_mm_rope_kernel, n_rope=n_rope)
    return pl.pallas_call(
        body,
        grid=(ncols // tn, m // tm),
        in_specs=in_specs,
        out_specs=pl.BlockSpec((tm, tn), lambda j, i: (i, j)),
        out_shape=jax.ShapeDtypeStruct((m, ncols), out_dtype),
        scratch_shapes=[pltpu.VMEM((k, tn), BF16)],
        compiler_params=_cp("arbitrary", "arbitrary"),
        name="mm_in",
    )(*args)


def _mm_res_kernel(a_ref, w_ref, x_ref, gate_ref, o_ref, wb_ref):
    @pl.when(pl.program_id(1) == 0)
    def _():
        wb_ref[...] = w_ref[...].astype(BF16)

    o_ref[...] = x_ref[...] + gate_ref[...] * _dot(a_ref[...], wb_ref[...])


def _mm_res(a, w, layer, x, gate):
    b, t, n = x.shape
    m, k = a.shape
    tn = _tile(n, 1024)
    tm = _tile(t, 1024, 8)
    per_b = t // tm
    out = pl.pallas_call(
        _mm_res_kernel,
        grid=(n // tn, m // tm),
        in_specs=[
            pl.BlockSpec((tm, k), lambda j, i: (i, 0)),
            pl.BlockSpec((None, k, tn), lambda j, i: (layer, 0, j)),
            pl.BlockSpec((tm, tn), lambda j, i: (i, j)),
            pl.BlockSpec((None, 1, tn), lambda j, i: (i // per_b, 0, j)),
        ],
        out_specs=pl.BlockSpec((tm, tn), lambda j, i: (i, j)),
        out_shape=jax.ShapeDtypeStruct((m, n), F32),
        scratch_shapes=[pltpu.VMEM((k, tn), BF16)],
        compiler_params=_cp("arbitrary", "arbitrary"),
        name="mm_out_res",
    )(a, w, x.reshape(m, n), gate)
    return out.reshape(b, t, n)


def _rope_tables(n):
    pos = jnp.arange(n)
    rows, cols = (pos // GRID_W).astype(F32), (pos % GRID_W).astype(F32)
    half = DA_QK_DIM // 2
    inv_freq = jnp.power(ROPE_BASE, -jnp.arange(0, half, 2, dtype=F32) / half)
    ang_r, ang_c = rows[:, None] * inv_freq[None, :], cols[:, None] * inv_freq[None, :]
    cos64 = jnp.concatenate([jnp.cos(ang_r)] * 2 + [jnp.cos(ang_c)] * 2, axis=-1)
    sin64 = jnp.concatenate([jnp.sin(ang_r)] * 2 + [jnp.sin(ang_c)] * 2, axis=-1)
    cos, sin = jnp.tile(cos64, (1, 2)), jnp.tile(sin64, (1, 2))
    first = (jnp.arange(LANE) % 32) < 16
    return cos, jnp.where(first, -sin, 0.0), jnp.where(first, 0.0, sin)


def _diff_attn_kernel(*refs, has_x, post_scale):
    if has_x:
        lam_ref, g_ref, q_ref, kc_ref, vc_ref, kx_ref, vx_ref, o_ref = refs
    else:
        lam_ref, g_ref, q_ref, kc_ref, vc_ref, o_ref = refs
    q = q_ref[...].astype(F32) * (DA_QK_DIM ** -0.5 * LOG2E)
    lane = lax.broadcasted_iota(I32, q.shape, 1)
    if has_x:
        k = jnp.concatenate([kc_ref[...], kx_ref[...]], axis=0)
        v = jnp.concatenate([vc_ref[...], vx_ref[...]], axis=0)
    else:
        k, v = kc_ref[...], vc_ref[...]

    nk = k.shape[0]
    kc_len = _tile(nk, 768)

    def attend(qm):
        m = l = acc = None
        for k0 in range(0, nk, kc_len):
            s = _dot(qm, k[k0:k0 + kc_len], NT)
            m_c = jnp.max(s, axis=-1, keepdims=True)
            if m is None:
                m = m_c
                p = jnp.exp2(s - m)
                l = jnp.sum(p, axis=-1, keepdims=True)
                acc = _dot(p.astype(BF16), v[k0:k0 + kc_len])
            else:
                m_new = jnp.maximum(m, m_c)
                alpha = jnp.exp2(m - m_new)
                p = jnp.exp2(s - m_new)
                l = alpha * l + jnp.sum(p, axis=-1, keepdims=True)
                acc = alpha * acc + _dot(p.astype(BF16), v[k0:k0 + kc_len])
                m = m_new
        return acc / l

    o = attend(jnp.where(lane < DA_QK_DIM, q, 0.0).astype(BF16)) \
        - lam_ref[...] * attend(jnp.where(lane >= DA_QK_DIM, q, 0.0).astype(BF16))
    y = o * lax.rsqrt(jnp.mean(o * o, axis=-1, keepdims=True) + EPS) * g_ref[...] * post_scale
    o_ref[...] = y.astype(o_ref.dtype)


def _diff_attn(q_arr, zc, kx_arr, vx_arr, lam, subln_g, lam_init, has_x):
    b, tq_all, _ = q_arr.shape
    c = zc.shape[1]
    tq = _tile(tq_all, 1024, 8)
    q_blk0, kc_blk0, vc_blk0 = 0, IN_WIDTHS[0] // LANE, 2 * IN_WIDTHS[0] // LANE
    kx_blk0 = kc_blk0
    in_specs = [
        pl.BlockSpec((1, 1), lambda bi, h, i: (0, 0)),
        pl.BlockSpec((1, DA_V_DIM), lambda bi, h, i: (0, 0)),
        pl.BlockSpec((None, tq, LANE), lambda bi, h, i: (bi, i, q_blk0 + h)),
        pl.BlockSpec((None, c, LANE), lambda bi, h, i: (bi, 0, kc_blk0 + h)),
        pl.BlockSpec((None, c, LANE), lambda bi, h, i: (bi, 0, vc_blk0 + h)),
    ]
    args = [lam.reshape(1, 1), subln_g.reshape(1, DA_V_DIM), q_arr, zc, zc]
    if has_x:
        n = kx_arr.shape[1]
        in_specs += [
            pl.BlockSpec((None, n, LANE), lambda bi, h, i: (bi, 0, kx_blk0 + h)),
            pl.BlockSpec((None, n, LANE), lambda bi, h, i: (bi, 0, vc_blk0 + h)),
        ]
        args += [kx_arr, vx_arr]
    return pl.pallas_call(
        functools.partial(_diff_attn_kernel, has_x=has_x, post_scale=1.0 - lam_init),
        grid=(b, DA_HEADS, tq_all // tq),
        in_specs=in_specs,
        out_specs=pl.BlockSpec((None, tq, LANE), lambda bi, h, i: (bi, i, h)),
        out_shape=jax.ShapeDtypeStruct((b, tq_all, DA_HEADS * DA_V_DIM), BF16),
        compiler_params=_cp("parallel", "parallel", "arbitrary"),
        name="diff_attn_x" if has_x else "diff_attn_c",
    )(*args)


def _group_cols(x, g):
    x = x.astype(F32)
    lane = lax.broadcasted_iota(I32, x.shape, 1)
    own = jnp.where(_lane_group(lane, WB_DIM) == g, x, pltpu.roll(x, WB_DIM, 1))
    return jnp.concatenate([own, own], axis=1).astype(BF16)


def _win_attn_kernel(*refs, has_x, n_tok):
    if has_x:
        sink_ref, q_ref, kc_ref, vc_ref, kp_ref, k0_ref, kn_ref, vp_ref, v0_ref, vn_ref, o_ref = refs
    else:
        sink_ref, q_ref, kc_ref, vc_ref, o_ref = refs
    blk = WB_BLOCK
    nqb = q_ref.shape[0] // blk
    if has_x:
        k_lat = jnp.concatenate([kp_ref[...], k0_ref[...], kn_ref[...]], axis=0)
        v_lat = jnp.concatenate([vp_ref[...], v0_ref[...], vn_ref[...]], axis=0)
    for qb in range(nqb):
        rows = slice(qb * blk, (qb + 1) * blk)
        if has_x:
            k_in = jnp.concatenate([k_lat[qb * blk:(qb + 3) * blk], kc_ref[...]], axis=0)
            v_in = jnp.concatenate([v_lat[qb * blk:(qb + 3) * blk], vc_ref[...]], axis=0)
        else:
            k_in, v_in = kc_ref[...], vc_ref[...]
        o_ref[rows, :] = _win_block(sink_ref, q_ref[rows, :], k_in, v_in, pl.program_id(1) * nqb + qb,
                                    has_x, n_tok).astype(o_ref.dtype)


def _win_block(sink_ref, q_blk, k_in, v_in, i, has_x, n_tok):
    rep = WB_HEADS // WB_KV_HEADS
    gw = rep * WB_DIM
    q_all = q_blk.astype(F32) * (WB_DIM ** -0.5 * LOG2E)
    blk = q_all.shape[0]
    lane = lax.broadcasted_iota(I32, (blk, gw), 1)
    if has_x:
        q_pos = i * blk + (lax.broadcasted_iota(I32, (rep * blk, 3 * blk), 0) & (blk - 1))
        k_pos = (i - 1) * blk + lax.broadcasted_iota(I32, (rep * blk, 3 * blk), 1)
        valid = (jnp.abs(q_pos - k_pos) <= WB_WINDOW) & (k_pos >= 0) & (k_pos < n_tok)
    outs = []
    for g in range(WB_KV_HEADS):
        q = q_all[:, g * gw:(g + 1) * gw]
        k, v = _group_cols(k_in, g), _group_cols(v_in, g)
        mine = [_lane_group(lane, WB_DIM) == r for r in range(rep)]
        qs = jnp.concatenate([jnp.where(mine[r], q, 0.0) for r in range(rep)], axis=0).astype(BF16)
        sink = jnp.concatenate([jnp.broadcast_to(sink_ref[:, (g * rep + r) * WB_DIM:(g * rep + r) * WB_DIM + 1], (blk, 1))
                                for r in range(rep)], axis=0)
        s = _dot(qs, k, NT)
        if has_x:
            s = jnp.concatenate([jnp.where(valid, s[:, :3 * blk], NEG_INF), s[:, 3 * blk:]], axis=1)
        m = jnp.maximum(jnp.max(s, axis=-1, keepdims=True), sink)
        pr = jnp.exp2(s - m)
        l = jnp.sum(pr, axis=-1, keepdims=True) + jnp.exp2(sink - m)
        o = _dot(pr.astype(BF16), v) / l
        acc = jnp.zeros(q.shape, F32)
        for r in range(rep):
            acc = acc + jnp.where(mine[r], o[r * blk:(r + 1) * blk], 0.0)
        outs.append(acc)
    return jnp.concatenate(outs, axis=1)


def _win_attn(q_arr, zc, kx_arr, vx_arr, sink, has_x):
    b, tq_all, _ = q_arr.shape
    c = zc.shape[1]
    blk = WB_BLOCK
    nb = tq_all // blk
    qw = WB_HEADS * WB_DIM
    sink_arr = jnp.repeat(sink.astype(F32) * LOG2E, WB_DIM).reshape(1, qw)
    k_blk, v_blk = qw // LANE, qw // LANE + 1
    nqb = 2 if has_x and nb % 2 == 0 else 1
    tq = nqb * blk
    in_specs = [
        pl.BlockSpec((1, qw), lambda bi, i: (0, 0)),
        pl.BlockSpec((None, tq, qw), lambda bi, i: (bi, i, 0)),
        pl.BlockSpec((None, c, LANE), lambda bi, i: (bi, 0, k_blk)),
        pl.BlockSpec((None, c, LANE), lambda bi, i: (bi, 0, v_blk)),
    ]
    args = [sink_arr, q_arr, zc, zc]
    n_tok = 0
    if has_x:
        n_tok = kx_arr.shape[1]
        prev = lambda i: jnp.maximum(i * nqb - 1, 0)
        nxt = lambda i: jnp.minimum((i + 1) * nqb, nb - 1)
        for arr, cb in ((kx_arr, k_blk), (vx_arr, v_blk)):
            in_specs += [
                pl.BlockSpec((None, blk, LANE), lambda bi, i, cb=cb: (bi, prev(i), cb)),
                pl.BlockSpec((None, tq, LANE), lambda bi, i, cb=cb: (bi, i, cb)),
                pl.BlockSpec((None, blk, LANE), lambda bi, i, cb=cb: (bi, nxt(i), cb)),
            ]
            args += [arr, arr, arr]
    return pl.pallas_call(
        functools.partial(_win_attn_kernel, has_x=has_x, n_tok=n_tok),
        grid=(b, tq_all // tq),
        in_specs=in_specs,
        out_specs=pl.BlockSpec((None, tq, qw), lambda bi, i: (bi, i, 0)),
        out_shape=jax.ShapeDtypeStruct((b, tq_all, qw), BF16),
        compiler_params=_cp("parallel", "arbitrary"),
        name="win_attn_x" if has_x else "win_attn_c",
    )(*args)


NA_QROWS = 4
NA_KROWS = 12


def _na_kernel(*refs, has_x, n_blocks):
    if has_x:
        q_ref, kc_ref, vc_ref, k_ref, v_ref, t2_ref, o_ref = refs
    else:
        q_ref, kc_ref, vc_ref, o_ref = refs
    tq = q_ref.shape[0]
    lane = lax.broadcasted_iota(I32, (tq, LANE), 1)
    per = LANE // NA_DIM
    if has_x:
        p = pl.program_id(1)
        is_first, is_last = p == 0, p == n_blocks - 1
        n_rows = n_blocks * NA_QROWS
        start = jnp.where(is_first, 0, jnp.where(is_last, n_rows - NA_KROWS, p * NA_QROWS - NA_ROWS // 2))
        off = pl.multiple_of(start * GRID_W, GRID_W)
        nk = NA_KROWS * GRID_W
        row_i = lax.broadcasted_iota(I32, (tq, nk), 0)
        key_i = lax.broadcasted_iota(I32, (tq, nk), 1)
        qi, col = _lane_group(row_i, GRID_W), row_i & (GRID_W - 1)
        kp, w = _lane_group(key_i, GRID_W), key_i & (GRID_W - 1)
        lo = jnp.where(is_first, 0, jnp.where(is_last, NA_KROWS - NA_ROWS, qi))
        col_start = jnp.clip(col - NA_COLS // 2, 0, GRID_W - NA_COLS)
        valid = (w >= col_start) & (w < col_start + NA_COLS) & (kp >= lo) & (kp < lo + NA_ROWS)
        n_pair = t2_ref.shape[1]
        first_pair = [jnp.where(is_first, NA_ROWS - 1 - i, jnp.where(is_last, -1 - i, NA_ROWS // 2 - 1 - i))
                      for i in range(NA_QROWS)]
    for j in range(NA_HEADS // per):
        cols = slice(j * LANE, (j + 1) * LANE)
        q = q_ref[:, cols].astype(F32) * (NA_DIM ** -0.5 * LOG2E)
        if has_x:
            k = jnp.concatenate([k_ref[pl.ds(off, nk), cols], kc_ref[:, cols]], axis=0)
            v = jnp.concatenate([v_ref[pl.ds(off, nk), cols], vc_ref[:, cols]], axis=0)
        else:
            k, v = kc_ref[:, cols], vc_ref[:, cols]
        acc = jnp.zeros(q.shape, F32)
        for hh in range(per):
            mine = _lane_group(lane, NA_DIM) == hh
            qh = jnp.where(mine, q, 0.0).astype(BF16)
            s = _dot(qh, k, NT)
            if has_x:
                head = j * per + hh
                bias = jnp.concatenate([
                    jnp.concatenate([t2_ref[head, jnp.clip(first_pair[i] + 2 * m, 0, n_pair - 1)]
                                     for m in range(NA_KROWS // 2)], axis=1)
                    for i in range(NA_QROWS)], axis=0)
                s = jnp.concatenate([jnp.where(valid, s[:, :nk] + bias, NEG_INF), s[:, nk:]], axis=1)
            m = jnp.max(s, axis=-1, keepdims=True)
            pr = jnp.exp2(s - m)
            l = jnp.sum(pr, axis=-1, keepdims=True)
            acc = acc + jnp.where(mine, _dot(pr.astype(BF16), v) / l, 0.0)
        o_ref[:, cols] = acc.astype(o_ref.dtype)


def _na_bias_table(rpb):
    r = rpb.astype(F32) * LOG2E
    edge = GRID_W - NA_COLS
    ext = jnp.concatenate([jnp.repeat(r[..., :1], edge, -1), r, jnp.repeat(r[..., -1:], edge, -1)], axis=-1)
    by_col = jnp.stack([ext[..., GRID_W - 1 - c:2 * GRID_W - 1 - c] for c in range(GRID_W)], axis=-2)
    return jnp.concatenate([by_col[:, :-1], by_col[:, 1:]], axis=-1)


def _na_attn(q_arr, zc, zx, bias_tab, layer, has_x):
    b, tq_all, _ = q_arr.shape
    c = zc.shape[1]
    hw = NA_HEADS * NA_DIM
    tq = NA_QROWS * GRID_W if has_x else _tile(tq_all, 256, 8)
    nq = tq_all // tq
    in_specs = [
        pl.BlockSpec((None, tq, hw), lambda bi, r: (bi, r, 0)),
        pl.BlockSpec((None, c, hw), lambda bi, r: (bi, 0, 1)),
        pl.BlockSpec((None, c, hw), lambda bi, r: (bi, 0, 2)),
    ]
    args = [q_arr, zc, zc]
    if has_x:
        n = zx.shape[1]
        assert tq_all % tq == 0 and nq * NA_QROWS >= NA_KROWS
        in_specs += [
            pl.BlockSpec((None, n, hw), lambda bi, r: (bi, 0, 1)),
            pl.BlockSpec((None, n, hw), lambda bi, r: (bi, 0, 2)),
            pl.BlockSpec((None,) + bias_tab.shape[1:], lambda bi, r: (layer, 0, 0, 0, 0)),
        ]
        args += [zx, zx, bias_tab]
    return pl.pallas_call(
        functools.partial(_na_kernel, has_x=has_x, n_blocks=nq),
        grid=(b, nq),
        in_specs=in_specs,
        out_specs=pl.BlockSpec((None, tq, hw), lambda bi, r: (bi, r, 0)),
        out_shape=jax.ShapeDtypeStruct((b, tq_all, hw), BF16),
        compiler_params=_cp("parallel", "arbitrary"),
        name="na_attn_x" if has_x else "na_attn_c",
    )(*args)


def _s5_kernel(*refs, ctx_out):
    if ctx_out:
        (uc_ref, ux_ref, d_ref, b1_ref, b2_ref, a1_ref, a2_ref, a3_ref, cp_ref, scat_ref, yc_ref, yx_ref,
         kbig, bb1, bb2, cb_ref, xc, xx, s1c, s2c, s1x, s2x, hc, hx) = refs
    else:
        (uc_ref, ux_ref, d_ref, b1_ref, b2_ref, a1_ref, a2_ref, a3_ref, cp_ref, scat_ref, yx_ref,
         kbig, bb1, bb2, cb_ref, xc, xx, s1c, s2c, s1x, s2x, hc, hx) = refs
        yc_ref = None
    dirn = pl.program_id(1)
    ell, hch = S5_CHUNK, S5_GROUP_CH
    gq = LANE // hch
    nb = ux_ref.shape[0]
    streams = ((uc_ref, xc, s1c, s2c, hc, yc_ref, uc_ref.shape[1] // ell),
               (ux_ref, xx, s1x, s2x, hx, yx_ref, ux_ref.shape[1] // ell))

    for i in range(ell):
        for j in range(ell):
            kbig[i * LANE:(i + 1) * LANE, j * LANE:(j + 1) * LANE] = d_ref[j - i + ell - 1]
    bb1[...] = jnp.zeros(bb1.shape, BF16)
    bb2[...] = jnp.zeros(bb2.shape, BF16)
    for i in range(ell):
        for gl in range(gq):
            r0 = i * LANE + gl * hch
            bb1[r0:r0 + hch, gl * LANE:(gl + 1) * LANE] = b1_ref[gl, i * hch:(i + 1) * hch, :].astype(BF16)
            bb2[r0:r0 + hch, gl * LANE:(gl + 1) * LANE] = b2_ref[gl, i * hch:(i + 1) * hch, :].astype(BF16)
    for gl in range(gq):
        cb_ref[gl * LANE:(gl + 1) * LANE, :] = _dot(cp_ref[gl], scat_ref[gl]).astype(BF16)

    @pl.when(dirn == 0)
    def _():
        for u_ref, x_s, _, _, _, _, nc in streams:
            for b in range(nb):
                for i in range(ell):
                    x_s[b * nc:(b + 1) * nc, i * LANE:(i + 1) * LANE] = u_ref[b, pl.ds(i, nc, stride=ell), :].astype(BF16)

    cpt = SUBLANE // nb
    for _, x_s, s1, s2, _, _, nc in streams:
        loc1 = _dot(x_s[...], bb1[...])
        loc2 = _dot(x_s[...], bb2[...])
        for k in range(gq):
            for b in range(nb):
                s1[k, pl.ds(b, nc, stride=nb), :] = loc1[b * nc:(b + 1) * nc, k * LANE:(k + 1) * LANE]
                s2[k, pl.ds(b, nc, stride=nb), :] = loc2[b * nc:(b + 1) * nc, k * LANE:(k + 1) * LANE]

    def lane_blocks(a_ref):
        return [jnp.broadcast_to(a_ref[:, k * LANE:(k + 1) * LANE], (SUBLANE, LANE)) for k in range(gq)]

    a1, a2, a3 = lane_blocks(a1_ref), lane_blocks(a2_ref), lane_blocks(a3_ref)
    fwd = dirn == 0
    row_grp = _lane_group(lax.broadcasted_iota(I32, (SUBLANE, LANE), 0), nb)

    def advance(v):
        return jnp.where(fwd, pltpu.roll(v, nb, 0), pltpu.roll(v, SUBLANE - nb, 0))

    zero = tuple(jnp.zeros((SUBLANE, LANE), F32) for _ in range(gq))
    carry = (zero, zero)
    for _, _, s1, s2, h, _, nc in streams:
        n_tiles = nc // cpt

        def step(t, vs, s1=s1, s2=s2, h=h, n_tiles=n_tiles):
            v1, v2 = list(vs[0]), list(vs[1])
            j = jnp.where(fwd, t, n_tiles - 1 - t)
            rows = pl.ds(pl.multiple_of(j * SUBLANE, SUBLANE), SUBLANE)
            for k in range(gq):
                loc1, loc2 = s1[k, rows, :], s2[k, rows, :]
                entered = v1[k]
                for u in range(cpt):
                    grp = jnp.where(fwd, u, cpt - 1 - u)
                    entered = jnp.where(row_grp == grp, v1[k], entered)
                    n1 = a1[k] * v1[k] + a2[k] * v2[k] + loc1
                    n2 = a1[k] * v2[k] + a3[k] * v1[k] + loc2
                    v1[k], v2[k] = advance(n1), advance(n2)
                h[k, rows, :] = entered
            return tuple(v1), tuple(v2)

        carry = lax.fori_loop(0, n_tiles, step, carry, unroll=2)

    for _, x_s, _, _, h, y_ref, nc in streams:
        if y_ref is None:
            continue
        h_all = jnp.concatenate(
            [jnp.concatenate([h[k, pl.ds(b, nc, stride=nb), :] for b in range(nb)], axis=0) for k in range(gq)],
            axis=1).astype(BF16)
        y = _dot(x_s[...], kbig[...]) + _dot(h_all, cb_ref[...])

        @pl.when(dirn == 0)
        def _():
            for b in range(nb):
                for j in range(ell):
                    y_ref[b, pl.ds(j, nc, stride=ell), :] = y[b * nc:(b + 1) * nc, j * LANE:(j + 1) * LANE]

        @pl.when(dirn == 1)
        def _():
            for b in range(nb):
                for j in range(ell):
                    y_ref[b, pl.ds(j, nc, stride=ell), :] += y[b * nc:(b + 1) * nc, j * LANE:(j + 1) * LANE]


def _s5_tables(a_re, a_im, log_step, b_re, b_im, c_re, c_im):
    ell, hch, p = S5_CHUNK, S5_GROUP_CH, S5_STATE
    lam = lax.complex(a_re.astype(F32), a_im.astype(F32))
    lam_dt = lam * jnp.exp(log_step.astype(F32))[:, :, None]
    a_bar = jnp.exp(lam_dt)
    b_bar = ((a_bar - 1.0) / lam)[:, :, :, None] * lax.complex(b_re.astype(F32), b_im.astype(F32))
    c_mat = lax.complex(c_re.astype(F32), c_im.astype(F32))
    gq = LANE // hch
    nq = S5_GROUPS // gq
    hp = lax.Precision.HIGHEST
    steps = jnp.arange(ell + 1, dtype=F32)
    apow = jnp.exp(lam_dt[:, :, None, :] * steps[None, None, :, None])
    apow_re, apow_im = apow.real, apow.imag
    bt_re, bt_im = jnp.swapaxes(b_bar.real, 2, 3), jnp.swapaxes(b_bar.imag, 2, 3)
    c_re_, c_im_ = c_mat.real, c_mat.imag

    def cmul(ar, ai, br, bi):
        return ar * br - ai * bi, ar * bi + ai * br

    m_re, m_im = cmul(apow_re[:, :, :ell, None, :], apow_im[:, :, :ell, None, :], bt_re[:, :, None], bt_im[:, :, None])
    kk = (jnp.einsum('dgkip,dgop->dgkio', m_re, c_re_, precision=hp)
          - jnp.einsum('dgkip,dgop->dgkio', m_im, c_im_, precision=hp))
    zeros = jnp.zeros_like(kk[0, :, 1:])
    signed = jnp.stack([jnp.concatenate([zeros, kk[0]], axis=1),
                        jnp.concatenate([jnp.flip(kk[1], axis=1), zeros], axis=1)])
    signed = signed.reshape(2, nq, gq, 2 * ell - 1, hch, hch).astype(BF16)
    dblk = jnp.concatenate(
        [jnp.pad(signed[:, :, g], ((0, 0), (0, 0), (0, 0), (0, 0), (g * hch, LANE - (g + 1) * hch))) for g in range(gq)],
        axis=3)
    i_idx = jnp.arange(ell)
    e_re = jnp.stack([apow_re[0][:, ell - 1 - i_idx], apow_re[1][:, i_idx]])
    e_im = jnp.stack([apow_im[0][:, ell - 1 - i_idx], apow_im[1][:, i_idx]])
    bp_re, bp_im = cmul(e_re[:, :, :, None, :], e_im[:, :, :, None, :], bt_re[:, :, None], bt_im[:, :, None])
    bp_re, bp_im = bp_re.reshape(2, S5_GROUPS, ell * hch, p), bp_im.reshape(2, S5_GROUPS, ell * hch, p)
    b1 = jnp.concatenate([bp_re, bp_im], axis=-1)
    b2 = jnp.concatenate([bp_im, bp_re], axis=-1)
    al_re, al_im = apow_re[:, :, ell, :], apow_im[:, :, ell, :]
    lanes = lambda u, v: jnp.concatenate([u, v], axis=-1).reshape(2, nq, 1, gq * 2 * p)
    a1, a2, a3 = lanes(al_re, al_re), lanes(-al_im, al_im), lanes(al_im, -al_im)
    f_re = jnp.stack([apow_re[0][:, 1 + i_idx], apow_re[1][:, ell - i_idx]])
    f_im = jnp.stack([apow_im[0][:, 1 + i_idx], apow_im[1][:, ell - i_idx]])
    g_re, g_im = cmul(c_re_[:, :, None], c_im_[:, :, None], f_re[:, :, :, None, :], f_im[:, :, :, None, :])
    cp = jnp.concatenate([jnp.transpose(g_re, (0, 1, 4, 2, 3)), -jnp.transpose(g_im, (0, 1, 4, 2, 3))], axis=2)
    cp = cp.reshape(2, S5_GROUPS, 2 * p, ell * hch).astype(BF16)
    return dblk, b1, b2, a1, a2, a3, cp


def _s5_out_scatter():
    ell, hch = S5_CHUNK, S5_GROUP_CH
    gq = LANE // hch
    src = jnp.arange(ell * hch)
    dst = (src // hch)[None, :] * LANE + jnp.arange(gq)[:, None] * hch + (src % hch)[None, :]
    return (dst[:, :, None] == jnp.arange(ell * LANE)[None, None, :]).astype(BF16)


def _s5_scan(zx, zc, tables, layer, ctx_out):
    b, n, _ = zx.shape
    c = zc.shape[1]
    ell, hch = S5_CHUNK, S5_GROUP_CH
    gq = LANE // hch
    nq = S5_GROUPS // gq
    p2 = 2 * S5_STATE
    ncx, ncc = n // ell, c // ell
    assert n % ell == 0 and c % ell == 0 and ncc % 16 == 0
    assert SUBLANE % b == 0 and b & (b - 1) == 0
    dblk, b1, b2, a1, a2, a3, cp = tables
    u0 = 0
    sw = gq * p2
    xw = ell * LANE
    a_spec = pl.BlockSpec((None, None, None, 1, sw), lambda q, d: (layer, d, q, 0, 0))
    b_spec = pl.BlockSpec((None, None, gq, ell * hch, p2), lambda q, d: (layer, d, q, 0, 0))
    y_specs = [pl.BlockSpec((b, n, LANE), lambda q, d: (0, 0, q))]
    y_shapes = [jax.ShapeDtypeStruct((b, n, S5_WIDTH), F32)]
    if ctx_out:
        y_specs = [pl.BlockSpec((b, c, LANE), lambda q, d: (0, 0, q))] + y_specs
        y_shapes = [jax.ShapeDtypeStruct((b, c, S5_WIDTH), F32)] + y_shapes
    out = pl.pallas_call(
        functools.partial(_s5_kernel, ctx_out=ctx_out),
        grid=(nq, 2),
        in_specs=[
            pl.BlockSpec((b, c, LANE), lambda q, d: (0, 0, u0 + q)),
            pl.BlockSpec((b, n, LANE), lambda q, d: (0, 0, u0 + q)),
            pl.BlockSpec((None, None, None, 2 * ell - 1, LANE, LANE), lambda q, d: (layer, d, q, 0, 0, 0)),
            b_spec, b_spec, a_spec, a_spec, a_spec,
            pl.BlockSpec((None, None, gq, p2, ell * hch), lambda q, d: (layer, d, q, 0, 0)),
            pl.BlockSpec((gq, ell * hch, xw), lambda q, d: (0, 0, 0)),
        ],
        out_specs=y_specs,
        out_shape=y_shapes,
        scratch_shapes=[
            pltpu.VMEM((xw, xw), BF16), pltpu.VMEM((xw, sw), BF16), pltpu.VMEM((xw, sw), BF16),
            pltpu.VMEM((sw, xw), BF16),
            pltpu.VMEM((b * ncc, xw), BF16), pltpu.VMEM((b * ncx, xw), BF16),
            pltpu.VMEM((gq, b * ncc, p2), F32), pltpu.VMEM((gq, b * ncc, p2), F32),
            pltpu.VMEM((gq, b * ncx, p2), F32), pltpu.VMEM((gq, b * ncx, p2), F32),
            pltpu.VMEM((gq, b * ncc, p2), F32), pltpu.VMEM((gq, b * ncx, p2), F32),
        ],
        compiler_params=_cp("parallel", "arbitrary"),
        name="s5_scan",
    )(zc, zx, dblk, b1, b2, a1, a2, a3, cp, _s5_out_scatter())
    return (out[0], out[1]) if ctx_out else (None, out[0])


def _s5_out_kernel(y_ref, u_ref, d_ref, w_ref, o_ref):
    g = jax.nn.gelu(y_ref[...] + d_ref[...] * u_ref[...])
    o_ref[...] = (g * jax.nn.sigmoid(_dot(g.astype(BF16), w_ref[...].astype(BF16)))).astype(o_ref.dtype)


def _s5_out(y, u, d_skip, w_glu):
    b, t, w = y.shape
    tr = _tile(t, 512, 8)
    row = pl.BlockSpec((None, tr, w), lambda bi, i: (bi, i, 0))
    return pl.pallas_call(
        _s5_out_kernel,
        grid=(b, t // tr),
        in_specs=[row, row, pl.BlockSpec((1, w), lambda bi, i: (0, 0)), pl.BlockSpec((w, w), lambda bi, i: (0, 0))],
        out_specs=row,
        out_shape=jax.ShapeDtypeStruct((b, t, w), BF16),
        compiler_params=_cp("parallel", "parallel"),
        name="s5_out",
    )(y, u, d_skip.reshape(1, w).astype(F32), w_glu)


def _merge_kernel(h_ref, ya_ref, yb_ref, yc_ref, yd_ref, wg0_ref, wg1_ref, wg2_ref, wg3_ref, wb_ref, o_ref, wgs_ref, wbs_ref):
    @pl.when(pl.program_id(1) == 0)
    def _():
        for n, wg_ref in enumerate((wg0_ref, wg1_ref, wg2_ref, wg3_ref)):
            wgs_ref[n] = wg_ref[...].astype(BF16)
        wbs_ref[...] = wb_ref[...].astype(BF16)

    h = h_ref[...]
    acc = None
    for n, y_ref in enumerate((ya_ref, yb_ref, yc_ref, yd_ref)):
        term = jax.nn.sigmoid(_dot(h, wgs_ref[n])) * _dot(y_ref[...], wbs_ref[n])
        acc = term if acc is None else acc + term
    o_ref[...] = acc.astype(o_ref.dtype)


def _merge(h, branches, w_in, w_branch, layer):
    m, d = h.shape
    bw = branches[0].shape[1]
    tn = 256
    tm = _tile(m, 1024, 8)
    g0 = N_MIX_IN // tn
    per = d // tn
    assert N_MIX_IN % tn == 0 and d % tn == 0
    wg_specs = [pl.BlockSpec((None, d, tn), lambda j, i, n=n: (layer, 0, g0 + n * per + j)) for n in range(N_BRANCH)]
    y_spec = pl.BlockSpec((tm, bw), lambda j, i: (i, 0))
    return pl.pallas_call(
        _merge_kernel,
        grid=(d // tn, m // tm),
        in_specs=[pl.BlockSpec((tm, d), lambda j, i: (i, 0)), y_spec, y_spec, y_spec, y_spec] + wg_specs
                 + [pl.BlockSpec((None, N_BRANCH, bw, tn), lambda j, i: (layer, 0, 0, j))],
        out_specs=pl.BlockSpec((tm, tn), lambda j, i: (i, j)),
        out_shape=jax.ShapeDtypeStruct((m, d), BF16),
        scratch_shapes=[pltpu.VMEM((N_BRANCH, d, tn), BF16), pltpu.VMEM((N_BRANCH, bw, tn), BF16)],
        compiler_params=_cp("arbitrary", "arbitrary"),
        name="merge",
    )(h, *branches, w_in, w_in, w_in, w_in, w_branch)


def _topk_kernel(lg_ref, slot_ref, idx_ref, gate_ref, tri_ref, *, cap):
    e, t = lg_ref.shape

    @pl.when(pl.program_id(0) == 0)
    def _():
        rows = 256 if t % 256 == 0 else t
        for r0 in range(0, t, rows):
            ri = lax.broadcasted_iota(I32, (rows, t), 0) + r0
            ci = lax.broadcasted_iota(I32, (rows, t), 1)
            tri_ref[r0:r0 + rows, :] = jnp.where(ri < ci, 1.0, 0.0).astype(BF16)

    lg = lg_ref[...]
    ex = jnp.exp(lg - jnp.max(lg, axis=0, keepdims=True))
    aff = ex / jnp.sum(ex, axis=0, keepdims=True)
    bits = pltpu.bitcast(aff, I32)

    def search(_, carry):
        lo, hi = carry
        mid = lo + lax.shift_right_logical(hi - lo + 1, 1)
        ok = _count(bits >= mid, 1) >= cap
        return jnp.where(ok, mid, lo), jnp.where(ok, hi, mid - 1)

    lo0 = jnp.zeros((e, 1), I32)
    hi0 = jnp.full((e, 1), 0x7F800000, I32)
    thr, _ = lax.fori_loop(0, 32, search, (lo0, hi0))
    gt = bits > thr
    eq = bits == thr
    need = cap - _count(gt, 1)
    tri = tri_ref[...]
    eq_before = _dot(jnp.where(eq, 1.0, 0.0).astype(BF16), tri)
    sel = gt | (eq & (eq_before < need))
    sel_before = _dot(jnp.where(sel, 1.0, 0.0).astype(BF16), tri)
    slot = jnp.where(sel, sel_before.astype(I32), -1)
    slot_ref[...] = slot

    tok = lax.broadcasted_iota(I32, (SUBLANE, t), 1)
    row = lax.broadcasted_iota(I32, (SUBLANE, t), 0)
    tok_hi, tok_lo = lax.shift_right_logical(tok, 6).astype(F32), (tok & 63).astype(F32)
    slot_iota = lax.broadcasted_iota(I32, (cap, t), 0)
    for ei in range(e):
        a = jnp.broadcast_to(aff[ei:ei + 1, :], (SUBLANE, t))
        a_hi = a.astype(BF16).astype(F32)
        a_mid = (a - a_hi).astype(BF16).astype(F32)
        a_lo = a - a_hi - a_mid
        feats = jnp.where(row == 0, tok_hi, jnp.where(row == 1, tok_lo, jnp.where(
            row == 2, a_hi, jnp.where(row == 3, a_mid, jnp.where(row == 4, a_lo, 0.0))))).astype(BF16)
        pick = jnp.where(slot_iota == slot[ei:ei + 1, :], 1.0, 0.0).astype(BF16)
        res = _dot(feats, pick, NT)
        idx_ref[ei:ei + 1, :] = (res[0:1] * 64.0 + res[1:2]).astype(I32)
        gate_ref[ei:ei + 1, :] = res[2:3] + res[3:4] + res[4:5]


def _topk(logits_t, b, cap):
    e, bt = logits_t.shape
    t = bt // b
    per_b = lambda w: pl.BlockSpec((None, e, w), lambda bi: (bi, 0, 0))
    return pl.pallas_call(
        functools.partial(_topk_kernel, cap=cap),
        grid=(b,),
        in_specs=[pl.BlockSpec((e, t), lambda bi: (0, bi))],
        out_specs=[per_b(t), per_b(cap), per_b(cap)],
        out_shape=[jax.ShapeDtypeStruct((b, e, t), I32), jax.ShapeDtypeStruct((b, e, cap), I32),
                   jax.ShapeDtypeStruct((b, e, cap), F32)],
        scratch_shapes=[pltpu.VMEM((t, t), BF16)],
        compiler_params=_cp("arbitrary"),
        name="route_topk",
    )(logits_t)


def _ffn_up_kernel(*refs, n_streams, n_f):
    ns = n_streams
    idx_refs, h_refs = refs[:ns], refs[ns:2 * ns]
    w1_ref, w3_ref = refs[2 * ns:2 * ns + 2]
    outs = refs[2 * ns + 2:3 * ns + 2]
    scratch = refs[3 * ns + 2:]
    gbufs, xss, sems = scratch[:ns], scratch[ns:2 * ns], scratch[2 * ns:3 * ns]
    e, f = pl.program_id(0), pl.program_id(1)
    n_e = pl.num_programs(0)
    slot = lax.rem(e, 2)

    for idx_ref, h_ref, gbuf, xs, sem in zip(idx_refs, h_refs, gbufs, xss, sems):
        rows = gbuf.shape[1]

        @pl.when((e == 0) & (f == 0))
        def _():
            def start(r, carry):
                pltpu.make_async_copy(h_ref.at[idx_ref[0, r]], gbuf.at[0, r], sem.at[0]).start()
                return carry
            lax.fori_loop(0, rows, start, 0)

        @pl.when(f == 0)
        def _():
            pltpu.make_async_copy(gbuf.at[slot], gbuf.at[slot], sem.at[slot]).wait()
            xs[...] = _unpack_halves(gbuf[slot])

    nxt = jnp.minimum(e + 1, n_e - 1)
    w1, w3 = w1_ref[...].astype(BF16), w3_ref[...].astype(BF16)
    for idx_ref, h_ref, gbuf, xs, sem, o_ref in zip(idx_refs, h_refs, gbufs, xss, sems, outs):
        x = xs[...]
        a = _dot(x, w1)
        g = _dot(x, w3)
        part = gbuf.shape[1] // n_f
        for i in range(part):
            r = f * part + i
            pltpu.make_async_copy(h_ref.at[idx_ref[nxt, r]], gbuf.at[1 - slot, r], sem.at[1 - slot]).start()
        o_ref[...] = ((a / (1.0 + jnp.exp(-a))) * g).astype(o_ref.dtype)

    @pl.when((e == n_e - 1) & (f == n_f - 1))
    def _():
        for gbuf, sem in zip(gbufs, sems):
            pltpu.make_async_copy(gbuf.at[1 - slot], gbuf.at[1 - slot], sem.at[1 - slot]).wait()


def _ffn_down_kernel(*refs, n_streams):
    mids = refs[:n_streams]
    gates = refs[n_streams:2 * n_streams]
    w2_ref = refs[2 * n_streams]
    outs = refs[2 * n_streams + 1:]
    w2 = w2_ref[...].astype(BF16)
    for m_ref, gate_ref, o_ref in zip(mids, gates, outs):
        o_ref[...] = (_dot(m_ref[...], w2) * gate_ref[...]).astype(o_ref.dtype)


def _ffn(idx_list, h_list, gate_list, w1, w3, w2, layer):
    ns = len(idx_list)
    e = idx_list[0].shape[0]
    d = 2 * h_list[0].shape[1]
    ff = w1.shape[-1]
    tf = _tile(ff, 512)
    tn = _tile(d, 1024)
    n_f = ff // tf
    rows = [idx.shape[1] for idx in idx_list]
    assert all(r % n_f == 0 for r in rows)
    hbm = pl.BlockSpec(memory_space=pl.ANY)
    mids = pl.pallas_call(
        functools.partial(_ffn_up_kernel, n_streams=ns, n_f=n_f),
        grid_spec=pltpu.PrefetchScalarGridSpec(
            num_scalar_prefetch=ns,
            grid=(e, n_f),
            in_specs=[hbm] * ns + [
                pl.BlockSpec((None, None, d, tf), lambda ei, f, *_: (layer, ei, 0, f)),
                pl.BlockSpec((None, None, d, tf), lambda ei, f, *_: (layer, ei, 0, f)),
            ],
            out_specs=[pl.BlockSpec((None, r, tf), lambda ei, f, *_: (ei, 0, f)) for r in rows],
            scratch_shapes=[pltpu.VMEM((2, r, d // 2), jnp.uint32) for r in rows]
                           + [pltpu.VMEM((r, d), BF16) for r in rows]
                           + [pltpu.SemaphoreType.DMA((2,)) for _ in rows],
        ),
        out_shape=[jax.ShapeDtypeStruct((e, r, ff), BF16) for r in rows],
        compiler_params=_cp("arbitrary", "arbitrary"),
        name="moe_ffn_up",
    )(*idx_list, *h_list, w1, w3)
    return pl.pallas_call(
        functools.partial(_ffn_down_kernel, n_streams=ns),
        grid=(e, d // tn),
        in_specs=[pl.BlockSpec((None, r, ff), lambda ei, j: (ei, 0, 0)) for r in rows]
                 + [pl.BlockSpec((None, r, 1), lambda ei, j: (ei, 0, 0)) for r in rows]
                 + [pl.BlockSpec((None, None, ff, tn), lambda ei, j: (layer, ei, 0, j))],
        out_specs=[pl.BlockSpec((None, r, tn), lambda ei, j: (ei, 0, j)) for r in rows],
        out_shape=[jax.ShapeDtypeStruct((e, r, d), BF16) for r in rows],
        compiler_params=_cp("parallel", "arbitrary"),
        name="moe_ffn_down",
    )(*mids, *gate_list, w2)


def _combine_kernel(slot_ref, y_ref, x_ref, gate_ref, o_ref, pt_ref, *, cap):
    e = slot_ref.shape[1]
    tm = slot_ref.shape[0]
    r0 = pl.multiple_of(pl.program_id(2) * tm, tm)

    @pl.when(pl.program_id(1) == 0)
    def _():
        slot = slot_ref[...]
        lane = lax.broadcasted_iota(I32, (tm, cap), 1)
        for ei in range(e):
            pt_ref[pl.ds(r0, tm), ei * cap:(ei + 1) * cap] = jnp.where(lane == slot[:, ei:ei + 1], 1.0, 0.0).astype(BF16)

    y = y_ref[...]
    acc = _dot(pt_ref[pl.ds(r0, tm), :], y.reshape(e * cap, y.shape[2]))
    o_ref[...] = x_ref[...] + gate_ref[...] * acc


def _combine(slot_te, ys, x, gate, cap):
    b, t, d = x.shape
    e = slot_te.shape[2]
    tm = _tile(t, 1024, 8)
    tn = _tile(d, 512)
    return pl.pallas_call(
        functools.partial(_combine_kernel, cap=cap),
        grid=(b, d // tn, t // tm),
        in_specs=[
            pl.BlockSpec((None, tm, e), lambda bi, j, i: (bi, i, 0)),
            pl.BlockSpec((e, cap, tn), lambda bi, j, i: (0, bi, j)),
            pl.BlockSpec((None, tm, tn), lambda bi, j, i: (bi, i, j)),
            pl.BlockSpec((None, 1, tn), lambda bi, j, i: (bi, 0, j)),
        ],
        out_specs=pl.BlockSpec((None, tm, tn), lambda bi, j, i: (bi, i, j)),
        out_shape=jax.ShapeDtypeStruct((b, t, d), F32),
        scratch_shapes=[pltpu.VMEM((t, e * cap), BF16)],
        compiler_params=_cp("parallel", "arbitrary", "arbitrary"),
        name="moe_combine",
    )(slot_te, ys, x, gate)


def kernel(x, c, ctx, c_ctx, ada_w, ada_b, norm1_g, norm2_g, w_in, da_lambda, da_subln_g, wb_sink, na_rpb, s5_a_re, s5_a_im, s5_log_step, s5_b_re, s5_b_im, s5_c_re, s5_c_im, s5_d, s5_glu_w, w_branch, w_out, w_router, w_e1, w_e3, w_e2, final_g):
    b, n, d = x.shape
    n_ctx = ctx.shape[1]
    depth = ada_w.shape[0]
    assert b + 1 <= ADA_ROWS and n % GRID_W == 0

    cs = jnp.zeros((ADA_ROWS, d), F32).at[:b].set(c).at[b].set(c_ctx)
    mods = _ada_mod(cs, ada_w, ada_b)
    rope_tab = _rope_tables(n)
    bias_tab = jax.vmap(_na_bias_table)(na_rpb)
    s5_tab = jax.vmap(_s5_tables)(s5_a_re, s5_a_im, s5_log_step, s5_b_re, s5_b_im, s5_c_re, s5_c_im)

    for l in range(depth):
        with_ctx = l < depth - 1
        mod_x = [mods[l, :b, k * d:(k + 1) * d].reshape(b, 1, d) for k in range(6)]
        mod_c = [jnp.broadcast_to(mods[l, b, k * d:(k + 1) * d], (b, 1, d)) for k in range(6)]

        hx = _norm_mod(x, norm1_g[l], mod_x[0], mod_x[1])
        hc = _norm_mod(ctx, norm1_g[l], mod_c[0], mod_c[1])
        proj = []
        for h, t, positions in ((hx, n, True), (hc, n_ctx, False)):
            h2d = h.reshape(b * t, d)
            proj.append([
                _mm(h2d, w_in, l, col0, ncols, tn, dt,
                    rope=(rope_tab, n_rope, t) if positions and n_rope else None).reshape(b, t, ncols)
                for col0, ncols, tn, dt, n_rope in MIX_PROJ])
        (za_x, zb_x, zn_x, zs_x), (za_c, zb_c, zn_c, zs_c) = proj

        lam_init = 0.8 - 0.6 * math.exp(-0.3 * l)
        lv = da_lambda[l].astype(F32)
        lam = jnp.exp(jnp.sum(lv[0] * lv[1])) - jnp.exp(jnp.sum(lv[2] * lv[3])) + lam_init

        ya_x = _diff_attn(za_x, za_c, za_x, za_x, lam, da_subln_g[l], lam_init, True)
        yb_x = _win_attn(zb_x, zb_c, zb_x, zb_x, wb_sink[l], True)
        yc_x = _na_attn(zn_x, zn_c, zn_x, bias_tab, l, True)
        ys_c, ys_x = _s5_scan(zs_x, zs_c, s5_tab, l, with_ctx)
        yd_x = _s5_out(ys_x, zs_x, s5_d[l], s5_glu_w[l])
        gx = _merge(hx.reshape(b * n, d), [t.reshape(b * n, -1) for t in (ya_x, yb_x, yc_x, yd_x)], w_in, w_branch, l)
        x = _mm_res(gx, w_out, l, x, mod_x[2])

        if with_ctx:
            ya_c = _diff_attn(za_c, za_c, None, None, lam, da_subln_g[l], lam_init, False)
            yb_c = _win_attn(zb_c, zb_c, None, None, wb_sink[l], False)
            yc_c = _na_attn(zn_c, zn_c, None, None, l, False)
            yd_c = _s5_out(ys_c, zs_c, s5_d[l], s5_glu_w[l])
            gc = _merge(hc.reshape(b * n_ctx, d), [t.reshape(b * n_ctx, -1) for t in (ya_c, yb_c, yc_c, yd_c)],
                        w_in, w_branch, l)
            ctx = _mm_res(gc, w_out, l, ctx, mod_c[2])

        w_router_t = jnp.transpose(w_router[l]).astype(F32)
        streams = [(x, mod_x)] + ([(ctx, mod_c)] if with_ctx else [])
        routed = []
        for s, mod in streams:
            t = s.shape[1]
            cap = EC_CAPACITY_FACTOR * t // N_EXPERTS
            h2p, logits_t = _norm_router(s, norm2_g[l], mod[3], mod[4], w_router_t)
            slot, idx, gate = _topk(logits_t, b, cap)
            by_expert = lambda v: jnp.transpose(v, (1, 0, 2)).reshape(N_EXPERTS, b * cap)
            flat_row = idx + (jnp.arange(b, dtype=I32) * t)[:, None, None]
            routed.append((jnp.transpose(slot, (0, 2, 1)), by_expert(flat_row), h2p.reshape(b * t, d // 2),
                           by_expert(gate)[:, :, None], cap))
        ys = _ffn([r[1] for r in routed], [r[2] for r in routed], [r[3] for r in routed], w_e1, w_e3, w_e2, l)
        x = _combine(routed[0][0], ys[0], x, mod_x[5], routed[0][4])
        if with_ctx:
            ctx = _combine(routed[1][0], ys[1], ctx, mod_c[5], routed[1][4])

    return _final_norm(x, final_g)
```
